```python
import jax, jax.numpy as jnp
from jax import lax
import numpy as np

D_MODEL = 2048
BATCH = 16
SEQ = 2048
DEPTH = 4

CONV_WIDTH = D_MODEL // 4
CONV_GROUPS = 8
CONV_KERNEL = 31
POOL_WIDTH = D_MODEL // 4
POOL_WINDOWS = (2, 4, 8, 16)
POOL_GROUP = POOL_WIDTH // len(POOL_WINDOWS)
QK_NOPE_DIM = 128
QK_ROPE_DIM = 64
V_HEAD_DIM = 128
MLA_HEADS = (D_MODEL - CONV_WIDTH - POOL_WIDTH) // V_HEAD_DIM
MLA_WIDTH = MLA_HEADS * V_HEAD_DIM
Q_LORA_RANK = D_MODEL // 4
KV_LORA_RANK = D_MODEL // 8
MIX_WIDTH = MLA_WIDTH + CONV_WIDTH + POOL_WIDTH
IN_COLS = Q_LORA_RANK + KV_LORA_RANK + QK_ROPE_DIM + 2 * CONV_WIDTH + POOL_WIDTH
D_FF = ((8 * D_MODEL // 3 + 255) // 256) * 256
FFN_CONV_KERNEL = 3
ROPE_THETA = 10000.0
Q_BLOCK = 128
LN_EPS = 1e-5
RMS_EPS = 1e-6
DEEPNORM_ALPHA = (2.0 * DEPTH) ** 0.25
DEEPNORM_BETA = (8.0 * DEPTH) ** -0.25

kernel_name = "hymba_style_mla_conformer_pool_hybrid"


def layer_norm(x, g, b):
    xf = x.astype(jnp.float32)
    mu = jnp.mean(xf, axis=-1, keepdims=True)
    var = jnp.mean(jnp.square(xf - mu), axis=-1, keepdims=True)
    return ((xf - mu) * lax.rsqrt(var + LN_EPS) * g.astype(jnp.float32) + b.astype(jnp.float32)).astype(x.dtype)


def rms_norm(x, g):
    xf = x.astype(jnp.float32)
    ms = jnp.mean(jnp.square(xf), axis=-1, keepdims=True)
    return (xf * lax.rsqrt(ms + RMS_EPS) * g.astype(jnp.float32)).astype(x.dtype)


def causal_dwconv(x, w, b):
    k, c = w.shape
    y = lax.conv_general_dilated(
        x, w[:, None, :].astype(x.dtype), window_strides=(1,), padding=[(k - 1, 0)],
        dimension_numbers=("NWC", "WIO", "NWC"), feature_group_count=c)
    return y + b.astype(x.dtype)


def rope_cos_sin(positions):
    inv = 1.0 / (ROPE_THETA ** (jnp.arange(0, QK_ROPE_DIM, 2, dtype=jnp.float32) / QK_ROPE_DIM))
    ang = positions.astype(jnp.float32)[..., None] * inv
    return jnp.cos(ang), jnp.sin(ang)


def apply_rope(x, cos, sin):
    xf = x.astype(jnp.float32)
    x1, x2 = jnp.split(xf, 2, axis=-1)
    return jnp.concatenate([x1 * cos - x2 * sin, x1 * sin + x2 * cos], axis=-1).astype(x.dtype)


def mla_mixer(c_q, c_kv, k_rope, q_norm_g, w_uq, kv_norm_g, w_ukv, cos, sin):
    bsz, seq, _ = c_q.shape
    q = jnp.einsum("bsr,rhd->bshd", rms_norm(c_q, q_norm_g), w_uq)
    q_nope = q[..., :QK_NOPE_DIM]
    q_rope = apply_rope(q[..., QK_NOPE_DIM:], cos[:, :, None, :], sin[:, :, None, :])
    kv = jnp.einsum("bsr,rhd->bshd", rms_norm(c_kv, kv_norm_g), w_ukv)
    k_nope = kv[..., :QK_NOPE_DIM]
    v = kv[..., QK_NOPE_DIM:]
    k_rope = apply_rope(k_rope, cos, sin)
    qb = min(Q_BLOCK, seq)
    nb = seq // qb
    qn_blocks = q_nope.reshape(bsz, nb, qb, MLA_HEADS, QK_NOPE_DIM).transpose(1, 0, 2, 3, 4)
    qr_blocks = q_rope.reshape(bsz, nb, qb, MLA_HEADS, QK_ROPE_DIM).transpose(1, 0, 2, 3, 4)
    key_idx = jnp.arange(seq)
    scale = (QK_NOPE_DIM + QK_ROPE_DIM) ** -0.5
    neg = jnp.finfo(jnp.float32).min

    def attend(args):
        qn, qr, start = args
        s = (jnp.einsum("bqhd,bkhd->bhqk", qn, k_nope, preferred_element_type=jnp.float32)
             + jnp.einsum("bqhr,bkr->bhqk", qr, k_rope, preferred_element_type=jnp.float32)) * scale
        q_idx = start + jnp.arange(qb)
        s = jnp.where(key_idx[None, :] <= q_idx[:, None], s, neg)
        p = jax.nn.softmax(s, axis=-1)
        return jnp.einsum("bhqk,bkhd->bqhd", p.astype(v.dtype), v)

    out = lax.map(attend, (qn_blocks, qr_blocks, jnp.arange(nb) * qb))
    return out.transpose(1, 0, 2, 3, 4).reshape(bsz, seq, MLA_WIDTH)


def conformer_conv_mixer(u, conv_w, conv_b, ln_g, ln_b):
    a, g = jnp.split(u, 2, axis=-1)
    h = a * jax.nn.sigmoid(g)
    h = causal_dwconv(h, conv_w, conv_b)
    h = layer_norm(h, ln_g, ln_b)
    return jax.nn.silu(h)


def pool_mixer(u, w_pool, scale):
    bsz, seq, c = u.shape
    uf = u.astype(jnp.float32)
    cs = jnp.concatenate([jnp.zeros((bsz, 1, c), jnp.float32), lax.cumsum(uf, axis=1)], axis=1)
    t = jnp.arange(seq)
    outs = []
    for gi, w in enumerate(POOL_WINDOWS):
        sl = slice(gi * POOL_GROUP, (gi + 1) * POOL_GROUP)
        lo = jnp.maximum(t + 1 - w, 0)
        win_sum = cs[:, 1:, sl] - cs[:, lo, sl]
        cnt = (t + 1 - lo).astype(jnp.float32)[None, :, None]
        outs.append(win_sum / cnt - uf[:, :, sl])
    d = jnp.stack(outs, axis=2).astype(u.dtype)
    y = jnp.einsum("bsgc,gcd->bsgd", d, w_pool).reshape(bsz, seq, c)
    return y * scale


def _fwd_setup_inputs(seed: int = 0) -> dict:
    key = jax.random.key(seed)
    ks = jax.random.split(key, 24)
    f32 = jnp.float32

    def nrm(k, shape, s):
        return jax.random.normal(k, shape, f32) * s

    def gain(k, shape):
        return 1.0 + 0.02 * jax.random.normal(k, shape, f32)

    L = DEPTH
    x = jax.random.normal(ks[0], (BATCH, SEQ, D_MODEL), f32)
    positions = jnp.broadcast_to(jnp.arange(SEQ, dtype=jnp.int32)[None, :], (BATCH, SEQ))
    return {
        "x": x,
        "positions": positions,
        "ln_in_g": gain(ks[1], (D_MODEL,)),
        "ln_in_b": nrm(ks[2], (D_MODEL,), 0.02),
        "w_in": nrm(ks[3], (L, D_MODEL, IN_COLS), D_MODEL ** -0.5),
        "q_norm_g": gain(ks[4], (L, Q_LORA_RANK)),
        "w_uq": nrm(ks[5], (L, Q_LORA_RANK, MLA_HEADS, QK_NOPE_DIM + QK_ROPE_DIM), Q_LORA_RANK ** -0.5),
        "kv_norm_g": gain(ks[6], (L, KV_LORA_RANK)),
        "w_ukv": nrm(ks[7], (L, KV_LORA_RANK, MLA_HEADS, QK_NOPE_DIM + V_HEAD_DIM), KV_LORA_RANK ** -0.5),
        "conv_w": nrm(ks[8], (L, CONV_KERNEL, CONV_WIDTH), CONV_KERNEL ** -0.5),
        "conv_b": nrm(ks[9], (L, CONV_WIDTH), 0.02),
        "conv_ln_g": gain(ks[10], (L, CONV_WIDTH)),
        "conv_ln_b": nrm(ks[11], (L, CONV_WIDTH), 0.02),
        "w_pool": nrm(ks[12], (L, len(POOL_WINDOWS), POOL_GROUP, POOL_GROUP), POOL_GROUP ** -0.5),
        "pool_scale": gain(ks[13], (L, POOL_WIDTH)),
        "w_out": nrm(ks[14], (L, MIX_WIDTH, D_MODEL), DEEPNORM_BETA * MIX_WIDTH ** -0.5),
        "ln1_g": gain(ks[15], (L, D_MODEL)),
        "ln1_b": nrm(ks[16], (L, D_MODEL), 0.02),
        "w_up": nrm(ks[17], (L, D_MODEL, 2 * D_FF), D_MODEL ** -0.5),
        "ffn_conv_w": nrm(ks[18], (L, FFN_CONV_KERNEL, 2 * D_FF), FFN_CONV_KERNEL ** -0.5),
        "ffn_conv_b": nrm(ks[19], (L, 2 * D_FF), 0.02),
        "w_down": nrm(ks[20], (L, D_FF, D_MODEL), DEEPNORM_BETA * D_FF ** -0.5),
        "ln2_g": gain(ks[21], (L, D_MODEL)),
        "ln2_b": nrm(ks[22], (L, D_MODEL), 0.02),
    }


def _fwd_reference(x, positions, ln_in_g, ln_in_b, w_in, q_norm_g, w_uq, kv_norm_g, w_ukv,
              conv_w, conv_b, conv_ln_g, conv_ln_b, w_pool, pool_scale, w_out, ln1_g, ln1_b,
              w_up, ffn_conv_w, ffn_conv_b, w_down, ln2_g, ln2_b):
    cos, sin = rope_cos_sin(positions)
    x = layer_norm(x, ln_in_g, ln_in_b)
    o1 = Q_LORA_RANK
    o2 = o1 + KV_LORA_RANK
    o3 = o2 + QK_ROPE_DIM
    o4 = o3 + 2 * CONV_WIDTH
    for l in range(DEPTH):
        h = jnp.einsum("bsd,dc->bsc", x, w_in[l])
        c_q, c_kv, k_rope = h[..., :o1], h[..., o1:o2], h[..., o2:o3]
        u_conv, u_pool = h[..., o3:o4], h[..., o4:]
        y_mla = mla_mixer(c_q, c_kv, k_rope, q_norm_g[l], w_uq[l], kv_norm_g[l], w_ukv[l], cos, sin)
        y_conv = conformer_conv_mixer(u_conv, conv_w[l], conv_b[l], conv_ln_g[l], conv_ln_b[l])
        y_pool = pool_mixer(u_pool, w_pool[l], pool_scale[l])
        mixed = jnp.concatenate([y_mla, y_conv, y_pool], axis=-1)
        y = jnp.einsum("bsc,cd->bsd", mixed, w_out[l])
        x = layer_norm(DEEPNORM_ALPHA * x + y, ln1_g[l], ln1_b[l])
        up = jnp.einsum("bsd,df->bsf", x, w_up[l])
        up = causal_dwconv(up, ffn_conv_w[l], ffn_conv_b[l])
        a, g = jnp.split(up, 2, axis=-1)
        y = jnp.einsum("bsf,fd->bsd", a * jax.nn.silu(g), w_down[l])
        x = layer_norm(DEEPNORM_ALPHA * x + y, ln2_g[l], ln2_b[l])
    return x


import jax as _jax
import jax.numpy as _jnp

TWIN_FORMAT = 'train_step'
FWD_PARAMS = ['x', 'positions', 'ln_in_g', 'ln_in_b', 'w_in', 'q_norm_g', 'w_uq', 'kv_norm_g', 'w_ukv', 'conv_w', 'conv_b', 'conv_ln_g', 'conv_ln_b', 'w_pool', 'pool_scale', 'w_out', 'ln1_g', 'ln1_b', 'w_up', 'ffn_conv_w', 'ffn_conv_b', 'w_down', 'ln2_g', 'ln2_b']
TWIN_WEIGHTS = ['ln_in_g', 'ln_in_b', 'w_in', 'q_norm_g', 'w_uq', 'kv_norm_g', 'w_ukv', 'conv_w', 'conv_b', 'conv_ln_g', 'conv_ln_b', 'w_pool', 'pool_scale', 'w_out', 'ln1_g', 'ln1_b', 'w_up', 'ffn_conv_w', 'ffn_conv_b', 'w_down', 'ln2_g', 'ln2_b']
TWIN_DIFF_INPUT = 'x'
TWIN_INPUTS = ['x', 'positions', 'ln_in_g', 'ln_in_b', 'w_in', 'q_norm_g', 'w_uq', 'kv_norm_g', 'w_ukv', 'conv_w', 'conv_b', 'conv_ln_g', 'conv_ln_b', 'w_pool', 'pool_scale', 'w_out', 'ln1_g', 'ln1_b', 'w_up', 'ffn_conv_w', 'ffn_conv_b', 'w_down', 'ln2_g', 'ln2_b', 'loss_target', 'm_ln_in_g', 'm_ln_in_b', 'm_w_in', 'm_q_norm_g', 'm_w_uq', 'm_kv_norm_g', 'm_w_ukv', 'm_conv_w', 'm_conv_b', 'm_conv_ln_g', 'm_conv_ln_b', 'm_w_pool', 'm_pool_scale', 'm_w_out', 'm_ln1_g', 'm_ln1_b', 'm_w_up', 'm_ffn_conv_w', 'm_ffn_conv_b', 'm_w_down', 'm_ln2_g', 'm_ln2_b', 'v_ln_in_g', 'v_ln_in_b', 'v_w_in', 'v_q_norm_g', 'v_w_uq', 'v_kv_norm_g', 'v_w_ukv', 'v_conv_w', 'v_conv_b', 'v_conv_ln_g', 'v_conv_ln_b', 'v_w_pool', 'v_pool_scale', 'v_w_out', 'v_ln1_g', 'v_ln1_b', 'v_w_up', 'v_ffn_conv_w', 'v_ffn_conv_b', 'v_w_down', 'v_ln2_g', 'v_ln2_b']
TWIN_OUTPUTS = ['loss', 'grad_x', 'grad_ln_in_g', 'grad_ln_in_b', 'grad_w_in', 'grad_q_norm_g', 'grad_w_uq', 'grad_kv_norm_g', 'grad_w_ukv', 'grad_conv_w', 'grad_conv_b', 'grad_conv_ln_g', 'grad_conv_ln_b', 'grad_w_pool', 'grad_pool_scale', 'grad_w_out', 'grad_ln1_g', 'grad_ln1_b', 'grad_w_up', 'grad_ffn_conv_w', 'grad_ffn_conv_b', 'grad_w_down', 'grad_ln2_g', 'grad_ln2_b', 'delta_ln_in_g', 'delta_ln_in_b', 'delta_w_in', 'delta_q_norm_g', 'delta_w_uq', 'delta_kv_norm_g', 'delta_w_ukv', 'delta_conv_w', 'delta_conv_b', 'delta_conv_ln_g', 'delta_conv_ln_b', 'delta_w_pool', 'delta_pool_scale', 'delta_w_out', 'delta_ln1_g', 'delta_ln1_b', 'delta_w_up', 'delta_ffn_conv_w', 'delta_ffn_conv_b', 'delta_w_down', 'delta_ln2_g', 'delta_ln2_b', 'new_m_ln_in_g', 'new_m_ln_in_b', 'new_m_w_in', 'new_m_q_norm_g', 'new_m_w_uq', 'new_m_kv_norm_g', 'new_m_w_ukv', 'new_m_conv_w', 'new_m_conv_b', 'new_m_conv_ln_g', 'new_m_conv_ln_b', 'new_m_w_pool', 'new_m_pool_scale', 'new_m_w_out', 'new_m_ln1_g', 'new_m_ln1_b', 'new_m_w_up', 'new_m_ffn_conv_w', 'new_m_ffn_conv_b', 'new_m_w_down', 'new_m_ln2_g', 'new_m_ln2_b', 'new_v_ln_in_g', 'new_v_ln_in_b', 'new_v_w_in', 'new_v_q_norm_g', 'new_v_w_uq', 'new_v_kv_norm_g', 'new_v_w_ukv', 'new_v_conv_w', 'new_v_conv_b', 'new_v_conv_ln_g', 'new_v_conv_ln_b', 'new_v_w_pool', 'new_v_pool_scale', 'new_v_w_out', 'new_v_ln1_g', 'new_v_ln1_b', 'new_v_w_up', 'new_v_ffn_conv_w', 'new_v_ffn_conv_b', 'new_v_w_down', 'new_v_ln2_g', 'new_v_ln2_b']
TWIN_LEAF_KINDS = {'loss': 'loss', 'grad_x': 'grad_x', 'grad_ln_in_g': 'grad_w', 'grad_ln_in_b': 'grad_w', 'grad_w_in': 'grad_w', 'grad_q_norm_g': 'grad_w', 'grad_w_uq': 'grad_w', 'grad_kv_norm_g': 'grad_w', 'grad_w_ukv': 'grad_w', 'grad_conv_w': 'grad_w', 'grad_conv_b': 'grad_w', 'grad_conv_ln_g': 'grad_w', 'grad_conv_ln_b': 'grad_w', 'grad_w_pool': 'grad_w', 'grad_pool_scale': 'grad_w', 'grad_w_out': 'grad_w', 'grad_ln1_g': 'grad_w', 'grad_ln1_b': 'grad_w', 'grad_w_up': 'grad_w', 'grad_ffn_conv_w': 'grad_w', 'grad_ffn_conv_b': 'grad_w', 'grad_w_down': 'grad_w', 'grad_ln2_g': 'grad_w', 'grad_ln2_b': 'grad_w', 'delta_ln_in_g': 'delta_w', 'delta_ln_in_b': 'delta_w', 'delta_w_in': 'delta_w', 'delta_q_norm_g': 'delta_w', 'delta_w_uq': 'delta_w', 'delta_kv_norm_g': 'delta_w', 'delta_w_ukv': 'delta_w', 'delta_conv_w': 'delta_w', 'delta_conv_b': 'delta_w', 'delta_conv_ln_g': 'delta_w', 'delta_conv_ln_b': 'delta_w', 'delta_w_pool': 'delta_w', 'delta_pool_scale': 'delta_w', 'delta_w_out': 'delta_w', 'delta_ln1_g': 'delta_w', 'delta_ln1_b': 'delta_w', 'delta_w_up': 'delta_w', 'delta_ffn_conv_w': 'delta_w', 'delta_ffn_conv_b': 'delta_w', 'delta_w_down': 'delta_w', 'delta_ln2_g': 'delta_w', 'delta_ln2_b': 'delta_w', 'new_m_ln_in_g': 'new_m', 'new_m_ln_in_b': 'new_m', 'new_m_w_in': 'new_m', 'new_m_q_norm_g': 'new_m', 'new_m_w_uq': 'new_m', 'new_m_kv_norm_g': 'new_m', 'new_m_w_ukv': 'new_m', 'new_m_conv_w': 'new_m', 'new_m_conv_b': 'new_m', 'new_m_conv_ln_g': 'new_m', 'new_m_conv_ln_b': 'new_m', 'new_m_w_pool': 'new_m', 'new_m_pool_scale': 'new_m', 'new_m_w_out': 'new_m', 'new_m_ln1_g': 'new_m', 'new_m_ln1_b': 'new_m', 'new_m_w_up': 'new_m', 'new_m_ffn_conv_w': 'new_m', 'new_m_ffn_conv_b': 'new_m', 'new_m_w_down': 'new_m', 'new_m_ln2_g': 'new_m', 'new_m_ln2_b': 'new_m', 'new_v_ln_in_g': 'new_v', 'new_v_ln_in_b': 'new_v', 'new_v_w_in': 'new_v', 'new_v_q_norm_g': 'new_v', 'new_v_w_uq': 'new_v', 'new_v_kv_norm_g': 'new_v', 'new_v_w_ukv': 'new_v', 'new_v_conv_w': 'new_v', 'new_v_conv_b': 'new_v', 'new_v_conv_ln_g': 'new_v', 'new_v_conv_ln_b': 'new_v', 'new_v_w_pool': 'new_v', 'new_v_pool_scale': 'new_v', 'new_v_w_out': 'new_v', 'new_v_ln1_g': 'new_v', 'new_v_ln1_b': 'new_v', 'new_v_w_up': 'new_v', 'new_v_ffn_conv_w': 'new_v', 'new_v_ffn_conv_b': 'new_v', 'new_v_w_down': 'new_v', 'new_v_ln2_g': 'new_v', 'new_v_ln2_b': 'new_v'}


def _forward(args):
    return _fwd_reference(*[args[k] for k in FWD_PARAMS])


def _output_shape():
    out = _jax.eval_shape(lambda: _forward(_fwd_setup_inputs(0)))
    return out.shape, out.dtype

N_MICROBATCH = 1
ADAM_LR = 0.001
ADAM_B1 = 0.9
ADAM_B2 = 0.999
ADAM_EPS = 1e-08
ADAM_WD = 0.01
ADAM_STEP = 10
PER_EXAMPLE_BATCH_AXIS = {'x': 0, 'positions': 0, 'loss_target': 0}
SHARED_INPUTS = []
_WEIGHT_DTYPES = {'ln_in_g': _jnp.float32, 'ln_in_b': _jnp.float32, 'w_in': _jnp.float32, 'q_norm_g': _jnp.float32, 'w_uq': _jnp.float32, 'kv_norm_g': _jnp.float32, 'w_ukv': _jnp.float32, 'conv_w': _jnp.float32, 'conv_b': _jnp.float32, 'conv_ln_g': _jnp.float32, 'conv_ln_b': _jnp.float32, 'w_pool': _jnp.float32, 'pool_scale': _jnp.float32, 'w_out': _jnp.float32, 'ln1_g': _jnp.float32, 'ln1_b': _jnp.float32, 'w_up': _jnp.float32, 'ffn_conv_w': _jnp.float32, 'ffn_conv_b': _jnp.float32, 'w_down': _jnp.float32, 'ln2_g': _jnp.float32, 'ln2_b': _jnp.float32}
MOMENT_SCALE = {'ln_in_g': 4.576314e-01, 'ln_in_b': 2.510992e-01, 'w_in': 1.257013e-02, 'q_norm_g': 5.840641e-03, 'w_uq': 3.317509e-03, 'kv_norm_g': 1.310655e-02, 'w_ukv': 4.427770e-03, 'conv_w': 1.436006e-02, 'conv_b': 4.596134e-02, 'conv_ln_g': 2.312875e-02, 'conv_ln_b': 2.914700e-02, 'w_pool': 1.994831e-02, 'pool_scale': 2.049180e-02, 'w_out': 3.018897e-02, 'ln1_g': 5.002015e-01, 'ln1_b': 2.478306e-01, 'w_up': 8.297189e-03, 'ffn_conv_w': 8.310830e-03, 'ffn_conv_b': 9.133611e-03, 'w_down': 3.225271e-02, 'ln2_g': 8.038615e+00, 'ln2_b': 5.802103e-01}


def _to_microbatches(a, axis):
    t = _jnp.moveaxis(a, axis, 0)
    t = t.reshape((N_MICROBATCH, t.shape[0] // N_MICROBATCH) + t.shape[1:])
    return _jnp.moveaxis(t, 1, axis + 1)


def setup_inputs(seed: int = 0) -> dict:
    inp = _fwd_setup_inputs(seed)
    key = _jax.random.fold_in(_jax.random.key(seed), 7919)
    shape, _ = _output_shape()
    out = dict(inp)
    out["loss_target"] = _jax.random.normal(_jax.random.fold_in(key, 0), shape, _jnp.float32)
    for i, name in enumerate(TWIN_WEIGHTS):
        w = inp[name].astype(_jnp.float32)
        if MOMENT_SCALE is None:
            s = _jnp.sqrt(_jnp.mean(_jnp.square(w)) + 1e-30)
        else:
            s = MOMENT_SCALE[name]
        km, kv = _jax.random.split(_jax.random.fold_in(key, i + 1))
        out[name] = w
        out["m_" + name] = s * _jax.random.normal(km, w.shape, _jnp.float32)
        out["v_" + name] = (s * s) * _jax.random.uniform(kv, w.shape, _jnp.float32, 0.5, 1.5)
    if N_MICROBATCH > 1:
        for name, axis in PER_EXAMPLE_BATCH_AXIS.items():
            out[name] = _to_microbatches(out[name], axis)
    return {'x': out['x'], 'positions': out['positions'], 'ln_in_g': out['ln_in_g'], 'ln_in_b': out['ln_in_b'], 'w_in': out['w_in'], 'q_norm_g': out['q_norm_g'], 'w_uq': out['w_uq'], 'kv_norm_g': out['kv_norm_g'], 'w_ukv': out['w_ukv'], 'conv_w': out['conv_w'], 'conv_b': out['conv_b'], 'conv_ln_g': out['conv_ln_g'], 'conv_ln_b': out['conv_ln_b'], 'w_pool': out['w_pool'], 'pool_scale': out['pool_scale'], 'w_out': out['w_out'], 'ln1_g': out['ln1_g'], 'ln1_b': out['ln1_b'], 'w_up': out['w_up'], 'ffn_conv_w': out['ffn_conv_w'], 'ffn_conv_b': out['ffn_conv_b'], 'w_down': out['w_down'], 'ln2_g': out['ln2_g'], 'ln2_b': out['ln2_b'], 'loss_target': out['loss_target'], 'm_ln_in_g': out['m_ln_in_g'], 'm_ln_in_b': out['m_ln_in_b'], 'm_w_in': out['m_w_in'], 'm_q_norm_g': out['m_q_norm_g'], 'm_w_uq': out['m_w_uq'], 'm_kv_norm_g': out['m_kv_norm_g'], 'm_w_ukv': out['m_w_ukv'], 'm_conv_w': out['m_conv_w'], 'm_conv_b': out['m_conv_b'], 'm_conv_ln_g': out['m_conv_ln_g'], 'm_conv_ln_b': out['m_conv_ln_b'], 'm_w_pool': out['m_w_pool'], 'm_pool_scale': out['m_pool_scale'], 'm_w_out': out['m_w_out'], 'm_ln1_g': out['m_ln1_g'], 'm_ln1_b': out['m_ln1_b'], 'm_w_up': out['m_w_up'], 'm_ffn_conv_w': out['m_ffn_conv_w'], 'm_ffn_conv_b': out['m_ffn_conv_b'], 'm_w_down': out['m_w_down'], 'm_ln2_g': out['m_ln2_g'], 'm_ln2_b': out['m_ln2_b'], 'v_ln_in_g': out['v_ln_in_g'], 'v_ln_in_b': out['v_ln_in_b'], 'v_w_in': out['v_w_in'], 'v_q_norm_g': out['v_q_norm_g'], 'v_w_uq': out['v_w_uq'], 'v_kv_norm_g': out['v_kv_norm_g'], 'v_w_ukv': out['v_w_ukv'], 'v_conv_w': out['v_conv_w'], 'v_conv_b': out['v_conv_b'], 'v_conv_ln_g': out['v_conv_ln_g'], 'v_conv_ln_b': out['v_conv_ln_b'], 'v_w_pool': out['v_w_pool'], 'v_pool_scale': out['v_pool_scale'], 'v_w_out': out['v_w_out'], 'v_ln1_g': out['v_ln1_g'], 'v_ln1_b': out['v_ln1_b'], 'v_w_up': out['v_w_up'], 'v_ffn_conv_w': out['v_ffn_conv_w'], 'v_ffn_conv_b': out['v_ffn_conv_b'], 'v_w_down': out['v_w_down'], 'v_ln2_g': out['v_ln2_g'], 'v_ln2_b': out['v_ln2_b']}


def _loss(weights, diff, rest, loss_target):
    with _jax.named_scope("forward"):
        args = {**rest, TWIN_DIFF_INPUT: diff, **{k: w.astype(_WEIGHT_DTYPES[k]) for k, w in weights.items()}}
        y = _forward(args)
    with _jax.named_scope("loss_head"):
        err = _jnp.square(y.astype(_jnp.float32) - loss_target)
        return 0.5 * _jnp.sum(_jnp.mean(err, axis=-1)) if err.ndim else 0.5 * err


def _adamw(w, g, m, v):
    m = ADAM_B1 * m + (1.0 - ADAM_B1) * g
    v = ADAM_B2 * v + (1.0 - ADAM_B2) * _jnp.square(g)
    m_hat = m / (1.0 - ADAM_B1 ** ADAM_STEP)
    v_hat = v / (1.0 - ADAM_B2 ** ADAM_STEP)
    delta = -ADAM_LR * (m_hat / (_jnp.sqrt(v_hat) + ADAM_EPS) + ADAM_WD * w)
    return delta, m, v


def reference(x, positions, ln_in_g, ln_in_b, w_in, q_norm_g, w_uq, kv_norm_g, w_ukv, conv_w, conv_b, conv_ln_g, conv_ln_b, w_pool, pool_scale, w_out, ln1_g, ln1_b, w_up, ffn_conv_w, ffn_conv_b, w_down, ln2_g, ln2_b, loss_target, m_ln_in_g, m_ln_in_b, m_w_in, m_q_norm_g, m_w_uq, m_kv_norm_g, m_w_ukv, m_conv_w, m_conv_b, m_conv_ln_g, m_conv_ln_b, m_w_pool, m_pool_scale, m_w_out, m_ln1_g, m_ln1_b, m_w_up, m_ffn_conv_w, m_ffn_conv_b, m_w_down, m_ln2_g, m_ln2_b, v_ln_in_g, v_ln_in_b, v_w_in, v_q_norm_g, v_w_uq, v_kv_norm_g, v_w_ukv, v_conv_w, v_conv_b, v_conv_ln_g, v_conv_ln_b, v_w_pool, v_pool_scale, v_w_out, v_ln1_g, v_ln1_b, v_w_up, v_ffn_conv_w, v_ffn_conv_b, v_w_down, v_ln2_g, v_ln2_b):
    given = dict(x=x, positions=positions, ln_in_g=ln_in_g, ln_in_b=ln_in_b, w_in=w_in, q_norm_g=q_norm_g, w_uq=w_uq, kv_norm_g=kv_norm_g, w_ukv=w_ukv, conv_w=conv_w, conv_b=conv_b, conv_ln_g=conv_ln_g, conv_ln_b=conv_ln_b, w_pool=w_pool, pool_scale=pool_scale, w_out=w_out, ln1_g=ln1_g, ln1_b=ln1_b, w_up=w_up, ffn_conv_w=ffn_conv_w, ffn_conv_b=ffn_conv_b, w_down=w_down, ln2_g=ln2_g, ln2_b=ln2_b, loss_target=loss_target, m_ln_in_g=m_ln_in_g, m_ln_in_b=m_ln_in_b, m_w_in=m_w_in, m_q_norm_g=m_q_norm_g, m_w_uq=m_w_uq, m_kv_norm_g=m_kv_norm_g, m_w_ukv=m_w_ukv, m_conv_w=m_conv_w, m_conv_b=m_conv_b, m_conv_ln_g=m_conv_ln_g, m_conv_ln_b=m_conv_ln_b, m_w_pool=m_w_pool, m_pool_scale=m_pool_scale, m_w_out=m_w_out, m_ln1_g=m_ln1_g, m_ln1_b=m_ln1_b, m_w_up=m_w_up, m_ffn_conv_w=m_ffn_conv_w, m_ffn_conv_b=m_ffn_conv_b, m_w_down=m_w_down, m_ln2_g=m_ln2_g, m_ln2_b=m_ln2_b, v_ln_in_g=v_ln_in_g, v_ln_in_b=v_ln_in_b, v_w_in=v_w_in, v_q_norm_g=v_q_norm_g, v_w_uq=v_w_uq, v_kv_norm_g=v_kv_norm_g, v_w_ukv=v_w_ukv, v_conv_w=v_conv_w, v_conv_b=v_conv_b, v_conv_ln_g=v_conv_ln_g, v_conv_ln_b=v_conv_ln_b, v_w_pool=v_w_pool, v_pool_scale=v_pool_scale, v_w_out=v_w_out, v_ln1_g=v_ln1_g, v_ln1_b=v_ln1_b, v_w_up=v_w_up, v_ffn_conv_w=v_ffn_conv_w, v_ffn_conv_b=v_ffn_conv_b, v_w_down=v_w_down, v_ln2_g=v_ln2_g, v_ln2_b=v_ln2_b)
    weights = {n: given[n] for n in TWIN_WEIGHTS}
    shared = {n: given[n] for n in SHARED_INPUTS}
    per_example = {n: given[n] for n in ['x', 'positions']}
    grad_fn = _jax.value_and_grad(_loss, argnums=(0, 1))

    def one_microbatch(ex, loss_target):
        ex = dict(ex)
        diff = ex.pop(TWIN_DIFF_INPUT)
        return grad_fn(weights, diff, {**shared, **ex}, loss_target)

    if N_MICROBATCH == 1:
        loss, (grad_w, grad_x) = one_microbatch(per_example, given["loss_target"])
    else:
        def body(carry, xs):
            loss_sum, grad_sum = carry
            l_k, (gw_k, gx_k) = one_microbatch(xs[0], xs[1])
            with _jax.named_scope("update"):
                return (loss_sum + l_k, _jax.tree.map(_jnp.add, grad_sum, gw_k)), gx_k

        init = (_jnp.zeros((), _jnp.float32), _jax.tree.map(_jnp.zeros_like, weights))
        (loss, grad_w), grad_x = _jax.lax.scan(body, init, (per_example, given["loss_target"]))
    with _jax.named_scope("update"):
        delta_w, new_m, new_v = {}, {}, {}
        for n in TWIN_WEIGHTS:
            delta_w[n], new_m[n], new_v[n] = _adamw(weights[n], grad_w[n], given["m_" + n], given["v_" + n])
    return (loss, grad_x, *[grad_w[n] for n in TWIN_WEIGHTS], *[delta_w[n] for n in TWIN_WEIGHTS],
            *[new_m[n] for n in TWIN_WEIGHTS], *[new_v[n] for n in TWIN_WEIGHTS])
```

```python
import functools
import math

import jax
import jax.numpy as jnp
from jax import lax
from jax.experimental import pallas as pl
from jax.experimental.pallas import tpu as pltpu

f32 = jnp.float32
bf16 = jnp.bfloat16

QK_NOPE_DIM = 128
QK_ROPE_DIM = 64
V_HEAD_DIM = 128
CONV_KERNEL = 31
FFN_CONV_KERNEL = 3
POOL_WINDOWS = (2, 4, 8, 16)
ROPE_THETA = 10000.0
LN_EPS = 1e-5
RMS_EPS = 1e-6
ADAM_LR = 0.001
ADAM_B1 = 0.9
ADAM_B2 = 0.999
ADAM_EPS = 1e-08
ADAM_WD = 0.01
ADAM_STEP = 10

N_DEV = 8
MESH_AXES = ("x", "y", "c")
V7X_VMEM_LIMIT_BYTES = 56 * 1024 * 1024
LANES = 128
NEG_INF = -1e30
MESH = pl.DeviceIdType.MESH


def _cparams(sem):
    return pltpu.CompilerParams(dimension_semantics=sem, vmem_limit_bytes=V7X_VMEM_LIMIT_BYTES)


def _tile(dim, pref):
    t = pref
    while t >= LANES:
        if dim % t == 0:
            return t
        t //= 2
    return dim


def _shift_down_raw(x, k):
    if k == 0:
        return x
    row = lax.broadcasted_iota(jnp.int32, x.shape, 0)
    return jnp.where(row >= k, pltpu.roll(x, k, axis=0), 0.0)


def _shift_up_raw(x, k):
    if k == 0:
        return x
    n = x.shape[0]
    row = lax.broadcasted_iota(jnp.int32, x.shape, 0)
    return jnp.where(row < n - k, pltpu.roll(x, n - k, axis=0), 0.0)


@functools.partial(jax.custom_vjp, nondiff_argnums=(1,))
def _shift_down(x, k):
    return _shift_down_raw(x, k)


def _shift_down_fwd(x, k):
    return _shift_down_raw(x, k), None


def _shift_down_bwd(k, _, g):
    return (_shift_up_raw(g, k),)


_shift_down.defvjp(_shift_down_fwd, _shift_down_bwd)


@jax.custom_vjp
def _dup_halves(p):
    return p + pltpu.roll(p, LANES // 2, axis=1)


def _dup_halves_fwd(p):
    return p + pltpu.roll(p, LANES // 2, axis=1), None


def _dup_halves_bwd(_, g):
    return (g + pltpu.roll(g, LANES // 2, axis=1),)


_dup_halves.defvjp(_dup_halves_fwd, _dup_halves_bwd)

_NN = (((1,), (0,)), ((), ()))
_NT = (((1,), (1,)), ((), ()))
_TN = (((0,), (0,)), ((), ()))


def _dot(a, b, dims):
    return lax.dot_general(a.astype(bf16), b.astype(bf16), dims, preferred_element_type=f32)


@jax.custom_vjp
def _mm_bf16(a, b):
    return _dot(a, b, _NN)


def _mm_bf16_fwd(a, b):
    return _dot(a, b, _NN), (a, b)


def _mm_bf16_bwd(res, g):
    a, b = res
    return _dot(g, b, _NT), _dot(a, g, _TN)


_mm_bf16.defvjp(_mm_bf16_fwd, _mm_bf16_bwd)


def _layer_norm(z, g, b):
    mu = jnp.mean(z, axis=-1, keepdims=True)
    var = jnp.mean(jnp.square(z - mu), axis=-1, keepdims=True)
    return (z - mu) * lax.rsqrt(var + LN_EPS) * g + b


def _rms_norm(x, g):
    ms = jnp.mean(jnp.square(x), axis=-1, keepdims=True)
    return x * lax.rsqrt(ms + RMS_EPS) * g


def _colsum(x):
    return jnp.sum(x, axis=0, keepdims=True)


def _matmul(a, b, mode, out_dtype, name, tm=1024, tn=1024, tk=512):
    if mode == "nn":
        (m, k), (k2, n) = a.shape, b.shape
    elif mode == "nt":
        (m, k), (n, k2) = a.shape, b.shape
    else:
        (k, m), (k2, n) = a.shape, b.shape
    assert k == k2, (name, a.shape, b.shape)
    tm, tn, tk = _tile(m, tm), _tile(n, tn), _tile(k, tk)
    nk = k // tk
    dims = {"nn": _NN, "nt": _NT, "tn": _TN}[mode]
    a_spec = pl.BlockSpec((tk, tm), lambda i, j, kk: (kk, i)) if mode == "tn" else pl.BlockSpec((tm, tk), lambda i, j, kk: (i, kk))
    b_spec = pl.BlockSpec((tn, tk), lambda i, j, kk: (j, kk)) if mode == "nt" else pl.BlockSpec((tk, tn), lambda i, j, kk: (kk, j))

    def body(a_ref, b_ref, o_ref, acc_ref):
        kk = pl.program_id(2)

        @pl.when(kk == 0)
        def _():
            acc_ref[...] = jnp.zeros_like(acc_ref)

        acc_ref[...] += _dot(a_ref[...], b_ref[...], dims)

        @pl.when(kk == nk - 1)
        def _():
            o_ref[...] = acc_ref[...].astype(out_dtype)

    return pl.pallas_call(
        body, name=name, grid=(m // tm, n // tn, nk),
        in_specs=[a_spec, b_spec],
        out_specs=pl.BlockSpec((tm, tn), lambda i, j, kk: (i, j)),
        out_shape=jax.ShapeDtypeStruct((m, n), out_dtype),
        scratch_shapes=[pltpu.VMEM((tm, tn), f32)],
        compiler_params=_cparams(("parallel", "parallel", "arbitrary")),
    )(a, b)


ROW_TILE = 256


def _rows(width, col_block=0):
    return pl.BlockSpec((ROW_TILE, width), lambda i, cb=col_block: (i, cb))


def _whole(shape):
    return pl.BlockSpec(shape, lambda i: (0,) * len(shape))


def _ln_fwd(x, y, g, b, alpha, name):
    t, d = x.shape

    def body(*refs):
        if y is None:
            x_ref, g_ref, b_ref, o_ref, ob_ref = refs
            z = x_ref[...]
        else:
            x_ref, y_ref, g_ref, b_ref, o_ref, ob_ref = refs
            z = alpha * x_ref[...] + y_ref[...]
        out = _layer_norm(z, g_ref[...], b_ref[...])
        o_ref[...] = out
        ob_ref[...] = out.astype(bf16)

    ins = [x] + ([] if y is None else [y]) + [g, b]
    specs = [_rows(d)] + ([] if y is None else [_rows(d)]) + [_whole((1, d)), _whole((1, d))]
    return pl.pallas_call(
        body, name=name, grid=(t // ROW_TILE,), in_specs=specs,
        out_specs=[_rows(d), _rows(d)],
        out_shape=[jax.ShapeDtypeStruct((t, d), f32), jax.ShapeDtypeStruct((t, d), bf16)],
        compiler_params=_cparams(("parallel",)),
    )(*ins)


def _ln_bwd(d_res, d_mm, x, y, g, b, alpha, name):
    t, d = x.shape
    has_res, has_mm, has_y = d_res is not None, d_mm is not None, y is not None

    def body(*refs):
        refs = list(refs)
        d_res_ref = refs.pop(0) if has_res else None
        d_mm_ref = refs.pop(0) if has_mm else None
        x_ref = refs.pop(0)
        y_ref = refs.pop(0) if has_y else None
        g_ref, b_ref, dz_ref, dzb_ref, dg_ref, db_ref = refs
        ct = None
        if has_res:
            ct = alpha * d_res_ref[...]
        if has_mm:
            ct = d_mm_ref[...] if ct is None else ct + d_mm_ref[...]
        z = x_ref[...] if not has_y else alpha * x_ref[...] + y_ref[...]
        _, vjp = jax.vjp(_layer_norm, z, g_ref[...], b_ref[...])
        dz, dg, db = vjp(ct)
        dz_ref[...] = dz
        dzb_ref[...] = dz.astype(bf16)

        @pl.when(pl.program_id(0) == 0)
        def _():
            dg_ref[...] = jnp.zeros_like(dg_ref)
            db_ref[...] = jnp.zeros_like(db_ref)

        dg_ref[...] += dg
        db_ref[...] += db

    ins = [a for a in (d_res, d_mm, x, y) if a is not None] + [g, b]
    specs = [_rows(d) for a in (d_res, d_mm, x, y) if a is not None] + [_whole((1, d)), _whole((1, d))]
    return pl.pallas_call(
        body, name=name, grid=(t // ROW_TILE,), in_specs=specs,
        out_specs=[_rows(d), _rows(d), _whole((1, d)), _whole((1, d))],
        out_shape=[jax.ShapeDtypeStruct((t, d), f32), jax.ShapeDtypeStruct((t, d), bf16),
                   jax.ShapeDtypeStruct((1, d), f32), jax.ShapeDtypeStruct((1, d), f32)],
        compiler_params=_cparams(("arbitrary",)),
    )(*ins)


def _loss_call(xf, target):
    t, d = xf.shape

    def body(x_ref, t_ref, dx_ref, loss_ref):
        err = x_ref[...] - t_ref[...]
        dx_ref[...] = err * (1.0 / d)

        @pl.when(pl.program_id(0) == 0)
        def _():
            loss_ref[...] = jnp.zeros_like(loss_ref)

        part = 0.5 * jnp.sum(jnp.mean(jnp.square(err), axis=-1, keepdims=True), axis=0, keepdims=True)
        loss_ref[...] += jnp.broadcast_to(part, loss_ref.shape)

    return pl.pallas_call(
        body, name="loss_head", grid=(t // ROW_TILE,), in_specs=[_rows(d), _rows(d)],
        out_specs=[_rows(d), _whole((1, LANES))],
        out_shape=[jax.ShapeDtypeStruct((t, d), f32), jax.ShapeDtypeStruct((1, LANES), f32)],
        compiler_params=_cparams(("arbitrary",)),
    )(xf, target)


def _rope_tables(pos, inv, scale):
    t = pos.shape[0]

    def body(pos_ref, inv_ref, cs_ref, qt_ref):
        ang = pos_ref[...].astype(f32) * inv_ref[...]
        lane = lax.broadcasted_iota(jnp.int32, ang.shape, 1)
        cs = jnp.where(lane < LANES // 2, jnp.cos(ang), jnp.sin(ang))
        cs_ref[...] = cs
        qt_ref[:, :LANES] = jnp.full((ROW_TILE, LANES), scale, f32)
        qt_ref[:, LANES:] = scale * cs

    return pl.pallas_call(
        body, name="rope_tables", grid=(t // ROW_TILE,),
        in_specs=[pl.BlockSpec((ROW_TILE, 1), lambda i: (i, 0)), _whole((1, LANES))],
        out_specs=[_rows(LANES), _rows(2 * LANES)],
        out_shape=[jax.ShapeDtypeStruct((t, LANES), f32), jax.ShapeDtypeStruct((t, 2 * LANES), f32)],
        compiler_params=_cparams(("parallel",)),
    )(pos, inv)


def _prep_fn(cq, ckv, kr, cs, gq, gkv):
    return _rms_norm(cq, gq), _rms_norm(ckv, gkv), _dup_halves(kr * cs)


def _prep_fwd(h, cs, gq, gkv, lay):
    t = h.shape[0]
    ql, kvl = lay["ql"], lay["kvl"]

    def body(cq_ref, ckv_ref, kr_ref, cs_ref, gq_ref, gkv_ref, qn_ref, kvn_ref, krd_ref):
        qn, kvn, krd = _prep_fn(cq_ref[...], ckv_ref[...], kr_ref[...], cs_ref[...], gq_ref[...], gkv_ref[...])
        qn_ref[...] = qn.astype(bf16)
        kvn_ref[...] = kvn.astype(bf16)
        krd_ref[...] = krd.astype(bf16)

    return pl.pallas_call(
        body, name="prep_fwd", grid=(t // ROW_TILE,),
        in_specs=[_rows(ql, lay["off_q"] // ql), _rows(kvl, lay["off_kv"] // kvl), _rows(LANES, lay["off_kr"] // LANES),
                  _rows(LANES), _whole((1, ql)), _whole((1, kvl))],
        out_specs=[_rows(ql), _rows(kvl), _rows(LANES)],
        out_shape=[jax.ShapeDtypeStruct((t, ql), bf16), jax.ShapeDtypeStruct((t, kvl), bf16),
                   jax.ShapeDtypeStruct((t, LANES), bf16)],
        compiler_params=_cparams(("parallel",)),
    )(h, h, h, cs, gq, gkv)


def _prep_bwd(h, cs, gq, gkv, d_qn, d_kvn, d_krd, lay):
    t = h.shape[0]
    ql, kvl = lay["ql"], lay["kvl"]

    def body(cq_ref, ckv_ref, kr_ref, cs_ref, gq_ref, gkv_ref, dqn_ref, dkvn_ref, dkrd_ref,
             dcq_ref, dckv_ref, dkr_ref, dgq_ref, dgkv_ref):
        _, vjp = jax.vjp(_prep_fn, cq_ref[...], ckv_ref[...], kr_ref[...], cs_ref[...], gq_ref[...], gkv_ref[...])
        dcq, dckv, dkr, _, dgq, dgkv = vjp((dqn_ref[...], dkvn_ref[...], dkrd_ref[...].astype(f32)))
        dcq_ref[...] = dcq.astype(bf16)
        dckv_ref[...] = dckv.astype(bf16)
        dkr_ref[...] = dkr.astype(bf16)

        @pl.when(pl.program_id(0) == 0)
        def _():
            dgq_ref[...] = jnp.zeros_like(dgq_ref)
            dgkv_ref[...] = jnp.zeros_like(dgkv_ref)

        dgq_ref[...] += dgq
        dgkv_ref[...] += dgkv

    return pl.pallas_call(
        body, name="prep_bwd", grid=(t // ROW_TILE,),
        in_specs=[_rows(ql, lay["off_q"] // ql), _rows(kvl, lay["off_kv"] // kvl), _rows(LANES, lay["off_kr"] // LANES),
                  _rows(LANES), _whole((1, ql)), _whole((1, kvl)), _rows(ql), _rows(kvl), _rows(LANES)],
        out_specs=[_rows(ql), _rows(kvl), _rows(LANES), _whole((1, ql)), _whole((1, kvl))],
        out_shape=[jax.ShapeDtypeStruct((t, ql), bf16), jax.ShapeDtypeStruct((t, kvl), bf16),
                   jax.ShapeDtypeStruct((t, LANES), bf16), jax.ShapeDtypeStruct((1, ql), f32),
                   jax.ShapeDtypeStruct((1, kvl), f32)],
        compiler_params=_cparams(("arbitrary",)),
    )(h, h, h, cs, gq, gkv, d_qn, d_kvn, d_krd)


def _convln_fn(hc, g, b):
    y = _layer_norm(hc, g, b)
    return y * jax.nn.sigmoid(y)


def _convln_fwd(hconv, g, b):
    t, cw = hconv.shape

    def body(h_ref, g_ref, b_ref, o_ref):
        o_ref[...] = _convln_fn(h_ref[...], g_ref[...], b_ref[...]).astype(bf16)

    return pl.pallas_call(
        body, name="convln_fwd", grid=(t // ROW_TILE,),
        in_specs=[_rows(cw), _whole((1, cw)), _whole((1, cw))], out_specs=_rows(cw),
        out_shape=jax.ShapeDtypeStruct((t, cw), bf16), compiler_params=_cparams(("parallel",)),
    )(hconv, g, b)


def _convln_bwd(hconv, g, b, d_mixed, col_block):
    t, cw = hconv.shape

    def body(h_ref, g_ref, b_ref, dy_ref, dh_ref, dg_ref, db_ref):
        _, vjp = jax.vjp(_convln_fn, h_ref[...], g_ref[...], b_ref[...])
        dh, dg, db = vjp(dy_ref[...])
        dh_ref[...] = dh

        @pl.when(pl.program_id(0) == 0)
        def _():
            dg_ref[...] = jnp.zeros_like(dg_ref)
            db_ref[...] = jnp.zeros_like(db_ref)

        dg_ref[...] += dg
        db_ref[...] += db

    return pl.pallas_call(
        body, name="convln_bwd", grid=(t // ROW_TILE,),
        in_specs=[_rows(cw), _whole((1, cw)), _whole((1, cw)), _rows(cw, col_block)],
        out_specs=[_rows(cw), _whole((1, cw)), _whole((1, cw))],
        out_shape=[jax.ShapeDtypeStruct((t, cw), f32), jax.ShapeDtypeStruct((1, cw), f32),
                   jax.ShapeDtypeStruct((1, cw), f32)],
        compiler_params=_cparams(("arbitrary",)),
    )(hconv, g, b, d_mixed)


SEQ_CT = 128


def _conv_fwd(h, w, b, nb, seq, cw):
    ct, kk = SEQ_CT, w.shape[0]
    ncb = cw // ct

    def body(h_ref, w_ref, b_ref, o_ref):
        blk = h_ref[...]
        a, g = blk[:, :ct], blk[:, ct:]
        hc = a * jax.nn.sigmoid(g)
        acc = jnp.zeros_like(hc)
        for k in range(kk):
            acc = acc + w_ref[k:k + 1, :] * _shift_down_raw(hc, kk - 1 - k)
        o_ref[...] = acc + b_ref[...]

    return pl.pallas_call(
        body, name="conv_fwd", grid=(ncb, nb),
        in_specs=[pl.BlockSpec((seq, 2 * ct), lambda j, bb: (bb, j)), pl.BlockSpec((kk, ct), lambda j, bb: (0, j)),
                  pl.BlockSpec((1, ct), lambda j, bb: (0, j))],
        out_specs=pl.BlockSpec((seq, ct), lambda j, bb: (bb, j)),
        out_shape=jax.ShapeDtypeStruct((nb * seq, cw), f32),
        compiler_params=_cparams(("parallel", "parallel")),
    )(h, w, b)


def _conv_bwd(h, w, d_hconv, nb, seq, cw):
    ct, kk = SEQ_CT, w.shape[0]
    ncb = cw // ct

    def body(h_ref, w_ref, dy_ref, dh_ref, dw_ref, db_ref):
        blk = h_ref[...]
        a, g = blk[:, :ct], blk[:, ct:]
        sg = jax.nn.sigmoid(g)
        hc = a * sg
        dy = dy_ref[...]
        dhc = jnp.zeros_like(hc)

        @pl.when(pl.program_id(1) == 0)
        def _():
            dw_ref[...] = jnp.zeros_like(dw_ref)
            db_ref[...] = jnp.zeros_like(db_ref)

        for k in range(kk):
            dhc = dhc + w_ref[k:k + 1, :] * _shift_up_raw(dy, kk - 1 - k)
            dw_ref[k:k + 1, :] += _colsum(dy * _shift_down_raw(hc, kk - 1 - k))
        db_ref[...] += _colsum(dy)
        dh_ref[:, :ct] = (dhc * sg).astype(bf16)
        dh_ref[:, ct:] = (dhc * a * sg * (1.0 - sg)).astype(bf16)

    return pl.pallas_call(
        body, name="conv_bwd", grid=(ncb, nb),
        in_specs=[pl.BlockSpec((seq, 2 * ct), lambda j, bb: (bb, j)), pl.BlockSpec((kk, ct), lambda j, bb: (0, j)),
                  pl.BlockSpec((seq, ct), lambda j, bb: (bb, j))],
        out_specs=[pl.BlockSpec((seq, 2 * ct), lambda j, bb: (bb, j)), pl.BlockSpec((kk, ct), lambda j, bb: (0, j)),
                   pl.BlockSpec((1, ct), lambda j, bb: (0, j))],
        out_shape=[jax.ShapeDtypeStruct((nb * seq, 2 * cw), bf16), jax.ShapeDtypeStruct((kk, cw), f32),
                   jax.ShapeDtypeStruct((1, cw), f32)],
        compiler_params=_cparams(("parallel", "arbitrary")),
    )(h, w, d_hconv)


def _pool_fn(u, wp, scale, pg):
    seq = u.shape[0]
    t1 = (lax.broadcasted_iota(jnp.int32, (seq, 1), 0) + 1).astype(f32)
    outs = []
    for gi, win in enumerate(POOL_WINDOWS):
        ug = u[:, gi * pg:(gi + 1) * pg]
        acc, span = ug, 1
        while span < win:
            acc = acc + _shift_down(acc, span)
            span *= 2
        d = acc / jnp.minimum(t1, float(win)) - ug
        outs.append(_mm_bf16(d, wp[gi * pg:(gi + 1) * pg, :]) * scale[:, gi * pg:(gi + 1) * pg])
    return outs


def _pool_fwd(h, wp, scale, lay, nb, seq):
    pw, pg = lay["pw"], lay["pg"]

    def body(u_ref, wp_ref, sc_ref, o_ref):
        outs = _pool_fn(u_ref[...], wp_ref[...], sc_ref[...], pg)
        for gi in range(len(POOL_WINDOWS)):
            o_ref[:, gi * pg:(gi + 1) * pg] = outs[gi].astype(bf16)

    return pl.pallas_call(
        body, name="pool_fwd", grid=(nb,),
        in_specs=[pl.BlockSpec((seq, pw), lambda bb: (bb, lay["off_pool"] // pw)), _whole((pw, pg)), _whole((1, pw))],
        out_specs=pl.BlockSpec((seq, pw), lambda bb: (bb, 0)),
        out_shape=jax.ShapeDtypeStruct((nb * seq, pw), bf16),
        compiler_params=_cparams(("parallel",)),
    )(h, wp, scale)


def _pool_bwd(h, wp, scale, d_mixed, col_block, lay, nb, seq):
    pw, pg = lay["pw"], lay["pg"]
    ng = len(POOL_WINDOWS)

    def body(u_ref, wp_ref, sc_ref, dy_ref, du_ref, dwp_ref, dsc_ref):
        _, vjp = jax.vjp(functools.partial(_pool_fn, pg=pg), u_ref[...], wp_ref[...], sc_ref[...])
        dy = dy_ref[...]
        du, dwp, dsc = vjp([dy[:, gi * pg:(gi + 1) * pg] for gi in range(ng)])
        du_ref[...] = du.astype(bf16)

        @pl.when(pl.program_id(0) == 0)
        def _():
            dwp_ref[...] = jnp.zeros_like(dwp_ref)
            dsc_ref[...] = jnp.zeros_like(dsc_ref)

        dwp_ref[...] += dwp
        dsc_ref[...] += dsc

    return pl.pallas_call(
        body, name="pool_bwd", grid=(nb,),
        in_specs=[pl.BlockSpec((seq, pw), lambda bb: (bb, lay["off_pool"] // pw)), _whole((pw, pg)), _whole((1, pw)),
                  pl.BlockSpec((seq, pw), lambda bb: (bb, col_block))],
        out_specs=[pl.BlockSpec((seq, pw), lambda bb: (bb, 0)), _whole((pw, pg)), _whole((1, pw))],
        out_shape=[jax.ShapeDtypeStruct((nb * seq, pw), bf16), jax.ShapeDtypeStruct((pw, pg), f32),
                   jax.ShapeDtypeStruct((1, pw), f32)],
        compiler_params=_cparams(("arbitrary",)),
    )(h, wp, scale, d_mixed)


def _ffn_act_fwd(up, w, b, nb, seq, dff):
    ct, kk = SEQ_CT, w.shape[0]

    def body(u_ref, w_ref, b_ref, o_ref):
        blk = u_ref[...]
        c = jnp.zeros_like(blk)
        for k in range(kk):
            c = c + w_ref[k:k + 1, :] * _shift_down_raw(blk, kk - 1 - k)
        c = c + b_ref[...]
        a, g = c[:, :ct], c[:, ct:]
        o_ref[...] = (a * g * jax.nn.sigmoid(g)).astype(bf16)

    return pl.pallas_call(
        body, name="ffn_act_fwd", grid=(dff // ct, nb),
        in_specs=[pl.BlockSpec((seq, 2 * ct), lambda j, bb: (bb, j)), pl.BlockSpec((kk, 2 * ct), lambda j, bb: (0, j)),
                  pl.BlockSpec((1, 2 * ct), lambda j, bb: (0, j))],
        out_specs=pl.BlockSpec((seq, ct), lambda j, bb: (bb, j)),
        out_shape=jax.ShapeDtypeStruct((nb * seq, dff), bf16),
        compiler_params=_cparams(("parallel", "parallel")),
    )(up, w, b)


def _ffn_act_bwd(up, w, b, d_act, nb, seq, dff):
    ct, kk = SEQ_CT, w.shape[0]

    def body(u_ref, w_ref, b_ref, da_ref, du_ref, dw_ref, db_ref):
        blk = u_ref[...]
        c = jnp.zeros_like(blk)
        for k in range(kk):
            c = c + w_ref[k:k + 1, :] * _shift_down_raw(blk, kk - 1 - k)
        c = c + b_ref[...]
        a, g = c[:, :ct], c[:, ct:]
        sg = jax.nn.sigmoid(g)
        dact = da_ref[...]

        @pl.when(pl.program_id(1) == 0)
        def _():
            dw_ref[...] = jnp.zeros_like(dw_ref)
            db_ref[...] = jnp.zeros_like(db_ref)

        halves = ((0, dact * g * sg), (ct, dact * a * sg * (1.0 + g * (1.0 - sg))))
        for off, dc in halves:
            src = blk[:, off:off + ct]
            dsrc = jnp.zeros_like(dc)
            for k in range(kk):
                dsrc = dsrc + w_ref[k:k + 1, off:off + ct] * _shift_up_raw(dc, kk - 1 - k)
                dw_ref[k:k + 1, off:off + ct] += _colsum(dc * _shift_down_raw(src, kk - 1 - k))
            db_ref[:, off:off + ct] += _colsum(dc)
            du_ref[:, off:off + ct] = dsrc.astype(bf16)

    return pl.pallas_call(
        body, name="ffn_act_bwd", grid=(dff // ct, nb),
        in_specs=[pl.BlockSpec((seq, 2 * ct), lambda j, bb: (bb, j)), pl.BlockSpec((kk, 2 * ct), lambda j, bb: (0, j)),
                  pl.BlockSpec((1, 2 * ct), lambda j, bb: (0, j)), pl.BlockSpec((seq, ct), lambda j, bb: (bb, j))],
        out_specs=[pl.BlockSpec((seq, 2 * ct), lambda j, bb: (bb, j)), pl.BlockSpec((kk, 2 * ct), lambda j, bb: (0, j)),
                   pl.BlockSpec((1, 2 * ct), lambda j, bb: (0, j))],
        out_shape=[jax.ShapeDtypeStruct((nb * seq, 2 * dff), bf16), jax.ShapeDtypeStruct((kk, 2 * dff), f32),
                   jax.ShapeDtypeStruct((1, 2 * dff), f32)],
        compiler_params=_cparams(("parallel", "arbitrary")),
    )(up, w, b, d_act)


def _scores(q_ref, qt_ref, kn_ref, krd_ref, qi, tq):
    q = (q_ref[...] * qt_ref[...]).astype(bf16)
    s = _dot(q[:, :LANES], kn_ref[...], _NT) + _dot(q[:, LANES:], krd_ref[...], _NT)
    row = qi * tq + lax.broadcasted_iota(jnp.int32, s.shape, 0)
    col = lax.broadcasted_iota(jnp.int32, s.shape, 1)
    return q, jnp.where(col <= row, s, NEG_INF)


def _attn_fwd(q_ext, qt, kv, krd, nb, seq, heads):
    tq = _tile(seq, 512)
    nq = seq // tq

    def body(q_ref, qt_ref, kn_ref, krd_ref, v_ref, o_ref, lse_ref):
        _, s = _scores(q_ref, qt_ref, kn_ref, krd_ref, pl.program_id(2), tq)
        m = jnp.max(s, axis=-1, keepdims=True)
        p = jnp.exp(s - m)
        l = jnp.sum(p, axis=-1, keepdims=True)
        o_ref[...] = (_dot(p, v_ref[...], _NN) / l).astype(bf16)
        lse_ref[...] = m + jnp.log(l)

    return pl.pallas_call(
        body, name="attn_fwd", grid=(nb, heads, nq),
        in_specs=[pl.BlockSpec((tq, 2 * LANES), lambda b, h, i: (b * nq + i, h)),
                  pl.BlockSpec((tq, 2 * LANES), lambda b, h, i: (b * nq + i, 0)),
                  pl.BlockSpec((seq, LANES), lambda b, h, i: (b, 2 * h)),
                  pl.BlockSpec((seq, LANES), lambda b, h, i: (b, 0)),
                  pl.BlockSpec((seq, LANES), lambda b, h, i: (b, 2 * h + 1))],
        out_specs=[pl.BlockSpec((tq, LANES), lambda b, h, i: (b * nq + i, h)),
                   pl.BlockSpec((None, tq, 1), lambda b, h, i: (b * heads + h, i, 0))],
        out_shape=[jax.ShapeDtypeStruct((nb * seq, heads * LANES), bf16),
                   jax.ShapeDtypeStruct((nb * heads, seq, 1), f32)],
        compiler_params=_cparams(("parallel", "parallel", "parallel")),
    )(q_ext, qt, kv, krd, kv)


def _attn_bwd(q_ext, qt, kv, krd, lse, d_mixed, nb, seq, heads):
    tq = _tile(seq, 512)
    nq = seq // tq

    def body(q_ref, qt_ref, kn_ref, krd_ref, v_ref, lse_ref, do_ref, dq_ref, dkv_ref, dkrd_ref, dkv_acc):
        h, qi = pl.program_id(1), pl.program_id(2)
        q, s = _scores(q_ref, qt_ref, kn_ref, krd_ref, qi, tq)
        p = jnp.exp(s - lse_ref[...])
        do = do_ref[...]
        dp = _dot(do, v_ref[...], _NT)
        ds = p * (dp - jnp.sum(p * dp, axis=-1, keepdims=True))
        qt_blk = qt_ref[...]
        dq_ref[:, :LANES] = (_dot(ds, kn_ref[...], _NN) * qt_blk[:, :LANES]).astype(bf16)
        dq_ref[:, LANES:] = (_dot(ds, krd_ref[...], _NN) * qt_blk[:, LANES:]).astype(bf16)

        @pl.when(qi == 0)
        def _():
            dkv_acc[...] = jnp.zeros_like(dkv_acc)

        @pl.when((qi == 0) & (h == 0))
        def _():
            dkrd_ref[...] = jnp.zeros_like(dkrd_ref)

        dkv_acc[:, :LANES] += _dot(ds, q[:, :LANES], _TN)
        dkv_acc[:, LANES:] += _dot(p, do, _TN)
        dkrd_ref[...] += _dot(ds, q[:, LANES:], _TN)

        @pl.when(qi == nq - 1)
        def _():
            dkv_ref[...] = dkv_acc[...].astype(bf16)

    return pl.pallas_call(
        body, name="attn_bwd", grid=(nb, heads, nq),
        in_specs=[pl.BlockSpec((tq, 2 * LANES), lambda b, h, i: (b * nq + i, h)),
                  pl.BlockSpec((tq, 2 * LANES), lambda b, h, i: (b * nq + i, 0)),
                  pl.BlockSpec((seq, LANES), lambda b, h, i: (b, 2 * h)),
                  pl.BlockSpec((seq, LANES), lambda b, h, i: (b, 0)),
                  pl.BlockSpec((seq, LANES), lambda b, h, i: (b, 2 * h + 1)),
                  pl.BlockSpec((None, tq, 1), lambda b, h, i: (b * heads + h, i, 0)),
                  pl.BlockSpec((tq, LANES), lambda b, h, i: (b * nq + i, h))],
        out_specs=[pl.BlockSpec((tq, 2 * LANES), lambda b, h, i: (b * nq + i, h)),
                   pl.BlockSpec((seq, 2 * LANES), lambda b, h, i: (b, h)),
                   pl.BlockSpec((seq, LANES), lambda b, h, i: (b, 0))],
        out_shape=[jax.ShapeDtypeStruct((nb * seq, heads * 2 * LANES), bf16),
                   jax.ShapeDtypeStruct((nb * seq, heads * 2 * LANES), bf16),
                   jax.ShapeDtypeStruct((nb * seq, LANES), f32)],
        scratch_shapes=[pltpu.VMEM((seq, 2 * LANES), f32)],
        compiler_params=_cparams(("parallel", "arbitrary", "arbitrary")),
    )(q_ext, qt, kv, krd, kv, lse, d_mixed)


_ANY = pl.BlockSpec(memory_space=pl.ANY)


def _all_gather(shard, name):
    def body(x_ref, out_ref, send_sems, recv_sems, local_sem):
        x, y, c = lax.axis_index("x"), lax.axis_index("y"), lax.axis_index("c")
        me, sibling = (x, y, c), (x, y, 1 - c)
        chips = [(1 - x, y), (x, 1 - y), (1 - x, 1 - y)]

        def slot(px, py, pc):
            return out_ref.at[4 * px + 2 * py + pc]

        def copy(k, block, to, src=None):
            return pltpu.make_async_remote_copy(
                src_ref=slot(*block) if src is None else src, dst_ref=slot(*block),
                send_sem=send_sems.at[k], recv_sem=recv_sems.at[k], device_id=to, device_id_type=MESH)

        mine = pltpu.make_async_copy(x_ref, slot(*me), local_sem)
        mine.start()
        first = [copy(0, me, sibling, src=x_ref)]
        first += [copy(1 + j, me, (*chip, c), src=x_ref) for j, chip in enumerate(chips)]
        for cp in first:
            cp.start()
        passed = [copy(4 + j, (*chip, c), sibling) for j, chip in enumerate(chips)]
        for j, chip in enumerate(chips):
            copy(1 + j, (*chip, c), me).wait_recv()
            passed[j].start()
        copy(0, sibling, me).wait_recv()
        for j, chip in enumerate(chips):
            copy(4 + j, (*chip, 1 - c), me).wait_recv()
        for cp in first + passed:
            cp.wait_send()
        mine.wait()

    return pl.pallas_call(
        body, name=name, out_shape=jax.ShapeDtypeStruct((N_DEV,) + shard.shape, shard.dtype),
        in_specs=[_ANY], out_specs=_ANY,
        scratch_shapes=[pltpu.SemaphoreType.DMA((7,)), pltpu.SemaphoreType.DMA((7,)), pltpu.SemaphoreType.DMA(())],
    )(shard)


def _sibling_exchange(g2, name):
    def body(g_ref, out_ref, send_sem, recv_sem):
        x, y, c = lax.axis_index("x"), lax.axis_index("y"), lax.axis_index("c")
        cp = pltpu.make_async_remote_copy(src_ref=g_ref.at[1 - c], dst_ref=out_ref, send_sem=send_sem, recv_sem=recv_sem,
                                          device_id=(x, y, 1 - c), device_id_type=MESH)
        cp.start()
        cp.wait()

    return pl.pallas_call(
        body, name=name, out_shape=jax.ShapeDtypeStruct(g2.shape[1:], g2.dtype), in_specs=[_ANY], out_specs=_ANY,
        scratch_shapes=[pltpu.SemaphoreType.DMA(()), pltpu.SemaphoreType.DMA(())],
    )(g2)


def _chip_exchange(p4, name):
    def body(p_ref, out_ref, send_sems, recv_sems):
        x, y, c = lax.axis_index("x"), lax.axis_index("y"), lax.axis_index("c")
        chips = [(1 - x, y), (x, 1 - y), (1 - x, 1 - y)]
        cps = [pltpu.make_async_remote_copy(src_ref=p_ref.at[2 * px + py], dst_ref=out_ref.at[j], send_sem=send_sems.at[j],
                                            recv_sem=recv_sems.at[j], device_id=(px, py, c), device_id_type=MESH)
               for j, (px, py) in enumerate(chips)]
        for cp in cps:
            cp.start()
        for cp in cps:
            cp.wait()

    return pl.pallas_call(
        body, name=name, out_shape=jax.ShapeDtypeStruct((3,) + p4.shape[1:], p4.dtype), in_specs=[_ANY], out_specs=_ANY,
        scratch_shapes=[pltpu.SemaphoreType.DMA((3,)), pltpu.SemaphoreType.DMA((3,))],
    )(p4)


def _pair_add(mine, theirs, name):
    n, rows, cols = mine.shape
    tr = _tile(rows, 512) if rows % 8 == 0 else rows
    if rows % tr:
        tr = rows

    def body(a_ref, b_ref, o_ref):
        o_ref[...] = (a_ref[...].astype(f32) + b_ref[...].astype(f32)).astype(bf16)

    spec = pl.BlockSpec((None, tr, cols), lambda s, i: (s, i, 0))
    return pl.pallas_call(
        body, name=name, grid=(n, rows // tr), in_specs=[spec, spec], out_specs=spec,
        out_shape=jax.ShapeDtypeStruct(mine.shape, bf16), compiler_params=_cparams(("parallel", "parallel")),
    )(mine, theirs)


def _adamw(parts, w, m, v, name):
    rows, cols = w.shape
    tr = _tile(rows, 256) if rows % 8 == 0 else rows
    if rows % tr:
        tr = rows
    while tr * cols * 4 > 1024 * 1024 and tr % 16 == 0:
        tr //= 2
    nparts = len(parts)
    c1 = 1.0 / (1.0 - ADAM_B1 ** ADAM_STEP)
    c2 = 1.0 / (1.0 - ADAM_B2 ** ADAM_STEP)

    def body(*refs):
        part_refs, (w_ref, m_ref, v_ref, g_ref, d_ref, nm_ref, nv_ref) = refs[:nparts], refs[nparts:]
        g = None
        for pr in part_refs:
            for s in range(pr.shape[0]):
                term = pr[s].astype(f32)
                g = term if g is None else g + term
        nm = ADAM_B1 * m_ref[...] + (1.0 - ADAM_B1) * g
        nv = ADAM_B2 * v_ref[...] + (1.0 - ADAM_B2) * jnp.square(g)
        g_ref[...] = g
        nm_ref[...] = nm
        nv_ref[...] = nv
        d_ref[...] = -ADAM_LR * ((nm * c1) / (jnp.sqrt(nv * c2) + ADAM_EPS) + ADAM_WD * w_ref[...])

    spec = pl.BlockSpec((tr, cols), lambda i: (i, 0))
    part_specs = [pl.BlockSpec((p.shape[0], tr, cols), lambda i: (0, i, 0)) for p in parts]
    out = jax.ShapeDtypeStruct((rows, cols), f32)
    return pl.pallas_call(
        body, name=name, grid=(rows // tr,), in_specs=part_specs + [spec, spec, spec],
        out_specs=[spec, spec, spec, spec], out_shape=[out, out, out, out],
        compiler_params=_cparams(("parallel",)),
    )(*parts, w, m, v)


def _gathered_to_full(gathered, axis):
    s = gathered.shape[1:]
    full = jnp.moveaxis(gathered, 0, axis)
    return full.reshape(s[:axis] + (N_DEV * s[axis],) + s[axis + 1:])


def _full_to_slots(full, axis):
    s = full.shape
    parts = full.reshape(s[:axis] + (4, 2, s[axis] // N_DEV) + s[axis + 1:])
    parts = jnp.moveaxis(parts, (axis, axis + 1), (1, 0))
    return parts


def _interleave_halves(w, ct):
    n = w.shape[-1] // 2
    t = w.reshape(w.shape[:-1] + (2, n // ct, ct))
    return jnp.swapaxes(t, -3, -2).reshape(w.shape)


def _deinterleave_halves(w, ct):
    n = w.shape[-1] // 2
    t = w.reshape(w.shape[:-1] + (n // ct, 2, ct))
    return jnp.swapaxes(t, -3, -2).reshape(w.shape)


def _rot_cols(w):
    half = QK_ROPE_DIM // 2
    return jnp.concatenate([-w[..., half:], w[..., :half]], axis=-1)


def _rot_cols_t(dw):
    half = QK_ROPE_DIM // 2
    return jnp.concatenate([dw[..., half:], -dw[..., :half]], axis=-1)


def kernel(x, positions, ln_in_g, ln_in_b, w_in, q_norm_g, w_uq, kv_norm_g, w_ukv, conv_w, conv_b, conv_ln_g, conv_ln_b, w_pool, pool_scale, w_out, ln1_g, ln1_b, w_up, ffn_conv_w, ffn_conv_b, w_down, ln2_g, ln2_b, loss_target, m_ln_in_g, m_ln_in_b, m_w_in, m_q_norm_g, m_w_uq, m_kv_norm_g, m_w_ukv, m_conv_w, m_conv_b, m_conv_ln_g, m_conv_ln_b, m_w_pool, m_pool_scale, m_w_out, m_ln1_g, m_ln1_b, m_w_up, m_ffn_conv_w, m_ffn_conv_b, m_w_down, m_ln2_g, m_ln2_b, v_ln_in_g, v_ln_in_b, v_w_in, v_q_norm_g, v_w_uq, v_kv_norm_g, v_w_ukv, v_conv_w, v_conv_b, v_conv_ln_g, v_conv_ln_b, v_w_pool, v_pool_scale, v_w_out, v_ln1_g, v_ln1_b, v_w_up, v_ffn_conv_w, v_ffn_conv_b, v_w_down, v_ln2_g, v_ln2_b):
    weights = dict(ln_in_g=ln_in_g, ln_in_b=ln_in_b, w_in=w_in, q_norm_g=q_norm_g, w_uq=w_uq, kv_norm_g=kv_norm_g,
                   w_ukv=w_ukv, conv_w=conv_w, conv_b=conv_b, conv_ln_g=conv_ln_g, conv_ln_b=conv_ln_b, w_pool=w_pool,
                   pool_scale=pool_scale, w_out=w_out, ln1_g=ln1_g, ln1_b=ln1_b, w_up=w_up, ffn_conv_w=ffn_conv_w,
                   ffn_conv_b=ffn_conv_b, w_down=w_down, ln2_g=ln2_g, ln2_b=ln2_b)
    mom1 = dict(ln_in_g=m_ln_in_g, ln_in_b=m_ln_in_b, w_in=m_w_in, q_norm_g=m_q_norm_g, w_uq=m_w_uq,
                kv_norm_g=m_kv_norm_g, w_ukv=m_w_ukv, conv_w=m_conv_w, conv_b=m_conv_b, conv_ln_g=m_conv_ln_g,
                conv_ln_b=m_conv_ln_b, w_pool=m_w_pool, pool_scale=m_pool_scale, w_out=m_w_out, ln1_g=m_ln1_g,
                ln1_b=m_ln1_b, w_up=m_w_up, ffn_conv_w=m_ffn_conv_w, ffn_conv_b=m_ffn_conv_b, w_down=m_w_down,
                ln2_g=m_ln2_g, ln2_b=m_ln2_b)
    mom2 = dict(ln_in_g=v_ln_in_g, ln_in_b=v_ln_in_b, w_in=v_w_in, q_norm_g=v_q_norm_g, w_uq=v_w_uq,
                kv_norm_g=v_kv_norm_g, w_ukv=v_w_ukv, conv_w=v_conv_w, conv_b=v_conv_b, conv_ln_g=v_conv_ln_g,
                conv_ln_b=v_conv_ln_b, w_pool=v_w_pool, pool_scale=v_pool_scale, w_out=v_w_out, ln1_g=v_ln1_g,
                ln1_b=v_ln1_b, w_up=v_w_up, ffn_conv_w=v_ffn_conv_w, ffn_conv_b=v_ffn_conv_b, w_down=v_w_down,
                ln2_g=v_ln2_g, ln2_b=v_ln2_b)
    names = list(weights)

    nb, seq, d = x.shape
    t = nb * seq
    depth = w_in.shape[0]
    ql, kvl, cw, pw = q_norm_g.shape[1], kv_norm_g.shape[1], conv_b.shape[1], pool_scale.shape[1]
    pg = w_pool.shape[-1]
    heads = w_uq.shape[2]
    dff = w_down.shape[1] * N_DEV
    alpha = (2.0 * depth) ** 0.25
    scale = float(QK_NOPE_DIM + QK_ROPE_DIM) ** -0.5
    lay = dict(ql=ql, kvl=kvl, pw=pw, pg=pg, off_q=2 * cw, off_pool=2 * cw + ql, off_kv=2 * cw + ql + pw,
               off_kr=2 * cw + ql + pw + kvl)
    o1, o2, o3, o4 = ql, ql + kvl, ql + kvl + QK_ROPE_DIM, ql + kvl + QK_ROPE_DIM + 2 * cw
    my_x, my_y, my_c = lax.axis_index("x"), lax.axis_index("y"), lax.axis_index("c")
    my_dev = 4 * my_x + 2 * my_y + my_c

    shard_axis = dict(w_in=2, w_uq=1, w_ukv=1, w_out=1, w_up=2, w_down=1, conv_w=2, ffn_conv_w=2)
    full = {}
    for n in ("w_in", "w_uq", "w_ukv", "w_out", "w_up", "w_down"):
        full[n] = _gathered_to_full(_all_gather(weights[n].astype(bf16), "ag_" + n), shard_axis[n])
    for n in ("conv_w", "ffn_conv_w"):
        full[n] = _gathered_to_full(_all_gather(weights[n], "ag_" + n), shard_axis[n])

    wi = full["w_in"]
    kr_cols = wi[..., o2:o3]
    w_in_pad = jnp.concatenate([_interleave_halves(wi[..., o3:o4], SEQ_CT), wi[..., :o1], wi[..., o4:], wi[..., o1:o2],
                                kr_cols, _rot_cols(kr_cols)], axis=-1)
    wq = full["w_uq"]
    w_uq_ext = jnp.concatenate([wq, _rot_cols(wq[..., QK_NOPE_DIM:])], axis=-1).reshape(depth, ql, heads * 2 * LANES)
    w_ukv_f = full["w_ukv"].reshape(depth, kvl, heads * 2 * LANES)
    w_up_t = _interleave_halves(full["w_up"], SEQ_CT)
    ffn_w_t = _interleave_halves(full["ffn_conv_w"], SEQ_CT)
    ffn_b_t = _interleave_halves(ffn_conv_b, SEQ_CT)
    w_pool_2d = w_pool.reshape(depth, pw, pg)

    half = QK_ROPE_DIM // 2
    inv = 1.0 / (ROPE_THETA ** (jnp.arange(0, QK_ROPE_DIM, 2, dtype=f32) / QK_ROPE_DIM))
    inv_lanes = jnp.tile(inv, LANES // half).reshape(1, LANES)
    cs, qt = _rope_tables(positions.reshape(t, 1), inv_lanes, scale)

    x2 = x.reshape(t, d)
    xs, xs_bf = _ln_fwd(x2, None, ln_in_g.reshape(1, d), ln_in_b.reshape(1, d), 1.0, "ln_in_fwd")
    saved = []
    for l in range(depth):
        gq, gkv = q_norm_g[l].reshape(1, ql), kv_norm_g[l].reshape(1, kvl)
        h = _matmul(xs_bf, w_in_pad[l], "nn", f32, "mm_in", tm=512, tn=w_in_pad.shape[-1])
        qn, kvn, krd = _prep_fwd(h, cs, gq, gkv, lay)
        q_ext = _matmul(qn, w_uq_ext[l], "nn", f32, "mm_uq")
        kv = _matmul(kvn, w_ukv_f[l], "nn", bf16, "mm_ukv")
        y_mla, lse = _attn_fwd(q_ext, qt, kv, krd, nb, seq, heads)
        hconv = _conv_fwd(h, full["conv_w"][l], conv_b[l].reshape(1, cw), nb, seq, cw)
        y_conv = _convln_fwd(hconv, conv_ln_g[l].reshape(1, cw), conv_ln_b[l].reshape(1, cw))
        y_pool = _pool_fwd(h, w_pool_2d[l], pool_scale[l].reshape(1, pw), lay, nb, seq)
        mixed = jnp.concatenate([y_mla, y_conv, y_pool], axis=-1)
        y1 = _matmul(mixed, full["w_out"][l], "nn", f32, "mm_out")
        x1, x1_bf = _ln_fwd(xs, y1, ln1_g[l].reshape(1, d), ln1_b[l].reshape(1, d), alpha, "ln1_fwd")
        up = _matmul(x1_bf, w_up_t[l], "nn", f32, "mm_up")
        act = _ffn_act_fwd(up, ffn_w_t[l], ffn_b_t[l].reshape(1, 2 * dff), nb, seq, dff)
        y2 = _matmul(act, full["w_down"][l], "nn", f32, "mm_down")
        xn, xn_bf = _ln_fwd(x1, y2, ln2_g[l].reshape(1, d), ln2_b[l].reshape(1, d), alpha, "ln2_fwd")
        saved.append(dict(xs=xs, xs_bf=xs_bf, h=h, qn=qn, kvn=kvn, krd=krd, q_ext=q_ext, kv=kv, lse=lse, hconv=hconv,
                          mixed=mixed, y1=y1, x1=x1, x1_bf=x1_bf, up=up, act=act, y2=y2))
        xs, xs_bf = xn, xn_bf

    d_stream, loss_row = _loss_call(xs, loss_target.reshape(t, d))
    loss = lax.psum(loss_row[0, 0], MESH_AXES)

    gw = {n: [None] * depth for n in names if n not in ("ln_in_g", "ln_in_b")}
    d_res, d_mm = None, d_stream
    for l in reversed(range(depth)):
        sv = saved[l]
        gq, gkv = q_norm_g[l].reshape(1, ql), kv_norm_g[l].reshape(1, kvl)
        dz2, dz2_bf, gw["ln2_g"][l], gw["ln2_b"][l] = _ln_bwd(
            d_res, d_mm, sv["x1"], sv["y2"], ln2_g[l].reshape(1, d), ln2_b[l].reshape(1, d), alpha, "ln2_bwd")
        d_act = _matmul(dz2_bf, full["w_down"][l], "nt", f32, "mm_down_dx", tn=512)
        gw["w_down"][l] = _matmul(sv["act"], dz2_bf, "tn", bf16, "mm_down_dw", tm=512, tk=512)
        d_up, dffw, dffb = _ffn_act_bwd(sv["up"], ffn_w_t[l], ffn_b_t[l].reshape(1, 2 * dff), d_act, nb, seq, dff)
        gw["ffn_conv_w"][l] = _deinterleave_halves(dffw, SEQ_CT)
        gw["ffn_conv_b"][l] = _deinterleave_halves(dffb, SEQ_CT).reshape(2 * dff)
        d_x1 = _matmul(d_up, w_up_t[l], "nt", f32, "mm_up_dx")
        gw["w_up"][l] = _deinterleave_halves(_matmul(sv["x1_bf"], d_up, "tn", bf16, "mm_up_dw"), SEQ_CT)
        dz1, dz1_bf, gw["ln1_g"][l], gw["ln1_b"][l] = _ln_bwd(
            dz2, d_x1, sv["xs"], sv["y1"], ln1_g[l].reshape(1, d), ln1_b[l].reshape(1, d), alpha, "ln1_bwd")
        d_mixed = _matmul(dz1_bf, full["w_out"][l], "nt", f32, "mm_out_dx")
        gw["w_out"][l] = _matmul(sv["mixed"], dz1_bf, "tn", bf16, "mm_out_dw")
        d_upool, dwp, dps = _pool_bwd(sv["h"], w_pool_2d[l], pool_scale[l].reshape(1, pw), d_mixed,
                                      (heads * LANES + cw) // pw, lay, nb, seq)
        gw["w_pool"][l] = dwp.reshape(w_pool.shape[1:])
        gw["pool_scale"][l] = dps.reshape(pw)
        d_hconv, dclg, dclb = _convln_bwd(sv["hconv"], conv_ln_g[l].reshape(1, cw), conv_ln_b[l].reshape(1, cw), d_mixed,
                                          heads * LANES // cw)
        gw["conv_ln_g"][l], gw["conv_ln_b"][l] = dclg.reshape(cw), dclb.reshape(cw)
        d_conv, gw["conv_w"][l], dcb = _conv_bwd(sv["h"], full["conv_w"][l], d_hconv, nb, seq, cw)
        gw["conv_b"][l] = dcb.reshape(cw)
        dq_ext, dkv, dkrd = _attn_bwd(sv["q_ext"], qt, sv["kv"], sv["krd"], sv["lse"], d_mixed, nb, seq, heads)
        d_qn = _matmul(dq_ext, w_uq_ext[l], "nt", f32, "mm_uq_dx")
        dwq = _matmul(sv["qn"], dq_ext, "tn", f32, "mm_uq_dw").reshape(ql, heads, 2 * LANES)
        dwq_rope = dwq[..., QK_NOPE_DIM:QK_NOPE_DIM + QK_ROPE_DIM] + _rot_cols_t(dwq[..., QK_NOPE_DIM + QK_ROPE_DIM:])
        gw["w_uq"][l] = jnp.concatenate([dwq[..., :QK_NOPE_DIM], dwq_rope], axis=-1).astype(bf16)
        d_kvn = _matmul(dkv, w_ukv_f[l], "nt", f32, "mm_ukv_dx")
        gw["w_ukv"][l] = _matmul(sv["kvn"], dkv, "tn", bf16, "mm_ukv_dw").reshape(kvl, heads, 2 * LANES)
        d_cq, d_ckv, d_kr, dgq, dgkv = _prep_bwd(sv["h"], cs, gq, gkv, d_qn, d_kvn, dkrd, lay)
        gw["q_norm_g"][l], gw["kv_norm_g"][l] = dgq.reshape(ql), dgkv.reshape(kvl)
        d_h = jnp.concatenate([d_conv, d_cq, d_upool, d_ckv, d_kr], axis=-1)
        d_xs = _matmul(d_h, w_in_pad[l], "nt", f32, "mm_in_dx", tm=512, tk=d_h.shape[-1])
        dwi = _matmul(sv["xs_bf"], d_h, "tn", f32, "mm_in_dw", tm=512, tn=d_h.shape[-1])
        dkr_cols = dwi[:, lay["off_kr"]:lay["off_kr"] + QK_ROPE_DIM] + _rot_cols_t(dwi[:, lay["off_kr"] + QK_ROPE_DIM:])
        gw["w_in"][l] = jnp.concatenate(
            [dwi[:, lay["off_q"]:lay["off_q"] + ql], dwi[:, lay["off_kv"]:lay["off_kv"] + kvl], dkr_cols,
             _deinterleave_halves(dwi[:, :2 * cw], SEQ_CT), dwi[:, lay["off_pool"]:lay["off_pool"] + pw]],
            axis=-1).astype(bf16)
        d_res, d_mm = dz1, d_xs

    grad_x, _, d_ln_in_g, d_ln_in_b = _ln_bwd(d_res, d_mm, x2, None, ln_in_g.reshape(1, d), ln_in_b.reshape(1, d), alpha,
                                               "ln_in_bwd")
    grad_x = grad_x.reshape(x.shape)

    grads, deltas, new_m, new_v = {}, {}, {}, {}

    def finish(n, parts):
        shp = weights[n].shape
        rows = math.prod(shp[:-1]) if len(shp) > 1 else 1
        as2d = lambda a: a.reshape(rows, shp[-1])
        parts = [p.reshape(p.shape[0], rows, shp[-1]) for p in parts]
        g, dl, nm, nv = _adamw(parts, as2d(weights[n]), as2d(mom1[n]), as2d(mom2[n]), "adamw_" + n)
        grads[n], deltas[n], new_m[n], new_v[n] = (a.reshape(shp) for a in (g, dl, nm, nv))

    for n in ("w_in", "w_uq", "w_ukv", "w_out", "w_up", "w_down"):
        slots = _full_to_slots(jnp.stack(gw[n]), shard_axis[n])
        theirs = _sibling_exchange(slots, "rs1_" + n)
        shard_shape = slots.shape[2:]
        rows = math.prod(shard_shape[:-1])
        mine = lax.dynamic_index_in_dim(slots, my_c, axis=0, keepdims=False)
        p4 = _pair_add(mine.reshape(4, rows, shard_shape[-1]), theirs.reshape(4, rows, shard_shape[-1]), "rs_add_" + n)
        p4 = p4.reshape((4,) + shard_shape)
        from_chips = _chip_exchange(p4, "rs2_" + n)
        own = lax.dynamic_index_in_dim(p4, 2 * my_x + my_y, axis=0, keepdims=True)
        finish(n, [own, from_chips])

    small = [n for n in names if n not in ("w_in", "w_uq", "w_ukv", "w_out", "w_up", "w_down")]
    partial = {"ln_in_g": d_ln_in_g.reshape(d), "ln_in_b": d_ln_in_b.reshape(d)}
    for n in small:
        if n not in partial:
            partial[n] = jnp.stack(gw[n])
    flat = jnp.concatenate([partial[n].astype(f32).reshape(-1) for n in small])
    gathered = _all_gather(flat.reshape(-1, LANES), "ag_small_grads").reshape(N_DEV, -1)
    off = 0
    for n in small:
        size = math.prod(partial[n].shape)
        part = gathered[:, off:off + size].reshape((N_DEV,) + partial[n].shape)
        off += size
        if n in ("conv_w", "ffn_conv_w"):
            width = weights[n].shape[-1]
            part = lax.dynamic_slice_in_dim(part, my_dev * width, width, axis=part.ndim - 1)
        finish(n, [part])

    return (loss, grad_x, *[grads[n] for n in names], *[deltas[n] for n in names], *[new_m[n] for n in names],
            *[new_v[n] for n in names])
```

```python
import functools
import math

import jax
import jax.numpy as jnp
from jax import lax
from jax.experimental import pallas as pl
from jax.experimental.pallas import tpu as pltpu

f32 = jnp.float32
bf16 = jnp.bfloat16

QK_NOPE_DIM = 128
QK_ROPE_DIM = 64
V_HEAD_DIM = 128
CONV_KERNEL = 31
FFN_CONV_KERNEL = 3
POOL_WINDOWS = (2, 4, 8, 16)
ROPE_THETA = 10000.0
LN_EPS = 1e-5
RMS_EPS = 1e-6
ADAM_LR = 0.001
ADAM_B1 = 0.9
ADAM_B2 = 0.999
ADAM_EPS = 1e-08
ADAM_WD = 0.01
ADAM_STEP = 10

N_DEV = 8
MESH_AXES = ("x", "y", "c")
V7X_VMEM_LIMIT_BYTES = 56 * 1024 * 1024
LANES = 128
NEG_INF = -1e30
MESH = pl.DeviceIdType.MESH


def _cparams(sem):
    return pltpu.CompilerParams(dimension_semantics=sem, vmem_limit_bytes=V7X_VMEM_LIMIT_BYTES)


def _tile(dim, pref):
    t = pref
    while t >= LANES:
        if dim % t == 0:
            return t
        t //= 2
    return dim


def _shift_down_raw(x, k):
    if k == 0:
        return x
    row = lax.broadcasted_iota(jnp.int32, x.shape, 0)
    return jnp.where(row >= k, pltpu.roll(x, k, axis=0), 0.0)


def _shift_up_raw(x, k):
    if k == 0:
        return x
    n = x.shape[0]
    row = lax.broadcasted_iota(jnp.int32, x.shape, 0)
    return jnp.where(row < n - k, pltpu.roll(x, n - k, axis=0), 0.0)


@functools.partial(jax.custom_vjp, nondiff_argnums=(1,))
def _shift_down(x, k):
    return _shift_down_raw(x, k)


def _shift_down_fwd(x, k):
    return _shift_down_raw(x, k), None


def _shift_down_bwd(k, _, g):
    return (_shift_up_raw(g, k),)


_shift_down.defvjp(_shift_down_fwd, _shift_down_bwd)


@jax.custom_vjp
def _dup_halves(p):
    return p + pltpu.roll(p, LANES // 2, axis=1)


def _dup_halves_fwd(p):
    return p + pltpu.roll(p, LANES // 2, axis=1), None


def _dup_halves_bwd(_, g):
    return (g + pltpu.roll(g, LANES // 2, axis=1),)


_dup_halves.defvjp(_dup_halves_fwd, _dup_halves_bwd)

_NN = (((1,), (0,)), ((), ()))
_NT = (((1,), (1,)), ((), ()))
_TN = (((0,), (0,)), ((), ()))


def _dot(a, b, dims):
    return lax.dot_general(a.astype(bf16), b.astype(bf16), dims, preferred_element_type=f32)


@jax.custom_vjp
def _mm_bf16(a, b):
    return _dot(a, b, _NN)


def _mm_bf16_fwd(a, b):
    return _dot(a, b, _NN), (a, b)


def _mm_bf16_bwd(res, g):
    a, b = res
    return _dot(g, b, _NT), _dot(a, g, _TN)


_mm_bf16.defvjp(_mm_bf16_fwd, _mm_bf16_bwd)


def _layer_norm(z, g, b):
    mu = jnp.mean(z, axis=-1, keepdims=True)
    var = jnp.mean(jnp.square(z - mu), axis=-1, keepdims=True)
    return (z - mu) * lax.rsqrt(var + LN_EPS) * g + b


def _rms_norm(x, g):
    ms = jnp.mean(jnp.square(x), axis=-1, keepdims=True)
    return x * lax.rsqrt(ms + RMS_EPS) * g


def _colsum(x):
    return jnp.sum(x, axis=0, keepdims=True)


def _matmul_core(a, b, mode, grid, a_spec, b_spec, o_spec, o_shape, tile, out_dtype, name):
    nk = grid[2]
    dims = {"nn": _NN, "nt": _NT, "tn": _TN}[mode]
    acc_in_out = out_dtype == f32

    def body(a_ref, b_ref, o_ref, *scratch):
        def prod():
            return _dot(a_ref[...], b_ref[...], dims)

        if nk == 1:
            o_ref[...] = prod().astype(out_dtype)
            return
        acc_ref = o_ref if acc_in_out else scratch[0]
        kk = pl.program_id(2)

        @pl.when(kk == 0)
        def _():
            acc_ref[...] = prod()

        if acc_in_out:
            @pl.when(kk > 0)
            def _():
                acc_ref[...] += prod()
        else:
            @pl.when((kk > 0) & (kk < nk - 1))
            def _():
                acc_ref[...] += prod()

            @pl.when(kk == nk - 1)
            def _():
                o_ref[...] = (acc_ref[...] + prod()).astype(out_dtype)

    scratch = [] if (nk == 1 or acc_in_out) else [pltpu.VMEM(tile, f32)]
    return pl.pallas_call(
        body, name=name, grid=grid, in_specs=[a_spec, b_spec], out_specs=o_spec,
        out_shape=jax.ShapeDtypeStruct(o_shape, out_dtype), scratch_shapes=scratch,
        compiler_params=_cparams(("parallel", "parallel", "arbitrary")),
    )(a, b)


def _matmul(a, b, mode, out_dtype, name, tm=1024, tn=1024, tk=2048):
    if mode == "nn":
        (m, k), (k2, n) = a.shape, b.shape
    elif mode == "nt":
        (m, k), (n, k2) = a.shape, b.shape
    else:
        (k, m), (k2, n) = a.shape, b.shape
    assert k == k2, (name, a.shape, b.shape)
    tm, tn, tk = _tile(m, tm), _tile(n, tn), _tile(k, tk)
    a_spec = pl.BlockSpec((tk, tm), lambda i, j, kk: (kk, i)) if mode == "tn" else pl.BlockSpec((tm, tk), lambda i, j, kk: (i, kk))
    b_spec = pl.BlockSpec((tn, tk), lambda i, j, kk: (j, kk)) if mode == "nt" else pl.BlockSpec((tk, tn), lambda i, j, kk: (kk, j))
    return _matmul_core(a, b, mode, (m // tm, n // tn, k // tk), a_spec, b_spec,
                        pl.BlockSpec((tm, tn), lambda i, j, kk: (i, j)), (m, n), (tm, tn), out_dtype, name)


def _matmul_up(x_bf, w_slots, name, tm=1024):
    m, k = x_bf.shape
    s, _, ns = w_slots.shape
    tm = _tile(m, tm)
    return _matmul_core(x_bf, w_slots, "nn", (m // tm, s, 1), pl.BlockSpec((tm, k), lambda i, j, kk: (i, 0)),
                        pl.BlockSpec((None, k, ns), lambda i, j, kk: (j, 0, 0)),
                        pl.BlockSpec((tm, ns), lambda i, j, kk: (i, j)), (m, s * ns), (tm, ns), f32, name)


def _matmul_up_dx(d3, w_slots, name, tm=1024, tn=1024):
    _, m, half = d3.shape
    s, n, ns = w_slots.shape
    per_half = half // ns
    assert 2 * per_half == s, (d3.shape, w_slots.shape)
    tm, tn = _tile(m, tm), _tile(n, tn)
    return _matmul_core(d3, w_slots, "nt", (m // tm, n // tn, s),
                        pl.BlockSpec((None, tm, ns), lambda i, j, kk: (kk // per_half, i, kk % per_half)),
                        pl.BlockSpec((None, tn, ns), lambda i, j, kk: (kk, j, 0)),
                        pl.BlockSpec((tm, tn), lambda i, j, kk: (i, j)), (m, n), (tm, tn), f32, name)


def _matmul_up_dw(x_bf, d3, n_slots, name, tm=1024, tk=2048):
    k, m = x_bf.shape
    _, _, half = d3.shape
    ns = 2 * half // n_slots
    per_half = n_slots // 2
    tm, tk = _tile(m, tm), _tile(k, tk)
    return _matmul_core(x_bf, d3, "tn", (m // tm, n_slots, k // tk), pl.BlockSpec((tk, tm), lambda i, j, kk: (kk, i)),
                        pl.BlockSpec((None, tk, ns), lambda i, j, kk: (j // per_half, kk, j % per_half)),
                        pl.BlockSpec((None, tm, ns), lambda i, j, kk: (j, i, 0)), (n_slots, m, ns), (tm, ns), bf16, name)


ROW_TILE = 256


def _rows(width, col_block=0):
    return pl.BlockSpec((ROW_TILE, width), lambda i, cb=col_block: (i, cb))


def _whole(shape):
    return pl.BlockSpec(shape, lambda i: (0,) * len(shape))


def _ln_fwd(x, y, g, b, alpha, name):
    t, d = x.shape

    def body(*refs):
        if y is None:
            x_ref, g_ref, b_ref, o_ref, ob_ref = refs
            z = x_ref[...]
        else:
            x_ref, y_ref, g_ref, b_ref, o_ref, ob_ref = refs
            z = alpha * x_ref[...] + y_ref[...]
        out = _layer_norm(z, g_ref[...], b_ref[...])
        o_ref[...] = out
        ob_ref[...] = out.astype(bf16)

    ins = [x] + ([] if y is None else [y]) + [g, b]
    specs = [_rows(d)] + ([] if y is None else [_rows(d)]) + [_whole((1, d)), _whole((1, d))]
    return pl.pallas_call(
        body, name=name, grid=(t // ROW_TILE,), in_specs=specs,
        out_specs=[_rows(d), _rows(d)],
        out_shape=[jax.ShapeDtypeStruct((t, d), f32), jax.ShapeDtypeStruct((t, d), bf16)],
        compiler_params=_cparams(("parallel",)),
    )(*ins)


def _ln_bwd(d_res, d_mm, x, y, g, b, alpha, name):
    t, d = x.shape
    has_res, has_mm, has_y = d_res is not None, d_mm is not None, y is not None

    def body(*refs):
        refs = list(refs)
        d_res_ref = refs.pop(0) if has_res else None
        d_mm_ref = refs.pop(0) if has_mm else None
        x_ref = refs.pop(0)
        y_ref = refs.pop(0) if has_y else None
        g_ref, b_ref, dz_ref, dzb_ref, dg_ref, db_ref = refs
        ct = None
        if has_res:
            ct = alpha * d_res_ref[...]
        if has_mm:
            ct = d_mm_ref[...] if ct is None else ct + d_mm_ref[...]
        z = x_ref[...] if not has_y else alpha * x_ref[...] + y_ref[...]
        _, vjp = jax.vjp(_layer_norm, z, g_ref[...], b_ref[...])
        dz, dg, db = vjp(ct)
        dz_ref[...] = dz
        dzb_ref[...] = dz.astype(bf16)

        @pl.when(pl.program_id(0) == 0)
        def _():
            dg_ref[...] = jnp.zeros_like(dg_ref)
            db_ref[...] = jnp.zeros_like(db_ref)

        dg_ref[...] += dg
        db_ref[...] += db

    ins = [a for a in (d_res, d_mm, x, y) if a is not None] + [g, b]
    specs = [_rows(d) for a in (d_res, d_mm, x, y) if a is not None] + [_whole((1, d)), _whole((1, d))]
    return pl.pallas_call(
        body, name=name, grid=(t // ROW_TILE,), in_specs=specs,
        out_specs=[_rows(d), _rows(d), _whole((1, d)), _whole((1, d))],
        out_shape=[jax.ShapeDtypeStruct((t, d), f32), jax.ShapeDtypeStruct((t, d), bf16),
                   jax.ShapeDtypeStruct((1, d), f32), jax.ShapeDtypeStruct((1, d), f32)],
        compiler_params=_cparams(("arbitrary",)),
    )(*ins)


def _loss_call(xf, target):
    t, d = xf.shape

    def body(x_ref, t_ref, dx_ref, loss_ref):
        err = x_ref[...] - t_ref[...]
        dx_ref[...] = err * (1.0 / d)

        @pl.when(pl.program_id(0) == 0)
        def _():
            loss_ref[...] = jnp.zeros_like(loss_ref)

        part = 0.5 * jnp.sum(jnp.mean(jnp.square(err), axis=-1, keepdims=True), axis=0, keepdims=True)
        loss_ref[...] += jnp.broadcast_to(part, loss_ref.shape)

    return pl.pallas_call(
        body, name="loss_head", grid=(t // ROW_TILE,), in_specs=[_rows(d), _rows(d)],
        out_specs=[_rows(d), _whole((1, LANES))],
        out_shape=[jax.ShapeDtypeStruct((t, d), f32), jax.ShapeDtypeStruct((1, LANES), f32)],
        compiler_params=_cparams(("arbitrary",)),
    )(xf, target)


def _rope_tables(pos, inv, scale):
    t = pos.shape[0]

    def body(pos_ref, inv_ref, cs_ref, qt_ref):
        ang = pos_ref[...].astype(f32) * inv_ref[...]
        lane = lax.broadcasted_iota(jnp.int32, ang.shape, 1)
        cs = jnp.where(lane < LANES // 2, jnp.cos(ang), jnp.sin(ang))
        cs_ref[...] = cs
        qt_ref[:, :LANES] = jnp.full((ROW_TILE, LANES), scale, f32)
        qt_ref[:, LANES:] = scale * cs

    return pl.pallas_call(
        body, name="rope_tables", grid=(t // ROW_TILE,),
        in_specs=[pl.BlockSpec((ROW_TILE, 1), lambda i: (i, 0)), _whole((1, LANES))],
        out_specs=[_rows(LANES), _rows(2 * LANES)],
        out_shape=[jax.ShapeDtypeStruct((t, LANES), f32), jax.ShapeDtypeStruct((t, 2 * LANES), f32)],
        compiler_params=_cparams(("parallel",)),
    )(pos, inv)


def _prep_fn(cq, ckv, kr, cs, gq, gkv):
    return _rms_norm(cq, gq), _rms_norm(ckv, gkv), _dup_halves(kr * cs)


def _prep_fwd(h, cs, gq, gkv, lay):
    t = h.shape[0]
    ql, kvl = lay["ql"], lay["kvl"]

    def body(cq_ref, ckv_ref, kr_ref, cs_ref, gq_ref, gkv_ref, qn_ref, kvn_ref, krd_ref):
        qn, kvn, krd = _prep_fn(cq_ref[...], ckv_ref[...], kr_ref[...], cs_ref[...], gq_ref[...], gkv_ref[...])
        qn_ref[...] = qn.astype(bf16)
        kvn_ref[...] = kvn.astype(bf16)
        krd_ref[...] = krd.astype(bf16)

    return pl.pallas_call(
        body, name="prep_fwd", grid=(t // ROW_TILE,),
        in_specs=[_rows(ql, lay["off_q"] // ql), _rows(kvl, lay["off_kv"] // kvl), _rows(LANES, lay["off_kr"] // LANES),
                  _rows(LANES), _whole((1, ql)), _whole((1, kvl))],
        out_specs=[_rows(ql), _rows(kvl), _rows(LANES)],
        out_shape=[jax.ShapeDtypeStruct((t, ql), bf16), jax.ShapeDtypeStruct((t, kvl), bf16),
                   jax.ShapeDtypeStruct((t, LANES), bf16)],
        compiler_params=_cparams(("parallel",)),
    )(h, h, h, cs, gq, gkv)


def _prep_bwd(h, cs, gq, gkv, d_qn, d_kvn, d_krd, lay):
    t = h.shape[0]
    ql, kvl = lay["ql"], lay["kvl"]

    def body(cq_ref, ckv_ref, kr_ref, cs_ref, gq_ref, gkv_ref, dqn_ref, dkvn_ref, dkrd_ref,
             dcq_ref, dckv_ref, dkr_ref, dgq_ref, dgkv_ref):
        _, vjp = jax.vjp(_prep_fn, cq_ref[...], ckv_ref[...], kr_ref[...], cs_ref[...], gq_ref[...], gkv_ref[...])
        dcq, dckv, dkr, _, dgq, dgkv = vjp((dqn_ref[...], dkvn_ref[...], dkrd_ref[...].astype(f32)))
        dcq_ref[...] = dcq.astype(bf16)
        dckv_ref[...] = dckv.astype(bf16)
        dkr_ref[...] = dkr.astype(bf16)

        @pl.when(pl.program_id(0) == 0)
        def _():
            dgq_ref[...] = jnp.zeros_like(dgq_ref)
            dgkv_ref[...] = jnp.zeros_like(dgkv_ref)

        dgq_ref[...] += dgq
        dgkv_ref[...] += dgkv

    return pl.pallas_call(
        body, name="prep_bwd", grid=(t // ROW_TILE,),
        in_specs=[_rows(ql, lay["off_q"] // ql), _rows(kvl, lay["off_kv"] // kvl), _rows(LANES, lay["off_kr"] // LANES),
                  _rows(LANES), _whole((1, ql)), _whole((1, kvl)), _rows(ql), _rows(kvl), _rows(LANES)],
        out_specs=[_rows(ql), _rows(kvl), _rows(LANES), _whole((1, ql)), _whole((1, kvl))],
        out_shape=[jax.ShapeDtypeStruct((t, ql), bf16), jax.ShapeDtypeStruct((t, kvl), bf16),
                   jax.ShapeDtypeStruct((t, LANES), bf16), jax.ShapeDtypeStruct((1, ql), f32),
                   jax.ShapeDtypeStruct((1, kvl), f32)],
        compiler_params=_cparams(("arbitrary",)),
    )(h, h, h, cs, gq, gkv, d_qn, d_kvn, d_krd)


def _convln_fn(hc, g, b):
    y = _layer_norm(hc, g, b)
    return y * jax.nn.sigmoid(y)


def _convln_fwd(hconv, g, b):
    t, cw = hconv.shape

    def body(h_ref, g_ref, b_ref, o_ref):
        o_ref[...] = _convln_fn(h_ref[...], g_ref[...], b_ref[...]).astype(bf16)

    return pl.pallas_call(
        body, name="convln_fwd", grid=(t // ROW_TILE,),
        in_specs=[_rows(cw), _whole((1, cw)), _whole((1, cw))], out_specs=_rows(cw),
        out_shape=jax.ShapeDtypeStruct((t, cw), bf16), compiler_params=_cparams(("parallel",)),
    )(hconv, g, b)


def _convln_bwd(hconv, g, b, d_mixed, col_block):
    t, cw = hconv.shape

    def body(h_ref, g_ref, b_ref, dy_ref, dh_ref, dg_ref, db_ref):
        _, vjp = jax.vjp(_convln_fn, h_ref[...], g_ref[...], b_ref[...])
        dh, dg, db = vjp(dy_ref[...])
        dh_ref[...] = dh

        @pl.when(pl.program_id(0) == 0)
        def _():
            dg_ref[...] = jnp.zeros_like(dg_ref)
            db_ref[...] = jnp.zeros_like(db_ref)

        dg_ref[...] += dg
        db_ref[...] += db

    return pl.pallas_call(
        body, name="convln_bwd", grid=(t // ROW_TILE,),
        in_specs=[_rows(cw), _whole((1, cw)), _whole((1, cw)), _rows(cw, col_block)],
        out_specs=[_rows(cw), _whole((1, cw)), _whole((1, cw))],
        out_shape=[jax.ShapeDtypeStruct((t, cw), f32), jax.ShapeDtypeStruct((1, cw), f32),
                   jax.ShapeDtypeStruct((1, cw), f32)],
        compiler_params=_cparams(("arbitrary",)),
    )(hconv, g, b, d_mixed)


SEQ_CT = 128


def _conv_fwd(h, w, b, nb, seq, cw):
    ct, kk = SEQ_CT, w.shape[0]
    ncb = cw // ct

    def body(h_ref, w_ref, b_ref, o_ref):
        blk = h_ref[...]
        a, g = blk[:, :ct], blk[:, ct:]
        hc = a * jax.nn.sigmoid(g)
        acc = jnp.zeros_like(hc)
        for k in range(kk):
            acc = acc + w_ref[k:k + 1, :] * _shift_down_raw(hc, kk - 1 - k)
        o_ref[...] = acc + b_ref[...]

    return pl.pallas_call(
        body, name="conv_fwd", grid=(ncb, nb),
        in_specs=[pl.BlockSpec((seq, 2 * ct), lambda j, bb: (bb, j)), pl.BlockSpec((kk, ct), lambda j, bb: (0, j)),
                  pl.BlockSpec((1, ct), lambda j, bb: (0, j))],
        out_specs=pl.BlockSpec((seq, ct), lambda j, bb: (bb, j)),
        out_shape=jax.ShapeDtypeStruct((nb * seq, cw), f32),
        compiler_params=_cparams(("parallel", "parallel")),
    )(h, w, b)


def _conv_bwd(h, w, d_hconv, nb, seq, cw):
    ct, kk = SEQ_CT, w.shape[0]
    ncb = cw // ct

    def body(h_ref, w_ref, dy_ref, dh_ref, dw_ref, db_ref):
        blk = h_ref[...]
        a, g = blk[:, :ct], blk[:, ct:]
        sg = jax.nn.sigmoid(g)
        hc = a * sg
        dy = dy_ref[...]
        dhc = jnp.zeros_like(hc)

        @pl.when(pl.program_id(1) == 0)
        def _():
            dw_ref[...] = jnp.zeros_like(dw_ref)
            db_ref[...] = jnp.zeros_like(db_ref)

        for k in range(kk):
            dhc = dhc + w_ref[k:k + 1, :] * _shift_up_raw(dy, kk - 1 - k)
            dw_ref[k:k + 1, :] += _colsum(dy * _shift_down_raw(hc, kk - 1 - k))
        db_ref[...] += _colsum(dy)
        dh_ref[:, :ct] = (dhc * sg).astype(bf16)
        dh_ref[:, ct:] = (dhc * a * sg * (1.0 - sg)).astype(bf16)

    return pl.pallas_call(
        body, name="conv_bwd", grid=(ncb, nb),
        in_specs=[pl.BlockSpec((seq, 2 * ct), lambda j, bb: (bb, j)), pl.BlockSpec((kk, ct), lambda j, bb: (0, j)),
                  pl.BlockSpec((seq, ct), lambda j, bb: (bb, j))],
        out_specs=[pl.BlockSpec((seq, 2 * ct), lambda j, bb: (bb, j)), pl.BlockSpec((kk, ct), lambda j, bb: (0, j)),
                   pl.BlockSpec((1, ct), lambda j, bb: (0, j))],
        out_shape=[jax.ShapeDtypeStruct((nb * seq, 2 * cw), bf16), jax.ShapeDtypeStruct((kk, cw), f32),
                   jax.ShapeDtypeStruct((1, cw), f32)],
        compiler_params=_cparams(("parallel", "arbitrary")),
    )(h, w, d_hconv)


def _pool_fn(u, wp, scale, pg):
    seq = u.shape[0]
    t1 = (lax.broadcasted_iota(jnp.int32, (seq, 1), 0) + 1).astype(f32)
    outs = []
    for gi, win in enumerate(POOL_WINDOWS):
        ug = u[:, gi * pg:(gi + 1) * pg]
        acc, span = ug, 1
        while span < win:
            acc = acc + _shift_down(acc, span)
            span *= 2
        d = acc / jnp.minimum(t1, float(win)) - ug
        outs.append(_mm_bf16(d, wp[gi * pg:(gi + 1) * pg, :]) * scale[:, gi * pg:(gi + 1) * pg])
    return outs


def _pool_fwd(h, wp, scale, lay, nb, seq):
    pw, pg = lay["pw"], lay["pg"]

    def body(u_ref, wp_ref, sc_ref, o_ref):
        outs = _pool_fn(u_ref[...], wp_ref[...], sc_ref[...], pg)
        for gi in range(len(POOL_WINDOWS)):
            o_ref[:, gi * pg:(gi + 1) * pg] = outs[gi].astype(bf16)

    return pl.pallas_call(
        body, name="pool_fwd", grid=(nb,),
        in_specs=[pl.BlockSpec((seq, pw), lambda bb: (bb, lay["off_pool"] // pw)), _whole((pw, pg)), _whole((1, pw))],
        out_specs=pl.BlockSpec((seq, pw), lambda bb: (bb, 0)),
        out_shape=jax.ShapeDtypeStruct((nb * seq, pw), bf16),
        compiler_params=_cparams(("parallel",)),
    )(h, wp, scale)


def _pool_bwd(h, wp, scale, d_mixed, col_block, lay, nb, seq):
    pw, pg = lay["pw"], lay["pg"]
    ng = len(POOL_WINDOWS)

    def body(u_ref, wp_ref, sc_ref, dy_ref, du_ref, dwp_ref, dsc_ref):
        _, vjp = jax.vjp(functools.partial(_pool_fn, pg=pg), u_ref[...], wp_ref[...], sc_ref[...])
        dy = dy_ref[...]
        du, dwp, dsc = vjp([dy[:, gi * pg:(gi + 1) * pg] for gi in range(ng)])
        du_ref[...] = du.astype(bf16)

        @pl.when(pl.program_id(0) == 0)
        def _():
            dwp_ref[...] = jnp.zeros_like(dwp_ref)
            dsc_ref[...] = jnp.zeros_like(dsc_ref)

        dwp_ref[...] += dwp
        dsc_ref[...] += dsc

    return pl.pallas_call(
        body, name="pool_bwd", grid=(nb,),
        in_specs=[pl.BlockSpec((seq, pw), lambda bb: (bb, lay["off_pool"] // pw)), _whole((pw, pg)), _whole((1, pw)),
                  pl.BlockSpec((seq, pw), lambda bb: (bb, col_block))],
        out_specs=[pl.BlockSpec((seq, pw), lambda bb: (bb, 0)), _whole((pw, pg)), _whole((1, pw))],
        out_shape=[jax.ShapeDtypeStruct((nb * seq, pw), bf16), jax.ShapeDtypeStruct((pw, pg), f32),
                   jax.ShapeDtypeStruct((1, pw), f32)],
        compiler_params=_cparams(("arbitrary",)),
    )(h, wp, scale, d_mixed)


def _ffn_conv(x, w_ref, b_ref):
    kk = w_ref.shape[0]
    c = b_ref[...] + w_ref[kk - 1:kk, :] * x
    for k in range(kk - 1):
        c = c + w_ref[k:k + 1, :] * _shift_down_raw(x, kk - 1 - k)
    return c


def _ffn_specs(seq, ct, kk, nct):
    return [pl.BlockSpec((seq, ct), lambda j, bb: (bb, j)), pl.BlockSpec((seq, ct), lambda j, bb: (bb, nct + j)),
            pl.BlockSpec((kk, ct), lambda j, bb: (0, j)), pl.BlockSpec((kk, ct), lambda j, bb: (0, nct + j)),
            pl.BlockSpec((1, ct), lambda j, bb: (0, j)), pl.BlockSpec((1, ct), lambda j, bb: (0, nct + j))]


def _ffn_act_fwd(up, w, b, nb, seq, dff):
    ct, kk = SEQ_CT, w.shape[0]

    def body(ua_ref, ug_ref, wa_ref, wg_ref, ba_ref, bg_ref, o_ref):
        a = _ffn_conv(ua_ref[...], wa_ref, ba_ref)
        g = _ffn_conv(ug_ref[...], wg_ref, bg_ref)
        o_ref[...] = (a * g * jax.nn.sigmoid(g)).astype(bf16)

    return pl.pallas_call(
        body, name="ffn_act_fwd", grid=(dff // ct, nb), in_specs=_ffn_specs(seq, ct, kk, dff // ct),
        out_specs=pl.BlockSpec((seq, ct), lambda j, bb: (bb, j)),
        out_shape=jax.ShapeDtypeStruct((nb * seq, dff), bf16),
        compiler_params=_cparams(("parallel", "parallel")),
    )(up, up, w, w, b, b)


def _ffn_act_bwd(up, w, b, d_act, nb, seq, dff):
    ct, kk = SEQ_CT, w.shape[0]

    def body(ua_ref, ug_ref, wa_ref, wg_ref, ba_ref, bg_ref, da_ref, du_ref, dw_ref, db_ref):
        ua, ug = ua_ref[...], ug_ref[...]
        a = _ffn_conv(ua, wa_ref, ba_ref)
        g = _ffn_conv(ug, wg_ref, bg_ref)
        sg = jax.nn.sigmoid(g)
        dact = da_ref[...]

        @pl.when(pl.program_id(1) == 0)
        def _():
            dw_ref[...] = jnp.zeros_like(dw_ref)
            db_ref[...] = jnp.zeros_like(db_ref)

        halves = ((0, ua, wa_ref, dact * g * sg), (1, ug, wg_ref, dact * a * sg * (1.0 + g * (1.0 - sg))))
        for hf, src, w_ref, dc in halves:
            dsrc = w_ref[kk - 1:kk, :] * dc
            dw_ref[hf, kk - 1:kk, :] += _colsum(dc * src)
            for k in range(kk - 1):
                dsrc = dsrc + w_ref[k:k + 1, :] * _shift_up_raw(dc, kk - 1 - k)
                dw_ref[hf, k:k + 1, :] += _colsum(dc * _shift_down_raw(src, kk - 1 - k))
            db_ref[hf] += _colsum(dc)
            du_ref[hf] = dsrc.astype(bf16)

    return pl.pallas_call(
        body, name="ffn_act_bwd", grid=(dff // ct, nb),
        in_specs=_ffn_specs(seq, ct, kk, dff // ct) + [pl.BlockSpec((seq, ct), lambda j, bb: (bb, j))],
        out_specs=[pl.BlockSpec((2, seq, ct), lambda j, bb: (0, bb, j)), pl.BlockSpec((2, kk, ct), lambda j, bb: (0, 0, j)),
                   pl.BlockSpec((2, 1, ct), lambda j, bb: (0, 0, j))],
        out_shape=[jax.ShapeDtypeStruct((2, nb * seq, dff), bf16), jax.ShapeDtypeStruct((2, kk, dff), f32),
                   jax.ShapeDtypeStruct((2, 1, dff), f32)],
        compiler_params=_cparams(("parallel", "arbitrary")),
    )(up, up, w, w, b, b, d_act)


def _scores(q_ref, qt_ref, kn_ref, krd_ref, qblk, tq):
    klen = (qblk + 1) * tq
    q = (q_ref[...] * qt_ref[...]).astype(bf16)
    s = _dot(q[:, :LANES], kn_ref[0:klen, :], _NT) + _dot(q[:, LANES:], krd_ref[0:klen, :], _NT)
    row = qblk * tq + lax.broadcasted_iota(jnp.int32, s.shape, 0)
    col = lax.broadcasted_iota(jnp.int32, s.shape, 1)
    return q, jnp.where(col <= row, s, NEG_INF)


def _per_q_block(nq, fn):
    qi = pl.program_id(2)
    for qblk in range(nq):
        pl.when(qi == qblk)(functools.partial(fn, qblk))


def _attn_fwd(q_ext, qt, kv, krd, nb, seq, heads):
    tq = _tile(seq, 512)
    nq = seq // tq

    def body(q_ref, qt_ref, kn_ref, krd_ref, v_ref, o_ref, lse_ref):
        def work(qblk):
            klen = (qblk + 1) * tq
            _, s = _scores(q_ref, qt_ref, kn_ref, krd_ref, qblk, tq)
            m = jnp.max(s, axis=-1, keepdims=True)
            p = jnp.exp(s - m)
            l = jnp.sum(p, axis=-1, keepdims=True)
            o_ref[...] = (_dot(p, v_ref[0:klen, :], _NN) / l).astype(bf16)
            lse_ref[...] = m + jnp.log(l)

        _per_q_block(nq, work)

    return pl.pallas_call(
        body, name="attn_fwd", grid=(nb, heads, nq),
        in_specs=[pl.BlockSpec((tq, 2 * LANES), lambda b, h, i: (b * nq + i, h)),
                  pl.BlockSpec((tq, 2 * LANES), lambda b, h, i: (b * nq + i, 0)),
                  pl.BlockSpec((seq, LANES), lambda b, h, i: (b, 2 * h)),
                  pl.BlockSpec((seq, LANES), lambda b, h, i: (b, 0)),
                  pl.BlockSpec((seq, LANES), lambda b, h, i: (b, 2 * h + 1))],
        out_specs=[pl.BlockSpec((tq, LANES), lambda b, h, i: (b * nq + i, h)),
                   pl.BlockSpec((None, tq, 1), lambda b, h, i: (b * heads + h, i, 0))],
        out_shape=[jax.ShapeDtypeStruct((nb * seq, heads * LANES), bf16),
                   jax.ShapeDtypeStruct((nb * heads, seq, 1), f32)],
        compiler_params=_cparams(("parallel", "parallel", "parallel")),
    )(q_ext, qt, kv, krd, kv)


def _attn_bwd(q_ext, qt, kv, krd, lse, d_mixed, nb, seq, heads):
    tq = _tile(seq, 512)
    nq = seq // tq

    def body(q_ref, qt_ref, kn_ref, krd_ref, v_ref, lse_ref, do_ref, dq_ref, dkv_ref, dkrd_ref, dkv_acc):
        h, qi = pl.program_id(1), pl.program_id(2)

        @pl.when(qi == 0)
        def _():
            dkv_acc[...] = jnp.zeros_like(dkv_acc)

        @pl.when((qi == 0) & (h == 0))
        def _():
            dkrd_ref[...] = jnp.zeros_like(dkrd_ref)

        def work(qblk):
            klen = (qblk + 1) * tq
            q, s = _scores(q_ref, qt_ref, kn_ref, krd_ref, qblk, tq)
            p = jnp.exp(s - lse_ref[...])
            do = do_ref[...]
            dp = _dot(do, v_ref[0:klen, :], _NT)
            ds = p * (dp - jnp.sum(p * dp, axis=-1, keepdims=True))
            qt_blk = qt_ref[...]
            dq_ref[:, :LANES] = (_dot(ds, kn_ref[0:klen, :], _NN) * qt_blk[:, :LANES]).astype(bf16)
            dq_ref[:, LANES:] = (_dot(ds, krd_ref[0:klen, :], _NN) * qt_blk[:, LANES:]).astype(bf16)
            dkv_acc[0:klen, :LANES] += _dot(ds, q[:, :LANES], _TN)
            dkv_acc[0:klen, LANES:] += _dot(p, do, _TN)
            dkrd_ref[0:klen, :] += _dot(ds, q[:, LANES:], _TN)

        _per_q_block(nq, work)

        @pl.when(qi == nq - 1)
        def _():
            dkv_ref[...] = dkv_acc[...].astype(bf16)

    return pl.pallas_call(
        body, name="attn_bwd", grid=(nb, heads, nq),
        in_specs=[pl.BlockSpec((tq, 2 * LANES), lambda b, h, i: (b * nq + i, h)),
                  pl.BlockSpec((tq, 2 * LANES), lambda b, h, i: (b * nq + i, 0)),
                  pl.BlockSpec((seq, LANES), lambda b, h, i: (b, 2 * h)),
                  pl.BlockSpec((seq, LANES), lambda b, h, i: (b, 0)),
                  pl.BlockSpec((seq, LANES), lambda b, h, i: (b, 2 * h + 1)),
                  pl.BlockSpec((None, tq, 1), lambda b, h, i: (b * heads + h, i, 0)),
                  pl.BlockSpec((tq, LANES), lambda b, h, i: (b * nq + i, h))],
        out_specs=[pl.BlockSpec((tq, 2 * LANES), lambda b, h, i: (b * nq + i, h)),
                   pl.BlockSpec((seq, 2 * LANES), lambda b, h, i: (b, h)),
                   pl.BlockSpec((seq, LANES), lambda b, h, i: (b, 0))],
        out_shape=[jax.ShapeDtypeStruct((nb * seq, heads * 2 * LANES), bf16),
                   jax.ShapeDtypeStruct((nb * seq, heads * 2 * LANES), bf16),
                   jax.ShapeDtypeStruct((nb * seq, LANES), f32)],
        scratch_shapes=[pltpu.VMEM((seq, 2 * LANES), f32)],
        compiler_params=_cparams(("parallel", "arbitrary", "arbitrary")),
    )(q_ext, qt, kv, krd, kv, lse, d_mixed)


_ANY = pl.BlockSpec(memory_space=pl.ANY)


def _mesh_pos():
    x, y, c = lax.axis_index("x"), lax.axis_index("y"), lax.axis_index("c")
    return x, y, c, [(1 - x, y), (x, 1 - y), (1 - x, 1 - y)]


def _all_gather(shards, name):
    n = len(shards)

    def body(*refs):
        x_refs, out_refs, (send_sems, recv_sems, local_sems) = refs[:n], refs[n:2 * n], refs[2 * n:]
        x, y, c, chips = _mesh_pos()
        me, sibling = (x, y, c), (x, y, 1 - c)

        def copy(t, k, block, to, from_shard=False):
            px, py, pc = block
            rows = out_refs[t].at[4 * px + 2 * py + pc]
            return pltpu.make_async_remote_copy(
                src_ref=x_refs[t] if from_shard else rows, dst_ref=rows,
                send_sem=send_sems.at[t, k], recv_sem=recv_sems.at[t, k], device_id=to, device_id_type=MESH)

        mine = [pltpu.make_async_copy(x_refs[t], out_refs[t].at[4 * x + 2 * y + c], local_sems.at[t]) for t in range(n)]
        first = [[copy(t, 0, me, sibling, True)] + [copy(t, 1 + j, me, (*chip, c), True) for j, chip in enumerate(chips)]
                 for t in range(n)]
        passed = [[copy(t, 4 + j, (*chip, c), sibling) for j, chip in enumerate(chips)] for t in range(n)]
        for t in range(n):
            mine[t].start()
            for cp in first[t]:
                cp.start()
        for j, chip in enumerate(chips):
            for t in range(n):
                copy(t, 1 + j, (*chip, c), me).wait_recv()
                passed[t][j].start()
        for t in range(n):
            copy(t, 0, sibling, me).wait_recv()
            for j, chip in enumerate(chips):
                copy(t, 4 + j, (*chip, 1 - c), me).wait_recv()
        for t in range(n):
            for cp in first[t] + passed[t]:
                cp.wait_send()
            mine[t].wait()

    return pl.pallas_call(
        body, name=name, out_shape=[jax.ShapeDtypeStruct((N_DEV,) + s.shape, s.dtype) for s in shards],
        in_specs=[_ANY] * n, out_specs=[_ANY] * n,
        scratch_shapes=[pltpu.SemaphoreType.DMA((n, 7)), pltpu.SemaphoreType.DMA((n, 7)), pltpu.SemaphoreType.DMA((n,))],
    )(*shards)


def _sibling_exchange(slots, name):
    n = len(slots)

    def body(*refs):
        g_refs, out_refs, (send_sems, recv_sems) = refs[:n], refs[n:2 * n], refs[2 * n:]
        x, y, c, _ = _mesh_pos()
        for t in range(n):
            for k in range(4):
                pltpu.make_async_remote_copy(
                    src_ref=g_refs[t].at[2 * k + (1 - c)], dst_ref=out_refs[t].at[k], send_sem=send_sems.at[t],
                    recv_sem=recv_sems.at[t], device_id=(x, y, 1 - c), device_id_type=MESH).start()
        for t in range(n):
            pltpu.make_async_remote_copy(
                src_ref=g_refs[t].at[pl.ds(0, 4)], dst_ref=out_refs[t], send_sem=send_sems.at[t],
                recv_sem=recv_sems.at[t], device_id=(x, y, 1 - c), device_id_type=MESH).wait()

    return pl.pallas_call(
        body, name=name, out_shape=[jax.ShapeDtypeStruct((4,) + s.shape[1:], s.dtype) for s in slots],
        in_specs=[_ANY] * n, out_specs=[_ANY] * n,
        scratch_shapes=[pltpu.SemaphoreType.DMA((n,)), pltpu.SemaphoreType.DMA((n,))],
    )(*slots)


def _chip_exchange(p4s, name):
    n = len(p4s)

    def body(*refs):
        p_refs, out_refs, (send_sems, recv_sems) = refs[:n], refs[n:2 * n], refs[2 * n:]
        x, y, c, chips = _mesh_pos()
        cps = [pltpu.make_async_remote_copy(
            src_ref=p_refs[t].at[2 * px + py], dst_ref=out_refs[t].at[j], send_sem=send_sems.at[t, j],
            recv_sem=recv_sems.at[t, j], device_id=(px, py, c), device_id_type=MESH)
            for t in range(n) for j, (px, py) in enumerate(chips)]
        for cp in cps:
            cp.start()
        for cp in cps:
            cp.wait()

    return pl.pallas_call(
        body, name=name, out_shape=[jax.ShapeDtypeStruct((3,) + p.shape[1:], p.dtype) for p in p4s],
        in_specs=[_ANY] * n, out_specs=[_ANY] * n,
        scratch_shapes=[pltpu.SemaphoreType.DMA((n, 3)), pltpu.SemaphoreType.DMA((n, 3))],
    )(*p4s)


def _row_tile(rows, cols, max_bytes=1024 * 1024):
    best = None
    for tr in range(16, rows + 1, 16):
        if rows % tr == 0 and tr * cols * 4 <= max_bytes:
            best = tr
    return best or rows


def _pair_add(slots, theirs, core, name):
    _, rows, cols = slots.shape
    tr = _row_tile(rows, cols)

    def body(c_ref, a_ref, b_ref, o_ref):
        o_ref[...] = (a_ref[...].astype(f32) + b_ref[...].astype(f32)).astype(bf16)

    return pl.pallas_call(
        body, name=name,
        grid_spec=pltpu.PrefetchScalarGridSpec(
            num_scalar_prefetch=1, grid=(4, rows // tr),
            in_specs=[pl.BlockSpec((None, tr, cols), lambda k, i, c_ref: (2 * k + c_ref[0], i, 0)),
                      pl.BlockSpec((None, tr, cols), lambda k, i, c_ref: (k, i, 0))],
            out_specs=pl.BlockSpec((None, tr, cols), lambda k, i, c_ref: (k, i, 0))),
        out_shape=jax.ShapeDtypeStruct((4, rows, cols), bf16), compiler_params=_cparams(("parallel", "parallel")),
    )(core, slots, theirs)


def _adam_update(g, w, m, v):
    c1 = 1.0 / (1.0 - ADAM_B1 ** ADAM_STEP)
    c2 = 1.0 / (1.0 - ADAM_B2 ** ADAM_STEP)
    nm = ADAM_B1 * m + (1.0 - ADAM_B1) * g
    nv = ADAM_B2 * v + (1.0 - ADAM_B2) * jnp.square(g)
    delta = -ADAM_LR * ((nm * c1) / (jnp.sqrt(nv * c2) + ADAM_EPS) + ADAM_WD * w)
    return delta, nm, nv


def _adamw_layers(p4s, chips, chip_idx, w, m, v, name):
    depth = len(p4s)
    _, rows_l, cols = p4s[0].shape
    tr = _row_tile(rows_l, cols)
    nr = rows_l // tr

    def body(idx_ref, *refs):
        p_refs, c_refs = refs[:depth], refs[depth:2 * depth]
        w_ref, m_ref, v_ref, g_ref, d_ref, nm_ref, nv_ref = refs[2 * depth:]
        layer = pl.program_id(0)
        for ll in range(depth):
            @pl.when(layer == ll)
            def _(ll=ll):
                g = p_refs[ll][...].astype(f32)
                for j in range(3):
                    g = g + c_refs[ll][j].astype(f32)
                delta, nm, nv = _adam_update(g, w_ref[...], m_ref[...], v_ref[...])
                g_ref[...] = g
                d_ref[...] = delta
                nm_ref[...] = nm
                nv_ref[...] = nv

    def of_layer(ll):
        return lambda l, i, idx_ref: jnp.where(l == ll, i, 0)

    p_specs = [pl.BlockSpec((None, tr, cols), lambda l, i, idx_ref, f=of_layer(ll): (idx_ref[0], f(l, i, idx_ref), 0))
               for ll in range(depth)]
    c_specs = [pl.BlockSpec((3, tr, cols), lambda l, i, idx_ref, f=of_layer(ll): (0, f(l, i, idx_ref), 0))
               for ll in range(depth)]
    spec = pl.BlockSpec((tr, cols), lambda l, i, idx_ref: (l * nr + i, 0))
    out = jax.ShapeDtypeStruct(w.shape, f32)
    return pl.pallas_call(
        body, name=name,
        grid_spec=pltpu.PrefetchScalarGridSpec(
            num_scalar_prefetch=1, grid=(depth, nr), in_specs=p_specs + c_specs + [spec, spec, spec],
            out_specs=[spec, spec, spec, spec]),
        out_shape=[out, out, out, out], compiler_params=_cparams(("parallel", "parallel")),
    )(chip_idx, *p4s, *chips, w, m, v)


def _adamw(parts, w, m, v, name):
    rows, cols = w.shape
    tr = _row_tile(rows, cols, 512 * 1024)
    nparts = len(parts)

    def body(*refs):
        part_refs, (w_ref, m_ref, v_ref, g_ref, d_ref, nm_ref, nv_ref) = refs[:nparts], refs[nparts:]
        g = None
        for pr in part_refs:
            for s in range(pr.shape[0]):
                term = pr[s].astype(f32)
                g = term if g is None else g + term
        delta, nm, nv = _adam_update(g, w_ref[...], m_ref[...], v_ref[...])
        g_ref[...] = g
        d_ref[...] = delta
        nm_ref[...] = nm
        nv_ref[...] = nv

    spec = pl.BlockSpec((tr, cols), lambda i: (i, 0))
    part_specs = [pl.BlockSpec((p.shape[0], tr, cols), lambda i: (0, i, 0)) for p in parts]
    out = jax.ShapeDtypeStruct((rows, cols), f32)
    return pl.pallas_call(
        body, name=name, grid=(rows // tr,), in_specs=part_specs + [spec, spec, spec],
        out_specs=[spec, spec, spec, spec], out_shape=[out, out, out, out],
        compiler_params=_cparams(("parallel",)),
    )(*parts, w, m, v)


def _gathered_to_full(gathered, axis):
    s = gathered.shape[1:]
    full = jnp.moveaxis(gathered, 0, axis)
    return full.reshape(s[:axis] + (N_DEV * s[axis],) + s[axis + 1:])


def _interleave_halves(w, ct):
    n = w.shape[-1] // 2
    t = w.reshape(w.shape[:-1] + (2, n // ct, ct))
    return jnp.swapaxes(t, -3, -2).reshape(w.shape)


def _deinterleave_halves(w, ct):
    n = w.shape[-1] // 2
    t = w.reshape(w.shape[:-1] + (n // ct, 2, ct))
    return jnp.swapaxes(t, -3, -2).reshape(w.shape)


def _rot_cols(w):
    half = QK_ROPE_DIM // 2
    return jnp.concatenate([-w[..., half:], w[..., :half]], axis=-1)


def _rot_cols_t(dw):
    half = QK_ROPE_DIM // 2
    return jnp.concatenate([dw[..., half:], -dw[..., :half]], axis=-1)


def kernel(x, positions, ln_in_g, ln_in_b, w_in, q_norm_g, w_uq, kv_norm_g, w_ukv, conv_w, conv_b, conv_ln_g, conv_ln_b, w_pool, pool_scale, w_out, ln1_g, ln1_b, w_up, ffn_conv_w, ffn_conv_b, w_down, ln2_g, ln2_b, loss_target, m_ln_in_g, m_ln_in_b, m_w_in, m_q_norm_g, m_w_uq, m_kv_norm_g, m_w_ukv, m_conv_w, m_conv_b, m_conv_ln_g, m_conv_ln_b, m_w_pool, m_pool_scale, m_w_out, m_ln1_g, m_ln1_b, m_w_up, m_ffn_conv_w, m_ffn_conv_b, m_w_down, m_ln2_g, m_ln2_b, v_ln_in_g, v_ln_in_b, v_w_in, v_q_norm_g, v_w_uq, v_kv_norm_g, v_w_ukv, v_conv_w, v_conv_b, v_conv_ln_g, v_conv_ln_b, v_w_pool, v_pool_scale, v_w_out, v_ln1_g, v_ln1_b, v_w_up, v_ffn_conv_w, v_ffn_conv_b, v_w_down, v_ln2_g, v_ln2_b):
    weights = dict(ln_in_g=ln_in_g, ln_in_b=ln_in_b, w_in=w_in, q_norm_g=q_norm_g, w_uq=w_uq, kv_norm_g=kv_norm_g,
                   w_ukv=w_ukv, conv_w=conv_w, conv_b=conv_b, conv_ln_g=conv_ln_g, conv_ln_b=conv_ln_b, w_pool=w_pool,
                   pool_scale=pool_scale, w_out=w_out, ln1_g=ln1_g, ln1_b=ln1_b, w_up=w_up, ffn_conv_w=ffn_conv_w,
                   ffn_conv_b=ffn_conv_b, w_down=w_down, ln2_g=ln2_g, ln2_b=ln2_b)
    mom1 = dict(ln_in_g=m_ln_in_g, ln_in_b=m_ln_in_b, w_in=m_w_in, q_norm_g=m_q_norm_g, w_uq=m_w_uq,
                kv_norm_g=m_kv_norm_g, w_ukv=m_w_ukv, conv_w=m_conv_w, conv_b=m_conv_b, conv_ln_g=m_conv_ln_g,
                conv_ln_b=m_conv_ln_b, w_pool=m_w_pool, pool_scale=m_pool_scale, w_out=m_w_out, ln1_g=m_ln1_g,
                ln1_b=m_ln1_b, w_up=m_w_up, ffn_conv_w=m_ffn_conv_w, ffn_conv_b=m_ffn_conv_b, w_down=m_w_down,
                ln2_g=m_ln2_g, ln2_b=m_ln2_b)
    mom2 = dict(ln_in_g=v_ln_in_g, ln_in_b=v_ln_in_b, w_in=v_w_in, q_norm_g=v_q_norm_g, w_uq=v_w_uq,
                kv_norm_g=v_kv_norm_g, w_ukv=v_w_ukv, conv_w=v_conv_w, conv_b=v_conv_b, conv_ln_g=v_conv_ln_g,
                conv_ln_b=v_conv_ln_b, w_pool=v_w_pool, pool_scale=v_pool_scale, w_out=v_w_out, ln1_g=v_ln1_g,
                ln1_b=v_ln1_b, w_up=v_w_up, ffn_conv_w=v_ffn_conv_w, ffn_conv_b=v_ffn_conv_b, w_down=v_w_down,
                ln2_g=v_ln2_g, ln2_b=v_ln2_b)
    names = list(weights)

    nb, seq, d = x.shape
    t = nb * seq
    depth = w_in.shape[0]
    ql, kvl, cw, pw = q_norm_g.shape[1], kv_norm_g.shape[1], conv_b.shape[1], pool_scale.shape[1]
    pg = w_pool.shape[-1]
    heads = w_uq.shape[2]
    dff = w_down.shape[1] * N_DEV
    alpha = (2.0 * depth) ** 0.25
    scale = float(QK_NOPE_DIM + QK_ROPE_DIM) ** -0.5
    lay = dict(ql=ql, kvl=kvl, pw=pw, pg=pg, off_q=2 * cw, off_pool=2 * cw + ql, off_kv=2 * cw + ql + pw,
               off_kr=2 * cw + ql + pw + kvl)
    o1, o2, o3, o4 = ql, ql + kvl, ql + kvl + QK_ROPE_DIM, ql + kvl + QK_ROPE_DIM + 2 * cw
    my_x, my_y, my_c = lax.axis_index("x"), lax.axis_index("y"), lax.axis_index("c")
    my_dev = 4 * my_x + 2 * my_y + my_c

    big = ("w_in", "w_uq", "w_ukv", "w_out", "w_up", "w_down")
    g_conv, g_ffn = _all_gather([conv_w, ffn_conv_w], "ag_conv_taps")
    conv_w_full, ffn_w_full = _gathered_to_full(g_conv, 2), _gathered_to_full(g_ffn, 2)
    w_pool_2d = w_pool.reshape(depth, pw, pg)

    def layer_weights(l):
        g_in, g_uq, g_ukv, g_out, g_up, g_down = _all_gather([weights[n][l].astype(bf16) for n in big], "ag_weights")
        wi = _gathered_to_full(g_in, 1)
        kr_cols = wi[:, o2:o3]
        w_in_pad = jnp.concatenate([_interleave_halves(wi[:, o3:o4], SEQ_CT), wi[:, :o1], wi[:, o4:], wi[:, o1:o2],
                                    kr_cols, _rot_cols(kr_cols)], axis=-1)
        wq = g_uq.reshape(ql, heads, QK_NOPE_DIM + QK_ROPE_DIM)
        w_uq_ext = jnp.concatenate([wq, _rot_cols(wq[..., QK_NOPE_DIM:])], axis=-1).reshape(ql, heads * 2 * LANES)
        return dict(w_in=w_in_pad, w_uq=w_uq_ext, w_ukv=g_ukv.reshape(kvl, heads * 2 * LANES), w_out=g_out.reshape(-1, d),
                    w_up=g_up, w_down=g_down.reshape(dff, d))

    half = QK_ROPE_DIM // 2
    inv = 1.0 / (ROPE_THETA ** (jnp.arange(0, QK_ROPE_DIM, 2, dtype=f32) / QK_ROPE_DIM))
    inv_lanes = jnp.tile(inv, LANES // half).reshape(1, LANES)
    cs, qt = _rope_tables(positions.reshape(t, 1), inv_lanes, scale)

    x2 = x.reshape(t, d)
    xs, xs_bf = _ln_fwd(x2, None, ln_in_g.reshape(1, d), ln_in_b.reshape(1, d), 1.0, "ln_in_fwd")
    saved = []
    in_pad = o4 + pw + QK_ROPE_DIM
    for l in range(depth):
        wl = layer_weights(l)
        gq, gkv = q_norm_g[l].reshape(1, ql), kv_norm_g[l].reshape(1, kvl)
        h = _matmul(xs_bf, wl["w_in"], "nn", f32, "mm_in", tm=512, tn=in_pad)
        qn, kvn, krd = _prep_fwd(h, cs, gq, gkv, lay)
        q_ext = _matmul(qn, wl["w_uq"], "nn", f32, "mm_uq", tn=2048)
        kv = _matmul(kvn, wl["w_ukv"], "nn", bf16, "mm_ukv", tn=2048)
        y_mla, lse = _attn_fwd(q_ext, qt, kv, krd, nb, seq, heads)
        hconv = _conv_fwd(h, conv_w_full[l], conv_b[l].reshape(1, cw), nb, seq, cw)
        y_conv = _convln_fwd(hconv, conv_ln_g[l].reshape(1, cw), conv_ln_b[l].reshape(1, cw))
        y_pool = _pool_fwd(h, w_pool_2d[l], pool_scale[l].reshape(1, pw), lay, nb, seq)
        mixed = jnp.concatenate([y_mla, y_conv, y_pool], axis=-1)
        y1 = _matmul(mixed, wl["w_out"], "nn", f32, "mm_out")
        x1, x1_bf = _ln_fwd(xs, y1, ln1_g[l].reshape(1, d), ln1_b[l].reshape(1, d), alpha, "ln1_fwd")
        up = _matmul_up(x1_bf, wl["w_up"], "mm_up")
        act = _ffn_act_fwd(up, ffn_w_full[l], ffn_conv_b[l].reshape(1, 2 * dff), nb, seq, dff)
        y2 = _matmul(act, wl["w_down"], "nn", f32, "mm_down", tk=dff // 2)
        xn, xn_bf = _ln_fwd(x1, y2, ln2_g[l].reshape(1, d), ln2_b[l].reshape(1, d), alpha, "ln2_fwd")
        saved.append(dict(xs=xs, xs_bf=xs_bf, h=h, qn=qn, kvn=kvn, krd=krd, q_ext=q_ext, kv=kv, lse=lse, hconv=hconv,
                          mixed=mixed, y1=y1, x1=x1, x1_bf=x1_bf, up=up, act=act, y2=y2, wl=wl))
        xs, xs_bf = xn, xn_bf

    d_stream, loss_row = _loss_call(xs, loss_target.reshape(t, d))
    loss = lax.psum(loss_row[0, 0], MESH_AXES)

    gw = {n: [None] * depth for n in names if n not in ("ln_in_g", "ln_in_b")}
    rs_own = {n: [None] * depth for n in big}
    rs_chips = {n: [None] * depth for n in big}
    core_idx = jnp.reshape(my_c, (1,)).astype(jnp.int32)
    chip_idx = jnp.reshape(2 * my_x + my_y, (1,)).astype(jnp.int32)

    def reduce_scatter(l, slots):
        theirs = _sibling_exchange([slots[n] for n in big], "rs_sibling")
        p4s = [_pair_add(slots[n], th, core_idx, "rs_add_" + n) for n, th in zip(big, theirs)]
        from_chips = _chip_exchange(p4s, "rs_chips")
        for n, p4, fc in zip(big, p4s, from_chips):
            rs_own[n][l], rs_chips[n][l] = p4, fc

    d_res, d_mm = None, d_stream
    for l in reversed(range(depth)):
        sv = saved[l]
        wl = sv["wl"]
        slots = {}
        gq, gkv = q_norm_g[l].reshape(1, ql), kv_norm_g[l].reshape(1, kvl)
        dz2, dz2_bf, gw["ln2_g"][l], gw["ln2_b"][l] = _ln_bwd(
            d_res, d_mm, sv["x1"], sv["y2"], ln2_g[l].reshape(1, d), ln2_b[l].reshape(1, d), alpha, "ln2_bwd")
        d_act = _matmul(dz2_bf, wl["w_down"], "nt", f32, "mm_down_dx", tn=dff // 4)
        slots["w_down"] = _matmul(sv["act"], dz2_bf, "tn", bf16, "mm_down_dw", tm=dff // 4).reshape(N_DEV, -1, d)
        d_up, dffw, dffb = _ffn_act_bwd(sv["up"], ffn_w_full[l], ffn_conv_b[l].reshape(1, 2 * dff), d_act, nb, seq, dff)
        gw["ffn_conv_w"][l] = jnp.concatenate([dffw[0], dffw[1]], axis=-1)
        gw["ffn_conv_b"][l] = dffb.reshape(2 * dff)
        d_x1 = _matmul_up_dx(d_up, wl["w_up"], "mm_up_dx")
        slots["w_up"] = _matmul_up_dw(sv["x1_bf"], d_up, N_DEV, "mm_up_dw")
        dz1, dz1_bf, gw["ln1_g"][l], gw["ln1_b"][l] = _ln_bwd(
            dz2, d_x1, sv["xs"], sv["y1"], ln1_g[l].reshape(1, d), ln1_b[l].reshape(1, d), alpha, "ln1_bwd")
        d_mixed = _matmul(dz1_bf, wl["w_out"], "nt", f32, "mm_out_dx")
        slots["w_out"] = _matmul(sv["mixed"], dz1_bf, "tn", bf16, "mm_out_dw").reshape(N_DEV, -1, d)
        d_upool, dwp, dps = _pool_bwd(sv["h"], w_pool_2d[l], pool_scale[l].reshape(1, pw), d_mixed,
                                      (heads * LANES + cw) // pw, lay, nb, seq)
        gw["w_pool"][l] = dwp.reshape(w_pool.shape[1:])
        gw["pool_scale"][l] = dps.reshape(pw)
        d_hconv, dclg, dclb = _convln_bwd(sv["hconv"], conv_ln_g[l].reshape(1, cw), conv_ln_b[l].reshape(1, cw), d_mixed,
                                          heads * LANES // cw)
        gw["conv_ln_g"][l], gw["conv_ln_b"][l] = dclg.reshape(cw), dclb.reshape(cw)
        d_conv, gw["conv_w"][l], dcb = _conv_bwd(sv["h"], conv_w_full[l], d_hconv, nb, seq, cw)
        gw["conv_b"][l] = dcb.reshape(cw)
        dq_ext, dkv, dkrd = _attn_bwd(sv["q_ext"], qt, sv["kv"], sv["krd"], sv["lse"], d_mixed, nb, seq, heads)
        d_qn = _matmul(dq_ext, wl["w_uq"], "nt", f32, "mm_uq_dx")
        dwq = _matmul(sv["qn"], dq_ext, "tn", f32, "mm_uq_dw", tn=2048, tk=1024).reshape(ql, heads, 2 * LANES)
        dwq_rope = dwq[..., QK_NOPE_DIM:QK_NOPE_DIM + QK_ROPE_DIM] + _rot_cols_t(dwq[..., QK_NOPE_DIM + QK_ROPE_DIM:])
        slots["w_uq"] = jnp.concatenate([dwq[..., :QK_NOPE_DIM], dwq_rope], axis=-1).astype(bf16).reshape(
            N_DEV, -1, QK_NOPE_DIM + QK_ROPE_DIM)
        d_kvn = _matmul(dkv, wl["w_ukv"], "nt", f32, "mm_ukv_dx")
        slots["w_ukv"] = _matmul(sv["kvn"], dkv, "tn", bf16, "mm_ukv_dw", tn=2048, tk=1024).reshape(N_DEV, -1, 2 * LANES)
        d_cq, d_ckv, d_kr, dgq, dgkv = _prep_bwd(sv["h"], cs, gq, gkv, d_qn, d_kvn, dkrd, lay)
        gw["q_norm_g"][l], gw["kv_norm_g"][l] = dgq.reshape(ql), dgkv.reshape(kvl)
        d_h = jnp.concatenate([d_conv, d_cq, d_upool, d_ckv, d_kr], axis=-1)
        d_xs = _matmul(d_h, wl["w_in"], "nt", f32, "mm_in_dx", tm=512, tk=in_pad)
        dwi = _matmul(sv["xs_bf"], d_h, "tn", f32, "mm_in_dw", tn=in_pad, tk=1024)
        dkr_cols = dwi[:, lay["off_kr"]:lay["off_kr"] + QK_ROPE_DIM] + _rot_cols_t(dwi[:, lay["off_kr"] + QK_ROPE_DIM:])
        dwi_nat = jnp.concatenate(
            [dwi[:, lay["off_q"]:lay["off_q"] + ql], dwi[:, lay["off_kv"]:lay["off_kv"] + kvl], dkr_cols,
             _deinterleave_halves(dwi[:, :2 * cw], SEQ_CT), dwi[:, lay["off_pool"]:lay["off_pool"] + pw]],
            axis=-1).astype(bf16)
        slots["w_in"] = jnp.moveaxis(dwi_nat.reshape(d, N_DEV, -1), 1, 0)
        reduce_scatter(l, slots)
        d_res, d_mm = dz1, d_xs

    grad_x, _, d_ln_in_g, d_ln_in_b = _ln_bwd(d_res, d_mm, x2, None, ln_in_g.reshape(1, d), ln_in_b.reshape(1, d), alpha,
                                               "ln_in_bwd")
    grad_x = grad_x.reshape(x.shape)

    grads, deltas, new_m, new_v = {}, {}, {}, {}

    def finish(n, parts):
        shp = weights[n].shape
        rows = math.prod(shp[:-1]) if len(shp) > 1 else 1
        as2d = lambda a: a.reshape(rows, shp[-1])
        parts = [p.reshape(p.shape[0], rows, shp[-1]) for p in parts]
        g, dl, nm, nv = _adamw(parts, as2d(weights[n]), as2d(mom1[n]), as2d(mom2[n]), "adamw_" + n)
        grads[n], deltas[n], new_m[n], new_v[n] = (a.reshape(shp) for a in (g, dl, nm, nv))

    for n in big:
        shp = weights[n].shape
        as2d = lambda a: a.reshape(-1, shp[-1])
        g, dl, nm, nv = _adamw_layers(rs_own[n], rs_chips[n], chip_idx, as2d(weights[n]), as2d(mom1[n]), as2d(mom2[n]),
                                      "adamw_" + n)
        grads[n], deltas[n], new_m[n], new_v[n] = (a.reshape(shp) for a in (g, dl, nm, nv))

    small = [n for n in names if n not in ("w_in", "w_uq", "w_ukv", "w_out", "w_up", "w_down")]
    partial = {"ln_in_g": d_ln_in_g.reshape(d), "ln_in_b": d_ln_in_b.reshape(d)}
    for n in small:
        if n not in partial:
            partial[n] = jnp.stack(gw[n])
    flat = jnp.concatenate([partial[n].astype(f32).reshape(-1) for n in small])
    gathered = _all_gather([flat.reshape(-1, LANES)], "ag_small_grads")[0].reshape(N_DEV, -1)
    off = 0
    for n in small:
        size = math.prod(partial[n].shape)
        part = gathered[:, off:off + size].reshape((N_DEV,) + partial[n].shape)
        off += size
        if n in ("conv_w", "ffn_conv_w"):
            width = weights[n].shape[-1]
            part = lax.dynamic_slice_in_dim(part, my_dev * width, width, axis=part.ndim - 1)
        finish(n, [part])

    return (loss, grad_x, *[grads[n] for n in names], *[deltas[n] for n in names], *[new_m[n] for n in names],
            *[new_v[n] for n in names])
```

```python
import functools
import math

import jax
import jax.numpy as jnp
from jax import lax
from jax.experimental import pallas as pl
from jax.experimental.pallas import tpu as pltpu

f32 = jnp.float32
bf16 = jnp.bfloat16

QK_NOPE_DIM = 128
QK_ROPE_DIM = 64
V_HEAD_DIM = 128
CONV_KERNEL = 31
FFN_CONV_KERNEL = 3
POOL_WINDOWS = (2, 4, 8, 16)
ROPE_THETA = 10000.0
LN_EPS = 1e-5
RMS_EPS = 1e-6
ADAM_LR = 0.001
ADAM_B1 = 0.9
ADAM_B2 = 0.999
ADAM_EPS = 1e-08
ADAM_WD = 0.01
ADAM_STEP = 10

N_DEV = 8
MESH_AXES = ("x", "y", "c")
V7X_VMEM_LIMIT_BYTES = 56 * 1024 * 1024
LANES = 128
NEG_INF = -1e30
MESH = pl.DeviceIdType.MESH


def _cparams(sem):
    return pltpu.CompilerParams(dimension_semantics=sem, vmem_limit_bytes=V7X_VMEM_LIMIT_BYTES)


def _tile(dim, pref):
    t = pref
    while t >= LANES:
        if dim % t == 0:
            return t
        t //= 2
    return dim


_ANY = pl.BlockSpec(memory_space=pl.ANY)


class _Comm:
    def __init__(self, inputs, out_shapes, sems, start, finish, aliases=None):
        self.inputs, self.out_shapes, self.sems = list(inputs), list(out_shapes), list(sems)
        self.start, self.finish, self.aliases = start, finish, dict(aliases or {})


def _call(body, name, grid, in_specs, out_specs, out_shape, args, scratch=(), sem=None, comm=None):
    in_specs, out_specs, out_shape, scratch = list(in_specs), list(out_specs), list(out_shape), list(scratch)
    if comm is None:
        outs = pl.pallas_call(body, name=name, grid=grid, in_specs=in_specs, out_specs=out_specs, out_shape=out_shape,
                              scratch_shapes=scratch, compiler_params=_cparams(sem))(*args)
        return list(outs), []
    n_in, n_out, n_scr = len(in_specs), len(out_specs), len(scratch)
    c_in, c_out = len(comm.inputs), len(comm.out_shapes)

    def carrier(*refs):
        refs = list(refs)
        ins, refs = refs[:n_in], refs[n_in:]
        c_ins, refs = refs[:c_in], refs[c_in:]
        outs, refs = refs[:n_out], refs[n_out:]
        c_outs, refs = refs[:c_out], refs[c_out:]
        scr, c_sems = refs[:n_scr], refs[n_scr:]
        ids = [pl.program_id(a) for a in range(len(grid))]
        first = functools.reduce(lambda p, q: p & q, [i == 0 for i in ids])
        last = functools.reduce(lambda p, q: p & q, [i == g - 1 for i, g in zip(ids, grid)])
        pl.when(first)(lambda: comm.start(c_ins, c_outs, c_sems))
        body(*ins, *outs, *scr)
        pl.when(last)(lambda: comm.finish(c_ins, c_outs, c_sems))

    outs = pl.pallas_call(
        carrier, name=name, grid=grid, in_specs=in_specs + [_ANY] * c_in, out_specs=out_specs + [_ANY] * c_out,
        out_shape=out_shape + comm.out_shapes, scratch_shapes=scratch + comm.sems,
        input_output_aliases={n_in + a: n_out + b for a, b in comm.aliases.items()},
        compiler_params=_cparams(("arbitrary",) * len(grid)),
    )(*args, *comm.inputs)
    return list(outs[:n_out]), list(outs[n_out:])


def _shift_down_raw(x, k):
    if k == 0:
        return x
    row = lax.broadcasted_iota(jnp.int32, x.shape, 0)
    return jnp.where(row >= k, pltpu.roll(x, k, axis=0), 0.0)


def _shift_up_raw(x, k):
    if k == 0:
        return x
    n = x.shape[0]
    row = lax.broadcasted_iota(jnp.int32, x.shape, 0)
    return jnp.where(row < n - k, pltpu.roll(x, n - k, axis=0), 0.0)


@functools.partial(jax.custom_vjp, nondiff_argnums=(1,))
def _shift_down(x, k):
    return _shift_down_raw(x, k)


def _shift_down_fwd(x, k):
    return _shift_down_raw(x, k), None


def _shift_down_bwd(k, _, g):
    return (_shift_up_raw(g, k),)


_shift_down.defvjp(_shift_down_fwd, _shift_down_bwd)


@jax.custom_vjp
def _dup_halves(p):
    return p + pltpu.roll(p, LANES // 2, axis=1)


def _dup_halves_fwd(p):
    return p + pltpu.roll(p, LANES // 2, axis=1), None


def _dup_halves_bwd(_, g):
    return (g + pltpu.roll(g, LANES // 2, axis=1),)


_dup_halves.defvjp(_dup_halves_fwd, _dup_halves_bwd)

_NN = (((1,), (0,)), ((), ()))
_NT = (((1,), (1,)), ((), ()))
_TN = (((0,), (0,)), ((), ()))


def _dot(a, b, dims):
    return lax.dot_general(a.astype(bf16), b.astype(bf16), dims, preferred_element_type=f32)


@jax.custom_vjp
def _mm_bf16(a, b):
    return _dot(a, b, _NN)


def _mm_bf16_fwd(a, b):
    return _dot(a, b, _NN), (a, b)


def _mm_bf16_bwd(res, g):
    a, b = res
    return _dot(g, b, _NT), _dot(a, g, _TN)


_mm_bf16.defvjp(_mm_bf16_fwd, _mm_bf16_bwd)


def _layer_norm(z, g, b):
    mu = jnp.mean(z, axis=-1, keepdims=True)
    var = jnp.mean(jnp.square(z - mu), axis=-1, keepdims=True)
    return (z - mu) * lax.rsqrt(var + LN_EPS) * g + b


def _rms_norm(x, g):
    ms = jnp.mean(jnp.square(x), axis=-1, keepdims=True)
    return x * lax.rsqrt(ms + RMS_EPS) * g


def _colsum(x):
    return jnp.sum(x, axis=0, keepdims=True)


def _matmul_core(a, b, mode, grid, a_spec, b_spec, o_spec, o_shape, tile, out_dtype, name, comm=None):
    nk = grid[2]
    dims = {"nn": _NN, "nt": _NT, "tn": _TN}[mode]
    acc_in_out = out_dtype == f32

    def body(a_ref, b_ref, o_ref, *scratch):
        def prod():
            return _dot(a_ref[...], b_ref[...], dims)

        if nk == 1:
            o_ref[...] = prod().astype(out_dtype)
            return
        acc_ref = o_ref if acc_in_out else scratch[0]
        kk = pl.program_id(2)

        @pl.when(kk == 0)
        def _():
            acc_ref[...] = prod()

        if acc_in_out:
            @pl.when(kk > 0)
            def _():
                acc_ref[...] += prod()
        else:
            @pl.when((kk > 0) & (kk < nk - 1))
            def _():
                acc_ref[...] += prod()

            @pl.when(kk == nk - 1)
            def _():
                o_ref[...] = (acc_ref[...] + prod()).astype(out_dtype)

    scratch = [] if (nk == 1 or acc_in_out) else [pltpu.VMEM(tile, f32)]
    (out,), comm_outs = _call(body, name, grid, [a_spec, b_spec], [o_spec], [jax.ShapeDtypeStruct(o_shape, out_dtype)],
                              (a, b), scratch, ("parallel", "parallel", "arbitrary"), comm)
    return out if comm is None else (out, comm_outs)


def _matmul(a, b, mode, out_dtype, name, tm=1024, tn=1024, tk=2048, comm=None):
    if mode == "nn":
        (m, k), (k2, n) = a.shape, b.shape
    elif mode == "nt":
        (m, k), (n, k2) = a.shape, b.shape
    else:
        (k, m), (k2, n) = a.shape, b.shape
    assert k == k2, (name, a.shape, b.shape)
    tm, tn, tk = _tile(m, tm), _tile(n, tn), _tile(k, tk)
    a_spec = pl.BlockSpec((tk, tm), lambda i, j, kk: (kk, i)) if mode == "tn" else pl.BlockSpec((tm, tk), lambda i, j, kk: (i, kk))
    b_spec = pl.BlockSpec((tn, tk), lambda i, j, kk: (j, kk)) if mode == "nt" else pl.BlockSpec((tk, tn), lambda i, j, kk: (kk, j))
    return _matmul_core(a, b, mode, (m // tm, n // tn, k // tk), a_spec, b_spec,
                        pl.BlockSpec((tm, tn), lambda i, j, kk: (i, j)), (m, n), (tm, tn), out_dtype, name, comm)


def _matmul_up(x_bf, w_slots, name, tm=1024, comm=None):
    m, k = x_bf.shape
    s, _, ns = w_slots.shape
    tm = _tile(m, tm)
    return _matmul_core(x_bf, w_slots, "nn", (m // tm, s, 1), pl.BlockSpec((tm, k), lambda i, j, kk: (i, 0)),
                        pl.BlockSpec((None, k, ns), lambda i, j, kk: (j, 0, 0)),
                        pl.BlockSpec((tm, ns), lambda i, j, kk: (i, j)), (m, s * ns), (tm, ns), f32, name, comm)


def _matmul_up_dx(d3, w_slots, name, tm=1024, tn=1024, comm=None):
    _, m, half = d3.shape
    s, n, ns = w_slots.shape
    per_half = half // ns
    assert 2 * per_half == s, (d3.shape, w_slots.shape)
    tm, tn = _tile(m, tm), _tile(n, tn)
    return _matmul_core(d3, w_slots, "nt", (m // tm, n // tn, s),
                        pl.BlockSpec((None, tm, ns), lambda i, j, kk: (kk // per_half, i, kk % per_half)),
                        pl.BlockSpec((None, tn, ns), lambda i, j, kk: (kk, j, 0)),
                        pl.BlockSpec((tm, tn), lambda i, j, kk: (i, j)), (m, n), (tm, tn), f32, name, comm)


def _matmul_up_dw(x_bf, d3, n_slots, name, tm=1024, tk=2048, comm=None):
    k, m = x_bf.shape
    _, _, half = d3.shape
    ns = 2 * half // n_slots
    per_half = n_slots // 2
    tm, tk = _tile(m, tm), _tile(k, tk)
    return _matmul_core(x_bf, d3, "tn", (m // tm, n_slots, k // tk), pl.BlockSpec((tk, tm), lambda i, j, kk: (kk, i)),
                        pl.BlockSpec((None, tk, ns), lambda i, j, kk: (j // per_half, kk, j % per_half)),
                        pl.BlockSpec((None, tm, ns), lambda i, j, kk: (j, i, 0)), (n_slots, m, ns), (tm, ns), bf16, name,
                        comm)


ROW_TILE = 256


def _rows(width, col_block=0):
    return pl.BlockSpec((ROW_TILE, width), lambda i, cb=col_block: (i, cb))


def _whole(shape):
    return pl.BlockSpec(shape, lambda i: (0,) * len(shape))


def _ln_fwd(x, y, g, b, alpha, name, comm=None):
    t, d = x.shape

    def body(*refs):
        if y is None:
            x_ref, g_ref, b_ref, o_ref, ob_ref = refs
            z = x_ref[...]
        else:
            x_ref, y_ref, g_ref, b_ref, o_ref, ob_ref = refs
            z = alpha * x_ref[...] + y_ref[...]
        out = _layer_norm(z, g_ref[...], b_ref[...])
        o_ref[...] = out
        ob_ref[...] = out.astype(bf16)

    ins = [x] + ([] if y is None else [y]) + [g, b]
    specs = [_rows(d)] + ([] if y is None else [_rows(d)]) + [_whole((1, d)), _whole((1, d))]
    (out, out_bf), comm_outs = _call(
        body, name, (t // ROW_TILE,), specs, [_rows(d), _rows(d)],
        [jax.ShapeDtypeStruct((t, d), f32), jax.ShapeDtypeStruct((t, d), bf16)], ins, (), ("parallel",), comm)
    return (out, out_bf) if comm is None else (out, out_bf, comm_outs)


def _ln_bwd(d_res, d_mm, x, y, g, b, alpha, name):
    t, d = x.shape
    has_res, has_mm, has_y = d_res is not None, d_mm is not None, y is not None

    def body(*refs):
        refs = list(refs)
        d_res_ref = refs.pop(0) if has_res else None
        d_mm_ref = refs.pop(0) if has_mm else None
        x_ref = refs.pop(0)
        y_ref = refs.pop(0) if has_y else None
        g_ref, b_ref, dz_ref, dzb_ref, dg_ref, db_ref = refs
        ct = None
        if has_res:
            ct = alpha * d_res_ref[...]
        if has_mm:
            ct = d_mm_ref[...] if ct is None else ct + d_mm_ref[...]
        z = x_ref[...] if not has_y else alpha * x_ref[...] + y_ref[...]
        _, vjp = jax.vjp(_layer_norm, z, g_ref[...], b_ref[...])
        dz, dg, db = vjp(ct)
        dz_ref[...] = dz
        dzb_ref[...] = dz.astype(bf16)

        @pl.when(pl.program_id(0) == 0)
        def _():
            dg_ref[...] = jnp.zeros_like(dg_ref)
            db_ref[...] = jnp.zeros_like(db_ref)

        dg_ref[...] += dg
        db_ref[...] += db

    ins = [a for a in (d_res, d_mm, x, y) if a is not None] + [g, b]
    specs = [_rows(d) for a in (d_res, d_mm, x, y) if a is not None] + [_whole((1, d)), _whole((1, d))]
    return pl.pallas_call(
        body, name=name, grid=(t // ROW_TILE,), in_specs=specs,
        out_specs=[_rows(d), _rows(d), _whole((1, d)), _whole((1, d))],
        out_shape=[jax.ShapeDtypeStruct((t, d), f32), jax.ShapeDtypeStruct((t, d), bf16),
                   jax.ShapeDtypeStruct((1, d), f32), jax.ShapeDtypeStruct((1, d), f32)],
        compiler_params=_cparams(("arbitrary",)),
    )(*ins)


def _loss_call(xf, target):
    t, d = xf.shape

    def body(x_ref, t_ref, dx_ref, loss_ref):
        err = x_ref[...] - t_ref[...]
        dx_ref[...] = err * (1.0 / d)

        @pl.when(pl.program_id(0) == 0)
        def _():
            loss_ref[...] = jnp.zeros_like(loss_ref)

        part = 0.5 * jnp.sum(jnp.mean(jnp.square(err), axis=-1, keepdims=True), axis=0, keepdims=True)
        loss_ref[...] += jnp.broadcast_to(part, loss_ref.shape)

    return pl.pallas_call(
        body, name="loss_head", grid=(t // ROW_TILE,), in_specs=[_rows(d), _rows(d)],
        out_specs=[_rows(d), _whole((1, LANES))],
        out_shape=[jax.ShapeDtypeStruct((t, d), f32), jax.ShapeDtypeStruct((1, LANES), f32)],
        compiler_params=_cparams(("arbitrary",)),
    )(xf, target)


def _rope_tables(pos, inv, scale):
    t = pos.shape[0]

    def body(pos_ref, inv_ref, cs_ref, qt_ref):
        ang = pos_ref[...].astype(f32) * inv_ref[...]
        lane = lax.broadcasted_iota(jnp.int32, ang.shape, 1)
        cs = jnp.where(lane < LANES // 2, jnp.cos(ang), jnp.sin(ang))
        cs_ref[...] = cs
        qt_ref[:, :LANES] = jnp.full((ROW_TILE, LANES), scale, f32)
        qt_ref[:, LANES:] = scale * cs

    return pl.pallas_call(
        body, name="rope_tables", grid=(t // ROW_TILE,),
        in_specs=[pl.BlockSpec((ROW_TILE, 1), lambda i: (i, 0)), _whole((1, LANES))],
        out_specs=[_rows(LANES), _rows(2 * LANES)],
        out_shape=[jax.ShapeDtypeStruct((t, LANES), f32), jax.ShapeDtypeStruct((t, 2 * LANES), f32)],
        compiler_params=_cparams(("parallel",)),
    )(pos, inv)


def _prep_fn(cq, ckv, kr, cs, gq, gkv):
    return _rms_norm(cq, gq), _rms_norm(ckv, gkv), _dup_halves(kr * cs)


def _prep_fwd(h, cs, gq, gkv, lay):
    t = h.shape[0]
    ql, kvl = lay["ql"], lay["kvl"]

    def body(cq_ref, ckv_ref, kr_ref, cs_ref, gq_ref, gkv_ref, qn_ref, kvn_ref, krd_ref):
        qn, kvn, krd = _prep_fn(cq_ref[...], ckv_ref[...], kr_ref[...], cs_ref[...], gq_ref[...], gkv_ref[...])
        qn_ref[...] = qn.astype(bf16)
        kvn_ref[...] = kvn.astype(bf16)
        krd_ref[...] = krd.astype(bf16)

    return pl.pallas_call(
        body, name="prep_fwd", grid=(t // ROW_TILE,),
        in_specs=[_rows(ql, lay["off_q"] // ql), _rows(kvl, lay["off_kv"] // kvl), _rows(LANES, lay["off_kr"] // LANES),
                  _rows(LANES), _whole((1, ql)), _whole((1, kvl))],
        out_specs=[_rows(ql), _rows(kvl), _rows(LANES)],
        out_shape=[jax.ShapeDtypeStruct((t, ql), bf16), jax.ShapeDtypeStruct((t, kvl), bf16),
                   jax.ShapeDtypeStruct((t, LANES), bf16)],
        compiler_params=_cparams(("parallel",)),
    )(h, h, h, cs, gq, gkv)


def _prep_bwd(h, cs, gq, gkv, d_qn, d_kvn, d_krd, lay):
    t = h.shape[0]
    ql, kvl = lay["ql"], lay["kvl"]

    def body(cq_ref, ckv_ref, kr_ref, cs_ref, gq_ref, gkv_ref, dqn_ref, dkvn_ref, dkrd_ref,
             dcq_ref, dckv_ref, dkr_ref, dgq_ref, dgkv_ref):
        _, vjp = jax.vjp(_prep_fn, cq_ref[...], ckv_ref[...], kr_ref[...], cs_ref[...], gq_ref[...], gkv_ref[...])
        dcq, dckv, dkr, _, dgq, dgkv = vjp((dqn_ref[...], dkvn_ref[...], dkrd_ref[...].astype(f32)))
        dcq_ref[...] = dcq.astype(bf16)
        dckv_ref[...] = dckv.astype(bf16)
        dkr_ref[...] = dkr.astype(bf16)

        @pl.when(pl.program_id(0) == 0)
        def _():
            dgq_ref[...] = jnp.zeros_like(dgq_ref)
            dgkv_ref[...] = jnp.zeros_like(dgkv_ref)

        dgq_ref[...] += dgq
        dgkv_ref[...] += dgkv

    return pl.pallas_call(
        body, name="prep_bwd", grid=(t // ROW_TILE,),
        in_specs=[_rows(ql, lay["off_q"] // ql), _rows(kvl, lay["off_kv"] // kvl), _rows(LANES, lay["off_kr"] // LANES),
                  _rows(LANES), _whole((1, ql)), _whole((1, kvl)), _rows(ql), _rows(kvl), _rows(LANES)],
        out_specs=[_rows(ql), _rows(kvl), _rows(LANES), _whole((1, ql)), _whole((1, kvl))],
        out_shape=[jax.ShapeDtypeStruct((t, ql), bf16), jax.ShapeDtypeStruct((t, kvl), bf16),
                   jax.ShapeDtypeStruct((t, LANES), bf16), jax.ShapeDtypeStruct((1, ql), f32),
                   jax.ShapeDtypeStruct((1, kvl), f32)],
        compiler_params=_cparams(("arbitrary",)),
    )(h, h, h, cs, gq, gkv, d_qn, d_kvn, d_krd)


def _convln_fn(hc, g, b):
    y = _layer_norm(hc, g, b)
    return y * jax.nn.sigmoid(y)


def _convln_fwd(hconv, g, b):
    t, cw = hconv.shape

    def body(h_ref, g_ref, b_ref, o_ref):
        o_ref[...] = _convln_fn(h_ref[...], g_ref[...], b_ref[...]).astype(bf16)

    return pl.pallas_call(
        body, name="convln_fwd", grid=(t // ROW_TILE,),
        in_specs=[_rows(cw), _whole((1, cw)), _whole((1, cw))], out_specs=_rows(cw),
        out_shape=jax.ShapeDtypeStruct((t, cw), bf16), compiler_params=_cparams(("parallel",)),
    )(hconv, g, b)


def _convln_bwd(hconv, g, b, d_mixed, col_block):
    t, cw = hconv.shape

    def body(h_ref, g_ref, b_ref, dy_ref, dh_ref, dg_ref, db_ref):
        _, vjp = jax.vjp(_convln_fn, h_ref[...], g_ref[...], b_ref[...])
        dh, dg, db = vjp(dy_ref[...])
        dh_ref[...] = dh

        @pl.when(pl.program_id(0) == 0)
        def _():
            dg_ref[...] = jnp.zeros_like(dg_ref)
            db_ref[...] = jnp.zeros_like(db_ref)

        dg_ref[...] += dg
        db_ref[...] += db

    return pl.pallas_call(
        body, name="convln_bwd", grid=(t // ROW_TILE,),
        in_specs=[_rows(cw), _whole((1, cw)), _whole((1, cw)), _rows(cw, col_block)],
        out_specs=[_rows(cw), _whole((1, cw)), _whole((1, cw))],
        out_shape=[jax.ShapeDtypeStruct((t, cw), f32), jax.ShapeDtypeStruct((1, cw), f32),
                   jax.ShapeDtypeStruct((1, cw), f32)],
        compiler_params=_cparams(("arbitrary",)),
    )(hconv, g, b, d_mixed)


SEQ_CT = 128


def _conv_fwd(h, w, b, nb, seq, cw):
    ct, kk = SEQ_CT, w.shape[0]
    ncb = cw // ct

    def body(h_ref, w_ref, b_ref, o_ref):
        blk = h_ref[...]
        a, g = blk[:, :ct], blk[:, ct:]
        hc = a * jax.nn.sigmoid(g)
        acc = jnp.zeros_like(hc)
        for k in range(kk):
            acc = acc + w_ref[k:k + 1, :] * _shift_down_raw(hc, kk - 1 - k)
        o_ref[...] = acc + b_ref[...]

    return pl.pallas_call(
        body, name="conv_fwd", grid=(ncb, nb),
        in_specs=[pl.BlockSpec((seq, 2 * ct), lambda j, bb: (bb, j)), pl.BlockSpec((kk, ct), lambda j, bb: (0, j)),
                  pl.BlockSpec((1, ct), lambda j, bb: (0, j))],
        out_specs=pl.BlockSpec((seq, ct), lambda j, bb: (bb, j)),
        out_shape=jax.ShapeDtypeStruct((nb * seq, cw), f32),
        compiler_params=_cparams(("parallel", "parallel")),
    )(h, w, b)


def _conv_bwd(h, w, d_hconv, nb, seq, cw):
    ct, kk = SEQ_CT, w.shape[0]
    ncb = cw // ct

    def body(h_ref, w_ref, dy_ref, dh_ref, dw_ref, db_ref):
        blk = h_ref[...]
        a, g = blk[:, :ct], blk[:, ct:]
        sg = jax.nn.sigmoid(g)
        hc = a * sg
        dy = dy_ref[...]
        dhc = jnp.zeros_like(hc)

        @pl.when(pl.program_id(1) == 0)
        def _():
            dw_ref[...] = jnp.zeros_like(dw_ref)
            db_ref[...] = jnp.zeros_like(db_ref)

        for k in range(kk):
            dhc = dhc + w_ref[k:k + 1, :] * _shift_up_raw(dy, kk - 1 - k)
            dw_ref[k:k + 1, :] += _colsum(dy * _shift_down_raw(hc, kk - 1 - k))
        db_ref[...] += _colsum(dy)
        dh_ref[:, :ct] = (dhc * sg).astype(bf16)
        dh_ref[:, ct:] = (dhc * a * sg * (1.0 - sg)).astype(bf16)

    return pl.pallas_call(
        body, name="conv_bwd", grid=(ncb, nb),
        in_specs=[pl.BlockSpec((seq, 2 * ct), lambda j, bb: (bb, j)), pl.BlockSpec((kk, ct), lambda j, bb: (0, j)),
                  pl.BlockSpec((seq, ct), lambda j, bb: (bb, j))],
        out_specs=[pl.BlockSpec((seq, 2 * ct), lambda j, bb: (bb, j)), pl.BlockSpec((kk, ct), lambda j, bb: (0, j)),
                   pl.BlockSpec((1, ct), lambda j, bb: (0, j))],
        out_shape=[jax.ShapeDtypeStruct((nb * seq, 2 * cw), bf16), jax.ShapeDtypeStruct((kk, cw), f32),
                   jax.ShapeDtypeStruct((1, cw), f32)],
        compiler_params=_cparams(("parallel", "arbitrary")),
    )(h, w, d_hconv)


def _pool_fn(u, wp, scale, pg):
    seq = u.shape[0]
    t1 = (lax.broadcasted_iota(jnp.int32, (seq, 1), 0) + 1).astype(f32)
    outs = []
    for gi, win in enumerate(POOL_WINDOWS):
        ug = u[:, gi * pg:(gi + 1) * pg]
        acc, span = ug, 1
        while span < win:
            acc = acc + _shift_down(acc, span)
            span *= 2
        d = acc / jnp.minimum(t1, float(win)) - ug
        outs.append(_mm_bf16(d, wp[gi * pg:(gi + 1) * pg, :]) * scale[:, gi * pg:(gi + 1) * pg])
    return outs


def _pool_fwd(h, wp, scale, lay, nb, seq):
    pw, pg = lay["pw"], lay["pg"]

    def body(u_ref, wp_ref, sc_ref, o_ref):
        outs = _pool_fn(u_ref[...], wp_ref[...], sc_ref[...], pg)
        for gi in range(len(POOL_WINDOWS)):
            o_ref[:, gi * pg:(gi + 1) * pg] = outs[gi].astype(bf16)

    return pl.pallas_call(
        body, name="pool_fwd", grid=(nb,),
        in_specs=[pl.BlockSpec((seq, pw), lambda bb: (bb, lay["off_pool"] // pw)), _whole((pw, pg)), _whole((1, pw))],
        out_specs=pl.BlockSpec((seq, pw), lambda bb: (bb, 0)),
        out_shape=jax.ShapeDtypeStruct((nb * seq, pw), bf16),
        compiler_params=_cparams(("parallel",)),
    )(h, wp, scale)


def _pool_bwd(h, wp, scale, d_mixed, col_block, lay, nb, seq):
    pw, pg = lay["pw"], lay["pg"]
    ng = len(POOL_WINDOWS)

    def body(u_ref, wp_ref, sc_ref, dy_ref, du_ref, dwp_ref, dsc_ref):
        _, vjp = jax.vjp(functools.partial(_pool_fn, pg=pg), u_ref[...], wp_ref[...], sc_ref[...])
        dy = dy_ref[...]
        du, dwp, dsc = vjp([dy[:, gi * pg:(gi + 1) * pg] for gi in range(ng)])
        du_ref[...] = du.astype(bf16)

        @pl.when(pl.program_id(0) == 0)
        def _():
            dwp_ref[...] = jnp.zeros_like(dwp_ref)
            dsc_ref[...] = jnp.zeros_like(dsc_ref)

        dwp_ref[...] += dwp
        dsc_ref[...] += dsc

    return pl.pallas_call(
        body, name="pool_bwd", grid=(nb,),
        in_specs=[pl.BlockSpec((seq, pw), lambda bb: (bb, lay["off_pool"] // pw)), _whole((pw, pg)), _whole((1, pw)),
                  pl.BlockSpec((seq, pw), lambda bb: (bb, col_block))],
        out_specs=[pl.BlockSpec((seq, pw), lambda bb: (bb, 0)), _whole((pw, pg)), _whole((1, pw))],
        out_shape=[jax.ShapeDtypeStruct((nb * seq, pw), bf16), jax.ShapeDtypeStruct((pw, pg), f32),
                   jax.ShapeDtypeStruct((1, pw), f32)],
        compiler_params=_cparams(("arbitrary",)),
    )(h, wp, scale, d_mixed)


def _ffn_conv(x, w_ref, b_ref):
    kk = w_ref.shape[0]
    c = b_ref[...] + w_ref[kk - 1:kk, :] * x
    for k in range(kk - 1):
        c = c + w_ref[k:k + 1, :] * _shift_down_raw(x, kk - 1 - k)
    return c


def _ffn_specs(seq, ct, kk, nct):
    return [pl.BlockSpec((seq, ct), lambda j, bb: (bb, j)), pl.BlockSpec((seq, ct), lambda j, bb: (bb, nct + j)),
            pl.BlockSpec((kk, ct), lambda j, bb: (0, j)), pl.BlockSpec((kk, ct), lambda j, bb: (0, nct + j)),
            pl.BlockSpec((1, ct), lambda j, bb: (0, j)), pl.BlockSpec((1, ct), lambda j, bb: (0, nct + j))]


def _ffn_act_fwd(up, w, b, nb, seq, dff):
    ct, kk = SEQ_CT, w.shape[0]

    def body(ua_ref, ug_ref, wa_ref, wg_ref, ba_ref, bg_ref, o_ref):
        a = _ffn_conv(ua_ref[...], wa_ref, ba_ref)
        g = _ffn_conv(ug_ref[...], wg_ref, bg_ref)
        o_ref[...] = (a * g * jax.nn.sigmoid(g)).astype(bf16)

    return pl.pallas_call(
        body, name="ffn_act_fwd", grid=(dff // ct, nb), in_specs=_ffn_specs(seq, ct, kk, dff // ct),
        out_specs=pl.BlockSpec((seq, ct), lambda j, bb: (bb, j)),
        out_shape=jax.ShapeDtypeStruct((nb * seq, dff), bf16),
        compiler_params=_cparams(("parallel", "parallel")),
    )(up, up, w, w, b, b)


def _ffn_act_bwd(up, w, b, d_act, nb, seq, dff):
    ct, kk = SEQ_CT, w.shape[0]

    def body(ua_ref, ug_ref, wa_ref, wg_ref, ba_ref, bg_ref, da_ref, du_ref, dw_ref, db_ref):
        ua, ug = ua_ref[...], ug_ref[...]
        a = _ffn_conv(ua, wa_ref, ba_ref)
        g = _ffn_conv(ug, wg_ref, bg_ref)
        sg = jax.nn.sigmoid(g)
        dact = da_ref[...]

        @pl.when(pl.program_id(1) == 0)
        def _():
            dw_ref[...] = jnp.zeros_like(dw_ref)
            db_ref[...] = jnp.zeros_like(db_ref)

        halves = ((0, ua, wa_ref, dact * g * sg), (1, ug, wg_ref, dact * a * sg * (1.0 + g * (1.0 - sg))))
        for hf, src, w_ref, dc in halves:
            dsrc = w_ref[kk - 1:kk, :] * dc
            dw_ref[hf, kk - 1:kk, :] += _colsum(dc * src)
            for k in range(kk - 1):
                dsrc = dsrc + w_ref[k:k + 1, :] * _shift_up_raw(dc, kk - 1 - k)
                dw_ref[hf, k:k + 1, :] += _colsum(dc * _shift_down_raw(src, kk - 1 - k))
            db_ref[hf] += _colsum(dc)
            du_ref[hf] = dsrc.astype(bf16)

    return pl.pallas_call(
        body, name="ffn_act_bwd", grid=(dff // ct, nb),
        in_specs=_ffn_specs(seq, ct, kk, dff // ct) + [pl.BlockSpec((seq, ct), lambda j, bb: (bb, j))],
        out_specs=[pl.BlockSpec((2, seq, ct), lambda j, bb: (0, bb, j)), pl.BlockSpec((2, kk, ct), lambda j, bb: (0, 0, j)),
                   pl.BlockSpec((2, 1, ct), lambda j, bb: (0, 0, j))],
        out_shape=[jax.ShapeDtypeStruct((2, nb * seq, dff), bf16), jax.ShapeDtypeStruct((2, kk, dff), f32),
                   jax.ShapeDtypeStruct((2, 1, dff), f32)],
        compiler_params=_cparams(("parallel", "arbitrary")),
    )(up, up, w, w, b, b, d_act)


def _scores(q_ref, qt_ref, kn_ref, krd_ref, qblk, tq):
    klen = (qblk + 1) * tq
    q = (q_ref[...] * qt_ref[...]).astype(bf16)
    s = _dot(q[:, :LANES], kn_ref[0:klen, :], _NT) + _dot(q[:, LANES:], krd_ref[0:klen, :], _NT)
    row = qblk * tq + lax.broadcasted_iota(jnp.int32, s.shape, 0)
    col = lax.broadcasted_iota(jnp.int32, s.shape, 1)
    return q, jnp.where(col <= row, s, NEG_INF)


def _per_q_block(nq, fn):
    qi = pl.program_id(2)
    for qblk in range(nq):
        pl.when(qi == qblk)(functools.partial(fn, qblk))


def _attn_fwd(q_ext, qt, kv, krd, nb, seq, heads, comm=None):
    tq = _tile(seq, 512)
    nq = seq // tq

    def body(q_ref, qt_ref, kn_ref, krd_ref, v_ref, o_ref, lse_ref):
        def work(qblk):
            klen = (qblk + 1) * tq
            _, s = _scores(q_ref, qt_ref, kn_ref, krd_ref, qblk, tq)
            m = jnp.max(s, axis=-1, keepdims=True)
            p = jnp.exp(s - m)
            l = jnp.sum(p, axis=-1, keepdims=True)
            o_ref[...] = (_dot(p, v_ref[0:klen, :], _NN) / l).astype(bf16)
            lse_ref[...] = m + jnp.log(l)

        _per_q_block(nq, work)

    (o, lse), comm_outs = _call(
        body, "attn_fwd", (nb, heads, nq),
        [pl.BlockSpec((tq, 2 * LANES), lambda b, h, i: (b * nq + i, h)),
         pl.BlockSpec((tq, 2 * LANES), lambda b, h, i: (b * nq + i, 0)),
         pl.BlockSpec((seq, LANES), lambda b, h, i: (b, 2 * h)),
         pl.BlockSpec((seq, LANES), lambda b, h, i: (b, 0)),
         pl.BlockSpec((seq, LANES), lambda b, h, i: (b, 2 * h + 1))],
        [pl.BlockSpec((tq, LANES), lambda b, h, i: (b * nq + i, h)),
         pl.BlockSpec((None, tq, 1), lambda b, h, i: (b * heads + h, i, 0))],
        [jax.ShapeDtypeStruct((nb * seq, heads * LANES), bf16), jax.ShapeDtypeStruct((nb * heads, seq, 1), f32)],
        (q_ext, qt, kv, krd, kv), (), ("parallel", "parallel", "parallel"), comm)
    return (o, lse) if comm is None else (o, lse, comm_outs)


def _attn_bwd(q_ext, qt, kv, krd, lse, d_mixed, nb, seq, heads, comm=None):
    tq = _tile(seq, 512)
    nq = seq // tq

    def body(q_ref, qt_ref, kn_ref, krd_ref, v_ref, lse_ref, do_ref, dq_ref, dkv_ref, dkrd_ref, dkv_acc):
        h, qi = pl.program_id(1), pl.program_id(2)

        @pl.when(qi == 0)
        def _():
            dkv_acc[...] = jnp.zeros_like(dkv_acc)

        @pl.when((qi == 0) & (h == 0))
        def _():
            dkrd_ref[...] = jnp.zeros_like(dkrd_ref)

        def work(qblk):
            klen = (qblk + 1) * tq
            q, s = _scores(q_ref, qt_ref, kn_ref, krd_ref, qblk, tq)
            p = jnp.exp(s - lse_ref[...])
            do = do_ref[...]
            dp = _dot(do, v_ref[0:klen, :], _NT)
            ds = p * (dp - jnp.sum(p * dp, axis=-1, keepdims=True))
            qt_blk = qt_ref[...]
            dq_ref[:, :LANES] = (_dot(ds, kn_ref[0:klen, :], _NN) * qt_blk[:, :LANES]).astype(bf16)
            dq_ref[:, LANES:] = (_dot(ds, krd_ref[0:klen, :], _NN) * qt_blk[:, LANES:]).astype(bf16)
            dkv_acc[0:klen, :LANES] += _dot(ds, q[:, :LANES], _TN)
            dkv_acc[0:klen, LANES:] += _dot(p, do, _TN)
            dkrd_ref[0:klen, :] += _dot(ds, q[:, LANES:], _TN)

        _per_q_block(nq, work)

        @pl.when(qi == nq - 1)
        def _():
            dkv_ref[...] = dkv_acc[...].astype(bf16)

    (dq, dkv, dkrd), comm_outs = _call(
        body, "attn_bwd", (nb, heads, nq),
        [pl.BlockSpec((tq, 2 * LANES), lambda b, h, i: (b * nq + i, h)),
         pl.BlockSpec((tq, 2 * LANES), lambda b, h, i: (b * nq + i, 0)),
         pl.BlockSpec((seq, LANES), lambda b, h, i: (b, 2 * h)),
         pl.BlockSpec((seq, LANES), lambda b, h, i: (b, 0)),
         pl.BlockSpec((seq, LANES), lambda b, h, i: (b, 2 * h + 1)),
         pl.BlockSpec((None, tq, 1), lambda b, h, i: (b * heads + h, i, 0)),
         pl.BlockSpec((tq, LANES), lambda b, h, i: (b * nq + i, h))],
        [pl.BlockSpec((tq, 2 * LANES), lambda b, h, i: (b * nq + i, h)),
         pl.BlockSpec((seq, 2 * LANES), lambda b, h, i: (b, h)),
         pl.BlockSpec((seq, LANES), lambda b, h, i: (b, 0))],
        [jax.ShapeDtypeStruct((nb * seq, heads * 2 * LANES), bf16),
         jax.ShapeDtypeStruct((nb * seq, heads * 2 * LANES), bf16),
         jax.ShapeDtypeStruct((nb * seq, LANES), f32)],
        (q_ext, qt, kv, krd, kv, lse, d_mixed), [pltpu.VMEM((seq, 2 * LANES), f32)],
        ("parallel", "arbitrary", "arbitrary"), comm)
    return (dq, dkv, dkrd) if comm is None else (dq, dkv, dkrd, comm_outs)


def _mesh_pos():
    x, y, c = lax.axis_index("x"), lax.axis_index("y"), lax.axis_index("c")
    return x, y, c, [(1 - x, y), (x, 1 - y), (1 - x, 1 - y)]


def _all_gather(shards, name):
    n = len(shards)

    def body(*refs):
        x_refs, out_refs, (send_sems, recv_sems, local_sems) = refs[:n], refs[n:2 * n], refs[2 * n:]
        x, y, c, chips = _mesh_pos()
        me, sibling = (x, y, c), (x, y, 1 - c)

        def copy(t, k, block, to, from_shard=False):
            px, py, pc = block
            rows = out_refs[t].at[4 * px + 2 * py + pc]
            return pltpu.make_async_remote_copy(
                src_ref=x_refs[t] if from_shard else rows, dst_ref=rows,
                send_sem=send_sems.at[t, k], recv_sem=recv_sems.at[t, k], device_id=to, device_id_type=MESH)

        mine = [pltpu.make_async_copy(x_refs[t], out_refs[t].at[4 * x + 2 * y + c], local_sems.at[t]) for t in range(n)]
        first = [[copy(t, 0, me, sibling, True)] + [copy(t, 1 + j, me, (*chip, c), True) for j, chip in enumerate(chips)]
                 for t in range(n)]
        passed = [[copy(t, 4 + j, (*chip, c), sibling) for j, chip in enumerate(chips)] for t in range(n)]
        for t in range(n):
            mine[t].start()
            for cp in first[t]:
                cp.start()
        for j, chip in enumerate(chips):
            for t in range(n):
                copy(t, 1 + j, (*chip, c), me).wait_recv()
                passed[t][j].start()
        for t in range(n):
            copy(t, 0, sibling, me).wait_recv()
            for j, chip in enumerate(chips):
                copy(t, 4 + j, (*chip, 1 - c), me).wait_recv()
        for t in range(n):
            for cp in first[t] + passed[t]:
                cp.wait_send()
            mine[t].wait()

    return pl.pallas_call(
        body, name=name, out_shape=[jax.ShapeDtypeStruct((N_DEV,) + s.shape, s.dtype) for s in shards],
        in_specs=[_ANY] * n, out_specs=[_ANY] * n,
        scratch_shapes=[pltpu.SemaphoreType.DMA((n, 7)), pltpu.SemaphoreType.DMA((n, 7)), pltpu.SemaphoreType.DMA((n,))],
    )(*shards)


def _gather_own(shards):
    n = len(shards)

    def remote(x_refs, out_refs, sems, arriving):
        send_sems, recv_sems, _ = sems
        x, y, c, chips = _mesh_pos()
        peers = [(x, y, 1 - c)] + [(*chip, c) for chip in chips]
        return [pltpu.make_async_remote_copy(
            src_ref=x_refs[t], dst_ref=out_refs[t].at[4 * px + 2 * py + pc if arriving else 4 * x + 2 * y + c],
            send_sem=send_sems.at[t, k], recv_sem=recv_sems.at[t, k], device_id=(px, py, pc), device_id_type=MESH)
            for t in range(n) for k, (px, py, pc) in enumerate(peers)]

    def local(x_refs, out_refs, sems):
        x, y, c, _ = _mesh_pos()
        return [pltpu.make_async_copy(x_refs[t], out_refs[t].at[4 * x + 2 * y + c], sems[2].at[t]) for t in range(n)]

    def start(x_refs, out_refs, sems):
        for cp in local(x_refs, out_refs, sems) + remote(x_refs, out_refs, sems, False):
            cp.start()

    def finish(x_refs, out_refs, sems):
        for cp in remote(x_refs, out_refs, sems, True):
            cp.wait_recv()
        for cp in remote(x_refs, out_refs, sems, False):
            cp.wait_send()
        for cp in local(x_refs, out_refs, sems):
            cp.wait()

    return _Comm(shards, [jax.ShapeDtypeStruct((N_DEV,) + s.shape, s.dtype) for s in shards],
                 [pltpu.SemaphoreType.DMA((n, 4)), pltpu.SemaphoreType.DMA((n, 4)), pltpu.SemaphoreType.DMA((n,))],
                 start, finish)


def _gather_pass(gathered):
    n = len(gathered)

    def copies(in_refs, out_refs, sems, arriving):
        send_sems, recv_sems = sems
        x, y, c, chips = _mesh_pos()
        return [pltpu.make_async_remote_copy(
            src_ref=in_refs[t].at[4 * px + 2 * py + c],
            dst_ref=out_refs[t].at[4 * px + 2 * py + (1 - c if arriving else c)],
            send_sem=send_sems.at[t, j], recv_sem=recv_sems.at[t, j], device_id=(x, y, 1 - c), device_id_type=MESH)
            for t in range(n) for j, (px, py) in enumerate(chips)]

    def start(in_refs, out_refs, sems):
        for cp in copies(in_refs, out_refs, sems, False):
            cp.start()

    def finish(in_refs, out_refs, sems):
        for cp in copies(in_refs, out_refs, sems, True):
            cp.wait_recv()
        for cp in copies(in_refs, out_refs, sems, False):
            cp.wait_send()

    return _Comm(gathered, [jax.ShapeDtypeStruct(g.shape, g.dtype) for g in gathered],
                 [pltpu.SemaphoreType.DMA((n, 3)), pltpu.SemaphoreType.DMA((n, 3))], start, finish,
                 aliases={t: t for t in range(n)})


def _sibling_swap(slots):
    n = len(slots)

    def start(g_refs, out_refs, sems):
        send_sems, recv_sems = sems
        x, y, c, _ = _mesh_pos()
        for t in range(n):
            for k in range(4):
                pltpu.make_async_remote_copy(
                    src_ref=g_refs[t].at[2 * k + (1 - c)], dst_ref=out_refs[t].at[k], send_sem=send_sems.at[t],
                    recv_sem=recv_sems.at[t], device_id=(x, y, 1 - c), device_id_type=MESH).start()

    def finish(g_refs, out_refs, sems):
        send_sems, recv_sems = sems
        x, y, c, _ = _mesh_pos()
        for t in range(n):
            pltpu.make_async_remote_copy(
                src_ref=g_refs[t].at[pl.ds(0, 4)], dst_ref=out_refs[t], send_sem=send_sems.at[t],
                recv_sem=recv_sems.at[t], device_id=(x, y, 1 - c), device_id_type=MESH).wait()

    return _Comm(slots, [jax.ShapeDtypeStruct((4,) + s.shape[1:], s.dtype) for s in slots],
                 [pltpu.SemaphoreType.DMA((n,)), pltpu.SemaphoreType.DMA((n,))], start, finish)


def _chip_swap(p4s):
    n = len(p4s)

    def copies(p_refs, out_refs, sems):
        send_sems, recv_sems = sems
        x, y, c, chips = _mesh_pos()
        return [pltpu.make_async_remote_copy(
            src_ref=p_refs[t].at[2 * px + py], dst_ref=out_refs[t].at[j], send_sem=send_sems.at[t, j],
            recv_sem=recv_sems.at[t, j], device_id=(px, py, c), device_id_type=MESH)
            for t in range(n) for j, (px, py) in enumerate(chips)]

    def start(p_refs, out_refs, sems):
        for cp in copies(p_refs, out_refs, sems):
            cp.start()

    def finish(p_refs, out_refs, sems):
        for cp in copies(p_refs, out_refs, sems):
            cp.wait()

    return _Comm(p4s, [jax.ShapeDtypeStruct((3,) + p.shape[1:], p.dtype) for p in p4s],
                 [pltpu.SemaphoreType.DMA((n, 3)), pltpu.SemaphoreType.DMA((n, 3))], start, finish)


def _comm_join(a, b):
    ai, ao, asem = len(a.inputs), len(a.out_shapes), len(a.sems)

    def start(ins, outs, sems):
        a.start(ins[:ai], outs[:ao], sems[:asem])
        b.start(ins[ai:], outs[ao:], sems[asem:])

    def finish(ins, outs, sems):
        a.finish(ins[:ai], outs[:ao], sems[:asem])
        b.finish(ins[ai:], outs[ao:], sems[asem:])

    aliases = dict(a.aliases)
    aliases.update({ai + i: ao + o for i, o in b.aliases.items()})
    return _Comm(a.inputs + b.inputs, a.out_shapes + b.out_shapes, a.sems + b.sems, start, finish, aliases)


def _comm_call(comm, name):
    c_in, c_out = len(comm.inputs), len(comm.out_shapes)

    def body(*refs):
        ins, outs, sems = refs[:c_in], refs[c_in:c_in + c_out], refs[c_in + c_out:]
        comm.start(ins, outs, sems)
        comm.finish(ins, outs, sems)

    return pl.pallas_call(
        body, name=name, out_shape=comm.out_shapes, in_specs=[_ANY] * c_in, out_specs=[_ANY] * c_out,
        scratch_shapes=comm.sems, input_output_aliases=comm.aliases,
    )(*comm.inputs)


def _row_tile(rows, cols, max_bytes=1024 * 1024):
    best = None
    for tr in range(16, rows + 1, 16):
        if rows % tr == 0 and tr * cols * 4 <= max_bytes:
            best = tr
    return best or rows


def _pair_add(slots, theirs, core, name):
    _, rows, cols = slots.shape
    tr = _row_tile(rows, cols)

    def body(c_ref, a_ref, b_ref, o_ref):
        o_ref[...] = (a_ref[...].astype(f32) + b_ref[...].astype(f32)).astype(bf16)

    return pl.pallas_call(
        body, name=name,
        grid_spec=pltpu.PrefetchScalarGridSpec(
            num_scalar_prefetch=1, grid=(4, rows // tr),
            in_specs=[pl.BlockSpec((None, tr, cols), lambda k, i, c_ref: (2 * k + c_ref[0], i, 0)),
                      pl.BlockSpec((None, tr, cols), lambda k, i, c_ref: (k, i, 0))],
            out_specs=pl.BlockSpec((None, tr, cols), lambda k, i, c_ref: (k, i, 0))),
        out_shape=jax.ShapeDtypeStruct((4, rows, cols), bf16), compiler_params=_cparams(("parallel", "parallel")),
    )(core, slots, theirs)


def _adam_update(g, w, m, v):
    c1 = 1.0 / (1.0 - ADAM_B1 ** ADAM_STEP)
    c2 = 1.0 / (1.0 - ADAM_B2 ** ADAM_STEP)
    nm = ADAM_B1 * m + (1.0 - ADAM_B1) * g
    nv = ADAM_B2 * v + (1.0 - ADAM_B2) * jnp.square(g)
    delta = -ADAM_LR * ((nm * c1) / (jnp.sqrt(nv * c2) + ADAM_EPS) + ADAM_WD * w)
    return delta, nm, nv


def _adamw_layers(p4s, chips, chip_idx, w, m, v, name):
    depth = len(p4s)
    _, rows_l, cols = p4s[0].shape
    tr = _row_tile(rows_l, cols)
    nr = rows_l // tr

    def body(idx_ref, *refs):
        p_refs, c_refs = refs[:depth], refs[depth:2 * depth]
        w_ref, m_ref, v_ref, g_ref, d_ref, nm_ref, nv_ref = refs[2 * depth:]
        layer = pl.program_id(0)
        for ll in range(depth):
            @pl.when(layer == ll)
            def _(ll=ll):
                g = p_refs[ll][...].astype(f32)
                for j in range(3):
                    g = g + c_refs[ll][j].astype(f32)
                delta, nm, nv = _adam_update(g, w_ref[...], m_ref[...], v_ref[...])
                g_ref[...] = g
                d_ref[...] = delta
                nm_ref[...] = nm
                nv_ref[...] = nv

    def of_layer(ll):
        return lambda l, i, idx_ref: jnp.where(l == ll, i, 0)

    p_specs = [pl.BlockSpec((None, tr, cols), lambda l, i, idx_ref, f=of_layer(ll): (idx_ref[0], f(l, i, idx_ref), 0))
               for ll in range(depth)]
    c_specs = [pl.BlockSpec((3, tr, cols), lambda l, i, idx_ref, f=of_layer(ll): (0, f(l, i, idx_ref), 0))
               for ll in range(depth)]
    spec = pl.BlockSpec((tr, cols), lambda l, i, idx_ref: (l * nr + i, 0))
    out = jax.ShapeDtypeStruct(w.shape, f32)
    return pl.pallas_call(
        body, name=name,
        grid_spec=pltpu.PrefetchScalarGridSpec(
            num_scalar_prefetch=1, grid=(depth, nr), in_specs=p_specs + c_specs + [spec, spec, spec],
            out_specs=[spec, spec, spec, spec]),
        out_shape=[out, out, out, out], compiler_params=_cparams(("parallel", "parallel")),
    )(chip_idx, *p4s, *chips, w, m, v)


def _adamw(parts, w, m, v, name):
    rows, cols = w.shape
    tr = _row_tile(rows, cols, 512 * 1024)
    nparts = len(parts)

    def body(*refs):
        part_refs, (w_ref, m_ref, v_ref, g_ref, d_ref, nm_ref, nv_ref) = refs[:nparts], refs[nparts:]
        g = None
        for pr in part_refs:
            for s in range(pr.shape[0]):
                term = pr[s].astype(f32)
                g = term if g is None else g + term
        delta, nm, nv = _adam_update(g, w_ref[...], m_ref[...], v_ref[...])
        g_ref[...] = g
        d_ref[...] = delta
        nm_ref[...] = nm
        nv_ref[...] = nv

    spec = pl.BlockSpec((tr, cols), lambda i: (i, 0))
    part_specs = [pl.BlockSpec((p.shape[0], tr, cols), lambda i: (0, i, 0)) for p in parts]
    out = jax.ShapeDtypeStruct((rows, cols), f32)
    return pl.pallas_call(
        body, name=name, grid=(rows // tr,), in_specs=part_specs + [spec, spec, spec],
        out_specs=[spec, spec, spec, spec], out_shape=[out, out, out, out],
        compiler_params=_cparams(("parallel",)),
    )(*parts, w, m, v)


def _gathered_to_full(gathered, axis):
    s = gathered.shape[1:]
    full = jnp.moveaxis(gathered, 0, axis)
    return full.reshape(s[:axis] + (N_DEV * s[axis],) + s[axis + 1:])


def _interleave_halves(w, ct):
    n = w.shape[-1] // 2
    t = w.reshape(w.shape[:-1] + (2, n // ct, ct))
    return jnp.swapaxes(t, -3, -2).reshape(w.shape)


def _deinterleave_halves(w, ct):
    n = w.shape[-1] // 2
    t = w.reshape(w.shape[:-1] + (n // ct, 2, ct))
    return jnp.swapaxes(t, -3, -2).reshape(w.shape)


def _rot_cols(w):
    half = QK_ROPE_DIM // 2
    return jnp.concatenate([-w[..., half:], w[..., :half]], axis=-1)


def _rot_cols_t(dw):
    half = QK_ROPE_DIM // 2
    return jnp.concatenate([dw[..., half:], -dw[..., :half]], axis=-1)


def kernel(x, positions, ln_in_g, ln_in_b, w_in, q_norm_g, w_uq, kv_norm_g, w_ukv, conv_w, conv_b, conv_ln_g, conv_ln_b, w_pool, pool_scale, w_out, ln1_g, ln1_b, w_up, ffn_conv_w, ffn_conv_b, w_down, ln2_g, ln2_b, loss_target, m_ln_in_g, m_ln_in_b, m_w_in, m_q_norm_g, m_w_uq, m_kv_norm_g, m_w_ukv, m_conv_w, m_conv_b, m_conv_ln_g, m_conv_ln_b, m_w_pool, m_pool_scale, m_w_out, m_ln1_g, m_ln1_b, m_w_up, m_ffn_conv_w, m_ffn_conv_b, m_w_down, m_ln2_g, m_ln2_b, v_ln_in_g, v_ln_in_b, v_w_in, v_q_norm_g, v_w_uq, v_kv_norm_g, v_w_ukv, v_conv_w, v_conv_b, v_conv_ln_g, v_conv_ln_b, v_w_pool, v_pool_scale, v_w_out, v_ln1_g, v_ln1_b, v_w_up, v_ffn_conv_w, v_ffn_conv_b, v_w_down, v_ln2_g, v_ln2_b):
    weights = dict(ln_in_g=ln_in_g, ln_in_b=ln_in_b, w_in=w_in, q_norm_g=q_norm_g, w_uq=w_uq, kv_norm_g=kv_norm_g,
                   w_ukv=w_ukv, conv_w=conv_w, conv_b=conv_b, conv_ln_g=conv_ln_g, conv_ln_b=conv_ln_b, w_pool=w_pool,
                   pool_scale=pool_scale, w_out=w_out, ln1_g=ln1_g, ln1_b=ln1_b, w_up=w_up, ffn_conv_w=ffn_conv_w,
                   ffn_conv_b=ffn_conv_b, w_down=w_down, ln2_g=ln2_g, ln2_b=ln2_b)
    mom1 = dict(ln_in_g=m_ln_in_g, ln_in_b=m_ln_in_b, w_in=m_w_in, q_norm_g=m_q_norm_g, w_uq=m_w_uq,
                kv_norm_g=m_kv_norm_g, w_ukv=m_w_ukv, conv_w=m_conv_w, conv_b=m_conv_b, conv_ln_g=m_conv_ln_g,
                conv_ln_b=m_conv_ln_b, w_pool=m_w_pool, pool_scale=m_pool_scale, w_out=m_w_out, ln1_g=m_ln1_g,
                ln1_b=m_ln1_b, w_up=m_w_up, ffn_conv_w=m_ffn_conv_w, ffn_conv_b=m_ffn_conv_b, w_down=m_w_down,
                ln2_g=m_ln2_g, ln2_b=m_ln2_b)
    mom2 = dict(ln_in_g=v_ln_in_g, ln_in_b=v_ln_in_b, w_in=v_w_in, q_norm_g=v_q_norm_g, w_uq=v_w_uq,
                kv_norm_g=v_kv_norm_g, w_ukv=v_w_ukv, conv_w=v_conv_w, conv_b=v_conv_b, conv_ln_g=v_conv_ln_g,
                conv_ln_b=v_conv_ln_b, w_pool=v_w_pool, pool_scale=v_pool_scale, w_out=v_w_out, ln1_g=v_ln1_g,
                ln1_b=v_ln1_b, w_up=v_w_up, ffn_conv_w=v_ffn_conv_w, ffn_conv_b=v_ffn_conv_b, w_down=v_w_down,
                ln2_g=v_ln2_g, ln2_b=v_ln2_b)
    names = list(weights)

    nb, seq, d = x.shape
    t = nb * seq
    depth = w_in.shape[0]
    ql, kvl, cw, pw = q_norm_g.shape[1], kv_norm_g.shape[1], conv_b.shape[1], pool_scale.shape[1]
    pg = w_pool.shape[-1]
    heads = w_uq.shape[2]
    dff = w_down.shape[1] * N_DEV
    alpha = (2.0 * depth) ** 0.25
    scale = float(QK_NOPE_DIM + QK_ROPE_DIM) ** -0.5
    lay = dict(ql=ql, kvl=kvl, pw=pw, pg=pg, off_q=2 * cw, off_pool=2 * cw + ql, off_kv=2 * cw + ql + pw,
               off_kr=2 * cw + ql + pw + kvl)
    o1, o2, o3, o4 = ql, ql + kvl, ql + kvl + QK_ROPE_DIM, ql + kvl + QK_ROPE_DIM + 2 * cw
    my_x, my_y, my_c = lax.axis_index("x"), lax.axis_index("y"), lax.axis_index("c")
    my_dev = 4 * my_x + 2 * my_y + my_c

    big = ("w_in", "w_uq", "w_ukv", "w_out", "w_up", "w_down")
    g_conv, g_ffn = _all_gather([conv_w, ffn_conv_w], "ag_conv_taps")
    conv_w_full, ffn_w_full = _gathered_to_full(g_conv, 2), _gathered_to_full(g_ffn, 2)
    w_pool_2d = w_pool.reshape(depth, pw, pg)

    rest = ("w_in", "w_uq", "w_ukv", "w_out")

    def bf16_shards(l, which):
        return [weights[n][l].astype(bf16) for n in which]

    def layer_weights(gathered):
        g_in, g_uq, g_ukv, g_out, g_up, g_down = (gathered[n] for n in big)
        wi = _gathered_to_full(g_in, 1)
        kr_cols = wi[:, o2:o3]
        w_in_pad = jnp.concatenate([_interleave_halves(wi[:, o3:o4], SEQ_CT), wi[:, :o1], wi[:, o4:], wi[:, o1:o2],
                                    kr_cols, _rot_cols(kr_cols)], axis=-1)
        wq = g_uq.reshape(ql, heads, QK_NOPE_DIM + QK_ROPE_DIM)
        w_uq_ext = jnp.concatenate([wq, _rot_cols(wq[..., QK_NOPE_DIM:])], axis=-1).reshape(ql, heads * 2 * LANES)
        return dict(w_in=w_in_pad, w_uq=w_uq_ext, w_ukv=g_ukv.reshape(kvl, heads * 2 * LANES), w_out=g_out.reshape(-1, d),
                    w_up=g_up, w_down=g_down.reshape(dff, d))

    half = QK_ROPE_DIM // 2
    inv = 1.0 / (ROPE_THETA ** (jnp.arange(0, QK_ROPE_DIM, 2, dtype=f32) / QK_ROPE_DIM))
    inv_lanes = jnp.tile(inv, LANES // half).reshape(1, LANES)
    cs, qt = _rope_tables(positions.reshape(t, 1), inv_lanes, scale)

    x2 = x.reshape(t, d)
    xs, xs_bf = _ln_fwd(x2, None, ln_in_g.reshape(1, d), ln_in_b.reshape(1, d), 1.0, "ln_in_fwd")
    saved = []
    in_pad = o4 + pw + QK_ROPE_DIM
    gathered = dict(zip(big, _all_gather(bf16_shards(0, big), "ag_weights")))
    for l in range(depth):
        wl = layer_weights(gathered)
        nxt = l + 1 < depth
        gathered = {}
        gq, gkv = q_norm_g[l].reshape(1, ql), kv_norm_g[l].reshape(1, kvl)
        h = _matmul(xs_bf, wl["w_in"], "nn", f32, "mm_in", tm=512, tn=in_pad)
        qn, kvn, krd = _prep_fwd(h, cs, gq, gkv, lay)
        q_ext = _matmul(qn, wl["w_uq"], "nn", f32, "mm_uq", tn=2048)
        kv = _matmul(kvn, wl["w_ukv"], "nn", bf16, "mm_ukv", tn=2048)
        if nxt:
            y_mla, lse, (g_down,) = _attn_fwd(q_ext, qt, kv, krd, nb, seq, heads,
                                              comm=_gather_own(bf16_shards(l + 1, ("w_down",))))
        else:
            y_mla, lse = _attn_fwd(q_ext, qt, kv, krd, nb, seq, heads)
        hconv = _conv_fwd(h, conv_w_full[l], conv_b[l].reshape(1, cw), nb, seq, cw)
        y_conv = _convln_fwd(hconv, conv_ln_g[l].reshape(1, cw), conv_ln_b[l].reshape(1, cw))
        y_pool = _pool_fwd(h, w_pool_2d[l], pool_scale[l].reshape(1, pw), lay, nb, seq)
        mixed = jnp.concatenate([y_mla, y_conv, y_pool], axis=-1)
        if nxt:
            y1, (gathered["w_down"],) = _matmul(mixed, wl["w_out"], "nn", f32, "mm_out", comm=_gather_pass([g_down]))
        else:
            y1 = _matmul(mixed, wl["w_out"], "nn", f32, "mm_out")
        x1, x1_bf = _ln_fwd(xs, y1, ln1_g[l].reshape(1, d), ln1_b[l].reshape(1, d), alpha, "ln1_fwd")
        if nxt:
            up, (g_up,) = _matmul_up(x1_bf, wl["w_up"], "mm_up", comm=_gather_own(bf16_shards(l + 1, ("w_up",))))
        else:
            up = _matmul_up(x1_bf, wl["w_up"], "mm_up")
        act = _ffn_act_fwd(up, ffn_w_full[l], ffn_conv_b[l].reshape(1, 2 * dff), nb, seq, dff)
        if nxt:
            y2, (gathered["w_up"], *g_rest) = _matmul(
                act, wl["w_down"], "nn", f32, "mm_down", tk=dff // 2,
                comm=_comm_join(_gather_pass([g_up]), _gather_own(bf16_shards(l + 1, rest))))
            xn, xn_bf, g_rest = _ln_fwd(x1, y2, ln2_g[l].reshape(1, d), ln2_b[l].reshape(1, d), alpha, "ln2_fwd",
                                        comm=_gather_pass(g_rest))
            gathered.update(zip(rest, g_rest))
        else:
            y2 = _matmul(act, wl["w_down"], "nn", f32, "mm_down", tk=dff // 2)
            xn, xn_bf = _ln_fwd(x1, y2, ln2_g[l].reshape(1, d), ln2_b[l].reshape(1, d), alpha, "ln2_fwd")
        saved.append(dict(xs=xs, xs_bf=xs_bf, h=h, qn=qn, kvn=kvn, krd=krd, q_ext=q_ext, kv=kv, lse=lse, hconv=hconv,
                          mixed=mixed, y1=y1, x1=x1, x1_bf=x1_bf, up=up, act=act, y2=y2, wl=wl))
        xs, xs_bf = xn, xn_bf

    d_stream, loss_row = _loss_call(xs, loss_target.reshape(t, d))
    loss = lax.psum(loss_row[0, 0], MESH_AXES)

    gw = {n: [None] * depth for n in names if n not in ("ln_in_g", "ln_in_b")}
    rs_own = {n: [None] * depth for n in big}
    rs_chips = {n: [None] * depth for n in big}
    core_idx = jnp.reshape(my_c, (1,)).astype(jnp.int32)
    chip_idx = jnp.reshape(2 * my_x + my_y, (1,)).astype(jnp.int32)

    def pair_add(l, which, slots, theirs):
        p4s = [_pair_add(slots[n], th, core_idx, "rs_add_" + n) for n, th in zip(which, theirs)]
        for n, p4 in zip(which, p4s):
            rs_own[n][l] = p4
        return p4s

    d_res, d_mm = None, d_stream
    pending = None
    for l in reversed(range(depth)):
        sv = saved[l]
        wl = sv["wl"]
        slots = {}
        gq, gkv = q_norm_g[l].reshape(1, ql), kv_norm_g[l].reshape(1, kvl)
        dz2, dz2_bf, gw["ln2_g"][l], gw["ln2_b"][l] = _ln_bwd(
            d_res, d_mm, sv["x1"], sv["y2"], ln2_g[l].reshape(1, d), ln2_b[l].reshape(1, d), alpha, "ln2_bwd")
        if pending is None:
            dw_down = _matmul(sv["act"], dz2_bf, "tn", bf16, "mm_down_dw", tm=dff // 4)
        else:
            dw_down, theirs = _matmul(sv["act"], dz2_bf, "tn", bf16, "mm_down_dw", tm=dff // 4,
                                      comm=_sibling_swap([pending[1][n] for n in rest]))
            rest_p4s = pair_add(pending[0], rest, pending[1], theirs)
        slots["w_down"] = dw_down.reshape(N_DEV, -1, d)
        d_act, theirs = _matmul(dz2_bf, wl["w_down"], "nt", f32, "mm_down_dx", tn=dff // 4,
                                comm=_sibling_swap([slots["w_down"]]))
        down_p4s = pair_add(l, ("w_down",), slots, theirs)
        d_up, dffw, dffb = _ffn_act_bwd(sv["up"], ffn_w_full[l], ffn_conv_b[l].reshape(1, 2 * dff), d_act, nb, seq, dff)
        gw["ffn_conv_w"][l] = jnp.concatenate([dffw[0], dffw[1]], axis=-1)
        gw["ffn_conv_b"][l] = dffb.reshape(2 * dff)
        d_x1, (rs_chips["w_down"][l],) = _matmul_up_dx(d_up, wl["w_up"], "mm_up_dx", comm=_chip_swap(down_p4s))
        if pending is None:
            slots["w_up"] = _matmul_up_dw(sv["x1_bf"], d_up, N_DEV, "mm_up_dw")
        else:
            slots["w_up"], from_chips = _matmul_up_dw(sv["x1_bf"], d_up, N_DEV, "mm_up_dw", comm=_chip_swap(rest_p4s))
            for n, fc in zip(rest, from_chips):
                rs_chips[n][pending[0]] = fc
        dz1, dz1_bf, gw["ln1_g"][l], gw["ln1_b"][l] = _ln_bwd(
            dz2, d_x1, sv["xs"], sv["y1"], ln1_g[l].reshape(1, d), ln1_b[l].reshape(1, d), alpha, "ln1_bwd")
        d_mixed, theirs = _matmul(dz1_bf, wl["w_out"], "nt", f32, "mm_out_dx", comm=_sibling_swap([slots["w_up"]]))
        up_p4s = pair_add(l, ("w_up",), slots, theirs)
        slots["w_out"] = _matmul(sv["mixed"], dz1_bf, "tn", bf16, "mm_out_dw").reshape(N_DEV, -1, d)
        d_upool, dwp, dps = _pool_bwd(sv["h"], w_pool_2d[l], pool_scale[l].reshape(1, pw), d_mixed,
                                      (heads * LANES + cw) // pw, lay, nb, seq)
        gw["w_pool"][l] = dwp.reshape(w_pool.shape[1:])
        gw["pool_scale"][l] = dps.reshape(pw)
        d_hconv, dclg, dclb = _convln_bwd(sv["hconv"], conv_ln_g[l].reshape(1, cw), conv_ln_b[l].reshape(1, cw), d_mixed,
                                          heads * LANES // cw)
        gw["conv_ln_g"][l], gw["conv_ln_b"][l] = dclg.reshape(cw), dclb.reshape(cw)
        d_conv, gw["conv_w"][l], dcb = _conv_bwd(sv["h"], conv_w_full[l], d_hconv, nb, seq, cw)
        gw["conv_b"][l] = dcb.reshape(cw)
        dq_ext, dkv, dkrd, (rs_chips["w_up"][l],) = _attn_bwd(sv["q_ext"], qt, sv["kv"], sv["krd"], sv["lse"], d_mixed, nb,
                                                              seq, heads, comm=_chip_swap(up_p4s))
        d_qn = _matmul(dq_ext, wl["w_uq"], "nt", f32, "mm_uq_dx")
        dwq = _matmul(sv["qn"], dq_ext, "tn", f32, "mm_uq_dw", tn=2048, tk=1024).reshape(ql, heads, 2 * LANES)
        dwq_rope = dwq[..., QK_NOPE_DIM:QK_NOPE_DIM + QK_ROPE_DIM] + _rot_cols_t(dwq[..., QK_NOPE_DIM + QK_ROPE_DIM:])
        slots["w_uq"] = jnp.concatenate([dwq[..., :QK_NOPE_DIM], dwq_rope], axis=-1).astype(bf16).reshape(
            N_DEV, -1, QK_NOPE_DIM + QK_ROPE_DIM)
        d_kvn = _matmul(dkv, wl["w_ukv"], "nt", f32, "mm_ukv_dx")
        slots["w_ukv"] = _matmul(sv["kvn"], dkv, "tn", bf16, "mm_ukv_dw", tn=2048, tk=1024).reshape(N_DEV, -1, 2 * LANES)
        d_cq, d_ckv, d_kr, dgq, dgkv = _prep_bwd(sv["h"], cs, gq, gkv, d_qn, d_kvn, dkrd, lay)
        gw["q_norm_g"][l], gw["kv_norm_g"][l] = dgq.reshape(ql), dgkv.reshape(kvl)
        d_h = jnp.concatenate([d_conv, d_cq, d_upool, d_ckv, d_kr], axis=-1)
        d_xs = _matmul(d_h, wl["w_in"], "nt", f32, "mm_in_dx", tm=512, tk=in_pad)
        dwi = _matmul(sv["xs_bf"], d_h, "tn", f32, "mm_in_dw", tn=in_pad, tk=1024)
        dkr_cols = dwi[:, lay["off_kr"]:lay["off_kr"] + QK_ROPE_DIM] + _rot_cols_t(dwi[:, lay["off_kr"] + QK_ROPE_DIM:])
        dwi_nat = jnp.concatenate(
            [dwi[:, lay["off_q"]:lay["off_q"] + ql], dwi[:, lay["off_kv"]:lay["off_kv"] + kvl], dkr_cols,
             _deinterleave_halves(dwi[:, :2 * cw], SEQ_CT), dwi[:, lay["off_pool"]:lay["off_pool"] + pw]],
            axis=-1).astype(bf16)
        slots["w_in"] = jnp.moveaxis(dwi_nat.reshape(d, N_DEV, -1), 1, 0)
        pending = (l, slots)
        d_res, d_mm = dz1, d_xs

    theirs = _comm_call(_sibling_swap([pending[1][n] for n in rest]), "rs_sibling")
    from_chips = _comm_call(_chip_swap(pair_add(pending[0], rest, pending[1], theirs)), "rs_chips")
    for n, fc in zip(rest, from_chips):
        rs_chips[n][pending[0]] = fc

    grad_x, _, d_ln_in_g, d_ln_in_b = _ln_bwd(d_res, d_mm, x2, None, ln_in_g.reshape(1, d), ln_in_b.reshape(1, d), alpha,
                                               "ln_in_bwd")
    grad_x = grad_x.reshape(x.shape)

    grads, deltas, new_m, new_v = {}, {}, {}, {}

    def finish(n, parts):
        shp = weights[n].shape
        rows = math.prod(shp[:-1]) if len(shp) > 1 else 1
        as2d = lambda a: a.reshape(rows, shp[-1])
        parts = [p.reshape(p.shape[0], rows, shp[-1]) for p in parts]
        g, dl, nm, nv = _adamw(parts, as2d(weights[n]), as2d(mom1[n]), as2d(mom2[n]), "adamw_" + n)
        grads[n], deltas[n], new_m[n], new_v[n] = (a.reshape(shp) for a in (g, dl, nm, nv))

    for n in big:
        shp = weights[n].shape
        as2d = lambda a: a.reshape(-1, shp[-1])
        g, dl, nm, nv = _adamw_layers(rs_own[n], rs_chips[n], chip_idx, as2d(weights[n]), as2d(mom1[n]), as2d(mom2[n]),
                                      "adamw_" + n)
        grads[n], deltas[n], new_m[n], new_v[n] = (a.reshape(shp) for a in (g, dl, nm, nv))

    small = [n for n in names if n not in ("w_in", "w_uq", "w_ukv", "w_out", "w_up", "w_down")]
    partial = {"ln_in_g": d_ln_in_g.reshape(d), "ln_in_b": d_ln_in_b.reshape(d)}
    for n in small:
        if n not in partial:
            partial[n] = jnp.stack(gw[n])
    flat = jnp.concatenate([partial[n].astype(f32).reshape(-1) for n in small])
    gathered = _all_gather([flat.reshape(-1, LANES)], "ag_small_grads")[0].reshape(N_DEV, -1)
    off = 0
    for n in small:
        size = math.prod(partial[n].shape)
        part = gathered[:, off:off + size].reshape((N_DEV,) + partial[n].shape)
        off += size
        if n in ("conv_w", "ffn_conv_w"):
            width = weights[n].shape[-1]
            part = lax.dynamic_slice_in_dim(part, my_dev * width, width, axis=part.ndim - 1)
        finish(n, [part])

    return (loss, grad_x, *[grads[n] for n in names], *[deltas[n] for n in names], *[new_m[n] for n in names],
            *[new_v[n] for n in names])
```

```python
import functools
import math

import jax
import jax.numpy as jnp
from jax import lax
from jax.experimental import pallas as pl
from jax.experimental.pallas import tpu as pltpu

f32 = jnp.float32
bf16 = jnp.bfloat16

QK_NOPE_DIM = 128
QK_ROPE_DIM = 64
V_HEAD_DIM = 128
CONV_KERNEL = 31
FFN_CONV_KERNEL = 3
POOL_WINDOWS = (2, 4, 8, 16)
ROPE_THETA = 10000.0
LN_EPS = 1e-5
RMS_EPS = 1e-6
ADAM_LR = 0.001
ADAM_B1 = 0.9
ADAM_B2 = 0.999
ADAM_EPS = 1e-08
ADAM_WD = 0.01
ADAM_STEP = 10

N_DEV = 8
MESH_AXES = ("x", "y", "c")
V7X_VMEM_LIMIT_BYTES = 56 * 1024 * 1024
LANES = 128
NEG_INF = -1e30
MESH = pl.DeviceIdType.MESH


def _cparams(sem):
    return pltpu.CompilerParams(dimension_semantics=sem, vmem_limit_bytes=V7X_VMEM_LIMIT_BYTES)


def _tile(dim, pref):
    t = pref
    while t >= LANES:
        if dim % t == 0:
            return t
        t //= 2
    return dim


_ANY = pl.BlockSpec(memory_space=pl.ANY)


class _Comm:
    def __init__(self, inputs, out_shapes, sems, start, finish, aliases=None):
        self.inputs, self.out_shapes, self.sems = list(inputs), list(out_shapes), list(sems)
        self.start, self.finish, self.aliases = start, finish, dict(aliases or {})


def _call(body, name, grid, in_specs, out_specs, out_shape, args, scratch=(), sem=None, comm=None):
    in_specs, out_specs, out_shape, scratch = list(in_specs), list(out_specs), list(out_shape), list(scratch)
    if comm is None:
        outs = pl.pallas_call(body, name=name, grid=grid, in_specs=in_specs, out_specs=out_specs, out_shape=out_shape,
                              scratch_shapes=scratch, compiler_params=_cparams(sem))(*args)
        return list(outs), []
    n_in, n_out, n_scr = len(in_specs), len(out_specs), len(scratch)
    c_in, c_out = len(comm.inputs), len(comm.out_shapes)

    def carrier(*refs):
        refs = list(refs)
        ins, refs = refs[:n_in], refs[n_in:]
        c_ins, refs = refs[:c_in], refs[c_in:]
        outs, refs = refs[:n_out], refs[n_out:]
        c_outs, refs = refs[:c_out], refs[c_out:]
        scr, c_sems = refs[:n_scr], refs[n_scr:]
        ids = [pl.program_id(a) for a in range(len(grid))]
        first = functools.reduce(lambda p, q: p & q, [i == 0 for i in ids])
        last = functools.reduce(lambda p, q: p & q, [i == g - 1 for i, g in zip(ids, grid)])
        pl.when(first)(lambda: comm.start(c_ins, c_outs, c_sems))
        body(*ins, *outs, *scr)
        pl.when(last)(lambda: comm.finish(c_ins, c_outs, c_sems))

    outs = pl.pallas_call(
        carrier, name=name, grid=grid, in_specs=in_specs + [_ANY] * c_in, out_specs=out_specs + [_ANY] * c_out,
        out_shape=out_shape + comm.out_shapes, scratch_shapes=scratch + comm.sems,
        input_output_aliases={n_in + a: n_out + b for a, b in comm.aliases.items()},
        compiler_params=_cparams(("arbitrary",) * len(grid)),
    )(*args, *comm.inputs)
    return list(outs[:n_out]), list(outs[n_out:])


def _shift_down_raw(x, k):
    if k == 0:
        return x
    row = lax.broadcasted_iota(jnp.int32, x.shape, 0)
    return jnp.where(row >= k, pltpu.roll(x, k, axis=0), 0.0)


def _shift_up_raw(x, k):
    if k == 0:
        return x
    n = x.shape[0]
    row = lax.broadcasted_iota(jnp.int32, x.shape, 0)
    return jnp.where(row < n - k, pltpu.roll(x, n - k, axis=0), 0.0)


@functools.partial(jax.custom_vjp, nondiff_argnums=(1,))
def _shift_down(x, k):
    return _shift_down_raw(x, k)


def _shift_down_fwd(x, k):
    return _shift_down_raw(x, k), None


def _shift_down_bwd(k, _, g):
    return (_shift_up_raw(g, k),)


_shift_down.defvjp(_shift_down_fwd, _shift_down_bwd)


@jax.custom_vjp
def _dup_halves(p):
    return p + pltpu.roll(p, LANES // 2, axis=1)


def _dup_halves_fwd(p):
    return p + pltpu.roll(p, LANES // 2, axis=1), None


def _dup_halves_bwd(_, g):
    return (g + pltpu.roll(g, LANES // 2, axis=1),)


_dup_halves.defvjp(_dup_halves_fwd, _dup_halves_bwd)

_NN = (((1,), (0,)), ((), ()))
_NT = (((1,), (1,)), ((), ()))
_TN = (((0,), (0,)), ((), ()))


def _dot(a, b, dims):
    return lax.dot_general(a.astype(bf16), b.astype(bf16), dims, preferred_element_type=f32)


@jax.custom_vjp
def _mm_bf16(a, b):
    return _dot(a, b, _NN)


def _mm_bf16_fwd(a, b):
    return _dot(a, b, _NN), (a, b)


def _mm_bf16_bwd(res, g):
    a, b = res
    return _dot(g, b, _NT), _dot(a, g, _TN)


_mm_bf16.defvjp(_mm_bf16_fwd, _mm_bf16_bwd)


def _layer_norm(z, g, b):
    mu = jnp.mean(z, axis=-1, keepdims=True)
    var = jnp.mean(jnp.square(z - mu), axis=-1, keepdims=True)
    return (z - mu) * lax.rsqrt(var + LN_EPS) * g + b


def _rms_norm(x, g):
    ms = jnp.mean(jnp.square(x), axis=-1, keepdims=True)
    return x * lax.rsqrt(ms + RMS_EPS) * g


def _colsum(x):
    return jnp.sum(x, axis=0, keepdims=True)


def _matmul_core(a, b, mode, grid, a_spec, b_spec, o_spec, o_shape, tile, out_dtype, name, comm=None):
    nk = grid[2]
    dims = {"nn": _NN, "nt": _NT, "tn": _TN}[mode]
    acc_in_out = out_dtype == f32

    def body(a_ref, b_ref, o_ref, *scratch):
        def prod():
            return _dot(a_ref[...], b_ref[...], dims)

        if nk == 1:
            o_ref[...] = prod().astype(out_dtype)
            return
        acc_ref = o_ref if acc_in_out else scratch[0]
        kk = pl.program_id(2)

        @pl.when(kk == 0)
        def _():
            acc_ref[...] = prod()

        if acc_in_out:
            @pl.when(kk > 0)
            def _():
                acc_ref[...] += prod()
        else:
            @pl.when((kk > 0) & (kk < nk - 1))
            def _():
                acc_ref[...] += prod()

            @pl.when(kk == nk - 1)
            def _():
                o_ref[...] = (acc_ref[...] + prod()).astype(out_dtype)

    scratch = [] if (nk == 1 or acc_in_out) else [pltpu.VMEM(tile, f32)]
    (out,), comm_outs = _call(body, name, grid, [a_spec, b_spec], [o_spec], [jax.ShapeDtypeStruct(o_shape, out_dtype)],
                              (a, b), scratch, ("parallel", "parallel", "arbitrary"), comm)
    return out if comm is None else (out, comm_outs)


def _matmul(a, b, mode, out_dtype, name, tm=1024, tn=1024, tk=2048, comm=None):
    if mode == "nn":
        (m, k), (k2, n) = a.shape, b.shape
    elif mode == "nt":
        (m, k), (n, k2) = a.shape, b.shape
    else:
        (k, m), (k2, n) = a.shape, b.shape
    assert k == k2, (name, a.shape, b.shape)
    tm, tn, tk = _tile(m, tm), _tile(n, tn), _tile(k, tk)
    a_spec = pl.BlockSpec((tk, tm), lambda i, j, kk: (kk, i)) if mode == "tn" else pl.BlockSpec((tm, tk), lambda i, j, kk: (i, kk))
    b_spec = pl.BlockSpec((tn, tk), lambda i, j, kk: (j, kk)) if mode == "nt" else pl.BlockSpec((tk, tn), lambda i, j, kk: (kk, j))
    return _matmul_core(a, b, mode, (m // tm, n // tn, k // tk), a_spec, b_spec,
                        pl.BlockSpec((tm, tn), lambda i, j, kk: (i, j)), (m, n), (tm, tn), out_dtype, name, comm)


def _matmul_up(x_bf, w_slots, name, tm=1024, comm=None):
    m, k = x_bf.shape
    s, _, ns = w_slots.shape
    tm = _tile(m, tm)
    return _matmul_core(x_bf, w_slots, "nn", (m // tm, s, 1), pl.BlockSpec((tm, k), lambda i, j, kk: (i, 0)),
                        pl.BlockSpec((None, k, ns), lambda i, j, kk: (j, 0, 0)),
                        pl.BlockSpec((tm, ns), lambda i, j, kk: (i, j)), (m, s * ns), (tm, ns), f32, name, comm)


def _matmul_up_dx(d3, w_slots, name, tm=1024, tn=1024, comm=None):
    _, m, half = d3.shape
    s, n, ns = w_slots.shape
    per_half = half // ns
    assert 2 * per_half == s, (d3.shape, w_slots.shape)
    tm, tn = _tile(m, tm), _tile(n, tn)
    return _matmul_core(d3, w_slots, "nt", (m // tm, n // tn, s),
                        pl.BlockSpec((None, tm, ns), lambda i, j, kk: (kk // per_half, i, kk % per_half)),
                        pl.BlockSpec((None, tn, ns), lambda i, j, kk: (kk, j, 0)),
                        pl.BlockSpec((tm, tn), lambda i, j, kk: (i, j)), (m, n), (tm, tn), f32, name, comm)


def _matmul_up_dw(x_bf, d3, n_slots, name, tm=1024, tk=2048, comm=None):
    k, m = x_bf.shape
    _, _, half = d3.shape
    ns = 2 * half // n_slots
    per_half = n_slots // 2
    tm, tk = _tile(m, tm), _tile(k, tk)
    return _matmul_core(x_bf, d3, "tn", (m // tm, n_slots, k // tk), pl.BlockSpec((tk, tm), lambda i, j, kk: (kk, i)),
                        pl.BlockSpec((None, tk, ns), lambda i, j, kk: (j // per_half, kk, j % per_half)),
                        pl.BlockSpec((None, tm, ns), lambda i, j, kk: (j, i, 0)), (n_slots, m, ns), (tm, ns), bf16, name,
                        comm)


ROW_TILE = 256


def _rows(width, col_block=0):
    return pl.BlockSpec((ROW_TILE, width), lambda i, cb=col_block: (i, cb))


def _whole(shape):
    return pl.BlockSpec(shape, lambda i: (0,) * len(shape))


def _ln_fwd(x, y, g, b, alpha, name, comm=None):
    t, d = x.shape

    def body(*refs):
        if y is None:
            x_ref, g_ref, b_ref, o_ref, ob_ref = refs
            z = x_ref[...]
        else:
            x_ref, y_ref, g_ref, b_ref, o_ref, ob_ref = refs
            z = alpha * x_ref[...] + y_ref[...]
        out = _layer_norm(z, g_ref[...], b_ref[...])
        o_ref[...] = out
        ob_ref[...] = out.astype(bf16)

    ins = [x] + ([] if y is None else [y]) + [g, b]
    specs = [_rows(d)] + ([] if y is None else [_rows(d)]) + [_whole((1, d)), _whole((1, d))]
    (out, out_bf), comm_outs = _call(
        body, name, (t // ROW_TILE,), specs, [_rows(d), _rows(d)],
        [jax.ShapeDtypeStruct((t, d), f32), jax.ShapeDtypeStruct((t, d), bf16)], ins, (), ("parallel",), comm)
    return (out, out_bf) if comm is None else (out, out_bf, comm_outs)


def _ln_bwd(d_res, d_mm, x, y, g, b, alpha, name):
    t, d = x.shape
    has_res, has_mm, has_y = d_res is not None, d_mm is not None, y is not None

    def body(*refs):
        refs = list(refs)
        d_res_ref = refs.pop(0) if has_res else None
        d_mm_ref = refs.pop(0) if has_mm else None
        x_ref = refs.pop(0)
        y_ref = refs.pop(0) if has_y else None
        g_ref, b_ref, dz_ref, dzb_ref, dg_ref, db_ref = refs
        ct = None
        if has_res:
            ct = alpha * d_res_ref[...]
        if has_mm:
            ct = d_mm_ref[...] if ct is None else ct + d_mm_ref[...]
        z = x_ref[...] if not has_y else alpha * x_ref[...] + y_ref[...]
        _, vjp = jax.vjp(_layer_norm, z, g_ref[...], b_ref[...])
        dz, dg, db = vjp(ct)
        dz_ref[...] = dz
        dzb_ref[...] = dz.astype(bf16)

        @pl.when(pl.program_id(0) == 0)
        def _():
            dg_ref[...] = jnp.zeros_like(dg_ref)
            db_ref[...] = jnp.zeros_like(db_ref)

        dg_ref[...] += dg
        db_ref[...] += db

    ins = [a for a in (d_res, d_mm, x, y) if a is not None] + [g, b]
    specs = [_rows(d) for a in (d_res, d_mm, x, y) if a is not None] + [_whole((1, d)), _whole((1, d))]
    return pl.pallas_call(
        body, name=name, grid=(t // ROW_TILE,), in_specs=specs,
        out_specs=[_rows(d), _rows(d), _whole((1, d)), _whole((1, d))],
        out_shape=[jax.ShapeDtypeStruct((t, d), f32), jax.ShapeDtypeStruct((t, d), bf16),
                   jax.ShapeDtypeStruct((1, d), f32), jax.ShapeDtypeStruct((1, d), f32)],
        compiler_params=_cparams(("arbitrary",)),
    )(*ins)


def _loss_call(xf, target):
    t, d = xf.shape

    def body(x_ref, t_ref, dx_ref, loss_ref):
        err = x_ref[...] - t_ref[...]
        dx_ref[...] = err * (1.0 / d)

        @pl.when(pl.program_id(0) == 0)
        def _():
            loss_ref[...] = jnp.zeros_like(loss_ref)

        part = 0.5 * jnp.sum(jnp.mean(jnp.square(err), axis=-1, keepdims=True), axis=0, keepdims=True)
        loss_ref[...] += jnp.broadcast_to(part, loss_ref.shape)

    return pl.pallas_call(
        body, name="loss_head", grid=(t // ROW_TILE,), in_specs=[_rows(d), _rows(d)],
        out_specs=[_rows(d), _whole((1, LANES))],
        out_shape=[jax.ShapeDtypeStruct((t, d), f32), jax.ShapeDtypeStruct((1, LANES), f32)],
        compiler_params=_cparams(("arbitrary",)),
    )(xf, target)


def _rope_tables(pos, inv, scale):
    t = pos.shape[0]

    def body(pos_ref, inv_ref, cs_ref, qt_ref):
        ang = pos_ref[...].astype(f32) * inv_ref[...]
        lane = lax.broadcasted_iota(jnp.int32, ang.shape, 1)
        cs = jnp.where(lane < LANES // 2, jnp.cos(ang), jnp.sin(ang))
        cs_ref[...] = cs
        qt_ref[:, :LANES] = jnp.full((ROW_TILE, LANES), scale, f32)
        qt_ref[:, LANES:] = scale * cs

    return pl.pallas_call(
        body, name="rope_tables", grid=(t // ROW_TILE,),
        in_specs=[pl.BlockSpec((ROW_TILE, 1), lambda i: (i, 0)), _whole((1, LANES))],
        out_specs=[_rows(LANES), _rows(2 * LANES)],
        out_shape=[jax.ShapeDtypeStruct((t, LANES), f32), jax.ShapeDtypeStruct((t, 2 * LANES), f32)],
        compiler_params=_cparams(("parallel",)),
    )(pos, inv)


def _prep_fn(cq, ckv, kr, cs, gq, gkv):
    return _rms_norm(cq, gq), _rms_norm(ckv, gkv), _dup_halves(kr * cs)


def _prep_fwd(h, cs, gq, gkv, lay):
    t = h.shape[0]
    ql, kvl = lay["ql"], lay["kvl"]

    def body(cq_ref, ckv_ref, kr_ref, cs_ref, gq_ref, gkv_ref, qn_ref, kvn_ref, krd_ref):
        qn, kvn, krd = _prep_fn(cq_ref[...], ckv_ref[...], kr_ref[...], cs_ref[...], gq_ref[...], gkv_ref[...])
        qn_ref[...] = qn.astype(bf16)
        kvn_ref[...] = kvn.astype(bf16)
        krd_ref[...] = krd.astype(bf16)

    return pl.pallas_call(
        body, name="prep_fwd", grid=(t // ROW_TILE,),
        in_specs=[_rows(ql, lay["off_q"] // ql), _rows(kvl, lay["off_kv"] // kvl), _rows(LANES, lay["off_kr"] // LANES),
                  _rows(LANES), _whole((1, ql)), _whole((1, kvl))],
        out_specs=[_rows(ql), _rows(kvl), _rows(LANES)],
        out_shape=[jax.ShapeDtypeStruct((t, ql), bf16), jax.ShapeDtypeStruct((t, kvl), bf16),
                   jax.ShapeDtypeStruct((t, LANES), bf16)],
        compiler_params=_cparams(("parallel",)),
    )(h, h, h, cs, gq, gkv)


def _prep_bwd(h, cs, gq, gkv, d_qn, d_kvn, d_krd, lay):
    t = h.shape[0]
    ql, kvl = lay["ql"], lay["kvl"]

    def body(cq_ref, ckv_ref, kr_ref, cs_ref, gq_ref, gkv_ref, dqn_ref, dkvn_ref, dkrd_ref,
             dcq_ref, dckv_ref, dkr_ref, dgq_ref, dgkv_ref):
        _, vjp = jax.vjp(_prep_fn, cq_ref[...], ckv_ref[...], kr_ref[...], cs_ref[...], gq_ref[...], gkv_ref[...])
        dcq, dckv, dkr, _, dgq, dgkv = vjp((dqn_ref[...], dkvn_ref[...], dkrd_ref[...].astype(f32)))
        dcq_ref[...] = dcq.astype(bf16)
        dckv_ref[...] = dckv.astype(bf16)
        dkr_ref[...] = dkr.astype(bf16)

        @pl.when(pl.program_id(0) == 0)
        def _():
            dgq_ref[...] = jnp.zeros_like(dgq_ref)
            dgkv_ref[...] = jnp.zeros_like(dgkv_ref)

        dgq_ref[...] += dgq
        dgkv_ref[...] += dgkv

    return pl.pallas_call(
        body, name="prep_bwd", grid=(t // ROW_TILE,),
        in_specs=[_rows(ql, lay["off_q"] // ql), _rows(kvl, lay["off_kv"] // kvl), _rows(LANES, lay["off_kr"] // LANES),
                  _rows(LANES), _whole((1, ql)), _whole((1, kvl)), _rows(ql), _rows(kvl), _rows(LANES)],
        out_specs=[_rows(ql), _rows(kvl), _rows(LANES), _whole((1, ql)), _whole((1, kvl))],
        out_shape=[jax.ShapeDtypeStruct((t, ql), bf16), jax.ShapeDtypeStruct((t, kvl), bf16),
                   jax.ShapeDtypeStruct((t, LANES), bf16), jax.ShapeDtypeStruct((1, ql), f32),
                   jax.ShapeDtypeStruct((1, kvl), f32)],
        compiler_params=_cparams(("arbitrary",)),
    )(h, h, h, cs, gq, gkv, d_qn, d_kvn, d_krd)


def _convln_fn(hc, g, b):
    y = _layer_norm(hc, g, b)
    return y * jax.nn.sigmoid(y)


def _convln_fwd(hconv, g, b):
    t, cw = hconv.shape

    def body(h_ref, g_ref, b_ref, o_ref):
        o_ref[...] = _convln_fn(h_ref[...], g_ref[...], b_ref[...]).astype(bf16)

    return pl.pallas_call(
        body, name="convln_fwd", grid=(t // ROW_TILE,),
        in_specs=[_rows(cw), _whole((1, cw)), _whole((1, cw))], out_specs=_rows(cw),
        out_shape=jax.ShapeDtypeStruct((t, cw), bf16), compiler_params=_cparams(("parallel",)),
    )(hconv, g, b)


def _convln_bwd(hconv, g, b, d_mixed, col_block):
    t, cw = hconv.shape

    def body(h_ref, g_ref, b_ref, dy_ref, dh_ref, dg_ref, db_ref):
        _, vjp = jax.vjp(_convln_fn, h_ref[...], g_ref[...], b_ref[...])
        dh, dg, db = vjp(dy_ref[...])
        dh_ref[...] = dh

        @pl.when(pl.program_id(0) == 0)
        def _():
            dg_ref[...] = jnp.zeros_like(dg_ref)
            db_ref[...] = jnp.zeros_like(db_ref)

        dg_ref[...] += dg
        db_ref[...] += db

    return pl.pallas_call(
        body, name="convln_bwd", grid=(t // ROW_TILE,),
        in_specs=[_rows(cw), _whole((1, cw)), _whole((1, cw)), _rows(cw, col_block)],
        out_specs=[_rows(cw), _whole((1, cw)), _whole((1, cw))],
        out_shape=[jax.ShapeDtypeStruct((t, cw), f32), jax.ShapeDtypeStruct((1, cw), f32),
                   jax.ShapeDtypeStruct((1, cw), f32)],
        compiler_params=_cparams(("arbitrary",)),
    )(hconv, g, b, d_mixed)


SEQ_CT = 128


def _conv_fwd(h, w, b, nb, seq, cw):
    ct, kk = SEQ_CT, w.shape[0]
    ncb = cw // ct

    def body(h_ref, w_ref, b_ref, o_ref):
        blk = h_ref[...]
        a, g = blk[:, :ct], blk[:, ct:]
        hc = a * jax.nn.sigmoid(g)
        acc = jnp.zeros_like(hc)
        for k in range(kk):
            acc = acc + w_ref[k:k + 1, :] * _shift_down_raw(hc, kk - 1 - k)
        o_ref[...] = acc + b_ref[...]

    return pl.pallas_call(
        body, name="conv_fwd", grid=(ncb, nb),
        in_specs=[pl.BlockSpec((seq, 2 * ct), lambda j, bb: (bb, j)), pl.BlockSpec((kk, ct), lambda j, bb: (0, j)),
                  pl.BlockSpec((1, ct), lambda j, bb: (0, j))],
        out_specs=pl.BlockSpec((seq, ct), lambda j, bb: (bb, j)),
        out_shape=jax.ShapeDtypeStruct((nb * seq, cw), f32),
        compiler_params=_cparams(("parallel", "parallel")),
    )(h, w, b)


def _conv_bwd(h, w, d_hconv, nb, seq, cw):
    ct, kk = SEQ_CT, w.shape[0]
    ncb = cw // ct

    def body(h_ref, w_ref, dy_ref, dh_ref, dw_ref, db_ref):
        blk = h_ref[...]
        a, g = blk[:, :ct], blk[:, ct:]
        sg = jax.nn.sigmoid(g)
        hc = a * sg
        dy = dy_ref[...]
        dhc = jnp.zeros_like(hc)

        @pl.when(pl.program_id(1) == 0)
        def _():
            dw_ref[...] = jnp.zeros_like(dw_ref)
            db_ref[...] = jnp.zeros_like(db_ref)

        for k in range(kk):
            dhc = dhc + w_ref[k:k + 1, :] * _shift_up_raw(dy, kk - 1 - k)
            dw_ref[k:k + 1, :] += _colsum(dy * _shift_down_raw(hc, kk - 1 - k))
        db_ref[...] += _colsum(dy)
        dh_ref[:, :ct] = (dhc * sg).astype(bf16)
        dh_ref[:, ct:] = (dhc * a * sg * (1.0 - sg)).astype(bf16)

    return pl.pallas_call(
        body, name="conv_bwd", grid=(ncb, nb),
        in_specs=[pl.BlockSpec((seq, 2 * ct), lambda j, bb: (bb, j)), pl.BlockSpec((kk, ct), lambda j, bb: (0, j)),
                  pl.BlockSpec((seq, ct), lambda j, bb: (bb, j))],
        out_specs=[pl.BlockSpec((seq, 2 * ct), lambda j, bb: (bb, j)), pl.BlockSpec((kk, ct), lambda j, bb: (0, j)),
                   pl.BlockSpec((1, ct), lambda j, bb: (0, j))],
        out_shape=[jax.ShapeDtypeStruct((nb * seq, 2 * cw), bf16), jax.ShapeDtypeStruct((kk, cw), f32),
                   jax.ShapeDtypeStruct((1, cw), f32)],
        compiler_params=_cparams(("parallel", "arbitrary")),
    )(h, w, d_hconv)


def _pool_fn(u, wp, scale, pg):
    seq = u.shape[0]
    t1 = (lax.broadcasted_iota(jnp.int32, (seq, 1), 0) + 1).astype(f32)
    outs = []
    for gi, win in enumerate(POOL_WINDOWS):
        ug = u[:, gi * pg:(gi + 1) * pg]
        acc, span = ug, 1
        while span < win:
            acc = acc + _shift_down(acc, span)
            span *= 2
        d = acc / jnp.minimum(t1, float(win)) - ug
        outs.append(_mm_bf16(d, wp[gi * pg:(gi + 1) * pg, :]) * scale[:, gi * pg:(gi + 1) * pg])
    return outs


def _pool_fwd(h, wp, scale, lay, nb, seq):
    pw, pg = lay["pw"], lay["pg"]

    def body(u_ref, wp_ref, sc_ref, o_ref):
        outs = _pool_fn(u_ref[...], wp_ref[...], sc_ref[...], pg)
        for gi in range(len(POOL_WINDOWS)):
            o_ref[:, gi * pg:(gi + 1) * pg] = outs[gi].astype(bf16)

    return pl.pallas_call(
        body, name="pool_fwd", grid=(nb,),
        in_specs=[pl.BlockSpec((seq, pw), lambda bb: (bb, lay["off_pool"] // pw)), _whole((pw, pg)), _whole((1, pw))],
        out_specs=pl.BlockSpec((seq, pw), lambda bb: (bb, 0)),
        out_shape=jax.ShapeDtypeStruct((nb * seq, pw), bf16),
        compiler_params=_cparams(("parallel",)),
    )(h, wp, scale)


def _pool_bwd(h, wp, scale, d_mixed, col_block, lay, nb, seq):
    pw, pg = lay["pw"], lay["pg"]
    ng = len(POOL_WINDOWS)

    def body(u_ref, wp_ref, sc_ref, dy_ref, du_ref, dwp_ref, dsc_ref):
        _, vjp = jax.vjp(functools.partial(_pool_fn, pg=pg), u_ref[...], wp_ref[...], sc_ref[...])
        dy = dy_ref[...]
        du, dwp, dsc = vjp([dy[:, gi * pg:(gi + 1) * pg] for gi in range(ng)])
        du_ref[...] = du.astype(bf16)

        @pl.when(pl.program_id(0) == 0)
        def _():
            dwp_ref[...] = jnp.zeros_like(dwp_ref)
            dsc_ref[...] = jnp.zeros_like(dsc_ref)

        dwp_ref[...] += dwp
        dsc_ref[...] += dsc

    return pl.pallas_call(
        body, name="pool_bwd", grid=(nb,),
        in_specs=[pl.BlockSpec((seq, pw), lambda bb: (bb, lay["off_pool"] // pw)), _whole((pw, pg)), _whole((1, pw)),
                  pl.BlockSpec((seq, pw), lambda bb: (bb, col_block))],
        out_specs=[pl.BlockSpec((seq, pw), lambda bb: (bb, 0)), _whole((pw, pg)), _whole((1, pw))],
        out_shape=[jax.ShapeDtypeStruct((nb * seq, pw), bf16), jax.ShapeDtypeStruct((pw, pg), f32),
                   jax.ShapeDtypeStruct((1, pw), f32)],
        compiler_params=_cparams(("arbitrary",)),
    )(h, wp, scale, d_mixed)


FFN_ROWS = 32
SUBLANES = 8


def _rows_before(ref, r0, s, n):
    if s == 0:
        return ref[r0:r0 + n, :]
    if r0 == 0:
        x = ref[0:n, :]
        row = lax.broadcasted_iota(jnp.int32, x.shape, 0)
        return jnp.where(row >= s, pltpu.roll(x, s, axis=0), 0.0)
    return ref[pl.ds(r0 - s, n), :]


def _ffn_conv_rows(x_ref, w_ref, b_ref, r0, n):
    kk = w_ref.shape[0]
    xs = [_rows_before(x_ref, r0, s, n) for s in range(kk)]
    c = b_ref[...] + w_ref[kk - 1:kk, :] * xs[0]
    for k in range(kk - 1):
        c = c + w_ref[k:k + 1, :] * xs[kk - 1 - k]
    return c, xs


def _fold_rows(x):
    out = x[0:SUBLANES]
    for i in range(1, x.shape[0] // SUBLANES):
        out = out + x[i * SUBLANES:(i + 1) * SUBLANES]
    return out


def _ffn_specs(seq, ct, kk, nct):
    return [pl.BlockSpec((seq, ct), lambda j, bb: (bb, j)), pl.BlockSpec((seq, ct), lambda j, bb: (bb, nct + j)),
            pl.BlockSpec((kk, ct), lambda j, bb: (0, j)), pl.BlockSpec((kk, ct), lambda j, bb: (0, nct + j)),
            pl.BlockSpec((1, ct), lambda j, bb: (0, j)), pl.BlockSpec((1, ct), lambda j, bb: (0, nct + j))]


def _ffn_act_fwd(up, w, b, nb, seq, dff, comm=None):
    ct, kk = SEQ_CT, w.shape[0]
    rows = min(FFN_ROWS, seq)

    def body(ua_ref, ug_ref, wa_ref, wg_ref, ba_ref, bg_ref, o_ref):
        for r0 in range(0, seq, rows):
            a, _ = _ffn_conv_rows(ua_ref, wa_ref, ba_ref, r0, rows)
            g, _ = _ffn_conv_rows(ug_ref, wg_ref, bg_ref, r0, rows)
            o_ref[r0:r0 + rows, :] = (a * g * jax.nn.sigmoid(g)).astype(bf16)

    (act,), comm_outs = _call(body, "ffn_act_fwd", (dff // ct, nb), _ffn_specs(seq, ct, kk, dff // ct),
                              [pl.BlockSpec((seq, ct), lambda j, bb: (bb, j))],
                              [jax.ShapeDtypeStruct((nb * seq, dff), bf16)], (up, up, w, w, b, b), (),
                              ("parallel", "parallel"), comm)
    return act if comm is None else (act, comm_outs)


def _ffn_act_bwd(up, w, b, d_act, nb, seq, dff, comm=None):
    ct, kk = SEQ_CT, w.shape[0]
    rows = min(FFN_ROWS, seq)

    def body(ua_ref, ug_ref, wa_ref, wg_ref, ba_ref, bg_ref, da_ref, du_ref, dw_ref, db_ref, dc_ref):
        @pl.when(pl.program_id(1) == 0)
        def _():
            dw_ref[...] = jnp.zeros_like(dw_ref)
            db_ref[...] = jnp.zeros_like(db_ref)

        dw_acc = [[jnp.zeros((SUBLANES, ct), f32) for _ in range(kk)] for _ in range(2)]
        db_acc = [jnp.zeros((SUBLANES, ct), f32) for _ in range(2)]
        for r0 in range(0, seq, rows):
            a, xa = _ffn_conv_rows(ua_ref, wa_ref, ba_ref, r0, rows)
            g, xg = _ffn_conv_rows(ug_ref, wg_ref, bg_ref, r0, rows)
            sg = jax.nn.sigmoid(g)
            dact = da_ref[r0:r0 + rows, :]
            dcs = (dact * g * sg, dact * a * sg * (1.0 + g * (1.0 - sg)))
            for hf, (dc, xs) in enumerate(zip(dcs, (xa, xg))):
                dc_ref[hf, r0:r0 + rows, :] = dc
                db_acc[hf] = db_acc[hf] + _fold_rows(dc)
                for k in range(kk):
                    dw_acc[hf][k] = dw_acc[hf][k] + _fold_rows(dc * xs[kk - 1 - k])
        for hf in range(2):
            dc_ref[hf, seq:seq + SUBLANES, :] = jnp.zeros((SUBLANES, ct), f32)
            db_ref[hf] += _colsum(db_acc[hf])
            for k in range(kk):
                dw_ref[hf, k:k + 1, :] += _colsum(dw_acc[hf][k])
        for r0 in range(0, seq, rows):
            for hf, w_ref in enumerate((wa_ref, wg_ref)):
                dsrc = w_ref[kk - 1:kk, :] * dc_ref[hf, r0:r0 + rows, :]
                for k in range(kk - 1):
                    dsrc = dsrc + w_ref[k:k + 1, :] * dc_ref[hf, pl.ds(r0 + kk - 1 - k, rows), :]
                du_ref[hf, r0:r0 + rows, :] = dsrc.astype(bf16)

    (d_up, dw, db), comm_outs = _call(
        body, "ffn_act_bwd", (dff // ct, nb),
        _ffn_specs(seq, ct, kk, dff // ct) + [pl.BlockSpec((seq, ct), lambda j, bb: (bb, j))],
        [pl.BlockSpec((2, seq, ct), lambda j, bb: (0, bb, j)), pl.BlockSpec((2, kk, ct), lambda j, bb: (0, 0, j)),
         pl.BlockSpec((2, 1, ct), lambda j, bb: (0, 0, j))],
        [jax.ShapeDtypeStruct((2, nb * seq, dff), bf16), jax.ShapeDtypeStruct((2, kk, dff), f32),
         jax.ShapeDtypeStruct((2, 1, dff), f32)],
        (up, up, w, w, b, b, d_act), [pltpu.VMEM((2, seq + SUBLANES, ct), f32)], ("parallel", "arbitrary"), comm)
    return (d_up, dw, db) if comm is None else (d_up, dw, db, comm_outs)


def _scores(q_ref, qt_ref, kn_ref, krd_ref, qblk, tq):
    klen = (qblk + 1) * tq
    q = (q_ref[...] * qt_ref[...]).astype(bf16)
    s = _dot(q[:, :LANES], kn_ref[0:klen, :], _NT) + _dot(q[:, LANES:], krd_ref[0:klen, :], _NT)
    row = qblk * tq + lax.broadcasted_iota(jnp.int32, s.shape, 0)
    col = lax.broadcasted_iota(jnp.int32, s.shape, 1)
    return q, jnp.where(col <= row, s, NEG_INF)


def _per_q_block(nq, fn):
    qi = pl.program_id(2)
    for qblk in range(nq):
        pl.when(qi == qblk)(functools.partial(fn, qblk))


def _attn_fwd(q_ext, qt, kv, krd, nb, seq, heads, comm=None):
    tq = _tile(seq, 512)
    nq = seq // tq

    def body(q_ref, qt_ref, kn_ref, krd_ref, v_ref, o_ref, lse_ref):
        def work(qblk):
            klen = (qblk + 1) * tq
            _, s = _scores(q_ref, qt_ref, kn_ref, krd_ref, qblk, tq)
            m = jnp.max(s, axis=-1, keepdims=True)
            p = jnp.exp(s - m)
            l = jnp.sum(p, axis=-1, keepdims=True)
            o_ref[...] = (_dot(p, v_ref[0:klen, :], _NN) / l).astype(bf16)
            lse_ref[...] = m + jnp.log(l)

        _per_q_block(nq, work)

    (o, lse), comm_outs = _call(
        body, "attn_fwd", (nb, heads, nq),
        [pl.BlockSpec((tq, 2 * LANES), lambda b, h, i: (b * nq + i, h)),
         pl.BlockSpec((tq, 2 * LANES), lambda b, h, i: (b * nq + i, 0)),
         pl.BlockSpec((seq, LANES), lambda b, h, i: (b, 2 * h)),
         pl.BlockSpec((seq, LANES), lambda b, h, i: (b, 0)),
         pl.BlockSpec((seq, LANES), lambda b, h, i: (b, 2 * h + 1))],
        [pl.BlockSpec((tq, LANES), lambda b, h, i: (b * nq + i, h)),
         pl.BlockSpec((None, tq, 1), lambda b, h, i: (b * heads + h, i, 0))],
        [jax.ShapeDtypeStruct((nb * seq, heads * LANES), bf16), jax.ShapeDtypeStruct((nb * heads, seq, 1), f32)],
        (q_ext, qt, kv, krd, kv), (), ("parallel", "parallel", "parallel"), comm)
    return (o, lse) if comm is None else (o, lse, comm_outs)


def _attn_bwd(q_ext, qt, kv, krd, lse, d_mixed, nb, seq, heads, comm=None):
    tq = _tile(seq, 512)
    nq = seq // tq

    def body(q_ref, qt_ref, kn_ref, krd_ref, v_ref, lse_ref, do_ref, dq_ref, dkv_ref, dkrd_ref, dkv_acc):
        h, qi = pl.program_id(1), pl.program_id(2)

        @pl.when(qi == 0)
        def _():
            dkv_acc[...] = jnp.zeros_like(dkv_acc)

        @pl.when((qi == 0) & (h == 0))
        def _():
            dkrd_ref[...] = jnp.zeros_like(dkrd_ref)

        def work(qblk):
            klen = (qblk + 1) * tq
            q, s = _scores(q_ref, qt_ref, kn_ref, krd_ref, qblk, tq)
            p = jnp.exp(s - lse_ref[...])
            do = do_ref[...]
            dp = _dot(do, v_ref[0:klen, :], _NT)
            ds = p * (dp - jnp.sum(p * dp, axis=-1, keepdims=True))
            qt_blk = qt_ref[...]
            dq_ref[:, :LANES] = (_dot(ds, kn_ref[0:klen, :], _NN) * qt_blk[:, :LANES]).astype(bf16)
            dq_ref[:, LANES:] = (_dot(ds, krd_ref[0:klen, :], _NN) * qt_blk[:, LANES:]).astype(bf16)
            dkv_acc[0:klen, :LANES] += _dot(ds, q[:, :LANES], _TN)
            dkv_acc[0:klen, LANES:] += _dot(p, do, _TN)
            dkrd_ref[0:klen, :] += _dot(ds, q[:, LANES:], _TN)

        _per_q_block(nq, work)

        @pl.when(qi == nq - 1)
        def _():
            dkv_ref[...] = dkv_acc[...].astype(bf16)

    (dq, dkv, dkrd), comm_outs = _call(
        body, "attn_bwd", (nb, heads, nq),
        [pl.BlockSpec((tq, 2 * LANES), lambda b, h, i: (b * nq + i, h)),
         pl.BlockSpec((tq, 2 * LANES), lambda b, h, i: (b * nq + i, 0)),
         pl.BlockSpec((seq, LANES), lambda b, h, i: (b, 2 * h)),
         pl.BlockSpec((seq, LANES), lambda b, h, i: (b, 0)),
         pl.BlockSpec((seq, LANES), lambda b, h, i: (b, 2 * h + 1)),
         pl.BlockSpec((None, tq, 1), lambda b, h, i: (b * heads + h, i, 0)),
         pl.BlockSpec((tq, LANES), lambda b, h, i: (b * nq + i, h))],
        [pl.BlockSpec((tq, 2 * LANES), lambda b, h, i: (b * nq + i, h)),
         pl.BlockSpec((seq, 2 * LANES), lambda b, h, i: (b, h)),
         pl.BlockSpec((seq, LANES), lambda b, h, i: (b, 0))],
        [jax.ShapeDtypeStruct((nb * seq, heads * 2 * LANES), bf16),
         jax.ShapeDtypeStruct((nb * seq, heads * 2 * LANES), bf16),
         jax.ShapeDtypeStruct((nb * seq, LANES), f32)],
        (q_ext, qt, kv, krd, kv, lse, d_mixed), [pltpu.VMEM((seq, 2 * LANES), f32)],
        ("parallel", "arbitrary", "arbitrary"), comm)
    return (dq, dkv, dkrd) if comm is None else (dq, dkv, dkrd, comm_outs)


def _mesh_pos():
    x, y, c = lax.axis_index("x"), lax.axis_index("y"), lax.axis_index("c")
    return x, y, c, [(1 - x, y), (x, 1 - y), (1 - x, 1 - y)]


def _all_gather(shards, name):
    n = len(shards)

    def body(*refs):
        x_refs, out_refs, (send_sems, recv_sems, local_sems) = refs[:n], refs[n:2 * n], refs[2 * n:]
        x, y, c, chips = _mesh_pos()
        me, sibling = (x, y, c), (x, y, 1 - c)

        def copy(t, k, block, to, from_shard=False):
            px, py, pc = block
            rows = out_refs[t].at[4 * px + 2 * py + pc]
            return pltpu.make_async_remote_copy(
                src_ref=x_refs[t] if from_shard else rows, dst_ref=rows,
                send_sem=send_sems.at[t, k], recv_sem=recv_sems.at[t, k], device_id=to, device_id_type=MESH)

        mine = [pltpu.make_async_copy(x_refs[t], out_refs[t].at[4 * x + 2 * y + c], local_sems.at[t]) for t in range(n)]
        first = [[copy(t, 0, me, sibling, True)] + [copy(t, 1 + j, me, (*chip, c), True) for j, chip in enumerate(chips)]
                 for t in range(n)]
        passed = [[copy(t, 4 + j, (*chip, c), sibling) for j, chip in enumerate(chips)] for t in range(n)]
        for t in range(n):
            mine[t].start()
            for cp in first[t]:
                cp.start()
        for j, chip in enumerate(chips):
            for t in range(n):
                copy(t, 1 + j, (*chip, c), me).wait_recv()
                passed[t][j].start()
        for t in range(n):
            copy(t, 0, sibling, me).wait_recv()
            for j, chip in enumerate(chips):
                copy(t, 4 + j, (*chip, 1 - c), me).wait_recv()
        for t in range(n):
            for cp in first[t] + passed[t]:
                cp.wait_send()
            mine[t].wait()

    return pl.pallas_call(
        body, name=name, out_shape=[jax.ShapeDtypeStruct((N_DEV,) + s.shape, s.dtype) for s in shards],
        in_specs=[_ANY] * n, out_specs=[_ANY] * n,
        scratch_shapes=[pltpu.SemaphoreType.DMA((n, 7)), pltpu.SemaphoreType.DMA((n, 7)), pltpu.SemaphoreType.DMA((n,))],
    )(*shards)


def _gather_own(shards):
    n = len(shards)

    def remote(x_refs, out_refs, sems, arriving):
        send_sems, recv_sems, _ = sems
        x, y, c, chips = _mesh_pos()
        peers = [(x, y, 1 - c)] + [(*chip, c) for chip in chips]
        return [pltpu.make_async_remote_copy(
            src_ref=x_refs[t], dst_ref=out_refs[t].at[4 * px + 2 * py + pc if arriving else 4 * x + 2 * y + c],
            send_sem=send_sems.at[t, k], recv_sem=recv_sems.at[t, k], device_id=(px, py, pc), device_id_type=MESH)
            for t in range(n) for k, (px, py, pc) in enumerate(peers)]

    def local(x_refs, out_refs, sems):
        x, y, c, _ = _mesh_pos()
        return [pltpu.make_async_copy(x_refs[t], out_refs[t].at[4 * x + 2 * y + c], sems[2].at[t]) for t in range(n)]

    def start(x_refs, out_refs, sems):
        for cp in local(x_refs, out_refs, sems) + remote(x_refs, out_refs, sems, False):
            cp.start()

    def finish(x_refs, out_refs, sems):
        for cp in remote(x_refs, out_refs, sems, True):
            cp.wait_recv()
        for cp in remote(x_refs, out_refs, sems, False):
            cp.wait_send()
        for cp in local(x_refs, out_refs, sems):
            cp.wait()

    return _Comm(shards, [jax.ShapeDtypeStruct((N_DEV,) + s.shape, s.dtype) for s in shards],
                 [pltpu.SemaphoreType.DMA((n, 4)), pltpu.SemaphoreType.DMA((n, 4)), pltpu.SemaphoreType.DMA((n,))],
                 start, finish)


def _gather_pass(gathered):
    n = len(gathered)

    def copies(in_refs, out_refs, sems, arriving):
        send_sems, recv_sems = sems
        x, y, c, chips = _mesh_pos()
        return [pltpu.make_async_remote_copy(
            src_ref=in_refs[t].at[4 * px + 2 * py + c],
            dst_ref=out_refs[t].at[4 * px + 2 * py + (1 - c if arriving else c)],
            send_sem=send_sems.at[t, j], recv_sem=recv_sems.at[t, j], device_id=(x, y, 1 - c), device_id_type=MESH)
            for t in range(n) for j, (px, py) in enumerate(chips)]

    def start(in_refs, out_refs, sems):
        for cp in copies(in_refs, out_refs, sems, False):
            cp.start()

    def finish(in_refs, out_refs, sems):
        for cp in copies(in_refs, out_refs, sems, True):
            cp.wait_recv()
        for cp in copies(in_refs, out_refs, sems, False):
            cp.wait_send()

    return _Comm(gathered, [jax.ShapeDtypeStruct(g.shape, g.dtype) for g in gathered],
                 [pltpu.SemaphoreType.DMA((n, 3)), pltpu.SemaphoreType.DMA((n, 3))], start, finish,
                 aliases={t: t for t in range(n)})


def _sibling_swap(slots):
    n = len(slots)

    def start(g_refs, out_refs, sems):
        send_sems, recv_sems = sems
        x, y, c, _ = _mesh_pos()
        for t in range(n):
            for k in range(4):
                pltpu.make_async_remote_copy(
                    src_ref=g_refs[t].at[2 * k + (1 - c)], dst_ref=out_refs[t].at[k], send_sem=send_sems.at[t],
                    recv_sem=recv_sems.at[t], device_id=(x, y, 1 - c), device_id_type=MESH).start()

    def finish(g_refs, out_refs, sems):
        send_sems, recv_sems = sems
        x, y, c, _ = _mesh_pos()
        for t in range(n):
            pltpu.make_async_remote_copy(
                src_ref=g_refs[t].at[pl.ds(0, 4)], dst_ref=out_refs[t], send_sem=send_sems.at[t],
                recv_sem=recv_sems.at[t], device_id=(x, y, 1 - c), device_id_type=MESH).wait()

    return _Comm(slots, [jax.ShapeDtypeStruct((4,) + s.shape[1:], s.dtype) for s in slots],
                 [pltpu.SemaphoreType.DMA((n,)), pltpu.SemaphoreType.DMA((n,))], start, finish)


def _chip_swap(p4s):
    n = len(p4s)

    def copies(p_refs, out_refs, sems):
        send_sems, recv_sems = sems
        x, y, c, chips = _mesh_pos()
        return [pltpu.make_async_remote_copy(
            src_ref=p_refs[t].at[2 * px + py], dst_ref=out_refs[t].at[j], send_sem=send_sems.at[t, j],
            recv_sem=recv_sems.at[t, j], device_id=(px, py, c), device_id_type=MESH)
            for t in range(n) for j, (px, py) in enumerate(chips)]

    def start(p_refs, out_refs, sems):
        for cp in copies(p_refs, out_refs, sems):
            cp.start()

    def finish(p_refs, out_refs, sems):
        for cp in copies(p_refs, out_refs, sems):
            cp.wait()

    return _Comm(p4s, [jax.ShapeDtypeStruct((3,) + p.shape[1:], p.dtype) for p in p4s],
                 [pltpu.SemaphoreType.DMA((n, 3)), pltpu.SemaphoreType.DMA((n, 3))], start, finish)


def _comm_join(a, b):
    ai, ao, asem = len(a.inputs), len(a.out_shapes), len(a.sems)

    def start(ins, outs, sems):
        a.start(ins[:ai], outs[:ao], sems[:asem])
        b.start(ins[ai:], outs[ao:], sems[asem:])

    def finish(ins, outs, sems):
        a.finish(ins[:ai], outs[:ao], sems[:asem])
        b.finish(ins[ai:], outs[ao:], sems[asem:])

    aliases = dict(a.aliases)
    aliases.update({ai + i: ao + o for i, o in b.aliases.items()})
    return _Comm(a.inputs + b.inputs, a.out_shapes + b.out_shapes, a.sems + b.sems, start, finish, aliases)


def _comm_call(comm, name):
    c_in, c_out = len(comm.inputs), len(comm.out_shapes)

    def body(*refs):
        ins, outs, sems = refs[:c_in], refs[c_in:c_in + c_out], refs[c_in + c_out:]
        comm.start(ins, outs, sems)
        comm.finish(ins, outs, sems)

    return pl.pallas_call(
        body, name=name, out_shape=comm.out_shapes, in_specs=[_ANY] * c_in, out_specs=[_ANY] * c_out,
        scratch_shapes=comm.sems, input_output_aliases=comm.aliases,
    )(*comm.inputs)


def _row_tile(rows, cols, max_bytes=1024 * 1024):
    best = None
    for tr in range(16, rows + 1, 16):
        if rows % tr == 0 and tr * cols * 4 <= max_bytes:
            best = tr
    return best or rows


def _pair_add(slots, theirs, core, name):
    _, rows, cols = slots.shape
    tr = _row_tile(rows, cols, 4 * 1024 * 1024)

    def body(c_ref, a_ref, b_ref, o_ref):
        o_ref[...] = (a_ref[...].astype(f32) + b_ref[...].astype(f32)).astype(bf16)

    return pl.pallas_call(
        body, name=name,
        grid_spec=pltpu.PrefetchScalarGridSpec(
            num_scalar_prefetch=1, grid=(4, rows // tr),
            in_specs=[pl.BlockSpec((None, tr, cols), lambda k, i, c_ref: (2 * k + c_ref[0], i, 0)),
                      pl.BlockSpec((None, tr, cols), lambda k, i, c_ref: (k, i, 0))],
            out_specs=pl.BlockSpec((None, tr, cols), lambda k, i, c_ref: (k, i, 0))),
        out_shape=jax.ShapeDtypeStruct((4, rows, cols), bf16), compiler_params=_cparams(("parallel", "parallel")),
    )(core, slots, theirs)


def _adam_update(g, w, m, v):
    c1 = 1.0 / (1.0 - ADAM_B1 ** ADAM_STEP)
    c2 = 1.0 / (1.0 - ADAM_B2 ** ADAM_STEP)
    nm = ADAM_B1 * m + (1.0 - ADAM_B1) * g
    nv = ADAM_B2 * v + (1.0 - ADAM_B2) * jnp.square(g)
    delta = -ADAM_LR * ((nm * c1) / (jnp.sqrt(nv * c2) + ADAM_EPS) + ADAM_WD * w)
    return delta, nm, nv


def _adamw_layers(p4s, chips, chip_idx, w, m, v, name):
    depth = len(p4s)
    _, rows_l, cols = p4s[0].shape
    tr = _row_tile(rows_l, cols)
    nr = rows_l // tr

    def body(idx_ref, *refs):
        p_refs, c_refs = refs[:depth], refs[depth:2 * depth]
        w_ref, m_ref, v_ref, g_ref, d_ref, nm_ref, nv_ref = refs[2 * depth:]
        layer = pl.program_id(0)
        for ll in range(depth):
            @pl.when(layer == ll)
            def _(ll=ll):
                g = p_refs[ll][...].astype(f32)
                for j in range(3):
                    g = g + c_refs[ll][j].astype(f32)
                delta, nm, nv = _adam_update(g, w_ref[...], m_ref[...], v_ref[...])
                g_ref[...] = g
                d_ref[...] = delta
                nm_ref[...] = nm
                nv_ref[...] = nv

    def of_layer(ll):
        return lambda l, i, idx_ref: jnp.where(l == ll, i, 0)

    p_specs = [pl.BlockSpec((None, tr, cols), lambda l, i, idx_ref, f=of_layer(ll): (idx_ref[0], f(l, i, idx_ref), 0))
               for ll in range(depth)]
    c_specs = [pl.BlockSpec((3, tr, cols), lambda l, i, idx_ref, f=of_layer(ll): (0, f(l, i, idx_ref), 0))
               for ll in range(depth)]
    spec = pl.BlockSpec((tr, cols), lambda l, i, idx_ref: (l * nr + i, 0))
    out = jax.ShapeDtypeStruct(w.shape, f32)
    return pl.pallas_call(
        body, name=name,
        grid_spec=pltpu.PrefetchScalarGridSpec(
            num_scalar_prefetch=1, grid=(depth, nr), in_specs=p_specs + c_specs + [spec, spec, spec],
            out_specs=[spec, spec, spec, spec]),
        out_shape=[out, out, out, out], compiler_params=_cparams(("parallel", "parallel")),
    )(chip_idx, *p4s, *chips, w, m, v)


def _adamw(parts, w, m, v, name):
    rows, cols = w.shape
    tr = _row_tile(rows, cols, 512 * 1024)
    nparts = len(parts)

    def body(*refs):
        part_refs, (w_ref, m_ref, v_ref, g_ref, d_ref, nm_ref, nv_ref) = refs[:nparts], refs[nparts:]
        g = None
        for pr in part_refs:
            for s in range(pr.shape[0]):
                term = pr[s].astype(f32)
                g = term if g is None else g + term
        delta, nm, nv = _adam_update(g, w_ref[...], m_ref[...], v_ref[...])
        g_ref[...] = g
        d_ref[...] = delta
        nm_ref[...] = nm
        nv_ref[...] = nv

    spec = pl.BlockSpec((tr, cols), lambda i: (i, 0))
    part_specs = [pl.BlockSpec((p.shape[0], tr, cols), lambda i: (0, i, 0)) for p in parts]
    out = jax.ShapeDtypeStruct((rows, cols), f32)
    return pl.pallas_call(
        body, name=name, grid=(rows // tr,), in_specs=part_specs + [spec, spec, spec],
        out_specs=[spec, spec, spec, spec], out_shape=[out, out, out, out],
        compiler_params=_cparams(("parallel",)),
    )(*parts, w, m, v)


def _gathered_to_full(gathered, axis):
    s = gathered.shape[1:]
    full = jnp.moveaxis(gathered, 0, axis)
    return full.reshape(s[:axis] + (N_DEV * s[axis],) + s[axis + 1:])


def _interleave_halves(w, ct):
    n = w.shape[-1] // 2
    t = w.reshape(w.shape[:-1] + (2, n // ct, ct))
    return jnp.swapaxes(t, -3, -2).reshape(w.shape)


def _deinterleave_halves(w, ct):
    n = w.shape[-1] // 2
    t = w.reshape(w.shape[:-1] + (n // ct, 2, ct))
    return jnp.swapaxes(t, -3, -2).reshape(w.shape)


def _rot_cols(w):
    half = QK_ROPE_DIM // 2
    return jnp.concatenate([-w[..., half:], w[..., :half]], axis=-1)


def _rot_cols_t(dw):
    half = QK_ROPE_DIM // 2
    return jnp.concatenate([dw[..., half:], -dw[..., :half]], axis=-1)


def kernel(x, positions, ln_in_g, ln_in_b, w_in, q_norm_g, w_uq, kv_norm_g, w_ukv, conv_w, conv_b, conv_ln_g, conv_ln_b, w_pool, pool_scale, w_out, ln1_g, ln1_b, w_up, ffn_conv_w, ffn_conv_b, w_down, ln2_g, ln2_b, loss_target, m_ln_in_g, m_ln_in_b, m_w_in, m_q_norm_g, m_w_uq, m_kv_norm_g, m_w_ukv, m_conv_w, m_conv_b, m_conv_ln_g, m_conv_ln_b, m_w_pool, m_pool_scale, m_w_out, m_ln1_g, m_ln1_b, m_w_up, m_ffn_conv_w, m_ffn_conv_b, m_w_down, m_ln2_g, m_ln2_b, v_ln_in_g, v_ln_in_b, v_w_in, v_q_norm_g, v_w_uq, v_kv_norm_g, v_w_ukv, v_conv_w, v_conv_b, v_conv_ln_g, v_conv_ln_b, v_w_pool, v_pool_scale, v_w_out, v_ln1_g, v_ln1_b, v_w_up, v_ffn_conv_w, v_ffn_conv_b, v_w_down, v_ln2_g, v_ln2_b):
    weights = dict(ln_in_g=ln_in_g, ln_in_b=ln_in_b, w_in=w_in, q_norm_g=q_norm_g, w_uq=w_uq, kv_norm_g=kv_norm_g,
                   w_ukv=w_ukv, conv_w=conv_w, conv_b=conv_b, conv_ln_g=conv_ln_g, conv_ln_b=conv_ln_b, w_pool=w_pool,
                   pool_scale=pool_scale, w_out=w_out, ln1_g=ln1_g, ln1_b=ln1_b, w_up=w_up, ffn_conv_w=ffn_conv_w,
                   ffn_conv_b=ffn_conv_b, w_down=w_down, ln2_g=ln2_g, ln2_b=ln2_b)
    mom1 = dict(ln_in_g=m_ln_in_g, ln_in_b=m_ln_in_b, w_in=m_w_in, q_norm_g=m_q_norm_g, w_uq=m_w_uq,
                kv_norm_g=m_kv_norm_g, w_ukv=m_w_ukv, conv_w=m_conv_w, conv_b=m_conv_b, conv_ln_g=m_conv_ln_g,
                conv_ln_b=m_conv_ln_b, w_pool=m_w_pool, pool_scale=m_pool_scale, w_out=m_w_out, ln1_g=m_ln1_g,
                ln1_b=m_ln1_b, w_up=m_w_up, ffn_conv_w=m_ffn_conv_w, ffn_conv_b=m_ffn_conv_b, w_down=m_w_down,
                ln2_g=m_ln2_g, ln2_b=m_ln2_b)
    mom2 = dict(ln_in_g=v_ln_in_g, ln_in_b=v_ln_in_b, w_in=v_w_in, q_norm_g=v_q_norm_g, w_uq=v_w_uq,
                kv_norm_g=v_kv_norm_g, w_ukv=v_w_ukv, conv_w=v_conv_w, conv_b=v_conv_b, conv_ln_g=v_conv_ln_g,
                conv_ln_b=v_conv_ln_b, w_pool=v_w_pool, pool_scale=v_pool_scale, w_out=v_w_out, ln1_g=v_ln1_g,
                ln1_b=v_ln1_b, w_up=v_w_up, ffn_conv_w=v_ffn_conv_w, ffn_conv_b=v_ffn_conv_b, w_down=v_w_down,
                ln2_g=v_ln2_g, ln2_b=v_ln2_b)
    names = list(weights)

    nb, seq, d = x.shape
    t = nb * seq
    depth = w_in.shape[0]
    ql, kvl, cw, pw = q_norm_g.shape[1], kv_norm_g.shape[1], conv_b.shape[1], pool_scale.shape[1]
    pg = w_pool.shape[-1]
    heads = w_uq.shape[2]
    dff = w_down.shape[1] * N_DEV
    alpha = (2.0 * depth) ** 0.25
    scale = float(QK_NOPE_DIM + QK_ROPE_DIM) ** -0.5
    lay = dict(ql=ql, kvl=kvl, pw=pw, pg=pg, off_q=2 * cw, off_pool=2 * cw + ql, off_kv=2 * cw + ql + pw,
               off_kr=2 * cw + ql + pw + kvl)
    o1, o2, o3, o4 = ql, ql + kvl, ql + kvl + QK_ROPE_DIM, ql + kvl + QK_ROPE_DIM + 2 * cw
    my_x, my_y, my_c = lax.axis_index("x"), lax.axis_index("y"), lax.axis_index("c")
    my_dev = 4 * my_x + 2 * my_y + my_c

    big = ("w_in", "w_uq", "w_ukv", "w_out", "w_up", "w_down")
    g_conv, g_ffn = _all_gather([conv_w, ffn_conv_w], "ag_conv_taps")
    conv_w_full, ffn_w_full = _gathered_to_full(g_conv, 2), _gathered_to_full(g_ffn, 2)
    w_pool_2d = w_pool.reshape(depth, pw, pg)

    rest = ("w_in", "w_uq", "w_ukv", "w_out")

    def bf16_shards(l, which):
        return [weights[n][l].astype(bf16) for n in which]

    def layer_weights(gathered):
        g_in, g_uq, g_ukv, g_out, g_up, g_down = (gathered[n] for n in big)
        wi = _gathered_to_full(g_in, 1)
        kr_cols = wi[:, o2:o3]
        w_in_pad = jnp.concatenate([_interleave_halves(wi[:, o3:o4], SEQ_CT), wi[:, :o1], wi[:, o4:], wi[:, o1:o2],
                                    kr_cols, _rot_cols(kr_cols)], axis=-1)
        wq = g_uq.reshape(ql, heads, QK_NOPE_DIM + QK_ROPE_DIM)
        w_uq_ext = jnp.concatenate([wq, _rot_cols(wq[..., QK_NOPE_DIM:])], axis=-1).reshape(ql, heads * 2 * LANES)
        return dict(w_in=w_in_pad, w_uq=w_uq_ext, w_ukv=g_ukv.reshape(kvl, heads * 2 * LANES), w_out=g_out.reshape(-1, d),
                    w_up=g_up, w_down=g_down.reshape(dff, d))

    half = QK_ROPE_DIM // 2
    inv = 1.0 / (ROPE_THETA ** (jnp.arange(0, QK_ROPE_DIM, 2, dtype=f32) / QK_ROPE_DIM))
    inv_lanes = jnp.tile(inv, LANES // half).reshape(1, LANES)
    cs, qt = _rope_tables(positions.reshape(t, 1), inv_lanes, scale)

    x2 = x.reshape(t, d)
    xs, xs_bf = _ln_fwd(x2, None, ln_in_g.reshape(1, d), ln_in_b.reshape(1, d), 1.0, "ln_in_fwd")
    saved = []
    in_pad = o4 + pw + QK_ROPE_DIM
    gathered = dict(zip(big, _all_gather(bf16_shards(0, big), "ag_weights")))
    for l in range(depth):
        wl = layer_weights(gathered)
        nxt = l + 1 < depth
        gathered = {}
        gq, gkv = q_norm_g[l].reshape(1, ql), kv_norm_g[l].reshape(1, kvl)
        h = _matmul(xs_bf, wl["w_in"], "nn", f32, "mm_in", tm=512, tn=in_pad)
        qn, kvn, krd = _prep_fwd(h, cs, gq, gkv, lay)
        q_ext = _matmul(qn, wl["w_uq"], "nn", f32, "mm_uq", tn=2048)
        kv = _matmul(kvn, wl["w_ukv"], "nn", bf16, "mm_ukv", tn=2048)
        if nxt:
            y_mla, lse, (g_down,) = _attn_fwd(q_ext, qt, kv, krd, nb, seq, heads,
                                              comm=_gather_own(bf16_shards(l + 1, ("w_down",))))
        else:
            y_mla, lse = _attn_fwd(q_ext, qt, kv, krd, nb, seq, heads)
        hconv = _conv_fwd(h, conv_w_full[l], conv_b[l].reshape(1, cw), nb, seq, cw)
        y_conv = _convln_fwd(hconv, conv_ln_g[l].reshape(1, cw), conv_ln_b[l].reshape(1, cw))
        y_pool = _pool_fwd(h, w_pool_2d[l], pool_scale[l].reshape(1, pw), lay, nb, seq)
        mixed = jnp.concatenate([y_mla, y_conv, y_pool], axis=-1)
        if nxt:
            y1, (gathered["w_down"],) = _matmul(mixed, wl["w_out"], "nn", f32, "mm_out", comm=_gather_pass([g_down]))
        else:
            y1 = _matmul(mixed, wl["w_out"], "nn", f32, "mm_out")
        x1, x1_bf = _ln_fwd(xs, y1, ln1_g[l].reshape(1, d), ln1_b[l].reshape(1, d), alpha, "ln1_fwd")
        if nxt:
            up, (g_up,) = _matmul_up(x1_bf, wl["w_up"], "mm_up", comm=_gather_own(bf16_shards(l + 1, ("w_up",))))
        else:
            up = _matmul_up(x1_bf, wl["w_up"], "mm_up")
        if nxt:
            act, g_rest = _ffn_act_fwd(up, ffn_w_full[l], ffn_conv_b[l].reshape(1, 2 * dff), nb, seq, dff,
                                       comm=_gather_own(bf16_shards(l + 1, rest)))
            y2, (gathered["w_up"],) = _matmul(act, wl["w_down"], "nn", f32, "mm_down", tk=dff // 2,
                                              comm=_gather_pass([g_up]))
            xn, xn_bf, g_rest = _ln_fwd(x1, y2, ln2_g[l].reshape(1, d), ln2_b[l].reshape(1, d), alpha, "ln2_fwd",
                                        comm=_gather_pass(g_rest))
            gathered.update(zip(rest, g_rest))
        else:
            act = _ffn_act_fwd(up, ffn_w_full[l], ffn_conv_b[l].reshape(1, 2 * dff), nb, seq, dff)
            y2 = _matmul(act, wl["w_down"], "nn", f32, "mm_down", tk=dff // 2)
            xn, xn_bf = _ln_fwd(x1, y2, ln2_g[l].reshape(1, d), ln2_b[l].reshape(1, d), alpha, "ln2_fwd")
        saved.append(dict(xs=xs, xs_bf=xs_bf, h=h, qn=qn, kvn=kvn, krd=krd, q_ext=q_ext, kv=kv, lse=lse, hconv=hconv,
                          mixed=mixed, y1=y1, x1=x1, x1_bf=x1_bf, up=up, act=act, y2=y2, wl=wl))
        xs, xs_bf = xn, xn_bf

    d_stream, loss_row = _loss_call(xs, loss_target.reshape(t, d))
    loss = lax.psum(loss_row[0, 0], MESH_AXES)

    gw = {n: [None] * depth for n in names if n not in ("ln_in_g", "ln_in_b")}
    rs_own = {n: [None] * depth for n in big}
    rs_chips = {n: [None] * depth for n in big}
    core_idx = jnp.reshape(my_c, (1,)).astype(jnp.int32)
    chip_idx = jnp.reshape(2 * my_x + my_y, (1,)).astype(jnp.int32)

    def pair_add(l, which, slots, theirs):
        p4s = [_pair_add(slots[n], th, core_idx, "rs_add_" + n) for n, th in zip(which, theirs)]
        for n, p4 in zip(which, p4s):
            rs_own[n][l] = p4
        return p4s

    d_res, d_mm = None, d_stream
    pending = None
    for l in reversed(range(depth)):
        sv = saved[l]
        wl = sv["wl"]
        slots = {}
        gq, gkv = q_norm_g[l].reshape(1, ql), kv_norm_g[l].reshape(1, kvl)
        dz2, dz2_bf, gw["ln2_g"][l], gw["ln2_b"][l] = _ln_bwd(
            d_res, d_mm, sv["x1"], sv["y2"], ln2_g[l].reshape(1, d), ln2_b[l].reshape(1, d), alpha, "ln2_bwd")
        if pending is None:
            dw_down = _matmul(sv["act"], dz2_bf, "tn", bf16, "mm_down_dw", tm=dff // 4)
        else:
            dw_down, theirs = _matmul(sv["act"], dz2_bf, "tn", bf16, "mm_down_dw", tm=dff // 4,
                                      comm=_sibling_swap([pending[1][n] for n in rest]))
            rest_p4s = pair_add(pending[0], rest, pending[1], theirs)
        slots["w_down"] = dw_down.reshape(N_DEV, -1, d)
        d_act, theirs = _matmul(dz2_bf, wl["w_down"], "nt", f32, "mm_down_dx", tn=dff // 4,
                                comm=_sibling_swap([slots["w_down"]]))
        down_p4s = pair_add(l, ("w_down",), slots, theirs)
        d_up, dffw, dffb = _ffn_act_bwd(sv["up"], ffn_w_full[l], ffn_conv_b[l].reshape(1, 2 * dff), d_act, nb, seq, dff)
        gw["ffn_conv_w"][l] = jnp.concatenate([dffw[0], dffw[1]], axis=-1)
        gw["ffn_conv_b"][l] = dffb.reshape(2 * dff)
        d_x1, (rs_chips["w_down"][l],) = _matmul_up_dx(d_up, wl["w_up"], "mm_up_dx", comm=_chip_swap(down_p4s))
        if pending is None:
            slots["w_up"] = _matmul_up_dw(sv["x1_bf"], d_up, N_DEV, "mm_up_dw")
        else:
            slots["w_up"], from_chips = _matmul_up_dw(sv["x1_bf"], d_up, N_DEV, "mm_up_dw", comm=_chip_swap(rest_p4s))
            for n, fc in zip(rest, from_chips):
                rs_chips[n][pending[0]] = fc
        dz1, dz1_bf, gw["ln1_g"][l], gw["ln1_b"][l] = _ln_bwd(
            dz2, d_x1, sv["xs"], sv["y1"], ln1_g[l].reshape(1, d), ln1_b[l].reshape(1, d), alpha, "ln1_bwd")
        d_mixed, theirs = _matmul(dz1_bf, wl["w_out"], "nt", f32, "mm_out_dx", comm=_sibling_swap([slots["w_up"]]))
        up_p4s = pair_add(l, ("w_up",), slots, theirs)
        slots["w_out"] = _matmul(sv["mixed"], dz1_bf, "tn", bf16, "mm_out_dw").reshape(N_DEV, -1, d)
        d_upool, dwp, dps = _pool_bwd(sv["h"], w_pool_2d[l], pool_scale[l].reshape(1, pw), d_mixed,
                                      (heads * LANES + cw) // pw, lay, nb, seq)
        gw["w_pool"][l] = dwp.reshape(w_pool.shape[1:])
        gw["pool_scale"][l] = dps.reshape(pw)
        d_hconv, dclg, dclb = _convln_bwd(sv["hconv"], conv_ln_g[l].reshape(1, cw), conv_ln_b[l].reshape(1, cw), d_mixed,
                                          heads * LANES // cw)
        gw["conv_ln_g"][l], gw["conv_ln_b"][l] = dclg.reshape(cw), dclb.reshape(cw)
        d_conv, gw["conv_w"][l], dcb = _conv_bwd(sv["h"], conv_w_full[l], d_hconv, nb, seq, cw)
        gw["conv_b"][l] = dcb.reshape(cw)
        dq_ext, dkv, dkrd, (rs_chips["w_up"][l],) = _attn_bwd(sv["q_ext"], qt, sv["kv"], sv["krd"], sv["lse"], d_mixed, nb,
                                                              seq, heads, comm=_chip_swap(up_p4s))
        d_qn = _matmul(dq_ext, wl["w_uq"], "nt", f32, "mm_uq_dx")
        dwq = _matmul(sv["qn"], dq_ext, "tn", f32, "mm_uq_dw", tn=2048, tk=1024).reshape(ql, heads, 2 * LANES)
        dwq_rope = dwq[..., QK_NOPE_DIM:QK_NOPE_DIM + QK_ROPE_DIM] + _rot_cols_t(dwq[..., QK_NOPE_DIM + QK_ROPE_DIM:])
        slots["w_uq"] = jnp.concatenate([dwq[..., :QK_NOPE_DIM], dwq_rope], axis=-1).astype(bf16).reshape(
            N_DEV, -1, QK_NOPE_DIM + QK_ROPE_DIM)
        d_kvn = _matmul(dkv, wl["w_ukv"], "nt", f32, "mm_ukv_dx")
        slots["w_ukv"] = _matmul(sv["kvn"], dkv, "tn", bf16, "mm_ukv_dw", tn=2048, tk=1024).reshape(N_DEV, -1, 2 * LANES)
        d_cq, d_ckv, d_kr, dgq, dgkv = _prep_bwd(sv["h"], cs, gq, gkv, d_qn, d_kvn, dkrd, lay)
        gw["q_norm_g"][l], gw["kv_norm_g"][l] = dgq.reshape(ql), dgkv.reshape(kvl)
        d_h = jnp.concatenate([d_conv, d_cq, d_upool, d_ckv, d_kr], axis=-1)
        d_xs = _matmul(d_h, wl["w_in"], "nt", f32, "mm_in_dx", tm=512, tk=in_pad)
        dwi = _matmul(sv["xs_bf"], d_h, "tn", f32, "mm_in_dw", tn=in_pad, tk=1024)
        dkr_cols = dwi[:, lay["off_kr"]:lay["off_kr"] + QK_ROPE_DIM] + _rot_cols_t(dwi[:, lay["off_kr"] + QK_ROPE_DIM:])
        dwi_nat = jnp.concatenate(
            [dwi[:, lay["off_q"]:lay["off_q"] + ql], dwi[:, lay["off_kv"]:lay["off_kv"] + kvl], dkr_cols,
             _deinterleave_halves(dwi[:, :2 * cw], SEQ_CT), dwi[:, lay["off_pool"]:lay["off_pool"] + pw]],
            axis=-1).astype(bf16)
        slots["w_in"] = jnp.moveaxis(dwi_nat.reshape(d, N_DEV, -1), 1, 0)
        pending = (l, slots)
        d_res, d_mm = dz1, d_xs

    theirs = _comm_call(_sibling_swap([pending[1][n] for n in rest]), "rs_sibling")
    from_chips = _comm_call(_chip_swap(pair_add(pending[0], rest, pending[1], theirs)), "rs_chips")
    for n, fc in zip(rest, from_chips):
        rs_chips[n][pending[0]] = fc

    grad_x, _, d_ln_in_g, d_ln_in_b = _ln_bwd(d_res, d_mm, x2, None, ln_in_g.reshape(1, d), ln_in_b.reshape(1, d), alpha,
                                               "ln_in_bwd")
    grad_x = grad_x.reshape(x.shape)

    grads, deltas, new_m, new_v = {}, {}, {}, {}

    def finish(n, parts):
        shp = weights[n].shape
        rows = math.prod(shp[:-1]) if len(shp) > 1 else 1
        as2d = lambda a: a.reshape(rows, shp[-1])
        parts = [p.reshape(p.shape[0], rows, shp[-1]) for p in parts]
        g, dl, nm, nv = _adamw(parts, as2d(weights[n]), as2d(mom1[n]), as2d(mom2[n]), "adamw_" + n)
        grads[n], deltas[n], new_m[n], new_v[n] = (a.reshape(shp) for a in (g, dl, nm, nv))

    for n in big:
        shp = weights[n].shape
        as2d = lambda a: a.reshape(-1, shp[-1])
        g, dl, nm, nv = _adamw_layers(rs_own[n], rs_chips[n], chip_idx, as2d(weights[n]), as2d(mom1[n]), as2d(mom2[n]),
                                      "adamw_" + n)
        grads[n], deltas[n], new_m[n], new_v[n] = (a.reshape(shp) for a in (g, dl, nm, nv))

    small = [n for n in names if n not in ("w_in", "w_uq", "w_ukv", "w_out", "w_up", "w_down")]
    partial = {"ln_in_g": d_ln_in_g.reshape(d), "ln_in_b": d_ln_in_b.reshape(d)}
    for n in small:
        if n not in partial:
            partial[n] = jnp.stack(gw[n])
    flat = jnp.concatenate([partial[n].astype(f32).reshape(-1) for n in small])
    gathered = _all_gather([flat.reshape(-1, LANES)], "ag_small_grads")[0].reshape(N_DEV, -1)
    off = 0
    for n in small:
        size = math.prod(partial[n].shape)
        part = gathered[:, off:off + size].reshape((N_DEV,) + partial[n].shape)
        off += size
        if n in ("conv_w", "ffn_conv_w"):
            width = weights[n].shape[-1]
            part = lax.dynamic_slice_in_dim(part, my_dev * width, width, axis=part.ndim - 1)
        finish(n, [part])

    return (loss, grad_x, *[grads[n] for n in names], *[deltas[n] for n in names], *[new_m[n] for n in names],
            *[new_v[n] for n in names])
```

```python
import functools
import math

import jax
import jax.numpy as jnp
from jax import lax
from jax.experimental import pallas as pl
from jax.experimental.pallas import tpu as pltpu

f32 = jnp.float32
bf16 = jnp.bfloat16

QK_NOPE_DIM = 128
QK_ROPE_DIM = 64
V_HEAD_DIM = 128
CONV_KERNEL = 31
FFN_CONV_KERNEL = 3
POOL_WINDOWS = (2, 4, 8, 16)
ROPE_THETA = 10000.0
LN_EPS = 1e-5
RMS_EPS = 1e-6
ADAM_LR = 0.001
ADAM_B1 = 0.9
ADAM_B2 = 0.999
ADAM_EPS = 1e-08
ADAM_WD = 0.01
ADAM_STEP = 10

N_DEV = 8
MESH_AXES = ("x", "y", "c")
V7X_VMEM_LIMIT_BYTES = 56 * 1024 * 1024
LANES = 128
NEG_INF = -1e30
MESH = pl.DeviceIdType.MESH


def _cparams(sem):
    return pltpu.CompilerParams(dimension_semantics=sem, vmem_limit_bytes=V7X_VMEM_LIMIT_BYTES)


def _tile(dim, pref):
    t = pref
    while t >= LANES:
        if dim % t == 0:
            return t
        t //= 2
    return dim


_ANY = pl.BlockSpec(memory_space=pl.ANY)


class _Comm:
    def __init__(self, inputs, out_shapes, sems, start, finish, aliases=None):
        self.inputs, self.out_shapes, self.sems = list(inputs), list(out_shapes), list(sems)
        self.start, self.finish, self.aliases = start, finish, dict(aliases or {})


def _call(body, name, grid, in_specs, out_specs, out_shape, args, scratch=(), sem=None, comm=None):
    in_specs, out_specs, out_shape, scratch = list(in_specs), list(out_specs), list(out_shape), list(scratch)
    if comm is None:
        outs = pl.pallas_call(body, name=name, grid=grid, in_specs=in_specs, out_specs=out_specs, out_shape=out_shape,
                              scratch_shapes=scratch, compiler_params=_cparams(sem))(*args)
        return list(outs), []
    n_in, n_out, n_scr = len(in_specs), len(out_specs), len(scratch)
    c_in, c_out = len(comm.inputs), len(comm.out_shapes)

    def carrier(*refs):
        refs = list(refs)
        ins, refs = refs[:n_in], refs[n_in:]
        c_ins, refs = refs[:c_in], refs[c_in:]
        outs, refs = refs[:n_out], refs[n_out:]
        c_outs, refs = refs[:c_out], refs[c_out:]
        scr, c_sems = refs[:n_scr], refs[n_scr:]
        ids = [pl.program_id(a) for a in range(len(grid))]
        first = functools.reduce(lambda p, q: p & q, [i == 0 for i in ids])
        last = functools.reduce(lambda p, q: p & q, [i == g - 1 for i, g in zip(ids, grid)])
        pl.when(first)(lambda: comm.start(c_ins, c_outs, c_sems))
        body(*ins, *outs, *scr)
        pl.when(last)(lambda: comm.finish(c_ins, c_outs, c_sems))

    outs = pl.pallas_call(
        carrier, name=name, grid=grid, in_specs=in_specs + [_ANY] * c_in, out_specs=out_specs + [_ANY] * c_out,
        out_shape=out_shape + comm.out_shapes, scratch_shapes=scratch + comm.sems,
        input_output_aliases={n_in + a: n_out + b for a, b in comm.aliases.items()},
        compiler_params=_cparams(("arbitrary",) * len(grid)),
    )(*args, *comm.inputs)
    return list(outs[:n_out]), list(outs[n_out:])


def _shift_down_raw(x, k):
    if k == 0:
        return x
    row = lax.broadcasted_iota(jnp.int32, x.shape, 0)
    return jnp.where(row >= k, pltpu.roll(x, k, axis=0), 0.0)


def _shift_up_raw(x, k):
    if k == 0:
        return x
    n = x.shape[0]
    row = lax.broadcasted_iota(jnp.int32, x.shape, 0)
    return jnp.where(row < n - k, pltpu.roll(x, n - k, axis=0), 0.0)


@functools.partial(jax.custom_vjp, nondiff_argnums=(1,))
def _shift_down(x, k):
    return _shift_down_raw(x, k)


def _shift_down_fwd(x, k):
    return _shift_down_raw(x, k), None


def _shift_down_bwd(k, _, g):
    return (_shift_up_raw(g, k),)


_shift_down.defvjp(_shift_down_fwd, _shift_down_bwd)


@jax.custom_vjp
def _dup_halves(p):
    return p + pltpu.roll(p, LANES // 2, axis=1)


def _dup_halves_fwd(p):
    return p + pltpu.roll(p, LANES // 2, axis=1), None


def _dup_halves_bwd(_, g):
    return (g + pltpu.roll(g, LANES // 2, axis=1),)


_dup_halves.defvjp(_dup_halves_fwd, _dup_halves_bwd)

_NN = (((1,), (0,)), ((), ()))
_NT = (((1,), (1,)), ((), ()))
_TN = (((0,), (0,)), ((), ()))


def _dot(a, b, dims):
    return lax.dot_general(a.astype(bf16), b.astype(bf16), dims, preferred_element_type=f32)


@jax.custom_vjp
def _mm_bf16(a, b):
    return _dot(a, b, _NN)


def _mm_bf16_fwd(a, b):
    return _dot(a, b, _NN), (a, b)


def _mm_bf16_bwd(res, g):
    a, b = res
    return _dot(g, b, _NT), _dot(a, g, _TN)


_mm_bf16.defvjp(_mm_bf16_fwd, _mm_bf16_bwd)


def _layer_norm(z, g, b):
    mu = jnp.mean(z, axis=-1, keepdims=True)
    var = jnp.mean(jnp.square(z - mu), axis=-1, keepdims=True)
    return (z - mu) * lax.rsqrt(var + LN_EPS) * g + b


def _rms_norm(x, g):
    ms = jnp.mean(jnp.square(x), axis=-1, keepdims=True)
    return x * lax.rsqrt(ms + RMS_EPS) * g


def _colsum(x):
    return jnp.sum(x, axis=0, keepdims=True)


def _matmul_core(a, b, mode, grid, a_spec, b_spec, o_spec, o_shape, tile, out_dtype, name, comm=None):
    nk = grid[2]
    dims = {"nn": _NN, "nt": _NT, "tn": _TN}[mode]
    acc_in_out = out_dtype == f32

    def body(a_ref, b_ref, o_ref, *scratch):
        def prod():
            return _dot(a_ref[...], b_ref[...], dims)

        if nk == 1:
            o_ref[...] = prod().astype(out_dtype)
            return
        acc_ref = o_ref if acc_in_out else scratch[0]
        kk = pl.program_id(2)

        @pl.when(kk == 0)
        def _():
            acc_ref[...] = prod()

        if acc_in_out:
            @pl.when(kk > 0)
            def _():
                acc_ref[...] += prod()
        else:
            @pl.when((kk > 0) & (kk < nk - 1))
            def _():
                acc_ref[...] += prod()

            @pl.when(kk == nk - 1)
            def _():
                o_ref[...] = (acc_ref[...] + prod()).astype(out_dtype)

    scratch = [] if (nk == 1 or acc_in_out) else [pltpu.VMEM(tile, f32)]
    (out,), comm_outs = _call(body, name, grid, [a_spec, b_spec], [o_spec], [jax.ShapeDtypeStruct(o_shape, out_dtype)],
                              (a, b), scratch, ("parallel", "parallel", "arbitrary"), comm)
    return out if comm is None else (out, comm_outs)


def _matmul(a, b, mode, out_dtype, name, tm=1024, tn=1024, tk=2048, comm=None):
    if mode == "nn":
        (m, k), (k2, n) = a.shape, b.shape
    elif mode == "nt":
        (m, k), (n, k2) = a.shape, b.shape
    else:
        (k, m), (k2, n) = a.shape, b.shape
    assert k == k2, (name, a.shape, b.shape)
    tm, tn, tk = _tile(m, tm), _tile(n, tn), _tile(k, tk)
    a_spec = pl.BlockSpec((tk, tm), lambda i, j, kk: (kk, i)) if mode == "tn" else pl.BlockSpec((tm, tk), lambda i, j, kk: (i, kk))
    b_spec = pl.BlockSpec((tn, tk), lambda i, j, kk: (j, kk)) if mode == "nt" else pl.BlockSpec((tk, tn), lambda i, j, kk: (kk, j))
    return _matmul_core(a, b, mode, (m // tm, n // tn, k // tk), a_spec, b_spec,
                        pl.BlockSpec((tm, tn), lambda i, j, kk: (i, j)), (m, n), (tm, tn), out_dtype, name, comm)


def _matmul_up(x_bf, w_slots, name, tm=1024, comm=None):
    m, k = x_bf.shape
    s, _, ns = w_slots.shape
    tm = _tile(m, tm)
    return _matmul_core(x_bf, w_slots, "nn", (m // tm, s, 1), pl.BlockSpec((tm, k), lambda i, j, kk: (i, 0)),
                        pl.BlockSpec((None, k, ns), lambda i, j, kk: (j, 0, 0)),
                        pl.BlockSpec((tm, ns), lambda i, j, kk: (i, j)), (m, s * ns), (tm, ns), f32, name, comm)


def _matmul_up_dx(d3, w_slots, name, tm=1024, tn=1024, comm=None):
    _, m, half = d3.shape
    s, n, ns = w_slots.shape
    per_half = half // ns
    assert 2 * per_half == s, (d3.shape, w_slots.shape)
    tm, tn = _tile(m, tm), _tile(n, tn)
    return _matmul_core(d3, w_slots, "nt", (m // tm, n // tn, s),
                        pl.BlockSpec((None, tm, ns), lambda i, j, kk: (kk // per_half, i, kk % per_half)),
                        pl.BlockSpec((None, tn, ns), lambda i, j, kk: (kk, j, 0)),
                        pl.BlockSpec((tm, tn), lambda i, j, kk: (i, j)), (m, n), (tm, tn), f32, name, comm)


def _matmul_up_dw(x_bf, d3, n_slots, name, tm=1024, tk=2048, comm=None):
    k, m = x_bf.shape
    _, _, half = d3.shape
    ns = 2 * half // n_slots
    per_half = n_slots // 2
    tm, tk = _tile(m, tm), _tile(k, tk)
    return _matmul_core(x_bf, d3, "tn", (m // tm, n_slots, k // tk), pl.BlockSpec((tk, tm), lambda i, j, kk: (kk, i)),
                        pl.BlockSpec((None, tk, ns), lambda i, j, kk: (j // per_half, kk, j % per_half)),
                        pl.BlockSpec((None, tm, ns), lambda i, j, kk: (j, i, 0)), (n_slots, m, ns), (tm, ns), bf16, name,
                        comm)


ROW_TILE = 256


def _rows(width, col_block=0):
    return pl.BlockSpec((ROW_TILE, width), lambda i, cb=col_block: (i, cb))


def _whole(shape):
    return pl.BlockSpec(shape, lambda i: (0,) * len(shape))


def _ln_fwd(x, y, g, b, alpha, name, comm=None):
    t, d = x.shape

    def body(*refs):
        if y is None:
            x_ref, g_ref, b_ref, o_ref, ob_ref = refs
            z = x_ref[...]
        else:
            x_ref, y_ref, g_ref, b_ref, o_ref, ob_ref = refs
            z = alpha * x_ref[...] + y_ref[...]
        out = _layer_norm(z, g_ref[...], b_ref[...])
        o_ref[...] = out
        ob_ref[...] = out.astype(bf16)

    ins = [x] + ([] if y is None else [y]) + [g, b]
    specs = [_rows(d)] + ([] if y is None else [_rows(d)]) + [_whole((1, d)), _whole((1, d))]
    (out, out_bf), comm_outs = _call(
        body, name, (t // ROW_TILE,), specs, [_rows(d), _rows(d)],
        [jax.ShapeDtypeStruct((t, d), f32), jax.ShapeDtypeStruct((t, d), bf16)], ins, (), ("parallel",), comm)
    return (out, out_bf) if comm is None else (out, out_bf, comm_outs)


def _ln_bwd(d_res, d_mm, x, y, g, b, alpha, name):
    t, d = x.shape
    has_res, has_mm, has_y = d_res is not None, d_mm is not None, y is not None

    def body(*refs):
        refs = list(refs)
        d_res_ref = refs.pop(0) if has_res else None
        d_mm_ref = refs.pop(0) if has_mm else None
        x_ref = refs.pop(0)
        y_ref = refs.pop(0) if has_y else None
        g_ref, b_ref, dz_ref, dzb_ref, dg_ref, db_ref = refs
        ct = None
        if has_res:
            ct = alpha * d_res_ref[...]
        if has_mm:
            ct = d_mm_ref[...] if ct is None else ct + d_mm_ref[...]
        z = x_ref[...] if not has_y else alpha * x_ref[...] + y_ref[...]
        _, vjp = jax.vjp(_layer_norm, z, g_ref[...], b_ref[...])
        dz, dg, db = vjp(ct)
        dz_ref[...] = dz
        dzb_ref[...] = dz.astype(bf16)

        @pl.when(pl.program_id(0) == 0)
        def _():
            dg_ref[...] = jnp.zeros_like(dg_ref)
            db_ref[...] = jnp.zeros_like(db_ref)

        dg_ref[...] += dg
        db_ref[...] += db

    ins = [a for a in (d_res, d_mm, x, y) if a is not None] + [g, b]
    specs = [_rows(d) for a in (d_res, d_mm, x, y) if a is not None] + [_whole((1, d)), _whole((1, d))]
    return pl.pallas_call(
        body, name=name, grid=(t // ROW_TILE,), in_specs=specs,
        out_specs=[_rows(d), _rows(d), _whole((1, d)), _whole((1, d))],
        out_shape=[jax.ShapeDtypeStruct((t, d), f32), jax.ShapeDtypeStruct((t, d), bf16),
                   jax.ShapeDtypeStruct((1, d), f32), jax.ShapeDtypeStruct((1, d), f32)],
        compiler_params=_cparams(("arbitrary",)),
    )(*ins)


def _loss_call(xf, target):
    t, d = xf.shape

    def body(x_ref, t_ref, dx_ref, loss_ref):
        err = x_ref[...] - t_ref[...]
        dx_ref[...] = err * (1.0 / d)

        @pl.when(pl.program_id(0) == 0)
        def _():
            loss_ref[...] = jnp.zeros_like(loss_ref)

        part = 0.5 * jnp.sum(jnp.mean(jnp.square(err), axis=-1, keepdims=True), axis=0, keepdims=True)
        loss_ref[...] += jnp.broadcast_to(part, loss_ref.shape)

    return pl.pallas_call(
        body, name="loss_head", grid=(t // ROW_TILE,), in_specs=[_rows(d), _rows(d)],
        out_specs=[_rows(d), _whole((1, LANES))],
        out_shape=[jax.ShapeDtypeStruct((t, d), f32), jax.ShapeDtypeStruct((1, LANES), f32)],
        compiler_params=_cparams(("arbitrary",)),
    )(xf, target)


def _rope_tables(pos, inv, scale):
    t = pos.shape[0]

    def body(pos_ref, inv_ref, cs_ref, qt_ref):
        ang = pos_ref[...].astype(f32) * inv_ref[...]
        lane = lax.broadcasted_iota(jnp.int32, ang.shape, 1)
        cs = jnp.where(lane < LANES // 2, jnp.cos(ang), jnp.sin(ang))
        cs_ref[...] = cs
        qt_ref[:, :LANES] = jnp.full((ROW_TILE, LANES), scale, f32)
        qt_ref[:, LANES:] = scale * cs

    return pl.pallas_call(
        body, name="rope_tables", grid=(t // ROW_TILE,),
        in_specs=[pl.BlockSpec((ROW_TILE, 1), lambda i: (i, 0)), _whole((1, LANES))],
        out_specs=[_rows(LANES), _rows(2 * LANES)],
        out_shape=[jax.ShapeDtypeStruct((t, LANES), f32), jax.ShapeDtypeStruct((t, 2 * LANES), f32)],
        compiler_params=_cparams(("parallel",)),
    )(pos, inv)


def _prep_fn(cq, ckv, kr, cs, gq, gkv):
    return _rms_norm(cq, gq), _rms_norm(ckv, gkv), _dup_halves(kr * cs)


def _prep_fwd(h, cs, gq, gkv, lay):
    t = h.shape[0]
    ql, kvl = lay["ql"], lay["kvl"]

    def body(cq_ref, ckv_ref, kr_ref, cs_ref, gq_ref, gkv_ref, qn_ref, kvn_ref, krd_ref):
        qn, kvn, krd = _prep_fn(cq_ref[...], ckv_ref[...], kr_ref[...], cs_ref[...], gq_ref[...], gkv_ref[...])
        qn_ref[...] = qn.astype(bf16)
        kvn_ref[...] = kvn.astype(bf16)
        krd_ref[...] = krd.astype(bf16)

    return pl.pallas_call(
        body, name="prep_fwd", grid=(t // ROW_TILE,),
        in_specs=[_rows(ql, lay["off_q"] // ql), _rows(kvl, lay["off_kv"] // kvl), _rows(LANES, lay["off_kr"] // LANES),
                  _rows(LANES), _whole((1, ql)), _whole((1, kvl))],
        out_specs=[_rows(ql), _rows(kvl), _rows(LANES)],
        out_shape=[jax.ShapeDtypeStruct((t, ql), bf16), jax.ShapeDtypeStruct((t, kvl), bf16),
                   jax.ShapeDtypeStruct((t, LANES), bf16)],
        compiler_params=_cparams(("parallel",)),
    )(h, h, h, cs, gq, gkv)


def _prep_bwd(h, cs, gq, gkv, d_qn, d_kvn, d_krd, lay):
    t = h.shape[0]
    ql, kvl = lay["ql"], lay["kvl"]

    def body(cq_ref, ckv_ref, kr_ref, cs_ref, gq_ref, gkv_ref, dqn_ref, dkvn_ref, dkrd_ref,
             dcq_ref, dckv_ref, dkr_ref, dgq_ref, dgkv_ref):
        _, vjp = jax.vjp(_prep_fn, cq_ref[...], ckv_ref[...], kr_ref[...], cs_ref[...], gq_ref[...], gkv_ref[...])
        dcq, dckv, dkr, _, dgq, dgkv = vjp((dqn_ref[...], dkvn_ref[...], dkrd_ref[...].astype(f32)))
        dcq_ref[...] = dcq.astype(bf16)
        dckv_ref[...] = dckv.astype(bf16)
        dkr_ref[...] = dkr.astype(bf16)

        @pl.when(pl.program_id(0) == 0)
        def _():
            dgq_ref[...] = jnp.zeros_like(dgq_ref)
            dgkv_ref[...] = jnp.zeros_like(dgkv_ref)

        dgq_ref[...] += dgq
        dgkv_ref[...] += dgkv

    return pl.pallas_call(
        body, name="prep_bwd", grid=(t // ROW_TILE,),
        in_specs=[_rows(ql, lay["off_q"] // ql), _rows(kvl, lay["off_kv"] // kvl), _rows(LANES, lay["off_kr"] // LANES),
                  _rows(LANES), _whole((1, ql)), _whole((1, kvl)), _rows(ql), _rows(kvl), _rows(LANES)],
        out_specs=[_rows(ql), _rows(kvl), _rows(LANES), _whole((1, ql)), _whole((1, kvl))],
        out_shape=[jax.ShapeDtypeStruct((t, ql), bf16), jax.ShapeDtypeStruct((t, kvl), bf16),
                   jax.ShapeDtypeStruct((t, LANES), bf16), jax.ShapeDtypeStruct((1, ql), f32),
                   jax.ShapeDtypeStruct((1, kvl), f32)],
        compiler_params=_cparams(("arbitrary",)),
    )(h, h, h, cs, gq, gkv, d_qn, d_kvn, d_krd)


def _convln_fn(hc, g, b):
    y = _layer_norm(hc, g, b)
    return y * jax.nn.sigmoid(y)


def _convln_fwd(hconv, g, b):
    t, cw = hconv.shape

    def body(h_ref, g_ref, b_ref, o_ref):
        o_ref[...] = _convln_fn(h_ref[...], g_ref[...], b_ref[...]).astype(bf16)

    return pl.pallas_call(
        body, name="convln_fwd", grid=(t // ROW_TILE,),
        in_specs=[_rows(cw), _whole((1, cw)), _whole((1, cw))], out_specs=_rows(cw),
        out_shape=jax.ShapeDtypeStruct((t, cw), bf16), compiler_params=_cparams(("parallel",)),
    )(hconv, g, b)


def _convln_bwd(hconv, g, b, d_mixed, col_block):
    t, cw = hconv.shape

    def body(h_ref, g_ref, b_ref, dy_ref, dh_ref, dg_ref, db_ref):
        _, vjp = jax.vjp(_convln_fn, h_ref[...], g_ref[...], b_ref[...])
        dh, dg, db = vjp(dy_ref[...])
        dh_ref[...] = dh

        @pl.when(pl.program_id(0) == 0)
        def _():
            dg_ref[...] = jnp.zeros_like(dg_ref)
            db_ref[...] = jnp.zeros_like(db_ref)

        dg_ref[...] += dg
        db_ref[...] += db

    return pl.pallas_call(
        body, name="convln_bwd", grid=(t // ROW_TILE,),
        in_specs=[_rows(cw), _whole((1, cw)), _whole((1, cw)), _rows(cw, col_block)],
        out_specs=[_rows(cw), _whole((1, cw)), _whole((1, cw))],
        out_shape=[jax.ShapeDtypeStruct((t, cw), f32), jax.ShapeDtypeStruct((1, cw), f32),
                   jax.ShapeDtypeStruct((1, cw), f32)],
        compiler_params=_cparams(("arbitrary",)),
    )(hconv, g, b, d_mixed)


SEQ_CT = 128


def _conv_fwd(h, w, b, nb, seq, cw):
    ct, kk = SEQ_CT, w.shape[0]
    ncb = cw // ct

    def body(h_ref, w_ref, b_ref, o_ref):
        blk = h_ref[...]
        a, g = blk[:, :ct], blk[:, ct:]
        hc = a * jax.nn.sigmoid(g)
        acc = jnp.zeros_like(hc)
        for k in range(kk):
            acc = acc + w_ref[k:k + 1, :] * _shift_down_raw(hc, kk - 1 - k)
        o_ref[...] = acc + b_ref[...]

    return pl.pallas_call(
        body, name="conv_fwd", grid=(ncb, nb),
        in_specs=[pl.BlockSpec((seq, 2 * ct), lambda j, bb: (bb, j)), pl.BlockSpec((kk, ct), lambda j, bb: (0, j)),
                  pl.BlockSpec((1, ct), lambda j, bb: (0, j))],
        out_specs=pl.BlockSpec((seq, ct), lambda j, bb: (bb, j)),
        out_shape=jax.ShapeDtypeStruct((nb * seq, cw), f32),
        compiler_params=_cparams(("parallel", "parallel")),
    )(h, w, b)


def _conv_bwd(h, w, d_hconv, nb, seq, cw):
    ct, kk = SEQ_CT, w.shape[0]
    ncb = cw // ct

    def body(h_ref, w_ref, dy_ref, dh_ref, dw_ref, db_ref):
        blk = h_ref[...]
        a, g = blk[:, :ct], blk[:, ct:]
        sg = jax.nn.sigmoid(g)
        hc = a * sg
        dy = dy_ref[...]
        dhc = jnp.zeros_like(hc)

        @pl.when(pl.program_id(1) == 0)
        def _():
            dw_ref[...] = jnp.zeros_like(dw_ref)
            db_ref[...] = jnp.zeros_like(db_ref)

        for k in range(kk):
            dhc = dhc + w_ref[k:k + 1, :] * _shift_up_raw(dy, kk - 1 - k)
            dw_ref[k:k + 1, :] += _colsum(dy * _shift_down_raw(hc, kk - 1 - k))
        db_ref[...] += _colsum(dy)
        dh_ref[:, :ct] = (dhc * sg).astype(bf16)
        dh_ref[:, ct:] = (dhc * a * sg * (1.0 - sg)).astype(bf16)

    return pl.pallas_call(
        body, name="conv_bwd", grid=(ncb, nb),
        in_specs=[pl.BlockSpec((seq, 2 * ct), lambda j, bb: (bb, j)), pl.BlockSpec((kk, ct), lambda j, bb: (0, j)),
                  pl.BlockSpec((seq, ct), lambda j, bb: (bb, j))],
        out_specs=[pl.BlockSpec((seq, 2 * ct), lambda j, bb: (bb, j)), pl.BlockSpec((kk, ct), lambda j, bb: (0, j)),
                   pl.BlockSpec((1, ct), lambda j, bb: (0, j))],
        out_shape=[jax.ShapeDtypeStruct((nb * seq, 2 * cw), bf16), jax.ShapeDtypeStruct((kk, cw), f32),
                   jax.ShapeDtypeStruct((1, cw), f32)],
        compiler_params=_cparams(("parallel", "arbitrary")),
    )(h, w, d_hconv)


def _pool_fn(u, wp, scale, pg):
    seq = u.shape[0]
    t1 = (lax.broadcasted_iota(jnp.int32, (seq, 1), 0) + 1).astype(f32)
    outs = []
    for gi, win in enumerate(POOL_WINDOWS):
        ug = u[:, gi * pg:(gi + 1) * pg]
        acc, span = ug, 1
        while span < win:
            acc = acc + _shift_down(acc, span)
            span *= 2
        d = acc / jnp.minimum(t1, float(win)) - ug
        outs.append(_mm_bf16(d, wp[gi * pg:(gi + 1) * pg, :]) * scale[:, gi * pg:(gi + 1) * pg])
    return outs


def _pool_fwd(h, wp, scale, lay, nb, seq):
    pw, pg = lay["pw"], lay["pg"]

    def body(u_ref, wp_ref, sc_ref, o_ref):
        outs = _pool_fn(u_ref[...], wp_ref[...], sc_ref[...], pg)
        for gi in range(len(POOL_WINDOWS)):
            o_ref[:, gi * pg:(gi + 1) * pg] = outs[gi].astype(bf16)

    return pl.pallas_call(
        body, name="pool_fwd", grid=(nb,),
        in_specs=[pl.BlockSpec((seq, pw), lambda bb: (bb, lay["off_pool"] // pw)), _whole((pw, pg)), _whole((1, pw))],
        out_specs=pl.BlockSpec((seq, pw), lambda bb: (bb, 0)),
        out_shape=jax.ShapeDtypeStruct((nb * seq, pw), bf16),
        compiler_params=_cparams(("parallel",)),
    )(h, wp, scale)


def _pool_bwd(h, wp, scale, d_mixed, col_block, lay, nb, seq):
    pw, pg = lay["pw"], lay["pg"]
    ng = len(POOL_WINDOWS)

    def body(u_ref, wp_ref, sc_ref, dy_ref, du_ref, dwp_ref, dsc_ref):
        _, vjp = jax.vjp(functools.partial(_pool_fn, pg=pg), u_ref[...], wp_ref[...], sc_ref[...])
        dy = dy_ref[...]
        du, dwp, dsc = vjp([dy[:, gi * pg:(gi + 1) * pg] for gi in range(ng)])
        du_ref[...] = du.astype(bf16)

        @pl.when(pl.program_id(0) == 0)
        def _():
            dwp_ref[...] = jnp.zeros_like(dwp_ref)
            dsc_ref[...] = jnp.zeros_like(dsc_ref)

        dwp_ref[...] += dwp
        dsc_ref[...] += dsc

    return pl.pallas_call(
        body, name="pool_bwd", grid=(nb,),
        in_specs=[pl.BlockSpec((seq, pw), lambda bb: (bb, lay["off_pool"] // pw)), _whole((pw, pg)), _whole((1, pw)),
                  pl.BlockSpec((seq, pw), lambda bb: (bb, col_block))],
        out_specs=[pl.BlockSpec((seq, pw), lambda bb: (bb, 0)), _whole((pw, pg)), _whole((1, pw))],
        out_shape=[jax.ShapeDtypeStruct((nb * seq, pw), bf16), jax.ShapeDtypeStruct((pw, pg), f32),
                   jax.ShapeDtypeStruct((1, pw), f32)],
        compiler_params=_cparams(("arbitrary",)),
    )(h, wp, scale, d_mixed)


FFN_ROWS = 32
SUBLANES = 8


def _rows_before(ref, r0, s, n):
    if s == 0:
        return ref[r0:r0 + n, :]
    if r0 == 0:
        x = ref[0:n, :]
        row = lax.broadcasted_iota(jnp.int32, x.shape, 0)
        return jnp.where(row >= s, pltpu.roll(x, s, axis=0), 0.0)
    return ref[pl.ds(r0 - s, n), :]


def _ffn_conv_rows(x_ref, w_ref, b_ref, r0, n):
    kk = w_ref.shape[0]
    xs = [_rows_before(x_ref, r0, s, n) for s in range(kk)]
    c = b_ref[...] + w_ref[kk - 1:kk, :] * xs[0]
    for k in range(kk - 1):
        c = c + w_ref[k:k + 1, :] * xs[kk - 1 - k]
    return c, xs


def _fold_rows(x):
    out = x[0:SUBLANES]
    for i in range(1, x.shape[0] // SUBLANES):
        out = out + x[i * SUBLANES:(i + 1) * SUBLANES]
    return out


def _ffn_specs(seq, ct, kk, nct):
    return [pl.BlockSpec((seq, ct), lambda j, bb: (bb, j)), pl.BlockSpec((seq, ct), lambda j, bb: (bb, nct + j)),
            pl.BlockSpec((kk, ct), lambda j, bb: (0, j)), pl.BlockSpec((kk, ct), lambda j, bb: (0, nct + j)),
            pl.BlockSpec((1, ct), lambda j, bb: (0, j)), pl.BlockSpec((1, ct), lambda j, bb: (0, nct + j))]


def _ffn_act_fwd(up, w, b, nb, seq, dff, comm=None):
    ct, kk = SEQ_CT, w.shape[0]
    rows = min(FFN_ROWS, seq)

    def body(ua_ref, ug_ref, wa_ref, wg_ref, ba_ref, bg_ref, o_ref):
        for r0 in range(0, seq, rows):
            a, _ = _ffn_conv_rows(ua_ref, wa_ref, ba_ref, r0, rows)
            g, _ = _ffn_conv_rows(ug_ref, wg_ref, bg_ref, r0, rows)
            o_ref[r0:r0 + rows, :] = (a * g * jax.nn.sigmoid(g)).astype(bf16)

    (act,), comm_outs = _call(body, "ffn_act_fwd", (dff // ct, nb), _ffn_specs(seq, ct, kk, dff // ct),
                              [pl.BlockSpec((seq, ct), lambda j, bb: (bb, j))],
                              [jax.ShapeDtypeStruct((nb * seq, dff), bf16)], (up, up, w, w, b, b), (),
                              ("parallel", "parallel"), comm)
    return act if comm is None else (act, comm_outs)


def _ffn_act_bwd(up, w, b, d_act, nb, seq, dff, comm=None):
    ct, kk = SEQ_CT, w.shape[0]
    rows = min(FFN_ROWS, seq)

    def body(ua_ref, ug_ref, wa_ref, wg_ref, ba_ref, bg_ref, da_ref, du_ref, dw_ref, db_ref, dc_ref):
        @pl.when(pl.program_id(1) == 0)
        def _():
            dw_ref[...] = jnp.zeros_like(dw_ref)
            db_ref[...] = jnp.zeros_like(db_ref)

        dw_acc = [[jnp.zeros((SUBLANES, ct), f32) for _ in range(kk)] for _ in range(2)]
        db_acc = [jnp.zeros((SUBLANES, ct), f32) for _ in range(2)]
        for r0 in range(0, seq, rows):
            a, xa = _ffn_conv_rows(ua_ref, wa_ref, ba_ref, r0, rows)
            g, xg = _ffn_conv_rows(ug_ref, wg_ref, bg_ref, r0, rows)
            sg = jax.nn.sigmoid(g)
            dact = da_ref[r0:r0 + rows, :]
            dcs = (dact * g * sg, dact * a * sg * (1.0 + g * (1.0 - sg)))
            for hf, (dc, xs) in enumerate(zip(dcs, (xa, xg))):
                dc_ref[hf, r0:r0 + rows, :] = dc
                db_acc[hf] = db_acc[hf] + _fold_rows(dc)
                for k in range(kk):
                    dw_acc[hf][k] = dw_acc[hf][k] + _fold_rows(dc * xs[kk - 1 - k])
        for hf in range(2):
            dc_ref[hf, seq:seq + SUBLANES, :] = jnp.zeros((SUBLANES, ct), f32)
            db_ref[hf] += _colsum(db_acc[hf])
            for k in range(kk):
                dw_ref[hf, k:k + 1, :] += _colsum(dw_acc[hf][k])
        for r0 in range(0, seq, rows):
            for hf, w_ref in enumerate((wa_ref, wg_ref)):
                dsrc = w_ref[kk - 1:kk, :] * dc_ref[hf, r0:r0 + rows, :]
                for k in range(kk - 1):
                    dsrc = dsrc + w_ref[k:k + 1, :] * dc_ref[hf, pl.ds(r0 + kk - 1 - k, rows), :]
                du_ref[hf, r0:r0 + rows, :] = dsrc.astype(bf16)

    (d_up, dw, db), comm_outs = _call(
        body, "ffn_act_bwd", (dff // ct, nb),
        _ffn_specs(seq, ct, kk, dff // ct) + [pl.BlockSpec((seq, ct), lambda j, bb: (bb, j))],
        [pl.BlockSpec((2, seq, ct), lambda j, bb: (0, bb, j)), pl.BlockSpec((2, kk, ct), lambda j, bb: (0, 0, j)),
         pl.BlockSpec((2, 1, ct), lambda j, bb: (0, 0, j))],
        [jax.ShapeDtypeStruct((2, nb * seq, dff), bf16), jax.ShapeDtypeStruct((2, kk, dff), f32),
         jax.ShapeDtypeStruct((2, 1, dff), f32)],
        (up, up, w, w, b, b, d_act), [pltpu.VMEM((2, seq + SUBLANES, ct), f32)], ("parallel", "arbitrary"), comm)
    return (d_up, dw, db) if comm is None else (d_up, dw, db, comm_outs)


def _fill_keys(k_scr, kn_ref, krd_ref):
    @pl.when(pl.program_id(2) == 0)
    def _():
        k_scr[:, :LANES] = kn_ref[...]
        k_scr[:, LANES:] = krd_ref[...]


def _scores(q_ref, qt_ref, k_scr, qblk, tq):
    klen = (qblk + 1) * tq
    q = (q_ref[...] * qt_ref[...]).astype(bf16)
    s = _dot(q, k_scr[0:klen, :], _NT)
    row = qblk * tq + lax.broadcasted_iota(jnp.int32, s.shape, 0)
    col = lax.broadcasted_iota(jnp.int32, s.shape, 1)
    return q, jnp.where(col <= row, s, NEG_INF)


def _per_q_block(nq, fn):
    qi = pl.program_id(2)
    for qblk in range(nq):
        pl.when(qi == qblk)(functools.partial(fn, qblk))


def _attn_fwd(q_ext, qt, kv, krd, nb, seq, heads, comm=None):
    tq = _tile(seq, 512)
    nq = seq // tq

    def body(q_ref, qt_ref, kn_ref, krd_ref, v_ref, o_ref, lse_ref, k_scr):
        _fill_keys(k_scr, kn_ref, krd_ref)

        def work(qblk):
            klen = (qblk + 1) * tq
            _, s = _scores(q_ref, qt_ref, k_scr, qblk, tq)
            m = jnp.max(s, axis=-1, keepdims=True)
            p = jnp.exp(s - m)
            l = jnp.sum(p, axis=-1, keepdims=True)
            o_ref[...] = (_dot(p, v_ref[0:klen, :], _NN) / l).astype(bf16)
            lse_ref[...] = m + jnp.log(l)

        _per_q_block(nq, work)

    (o, lse), comm_outs = _call(
        body, "attn_fwd", (nb, heads, nq),
        [pl.BlockSpec((tq, 2 * LANES), lambda b, h, i: (b * nq + i, h)),
         pl.BlockSpec((tq, 2 * LANES), lambda b, h, i: (b * nq + i, 0)),
         pl.BlockSpec((seq, LANES), lambda b, h, i: (b, 2 * h)),
         pl.BlockSpec((seq, LANES), lambda b, h, i: (b, 0)),
         pl.BlockSpec((seq, LANES), lambda b, h, i: (b, 2 * h + 1))],
        [pl.BlockSpec((tq, LANES), lambda b, h, i: (b * nq + i, h)),
         pl.BlockSpec((None, tq, 1), lambda b, h, i: (b * heads + h, i, 0))],
        [jax.ShapeDtypeStruct((nb * seq, heads * LANES), bf16), jax.ShapeDtypeStruct((nb * heads, seq, 1), f32)],
        (q_ext, qt, kv, krd, kv), [pltpu.VMEM((seq, 2 * LANES), bf16)], ("parallel", "parallel", "arbitrary"), comm)
    return (o, lse) if comm is None else (o, lse, comm_outs)


def _attn_bwd(q_ext, qt, kv, krd, lse, d_mixed, nb, seq, heads, comm=None):
    tq = _tile(seq, 512)
    nq = seq // tq

    def body(q_ref, qt_ref, kn_ref, krd_ref, v_ref, lse_ref, do_ref, dq_ref, dkv_ref, dkrd_ref, dkv_acc, k_scr):
        h, qi = pl.program_id(1), pl.program_id(2)
        _fill_keys(k_scr, kn_ref, krd_ref)

        @pl.when(qi == 0)
        def _():
            dkv_acc[...] = jnp.zeros_like(dkv_acc)

        @pl.when((qi == 0) & (h == 0))
        def _():
            dkrd_ref[...] = jnp.zeros_like(dkrd_ref)

        def work(qblk):
            klen = (qblk + 1) * tq
            q, s = _scores(q_ref, qt_ref, k_scr, qblk, tq)
            p = jnp.exp(s - lse_ref[...])
            do = do_ref[...]
            dp = _dot(do, v_ref[0:klen, :], _NT)
            ds = (p * (dp - jnp.sum(p * dp, axis=-1, keepdims=True))).astype(bf16)
            dq_ref[...] = (_dot(ds, k_scr[0:klen, :], _NN) * qt_ref[...]).astype(bf16)
            dk = _dot(ds, q, _TN)
            dkv_acc[0:klen, :LANES] += dk[:, :LANES]
            dkv_acc[0:klen, LANES:] += _dot(p, do, _TN)
            dkrd_ref[0:klen, :] += dk[:, LANES:]

        _per_q_block(nq, work)

        @pl.when(qi == nq - 1)
        def _():
            dkv_ref[...] = dkv_acc[...].astype(bf16)

    (dq, dkv, dkrd), comm_outs = _call(
        body, "attn_bwd", (nb, heads, nq),
        [pl.BlockSpec((tq, 2 * LANES), lambda b, h, i: (b * nq + i, h)),
         pl.BlockSpec((tq, 2 * LANES), lambda b, h, i: (b * nq + i, 0)),
         pl.BlockSpec((seq, LANES), lambda b, h, i: (b, 2 * h)),
         pl.BlockSpec((seq, LANES), lambda b, h, i: (b, 0)),
         pl.BlockSpec((seq, LANES), lambda b, h, i: (b, 2 * h + 1)),
         pl.BlockSpec((None, tq, 1), lambda b, h, i: (b * heads + h, i, 0)),
         pl.BlockSpec((tq, LANES), lambda b, h, i: (b * nq + i, h))],
        [pl.BlockSpec((tq, 2 * LANES), lambda b, h, i: (b * nq + i, h)),
         pl.BlockSpec((seq, 2 * LANES), lambda b, h, i: (b, h)),
         pl.BlockSpec((seq, LANES), lambda b, h, i: (b, 0))],
        [jax.ShapeDtypeStruct((nb * seq, heads * 2 * LANES), bf16),
         jax.ShapeDtypeStruct((nb * seq, heads * 2 * LANES), bf16),
         jax.ShapeDtypeStruct((nb * seq, LANES), f32)],
        (q_ext, qt, kv, krd, kv, lse, d_mixed), [pltpu.VMEM((seq, 2 * LANES), f32), pltpu.VMEM((seq, 2 * LANES), bf16)],
        ("parallel", "arbitrary", "arbitrary"), comm)
    return (dq, dkv, dkrd) if comm is None else (dq, dkv, dkrd, comm_outs)


def _mesh_pos():
    x, y, c = lax.axis_index("x"), lax.axis_index("y"), lax.axis_index("c")
    return x, y, c, [(1 - x, y), (x, 1 - y), (1 - x, 1 - y)]


def _all_gather(shards, name):
    n = len(shards)

    def body(*refs):
        x_refs, out_refs, (send_sems, recv_sems, local_sems) = refs[:n], refs[n:2 * n], refs[2 * n:]
        x, y, c, chips = _mesh_pos()
        me, sibling = (x, y, c), (x, y, 1 - c)

        def copy(t, k, block, to, from_shard=False):
            px, py, pc = block
            rows = out_refs[t].at[4 * px + 2 * py + pc]
            return pltpu.make_async_remote_copy(
                src_ref=x_refs[t] if from_shard else rows, dst_ref=rows,
                send_sem=send_sems.at[t, k], recv_sem=recv_sems.at[t, k], device_id=to, device_id_type=MESH)

        mine = [pltpu.make_async_copy(x_refs[t], out_refs[t].at[4 * x + 2 * y + c], local_sems.at[t]) for t in range(n)]
        first = [[copy(t, 0, me, sibling, True)] + [copy(t, 1 + j, me, (*chip, c), True) for j, chip in enumerate(chips)]
                 for t in range(n)]
        passed = [[copy(t, 4 + j, (*chip, c), sibling) for j, chip in enumerate(chips)] for t in range(n)]
        for t in range(n):
            mine[t].start()
            for cp in first[t]:
                cp.start()
        for j, chip in enumerate(chips):
            for t in range(n):
                copy(t, 1 + j, (*chip, c), me).wait_recv()
                passed[t][j].start()
        for t in range(n):
            copy(t, 0, sibling, me).wait_recv()
            for j, chip in enumerate(chips):
                copy(t, 4 + j, (*chip, 1 - c), me).wait_recv()
        for t in range(n):
            for cp in first[t] + passed[t]:
                cp.wait_send()
            mine[t].wait()

    return pl.pallas_call(
        body, name=name, out_shape=[jax.ShapeDtypeStruct((N_DEV,) + s.shape, s.dtype) for s in shards],
        in_specs=[_ANY] * n, out_specs=[_ANY] * n,
        scratch_shapes=[pltpu.SemaphoreType.DMA((n, 7)), pltpu.SemaphoreType.DMA((n, 7)), pltpu.SemaphoreType.DMA((n,))],
    )(*shards)


def _gather_own(shards):
    n = len(shards)

    def remote(x_refs, out_refs, sems, arriving):
        send_sems, recv_sems, _ = sems
        x, y, c, chips = _mesh_pos()
        peers = [(x, y, 1 - c)] + [(*chip, c) for chip in chips]
        return [pltpu.make_async_remote_copy(
            src_ref=x_refs[t], dst_ref=out_refs[t].at[4 * px + 2 * py + pc if arriving else 4 * x + 2 * y + c],
            send_sem=send_sems.at[t, k], recv_sem=recv_sems.at[t, k], device_id=(px, py, pc), device_id_type=MESH)
            for t in range(n) for k, (px, py, pc) in enumerate(peers)]

    def local(x_refs, out_refs, sems):
        x, y, c, _ = _mesh_pos()
        return [pltpu.make_async_copy(x_refs[t], out_refs[t].at[4 * x + 2 * y + c], sems[2].at[t]) for t in range(n)]

    def start(x_refs, out_refs, sems):
        for cp in local(x_refs, out_refs, sems) + remote(x_refs, out_refs, sems, False):
            cp.start()

    def finish(x_refs, out_refs, sems):
        for cp in remote(x_refs, out_refs, sems, True):
            cp.wait_recv()
        for cp in remote(x_refs, out_refs, sems, False):
            cp.wait_send()
        for cp in local(x_refs, out_refs, sems):
            cp.wait()

    return _Comm(shards, [jax.ShapeDtypeStruct((N_DEV,) + s.shape, s.dtype) for s in shards],
                 [pltpu.SemaphoreType.DMA((n, 4)), pltpu.SemaphoreType.DMA((n, 4)), pltpu.SemaphoreType.DMA((n,))],
                 start, finish)


def _gather_pass(gathered):
    n = len(gathered)

    def copies(in_refs, out_refs, sems, arriving):
        send_sems, recv_sems = sems
        x, y, c, chips = _mesh_pos()
        return [pltpu.make_async_remote_copy(
            src_ref=in_refs[t].at[4 * px + 2 * py + c],
            dst_ref=out_refs[t].at[4 * px + 2 * py + (1 - c if arriving else c)],
            send_sem=send_sems.at[t, j], recv_sem=recv_sems.at[t, j], device_id=(x, y, 1 - c), device_id_type=MESH)
            for t in range(n) for j, (px, py) in enumerate(chips)]

    def start(in_refs, out_refs, sems):
        for cp in copies(in_refs, out_refs, sems, False):
            cp.start()

    def finish(in_refs, out_refs, sems):
        for cp in copies(in_refs, out_refs, sems, True):
            cp.wait_recv()
        for cp in copies(in_refs, out_refs, sems, False):
            cp.wait_send()

    return _Comm(gathered, [jax.ShapeDtypeStruct(g.shape, g.dtype) for g in gathered],
                 [pltpu.SemaphoreType.DMA((n, 3)), pltpu.SemaphoreType.DMA((n, 3))], start, finish,
                 aliases={t: t for t in range(n)})


def _sibling_swap(slots):
    n = len(slots)

    def start(g_refs, out_refs, sems):
        send_sems, recv_sems = sems
        x, y, c, _ = _mesh_pos()
        for t in range(n):
            for k in range(4):
                pltpu.make_async_remote_copy(
                    src_ref=g_refs[t].at[2 * k + (1 - c)], dst_ref=out_refs[t].at[k], send_sem=send_sems.at[t],
                    recv_sem=recv_sems.at[t], device_id=(x, y, 1 - c), device_id_type=MESH).start()

    def finish(g_refs, out_refs, sems):
        send_sems, recv_sems = sems
        x, y, c, _ = _mesh_pos()
        for t in range(n):
            pltpu.make_async_remote_copy(
                src_ref=g_refs[t].at[pl.ds(0, 4)], dst_ref=out_refs[t], send_sem=send_sems.at[t],
                recv_sem=recv_sems.at[t], device_id=(x, y, 1 - c), device_id_type=MESH).wait()

    return _Comm(slots, [jax.ShapeDtypeStruct((4,) + s.shape[1:], s.dtype) for s in slots],
                 [pltpu.SemaphoreType.DMA((n,)), pltpu.SemaphoreType.DMA((n,))], start, finish)


def _chip_swap(p4s):
    n = len(p4s)

    def copies(p_refs, out_refs, sems):
        send_sems, recv_sems = sems
        x, y, c, chips = _mesh_pos()
        return [pltpu.make_async_remote_copy(
            src_ref=p_refs[t].at[2 * px + py], dst_ref=out_refs[t].at[j], send_sem=send_sems.at[t, j],
            recv_sem=recv_sems.at[t, j], device_id=(px, py, c), device_id_type=MESH)
            for t in range(n) for j, (px, py) in enumerate(chips)]

    def start(p_refs, out_refs, sems):
        for cp in copies(p_refs, out_refs, sems):
            cp.start()

    def finish(p_refs, out_refs, sems):
        for cp in copies(p_refs, out_refs, sems):
            cp.wait()

    return _Comm(p4s, [jax.ShapeDtypeStruct((3,) + p.shape[1:], p.dtype) for p in p4s],
                 [pltpu.SemaphoreType.DMA((n, 3)), pltpu.SemaphoreType.DMA((n, 3))], start, finish)


def _comm_join(a, b):
    ai, ao, asem = len(a.inputs), len(a.out_shapes), len(a.sems)

    def start(ins, outs, sems):
        a.start(ins[:ai], outs[:ao], sems[:asem])
        b.start(ins[ai:], outs[ao:], sems[asem:])

    def finish(ins, outs, sems):
        a.finish(ins[:ai], outs[:ao], sems[:asem])
        b.finish(ins[ai:], outs[ao:], sems[asem:])

    aliases = dict(a.aliases)
    aliases.update({ai + i: ao + o for i, o in b.aliases.items()})
    return _Comm(a.inputs + b.inputs, a.out_shapes + b.out_shapes, a.sems + b.sems, start, finish, aliases)


def _comm_call(comm, name):
    c_in, c_out = len(comm.inputs), len(comm.out_shapes)

    def body(*refs):
        ins, outs, sems = refs[:c_in], refs[c_in:c_in + c_out], refs[c_in + c_out:]
        comm.start(ins, outs, sems)
        comm.finish(ins, outs, sems)

    return pl.pallas_call(
        body, name=name, out_shape=comm.out_shapes, in_specs=[_ANY] * c_in, out_specs=[_ANY] * c_out,
        scratch_shapes=comm.sems, input_output_aliases=comm.aliases,
    )(*comm.inputs)


def _row_tile(rows, cols, max_bytes=1024 * 1024):
    best = None
    for tr in range(16, rows + 1, 16):
        if rows % tr == 0 and tr * cols * 4 <= max_bytes:
            best = tr
    return best or rows


def _pair_add(slots, theirs, core, name):
    _, rows, cols = slots.shape
    tr = _row_tile(rows, cols, 4 * 1024 * 1024)

    def body(c_ref, a_ref, b_ref, o_ref):
        o_ref[...] = (a_ref[...].astype(f32) + b_ref[...].astype(f32)).astype(bf16)

    return pl.pallas_call(
        body, name=name,
        grid_spec=pltpu.PrefetchScalarGridSpec(
            num_scalar_prefetch=1, grid=(4, rows // tr),
            in_specs=[pl.BlockSpec((None, tr, cols), lambda k, i, c_ref: (2 * k + c_ref[0], i, 0)),
                      pl.BlockSpec((None, tr, cols), lambda k, i, c_ref: (k, i, 0))],
            out_specs=pl.BlockSpec((None, tr, cols), lambda k, i, c_ref: (k, i, 0))),
        out_shape=jax.ShapeDtypeStruct((4, rows, cols), bf16), compiler_params=_cparams(("parallel", "parallel")),
    )(core, slots, theirs)


def _adam_update(g, w, m, v):
    c1 = 1.0 / (1.0 - ADAM_B1 ** ADAM_STEP)
    c2 = 1.0 / (1.0 - ADAM_B2 ** ADAM_STEP)
    nm = ADAM_B1 * m + (1.0 - ADAM_B1) * g
    nv = ADAM_B2 * v + (1.0 - ADAM_B2) * jnp.square(g)
    delta = -ADAM_LR * ((nm * c1) / (jnp.sqrt(nv * c2) + ADAM_EPS) + ADAM_WD * w)
    return delta, nm, nv


def _adamw_layers(p4s, chips, chip_idx, w, m, v, name):
    depth = len(p4s)
    _, rows_l, cols = p4s[0].shape
    tr = _row_tile(rows_l, cols)
    nr = rows_l // tr

    def body(idx_ref, *refs):
        p_refs, c_refs = refs[:depth], refs[depth:2 * depth]
        w_ref, m_ref, v_ref, g_ref, d_ref, nm_ref, nv_ref = refs[2 * depth:]
        layer = pl.program_id(0)
        for ll in range(depth):
            @pl.when(layer == ll)
            def _(ll=ll):
                g = p_refs[ll][...].astype(f32)
                for j in range(3):
                    g = g + c_refs[ll][j].astype(f32)
                delta, nm, nv = _adam_update(g, w_ref[...], m_ref[...], v_ref[...])
                g_ref[...] = g
                d_ref[...] = delta
                nm_ref[...] = nm
                nv_ref[...] = nv

    def of_layer(ll):
        return lambda l, i, idx_ref: jnp.where(l == ll, i, 0)

    p_specs = [pl.BlockSpec((None, tr, cols), lambda l, i, idx_ref, f=of_layer(ll): (idx_ref[0], f(l, i, idx_ref), 0))
               for ll in range(depth)]
    c_specs = [pl.BlockSpec((3, tr, cols), lambda l, i, idx_ref, f=of_layer(ll): (0, f(l, i, idx_ref), 0))
               for ll in range(depth)]
    spec = pl.BlockSpec((tr, cols), lambda l, i, idx_ref: (l * nr + i, 0))
    out = jax.ShapeDtypeStruct(w.shape, f32)
    return pl.pallas_call(
        body, name=name,
        grid_spec=pltpu.PrefetchScalarGridSpec(
            num_scalar_prefetch=1, grid=(depth, nr), in_specs=p_specs + c_specs + [spec, spec, spec],
            out_specs=[spec, spec, spec, spec]),
        out_shape=[out, out, out, out], compiler_params=_cparams(("parallel", "parallel")),
    )(chip_idx, *p4s, *chips, w, m, v)


def _adamw(parts, w, m, v, name):
    rows, cols = w.shape
    tr = _row_tile(rows, cols, 512 * 1024)
    nparts = len(parts)

    def body(*refs):
        part_refs, (w_ref, m_ref, v_ref, g_ref, d_ref, nm_ref, nv_ref) = refs[:nparts], refs[nparts:]
        g = None
        for pr in part_refs:
            for s in range(pr.shape[0]):
                term = pr[s].astype(f32)
                g = term if g is None else g + term
        delta, nm, nv = _adam_update(g, w_ref[...], m_ref[...], v_ref[...])
        g_ref[...] = g
        d_ref[...] = delta
        nm_ref[...] = nm
        nv_ref[...] = nv

    spec = pl.BlockSpec((tr, cols), lambda i: (i, 0))
    part_specs = [pl.BlockSpec((p.shape[0], tr, cols), lambda i: (0, i, 0)) for p in parts]
    out = jax.ShapeDtypeStruct((rows, cols), f32)
    return pl.pallas_call(
        body, name=name, grid=(rows // tr,), in_specs=part_specs + [spec, spec, spec],
        out_specs=[spec, spec, spec, spec], out_shape=[out, out, out, out],
        compiler_params=_cparams(("parallel",)),
    )(*parts, w, m, v)


def _gathered_to_full(gathered, axis):
    s = gathered.shape[1:]
    full = jnp.moveaxis(gathered, 0, axis)
    return full.reshape(s[:axis] + (N_DEV * s[axis],) + s[axis + 1:])


def _interleave_halves(w, ct):
    n = w.shape[-1] // 2
    t = w.reshape(w.shape[:-1] + (2, n // ct, ct))
    return jnp.swapaxes(t, -3, -2).reshape(w.shape)


def _deinterleave_halves(w, ct):
    n = w.shape[-1] // 2
    t = w.reshape(w.shape[:-1] + (n // ct, 2, ct))
    return jnp.swapaxes(t, -3, -2).reshape(w.shape)


def _rot_cols(w):
    half = QK_ROPE_DIM // 2
    return jnp.concatenate([-w[..., half:], w[..., :half]], axis=-1)


def _rot_cols_t(dw):
    half = QK_ROPE_DIM // 2
    return jnp.concatenate([dw[..., half:], -dw[..., :half]], axis=-1)


def kernel(x, positions, ln_in_g, ln_in_b, w_in, q_norm_g, w_uq, kv_norm_g, w_ukv, conv_w, conv_b, conv_ln_g, conv_ln_b, w_pool, pool_scale, w_out, ln1_g, ln1_b, w_up, ffn_conv_w, ffn_conv_b, w_down, ln2_g, ln2_b, loss_target, m_ln_in_g, m_ln_in_b, m_w_in, m_q_norm_g, m_w_uq, m_kv_norm_g, m_w_ukv, m_conv_w, m_conv_b, m_conv_ln_g, m_conv_ln_b, m_w_pool, m_pool_scale, m_w_out, m_ln1_g, m_ln1_b, m_w_up, m_ffn_conv_w, m_ffn_conv_b, m_w_down, m_ln2_g, m_ln2_b, v_ln_in_g, v_ln_in_b, v_w_in, v_q_norm_g, v_w_uq, v_kv_norm_g, v_w_ukv, v_conv_w, v_conv_b, v_conv_ln_g, v_conv_ln_b, v_w_pool, v_pool_scale, v_w_out, v_ln1_g, v_ln1_b, v_w_up, v_ffn_conv_w, v_ffn_conv_b, v_w_down, v_ln2_g, v_ln2_b):
    weights = dict(ln_in_g=ln_in_g, ln_in_b=ln_in_b, w_in=w_in, q_norm_g=q_norm_g, w_uq=w_uq, kv_norm_g=kv_norm_g,
                   w_ukv=w_ukv, conv_w=conv_w, conv_b=conv_b, conv_ln_g=conv_ln_g, conv_ln_b=conv_ln_b, w_pool=w_pool,
                   pool_scale=pool_scale, w_out=w_out, ln1_g=ln1_g, ln1_b=ln1_b, w_up=w_up, ffn_conv_w=ffn_conv_w,
                   ffn_conv_b=ffn_conv_b, w_down=w_down, ln2_g=ln2_g, ln2_b=ln2_b)
    mom1 = dict(ln_in_g=m_ln_in_g, ln_in_b=m_ln_in_b, w_in=m_w_in, q_norm_g=m_q_norm_g, w_uq=m_w_uq,
                kv_norm_g=m_kv_norm_g, w_ukv=m_w_ukv, conv_w=m_conv_w, conv_b=m_conv_b, conv_ln_g=m_conv_ln_g,
                conv_ln_b=m_conv_ln_b, w_pool=m_w_pool, pool_scale=m_pool_scale, w_out=m_w_out, ln1_g=m_ln1_g,
                ln1_b=m_ln1_b, w_up=m_w_up, ffn_conv_w=m_ffn_conv_w, ffn_conv_b=m_ffn_conv_b, w_down=m_w_down,
                ln2_g=m_ln2_g, ln2_b=m_ln2_b)
    mom2 = dict(ln_in_g=v_ln_in_g, ln_in_b=v_ln_in_b, w_in=v_w_in, q_norm_g=v_q_norm_g, w_uq=v_w_uq,
                kv_norm_g=v_kv_norm_g, w_ukv=v_w_ukv, conv_w=v_conv_w, conv_b=v_conv_b, conv_ln_g=v_conv_ln_g,
                conv_ln_b=v_conv_ln_b, w_pool=v_w_pool, pool_scale=v_pool_scale, w_out=v_w_out, ln1_g=v_ln1_g,
                ln1_b=v_ln1_b, w_up=v_w_up, ffn_conv_w=v_ffn_conv_w, ffn_conv_b=v_ffn_conv_b, w_down=v_w_down,
                ln2_g=v_ln2_g, ln2_b=v_ln2_b)
    names = list(weights)

    nb, seq, d = x.shape
    t = nb * seq
    depth = w_in.shape[0]
    ql, kvl, cw, pw = q_norm_g.shape[1], kv_norm_g.shape[1], conv_b.shape[1], pool_scale.shape[1]
    pg = w_pool.shape[-1]
    heads = w_uq.shape[2]
    dff = w_down.shape[1] * N_DEV
    alpha = (2.0 * depth) ** 0.25
    scale = float(QK_NOPE_DIM + QK_ROPE_DIM) ** -0.5
    lay = dict(ql=ql, kvl=kvl, pw=pw, pg=pg, off_q=2 * cw, off_pool=2 * cw + ql, off_kv=2 * cw + ql + pw,
               off_kr=2 * cw + ql + pw + kvl)
    o1, o2, o3, o4 = ql, ql + kvl, ql + kvl + QK_ROPE_DIM, ql + kvl + QK_ROPE_DIM + 2 * cw
    my_x, my_y, my_c = lax.axis_index("x"), lax.axis_index("y"), lax.axis_index("c")
    my_dev = 4 * my_x + 2 * my_y + my_c

    big = ("w_in", "w_uq", "w_ukv", "w_out", "w_up", "w_down")
    g_conv, g_ffn = _all_gather([conv_w, ffn_conv_w], "ag_conv_taps")
    conv_w_full, ffn_w_full = _gathered_to_full(g_conv, 2), _gathered_to_full(g_ffn, 2)
    w_pool_2d = w_pool.reshape(depth, pw, pg)

    rest = ("w_in", "w_uq", "w_ukv", "w_out")

    def bf16_shards(l, which):
        return [weights[n][l].astype(bf16) for n in which]

    def small_weights(gathered):
        g_in, g_uq, g_ukv, g_out = (gathered[n] for n in rest)
        wi = _gathered_to_full(g_in, 1)
        kr_cols = wi[:, o2:o3]
        w_in_pad = jnp.concatenate([_interleave_halves(wi[:, o3:o4], SEQ_CT), wi[:, :o1], wi[:, o4:], wi[:, o1:o2],
                                    kr_cols, _rot_cols(kr_cols)], axis=-1)
        wq = g_uq.reshape(ql, heads, QK_NOPE_DIM + QK_ROPE_DIM)
        w_uq_ext = jnp.concatenate([wq, _rot_cols(wq[..., QK_NOPE_DIM:])], axis=-1).reshape(ql, heads * 2 * LANES)
        return w_in_pad, w_uq_ext, g_ukv.reshape(kvl, heads * 2 * LANES), g_out.reshape(-1, d)

    half = QK_ROPE_DIM // 2
    inv = 1.0 / (ROPE_THETA ** (jnp.arange(0, QK_ROPE_DIM, 2, dtype=f32) / QK_ROPE_DIM))
    inv_lanes = jnp.tile(inv, LANES // half).reshape(1, LANES)
    cs, qt = _rope_tables(positions.reshape(t, 1), inv_lanes, scale)

    x2 = x.reshape(t, d)
    xs, xs_bf = _ln_fwd(x2, None, ln_in_g.reshape(1, d), ln_in_b.reshape(1, d), 1.0, "ln_in_fwd")
    saved = []
    in_pad = o4 + pw + QK_ROPE_DIM
    gathered = dict(zip(rest, _all_gather(bf16_shards(0, rest), "ag_weights")))
    for l in range(depth):
        nxt = l + 1 < depth
        gq, gkv = q_norm_g[l].reshape(1, ql), kv_norm_g[l].reshape(1, kvl)
        w_in_l, w_uq_l, w_ukv_l, w_out_l = small_weights(gathered)
        h = _matmul(xs_bf, w_in_l, "nn", f32, "mm_in", tm=512, tn=in_pad)
        qn, kvn, krd = _prep_fwd(h, cs, gq, gkv, lay)
        q_ext = _matmul(qn, w_uq_l, "nn", f32, "mm_uq", tn=2048)
        kv = _matmul(kvn, w_ukv_l, "nn", bf16, "mm_ukv", tn=2048)
        riders = ("w_down", "w_up") if l == 0 else ("w_down",)
        y_mla, lse, g_half = _attn_fwd(q_ext, qt, kv, krd, nb, seq, heads, comm=_gather_own(bf16_shards(l, riders)))
        hconv = _conv_fwd(h, conv_w_full[l], conv_b[l].reshape(1, cw), nb, seq, cw)
        y_conv = _convln_fwd(hconv, conv_ln_g[l].reshape(1, cw), conv_ln_b[l].reshape(1, cw))
        y_pool = _pool_fwd(h, w_pool_2d[l], pool_scale[l].reshape(1, pw), lay, nb, seq)
        mixed = jnp.concatenate([y_mla, y_conv, y_pool], axis=-1)
        y1, g_full = _matmul(mixed, w_out_l, "nn", f32, "mm_out", comm=_gather_pass(g_half))
        gathered.update(zip(riders, g_full))
        wl = dict(w_in=w_in_l, w_uq=w_uq_l, w_ukv=w_ukv_l, w_out=w_out_l, w_up=gathered["w_up"],
                  w_down=gathered["w_down"].reshape(dff, d))
        gathered = {}
        x1, x1_bf = _ln_fwd(xs, y1, ln1_g[l].reshape(1, d), ln1_b[l].reshape(1, d), alpha, "ln1_fwd")
        if nxt:
            up, (g_up,) = _matmul_up(x1_bf, wl["w_up"], "mm_up", comm=_gather_own(bf16_shards(l + 1, ("w_up",))))
            act, g_rest = _ffn_act_fwd(up, ffn_w_full[l], ffn_conv_b[l].reshape(1, 2 * dff), nb, seq, dff,
                                       comm=_gather_own(bf16_shards(l + 1, rest[1:])))
            y2, (gathered["w_up"], g_in) = _matmul(
                act, wl["w_down"], "nn", f32, "mm_down", tk=dff // 2,
                comm=_comm_join(_gather_pass([g_up]), _gather_own(bf16_shards(l + 1, rest[:1]))))
            xn, xn_bf, g_rest = _ln_fwd(x1, y2, ln2_g[l].reshape(1, d), ln2_b[l].reshape(1, d), alpha, "ln2_fwd",
                                        comm=_gather_pass([g_in] + g_rest))
            gathered.update(zip(rest, g_rest))
        else:
            up = _matmul_up(x1_bf, wl["w_up"], "mm_up")
            act = _ffn_act_fwd(up, ffn_w_full[l], ffn_conv_b[l].reshape(1, 2 * dff), nb, seq, dff)
            y2 = _matmul(act, wl["w_down"], "nn", f32, "mm_down", tk=dff // 2)
            xn, xn_bf = _ln_fwd(x1, y2, ln2_g[l].reshape(1, d), ln2_b[l].reshape(1, d), alpha, "ln2_fwd")
        saved.append(dict(xs=xs, xs_bf=xs_bf, h=h, qn=qn, kvn=kvn, krd=krd, q_ext=q_ext, kv=kv, lse=lse, hconv=hconv,
                          mixed=mixed, y1=y1, x1=x1, x1_bf=x1_bf, up=up, act=act, y2=y2, wl=wl))
        xs, xs_bf = xn, xn_bf

    d_stream, loss_row = _loss_call(xs, loss_target.reshape(t, d))
    loss = lax.psum(loss_row[0, 0], MESH_AXES)

    gw = {n: [None] * depth for n in names if n not in ("ln_in_g", "ln_in_b")}
    rs_own = {n: [None] * depth for n in big}
    rs_chips = {n: [None] * depth for n in big}
    core_idx = jnp.reshape(my_c, (1,)).astype(jnp.int32)
    chip_idx = jnp.reshape(2 * my_x + my_y, (1,)).astype(jnp.int32)

    def pair_add(l, which, slots, theirs):
        p4s = [_pair_add(slots[n], th, core_idx, "rs_add_" + n) for n, th in zip(which, theirs)]
        for n, p4 in zip(which, p4s):
            rs_own[n][l] = p4
        return p4s

    d_res, d_mm = None, d_stream
    pending = None
    for l in reversed(range(depth)):
        sv = saved[l]
        wl = sv["wl"]
        slots = {}
        gq, gkv = q_norm_g[l].reshape(1, ql), kv_norm_g[l].reshape(1, kvl)
        dz2, dz2_bf, gw["ln2_g"][l], gw["ln2_b"][l] = _ln_bwd(
            d_res, d_mm, sv["x1"], sv["y2"], ln2_g[l].reshape(1, d), ln2_b[l].reshape(1, d), alpha, "ln2_bwd")
        if pending is None:
            dw_down = _matmul(sv["act"], dz2_bf, "tn", bf16, "mm_down_dw", tm=dff // 4)
        else:
            dw_down, theirs = _matmul(sv["act"], dz2_bf, "tn", bf16, "mm_down_dw", tm=dff // 4,
                                      comm=_sibling_swap([pending[1][n] for n in rest]))
            rest_p4s = pair_add(pending[0], rest, pending[1], theirs)
        slots["w_down"] = dw_down.reshape(N_DEV, -1, d)
        d_act, theirs = _matmul(dz2_bf, wl["w_down"], "nt", f32, "mm_down_dx", tn=dff // 4,
                                comm=_sibling_swap([slots["w_down"]]))
        down_p4s = pair_add(l, ("w_down",), slots, theirs)
        d_up, dffw, dffb = _ffn_act_bwd(sv["up"], ffn_w_full[l], ffn_conv_b[l].reshape(1, 2 * dff), d_act, nb, seq, dff)
        gw["ffn_conv_w"][l] = jnp.concatenate([dffw[0], dffw[1]], axis=-1)
        gw["ffn_conv_b"][l] = dffb.reshape(2 * dff)
        d_x1, (rs_chips["w_down"][l],) = _matmul_up_dx(d_up, wl["w_up"], "mm_up_dx", comm=_chip_swap(down_p4s))
        if pending is None:
            slots["w_up"] = _matmul_up_dw(sv["x1_bf"], d_up, N_DEV, "mm_up_dw")
        else:
            slots["w_up"], from_chips = _matmul_up_dw(sv["x1_bf"], d_up, N_DEV, "mm_up_dw", comm=_chip_swap(rest_p4s))
            for n, fc in zip(rest, from_chips):
                rs_chips[n][pending[0]] = fc
        dz1, dz1_bf, gw["ln1_g"][l], gw["ln1_b"][l] = _ln_bwd(
            dz2, d_x1, sv["xs"], sv["y1"], ln1_g[l].reshape(1, d), ln1_b[l].reshape(1, d), alpha, "ln1_bwd")
        d_mixed, theirs = _matmul(dz1_bf, wl["w_out"], "nt", f32, "mm_out_dx", comm=_sibling_swap([slots["w_up"]]))
        up_p4s = pair_add(l, ("w_up",), slots, theirs)
        slots["w_out"] = _matmul(sv["mixed"], dz1_bf, "tn", bf16, "mm_out_dw").reshape(N_DEV, -1, d)
        d_upool, dwp, dps = _pool_bwd(sv["h"], w_pool_2d[l], pool_scale[l].reshape(1, pw), d_mixed,
                                      (heads * LANES + cw) // pw, lay, nb, seq)
        gw["w_pool"][l] = dwp.reshape(w_pool.shape[1:])
        gw["pool_scale"][l] = dps.reshape(pw)
        d_hconv, dclg, dclb = _convln_bwd(sv["hconv"], conv_ln_g[l].reshape(1, cw), conv_ln_b[l].reshape(1, cw), d_mixed,
                                          heads * LANES // cw)
        gw["conv_ln_g"][l], gw["conv_ln_b"][l] = dclg.reshape(cw), dclb.reshape(cw)
        d_conv, gw["conv_w"][l], dcb = _conv_bwd(sv["h"], conv_w_full[l], d_hconv, nb, seq, cw)
        gw["conv_b"][l] = dcb.reshape(cw)
        dq_ext, dkv, dkrd, (rs_chips["w_up"][l],) = _attn_bwd(sv["q_ext"], qt, sv["kv"], sv["krd"], sv["lse"], d_mixed, nb,
                                                              seq, heads, comm=_chip_swap(up_p4s))
        d_qn = _matmul(dq_ext, wl["w_uq"], "nt", f32, "mm_uq_dx")
        dwq = _matmul(sv["qn"], dq_ext, "tn", f32, "mm_uq_dw", tn=2048, tk=1024).reshape(ql, heads, 2 * LANES)
        dwq_rope = dwq[..., QK_NOPE_DIM:QK_NOPE_DIM + QK_ROPE_DIM] + _rot_cols_t(dwq[..., QK_NOPE_DIM + QK_ROPE_DIM:])
        slots["w_uq"] = jnp.concatenate([dwq[..., :QK_NOPE_DIM], dwq_rope], axis=-1).astype(bf16).reshape(
            N_DEV, -1, QK_NOPE_DIM + QK_ROPE_DIM)
        d_kvn = _matmul(dkv, wl["w_ukv"], "nt", f32, "mm_ukv_dx")
        slots["w_ukv"] = _matmul(sv["kvn"], dkv, "tn", bf16, "mm_ukv_dw", tn=2048, tk=1024).reshape(N_DEV, -1, 2 * LANES)
        d_cq, d_ckv, d_kr, dgq, dgkv = _prep_bwd(sv["h"], cs, gq, gkv, d_qn, d_kvn, dkrd, lay)
        gw["q_norm_g"][l], gw["kv_norm_g"][l] = dgq.reshape(ql), dgkv.reshape(kvl)
        d_h = jnp.concatenate([d_conv, d_cq, d_upool, d_ckv, d_kr], axis=-1)
        d_xs = _matmul(d_h, wl["w_in"], "nt", f32, "mm_in_dx", tm=512, tk=in_pad)
        dwi = _matmul(sv["xs_bf"], d_h, "tn", f32, "mm_in_dw", tn=in_pad, tk=1024)
        dkr_cols = dwi[:, lay["off_kr"]:lay["off_kr"] + QK_ROPE_DIM] + _rot_cols_t(dwi[:, lay["off_kr"] + QK_ROPE_DIM:])
        dwi_nat = jnp.concatenate(
            [dwi[:, lay["off_q"]:lay["off_q"] + ql], dwi[:, lay["off_kv"]:lay["off_kv"] + kvl], dkr_cols,
             _deinterleave_halves(dwi[:, :2 * cw], SEQ_CT), dwi[:, lay["off_pool"]:lay["off_pool"] + pw]],
            axis=-1).astype(bf16)
        slots["w_in"] = jnp.moveaxis(dwi_nat.reshape(d, N_DEV, -1), 1, 0)
        pending = (l, slots)
        d_res, d_mm = dz1, d_xs

    theirs = _comm_call(_sibling_swap([pending[1][n] for n in rest]), "rs_sibling")
    from_chips = _comm_call(_chip_swap(pair_add(pending[0], rest, pending[1], theirs)), "rs_chips")
    for n, fc in zip(rest, from_chips):
        rs_chips[n][pending[0]] = fc

    grad_x, _, d_ln_in_g, d_ln_in_b = _ln_bwd(d_res, d_mm, x2, None, ln_in_g.reshape(1, d), ln_in_b.reshape(1, d), alpha,
                                               "ln_in_bwd")
    grad_x = grad_x.reshape(x.shape)

    grads, deltas, new_m, new_v = {}, {}, {}, {}

    def finish(n, parts):
        shp = weights[n].shape
        rows = math.prod(shp[:-1]) if len(shp) > 1 else 1
        as2d = lambda a: a.reshape(rows, shp[-1])
        parts = [p.reshape(p.shape[0], rows, shp[-1]) for p in parts]
        g, dl, nm, nv = _adamw(parts, as2d(weights[n]), as2d(mom1[n]), as2d(mom2[n]), "adamw_" + n)
        grads[n], deltas[n], new_m[n], new_v[n] = (a.reshape(shp) for a in (g, dl, nm, nv))

    for n in big:
        shp = weights[n].shape
        as2d = lambda a: a.reshape(-1, shp[-1])
        g, dl, nm, nv = _adamw_layers(rs_own[n], rs_chips[n], chip_idx, as2d(weights[n]), as2d(mom1[n]), as2d(mom2[n]),
                                      "adamw_" + n)
        grads[n], deltas[n], new_m[n], new_v[n] = (a.reshape(shp) for a in (g, dl, nm, nv))

    small = [n for n in names if n not in ("w_in", "w_uq", "w_ukv", "w_out", "w_up", "w_down")]
    partial = {"ln_in_g": d_ln_in_g.reshape(d), "ln_in_b": d_ln_in_b.reshape(d)}
    for n in small:
        if n not in partial:
            partial[n] = jnp.stack(gw[n])
    flat = jnp.concatenate([partial[n].astype(f32).reshape(-1) for n in small])
    gathered = _all_gather([flat.reshape(-1, LANES)], "ag_small_grads")[0].reshape(N_DEV, -1)
    off = 0
    for n in small:
        size = math.prod(partial[n].shape)
        part = gathered[:, off:off + size].reshape((N_DEV,) + partial[n].shape)
        off += size
        if n in ("conv_w", "ffn_conv_w"):
            width = weights[n].shape[-1]
            part = lax.dynamic_slice_in_dim(part, my_dev * width, width, axis=part.ndim - 1)
        finish(n, [part])

    return (loss, grad_x, *[grads[n] for n in names], *[deltas[n] for n in names], *[new_m[n] for n in names],
            *[new_v[n] for n in names])
```

```python
import functools
import math

import jax
import jax.numpy as jnp
from jax import lax
from jax.experimental import pallas as pl
from jax.experimental.pallas import tpu as pltpu

f32 = jnp.float32
bf16 = jnp.bfloat16

QK_NOPE_DIM = 128
QK_ROPE_DIM = 64
V_HEAD_DIM = 128
CONV_KERNEL = 31
FFN_CONV_KERNEL = 3
POOL_WINDOWS = (2, 4, 8, 16)
ROPE_THETA = 10000.0
LN_EPS = 1e-5
RMS_EPS = 1e-6
ADAM_LR = 0.001
ADAM_B1 = 0.9
ADAM_B2 = 0.999
ADAM_EPS = 1e-08
ADAM_WD = 0.01
ADAM_STEP = 10

N_DEV = 8
MESH_AXES = ("x", "y", "c")
V7X_VMEM_LIMIT_BYTES = 56 * 1024 * 1024
LANES = 128
NEG_INF = -1e30
MESH = pl.DeviceIdType.MESH


def _cparams(sem):
    return pltpu.CompilerParams(dimension_semantics=sem, vmem_limit_bytes=V7X_VMEM_LIMIT_BYTES)


def _tile(dim, pref):
    t = pref
    while t >= LANES:
        if dim % t == 0:
            return t
        t //= 2
    return dim


_ANY = pl.BlockSpec(memory_space=pl.ANY)


class _Comm:
    def __init__(self, inputs, out_shapes, sems, start, finish, aliases=None):
        self.inputs, self.out_shapes, self.sems = list(inputs), list(out_shapes), list(sems)
        self.start, self.finish, self.aliases = start, finish, dict(aliases or {})


def _call(body, name, grid, in_specs, out_specs, out_shape, args, scratch=(), sem=None, comm=None, prefetch=()):
    in_specs, out_specs, out_shape, scratch = list(in_specs), list(out_specs), list(out_shape), list(scratch)
    n_pre, n_in, n_out, n_scr = len(prefetch), len(in_specs), len(out_specs), len(scratch)
    c_in, c_out = (len(comm.inputs), len(comm.out_shapes)) if comm else (0, 0)

    def carrier(*refs):
        refs = list(refs)
        pre, refs = refs[:n_pre], refs[n_pre:]
        ins, refs = refs[:n_in], refs[n_in:]
        c_ins, refs = refs[:c_in], refs[c_in:]
        outs, refs = refs[:n_out], refs[n_out:]
        c_outs, refs = refs[:c_out], refs[c_out:]
        scr, c_sems = refs[:n_scr], refs[n_scr:]
        ids = [pl.program_id(a) for a in range(len(grid))]
        first = functools.reduce(lambda p, q: p & q, [i == 0 for i in ids])
        last = functools.reduce(lambda p, q: p & q, [i == g - 1 for i, g in zip(ids, grid)])
        pl.when(first)(lambda: comm.start(c_ins, c_outs, c_sems))
        body(*pre, *ins, *outs, *scr)
        pl.when(last)(lambda: comm.finish(c_ins, c_outs, c_sems))

    grid_spec = pltpu.PrefetchScalarGridSpec(
        num_scalar_prefetch=n_pre, grid=grid, in_specs=in_specs + [_ANY] * c_in, out_specs=out_specs + [_ANY] * c_out,
        scratch_shapes=scratch + (comm.sems if comm else []))
    if comm is None:
        outs = pl.pallas_call(body, name=name, grid_spec=grid_spec, out_shape=out_shape,
                              compiler_params=_cparams(sem))(*prefetch, *args)
        return list(outs), []
    outs = pl.pallas_call(
        carrier, name=name, grid_spec=grid_spec, out_shape=out_shape + comm.out_shapes,
        input_output_aliases={n_pre + n_in + a: n_out + b for a, b in comm.aliases.items()},
        compiler_params=_cparams(("arbitrary",) * len(grid)),
    )(*prefetch, *args, *comm.inputs)
    return list(outs[:n_out]), list(outs[n_out:])


def _shift_down_raw(x, k):
    if k == 0:
        return x
    row = lax.broadcasted_iota(jnp.int32, x.shape, 0)
    return jnp.where(row >= k, pltpu.roll(x, k, axis=0), 0.0)


def _shift_up_raw(x, k):
    if k == 0:
        return x
    n = x.shape[0]
    row = lax.broadcasted_iota(jnp.int32, x.shape, 0)
    return jnp.where(row < n - k, pltpu.roll(x, n - k, axis=0), 0.0)


@functools.partial(jax.custom_vjp, nondiff_argnums=(1,))
def _shift_down(x, k):
    return _shift_down_raw(x, k)


def _shift_down_fwd(x, k):
    return _shift_down_raw(x, k), None


def _shift_down_bwd(k, _, g):
    return (_shift_up_raw(g, k),)


_shift_down.defvjp(_shift_down_fwd, _shift_down_bwd)


@jax.custom_vjp
def _dup_halves(p):
    return p + pltpu.roll(p, LANES // 2, axis=1)


def _dup_halves_fwd(p):
    return p + pltpu.roll(p, LANES // 2, axis=1), None


def _dup_halves_bwd(_, g):
    return (g + pltpu.roll(g, LANES // 2, axis=1),)


_dup_halves.defvjp(_dup_halves_fwd, _dup_halves_bwd)

_NN = (((1,), (0,)), ((), ()))
_NT = (((1,), (1,)), ((), ()))
_TN = (((0,), (0,)), ((), ()))


def _dot(a, b, dims):
    return lax.dot_general(a.astype(bf16), b.astype(bf16), dims, preferred_element_type=f32)


@jax.custom_vjp
def _mm_bf16(a, b):
    return _dot(a, b, _NN)


def _mm_bf16_fwd(a, b):
    return _dot(a, b, _NN), (a, b)


def _mm_bf16_bwd(res, g):
    a, b = res
    return _dot(g, b, _NT), _dot(a, g, _TN)


_mm_bf16.defvjp(_mm_bf16_fwd, _mm_bf16_bwd)


def _layer_norm(z, g, b):
    mu = jnp.mean(z, axis=-1, keepdims=True)
    var = jnp.mean(jnp.square(z - mu), axis=-1, keepdims=True)
    return (z - mu) * lax.rsqrt(var + LN_EPS) * g + b


def _rms_norm(x, g):
    ms = jnp.mean(jnp.square(x), axis=-1, keepdims=True)
    return x * lax.rsqrt(ms + RMS_EPS) * g


def _colsum(x):
    return jnp.sum(x, axis=0, keepdims=True)


def _matmul_core(a, b, mode, grid, a_spec, b_spec, o_spec, o_shape, tile, out_dtype, name, comm=None):
    nk = grid[2]
    dims = {"nn": _NN, "nt": _NT, "tn": _TN}[mode]
    acc_in_out = out_dtype == f32

    def body(a_ref, b_ref, o_ref, *scratch):
        def prod():
            return _dot(a_ref[...], b_ref[...], dims)

        if nk == 1:
            o_ref[...] = prod().astype(out_dtype)
            return
        acc_ref = o_ref if acc_in_out else scratch[0]
        kk = pl.program_id(2)

        @pl.when(kk == 0)
        def _():
            acc_ref[...] = prod()

        if acc_in_out:
            @pl.when(kk > 0)
            def _():
                acc_ref[...] += prod()
        else:
            @pl.when((kk > 0) & (kk < nk - 1))
            def _():
                acc_ref[...] += prod()

            @pl.when(kk == nk - 1)
            def _():
                o_ref[...] = (acc_ref[...] + prod()).astype(out_dtype)

    scratch = [] if (nk == 1 or acc_in_out) else [pltpu.VMEM(tile, f32)]
    (out,), comm_outs = _call(body, name, grid, [a_spec, b_spec], [o_spec], [jax.ShapeDtypeStruct(o_shape, out_dtype)],
                              (a, b), scratch, ("parallel", "parallel", "arbitrary"), comm)
    return out if comm is None else (out, comm_outs)


def _matmul(a, b, mode, out_dtype, name, tm=1024, tn=1024, tk=2048, comm=None):
    if mode == "nn":
        (m, k), (k2, n) = a.shape, b.shape
    elif mode == "nt":
        (m, k), (n, k2) = a.shape, b.shape
    else:
        (k, m), (k2, n) = a.shape, b.shape
    assert k == k2, (name, a.shape, b.shape)
    tm, tn, tk = _tile(m, tm), _tile(n, tn), _tile(k, tk)
    a_spec = pl.BlockSpec((tk, tm), lambda i, j, kk: (kk, i)) if mode == "tn" else pl.BlockSpec((tm, tk), lambda i, j, kk: (i, kk))
    b_spec = pl.BlockSpec((tn, tk), lambda i, j, kk: (j, kk)) if mode == "nt" else pl.BlockSpec((tk, tn), lambda i, j, kk: (kk, j))
    return _matmul_core(a, b, mode, (m // tm, n // tn, k // tk), a_spec, b_spec,
                        pl.BlockSpec((tm, tn), lambda i, j, kk: (i, j)), (m, n), (tm, tn), out_dtype, name, comm)


def _matmul_up(x_bf, w_slots, name, tm=1024, comm=None):
    m, k = x_bf.shape
    s, _, ns = w_slots.shape
    tm = _tile(m, tm)
    return _matmul_core(x_bf, w_slots, "nn", (m // tm, s, 1), pl.BlockSpec((tm, k), lambda i, j, kk: (i, 0)),
                        pl.BlockSpec((None, k, ns), lambda i, j, kk: (j, 0, 0)),
                        pl.BlockSpec((tm, ns), lambda i, j, kk: (i, j)), (m, s * ns), (tm, ns), f32, name, comm)


def _matmul_up_dx(d3, w_slots, name, tm=1024, tn=1024, comm=None):
    _, m, half = d3.shape
    s, n, ns = w_slots.shape
    per_half = half // ns
    assert 2 * per_half == s, (d3.shape, w_slots.shape)
    tm, tn = _tile(m, tm), _tile(n, tn)
    return _matmul_core(d3, w_slots, "nt", (m // tm, n // tn, s),
                        pl.BlockSpec((None, tm, ns), lambda i, j, kk: (kk // per_half, i, kk % per_half)),
                        pl.BlockSpec((None, tn, ns), lambda i, j, kk: (kk, j, 0)),
                        pl.BlockSpec((tm, tn), lambda i, j, kk: (i, j)), (m, n), (tm, tn), f32, name, comm)


def _matmul_up_dw(x_bf, d3, n_slots, name, tm=1024, tk=2048, comm=None):
    k, m = x_bf.shape
    _, _, half = d3.shape
    ns = 2 * half // n_slots
    per_half = n_slots // 2
    tm, tk = _tile(m, tm), _tile(k, tk)
    return _matmul_core(x_bf, d3, "tn", (m // tm, n_slots, k // tk), pl.BlockSpec((tk, tm), lambda i, j, kk: (kk, i)),
                        pl.BlockSpec((None, tk, ns), lambda i, j, kk: (j // per_half, kk, j % per_half)),
                        pl.BlockSpec((None, tm, ns), lambda i, j, kk: (j, i, 0)), (n_slots, m, ns), (tm, ns), bf16, name,
                        comm)


ROW_TILE = 256


def _rows(width, col_block=0):
    return pl.BlockSpec((ROW_TILE, width), lambda i, cb=col_block: (i, cb))


def _whole(shape):
    return pl.BlockSpec(shape, lambda i: (0,) * len(shape))


def _ln_fwd(x, y, g, b, alpha, name, comm=None):
    t, d = x.shape

    def body(*refs):
        if y is None:
            x_ref, g_ref, b_ref, o_ref, ob_ref = refs
            z = x_ref[...]
        else:
            x_ref, y_ref, g_ref, b_ref, o_ref, ob_ref = refs
            z = alpha * x_ref[...] + y_ref[...]
        out = _layer_norm(z, g_ref[...], b_ref[...])
        o_ref[...] = out
        ob_ref[...] = out.astype(bf16)

    ins = [x] + ([] if y is None else [y]) + [g, b]
    specs = [_rows(d)] + ([] if y is None else [_rows(d)]) + [_whole((1, d)), _whole((1, d))]
    (out, out_bf), comm_outs = _call(
        body, name, (t // ROW_TILE,), specs, [_rows(d), _rows(d)],
        [jax.ShapeDtypeStruct((t, d), f32), jax.ShapeDtypeStruct((t, d), bf16)], ins, (), ("parallel",), comm)
    return (out, out_bf) if comm is None else (out, out_bf, comm_outs)


def _ln_bwd(d_res, d_mm, x, y, g, b, alpha, name):
    t, d = x.shape
    has_res, has_mm, has_y = d_res is not None, d_mm is not None, y is not None

    def body(*refs):
        refs = list(refs)
        d_res_ref = refs.pop(0) if has_res else None
        d_mm_ref = refs.pop(0) if has_mm else None
        x_ref = refs.pop(0)
        y_ref = refs.pop(0) if has_y else None
        g_ref, b_ref, dz_ref, dzb_ref, dg_ref, db_ref = refs
        ct = None
        if has_res:
            ct = alpha * d_res_ref[...]
        if has_mm:
            ct = d_mm_ref[...] if ct is None else ct + d_mm_ref[...]
        z = x_ref[...] if not has_y else alpha * x_ref[...] + y_ref[...]
        _, vjp = jax.vjp(_layer_norm, z, g_ref[...], b_ref[...])
        dz, dg, db = vjp(ct)
        dz_ref[...] = dz
        dzb_ref[...] = dz.astype(bf16)

        @pl.when(pl.program_id(0) == 0)
        def _():
            dg_ref[...] = jnp.zeros_like(dg_ref)
            db_ref[...] = jnp.zeros_like(db_ref)

        dg_ref[...] += dg
        db_ref[...] += db

    ins = [a for a in (d_res, d_mm, x, y) if a is not None] + [g, b]
    specs = [_rows(d) for a in (d_res, d_mm, x, y) if a is not None] + [_whole((1, d)), _whole((1, d))]
    return pl.pallas_call(
        body, name=name, grid=(t // ROW_TILE,), in_specs=specs,
        out_specs=[_rows(d), _rows(d), _whole((1, d)), _whole((1, d))],
        out_shape=[jax.ShapeDtypeStruct((t, d), f32), jax.ShapeDtypeStruct((t, d), bf16),
                   jax.ShapeDtypeStruct((1, d), f32), jax.ShapeDtypeStruct((1, d), f32)],
        compiler_params=_cparams(("arbitrary",)),
    )(*ins)


def _loss_call(xf, target):
    t, d = xf.shape

    def body(x_ref, t_ref, dx_ref, loss_ref):
        err = x_ref[...] - t_ref[...]
        dx_ref[...] = err * (1.0 / d)

        @pl.when(pl.program_id(0) == 0)
        def _():
            loss_ref[...] = jnp.zeros_like(loss_ref)

        part = 0.5 * jnp.sum(jnp.mean(jnp.square(err), axis=-1, keepdims=True), axis=0, keepdims=True)
        loss_ref[...] += jnp.broadcast_to(part, loss_ref.shape)

    return pl.pallas_call(
        body, name="loss_head", grid=(t // ROW_TILE,), in_specs=[_rows(d), _rows(d)],
        out_specs=[_rows(d), _whole((1, LANES))],
        out_shape=[jax.ShapeDtypeStruct((t, d), f32), jax.ShapeDtypeStruct((1, LANES), f32)],
        compiler_params=_cparams(("arbitrary",)),
    )(xf, target)


def _rope_tables(pos, inv, scale):
    t = pos.shape[0]

    def body(pos_ref, inv_ref, cs_ref, qt_ref):
        ang = pos_ref[...].astype(f32) * inv_ref[...]
        lane = lax.broadcasted_iota(jnp.int32, ang.shape, 1)
        cs = jnp.where(lane < LANES // 2, jnp.cos(ang), jnp.sin(ang))
        cs_ref[...] = cs
        qt_ref[:, :LANES] = jnp.full((ROW_TILE, LANES), scale, f32)
        qt_ref[:, LANES:] = scale * cs

    return pl.pallas_call(
        body, name="rope_tables", grid=(t // ROW_TILE,),
        in_specs=[pl.BlockSpec((ROW_TILE, 1), lambda i: (i, 0)), _whole((1, LANES))],
        out_specs=[_rows(LANES), _rows(2 * LANES)],
        out_shape=[jax.ShapeDtypeStruct((t, LANES), f32), jax.ShapeDtypeStruct((t, 2 * LANES), f32)],
        compiler_params=_cparams(("parallel",)),
    )(pos, inv)


def _prep_fn(cq, ckv, kr, cs, gq, gkv):
    return _rms_norm(cq, gq), _rms_norm(ckv, gkv), _dup_halves(kr * cs)


def _prep_fwd(h, cs, gq, gkv, lay):
    t = h.shape[0]
    ql, kvl = lay["ql"], lay["kvl"]

    def body(cq_ref, ckv_ref, kr_ref, cs_ref, gq_ref, gkv_ref, qn_ref, kvn_ref, krd_ref):
        qn, kvn, krd = _prep_fn(cq_ref[...], ckv_ref[...], kr_ref[...], cs_ref[...], gq_ref[...], gkv_ref[...])
        qn_ref[...] = qn.astype(bf16)
        kvn_ref[...] = kvn.astype(bf16)
        krd_ref[...] = krd.astype(bf16)

    return pl.pallas_call(
        body, name="prep_fwd", grid=(t // ROW_TILE,),
        in_specs=[_rows(ql, lay["off_q"] // ql), _rows(kvl, lay["off_kv"] // kvl), _rows(LANES, lay["off_kr"] // LANES),
                  _rows(LANES), _whole((1, ql)), _whole((1, kvl))],
        out_specs=[_rows(ql), _rows(kvl), _rows(LANES)],
        out_shape=[jax.ShapeDtypeStruct((t, ql), bf16), jax.ShapeDtypeStruct((t, kvl), bf16),
                   jax.ShapeDtypeStruct((t, LANES), bf16)],
        compiler_params=_cparams(("parallel",)),
    )(h, h, h, cs, gq, gkv)


def _prep_bwd(h, cs, gq, gkv, d_qn, d_kvn, d_krd, lay):
    t = h.shape[0]
    ql, kvl = lay["ql"], lay["kvl"]

    def body(cq_ref, ckv_ref, kr_ref, cs_ref, gq_ref, gkv_ref, dqn_ref, dkvn_ref, dkrd_ref,
             dcq_ref, dckv_ref, dkr_ref, dgq_ref, dgkv_ref):
        _, vjp = jax.vjp(_prep_fn, cq_ref[...], ckv_ref[...], kr_ref[...], cs_ref[...], gq_ref[...], gkv_ref[...])
        dcq, dckv, dkr, _, dgq, dgkv = vjp((dqn_ref[...], dkvn_ref[...], dkrd_ref[...].astype(f32)))
        dcq_ref[...] = dcq.astype(bf16)
        dckv_ref[...] = dckv.astype(bf16)
        dkr_ref[...] = dkr.astype(bf16)

        @pl.when(pl.program_id(0) == 0)
        def _():
            dgq_ref[...] = jnp.zeros_like(dgq_ref)
            dgkv_ref[...] = jnp.zeros_like(dgkv_ref)

        dgq_ref[...] += dgq
        dgkv_ref[...] += dgkv

    return pl.pallas_call(
        body, name="prep_bwd", grid=(t // ROW_TILE,),
        in_specs=[_rows(ql, lay["off_q"] // ql), _rows(kvl, lay["off_kv"] // kvl), _rows(LANES, lay["off_kr"] // LANES),
                  _rows(LANES), _whole((1, ql)), _whole((1, kvl)), _rows(ql), _rows(kvl), _rows(LANES)],
        out_specs=[_rows(ql), _rows(kvl), _rows(LANES), _whole((1, ql)), _whole((1, kvl))],
        out_shape=[jax.ShapeDtypeStruct((t, ql), bf16), jax.ShapeDtypeStruct((t, kvl), bf16),
                   jax.ShapeDtypeStruct((t, LANES), bf16), jax.ShapeDtypeStruct((1, ql), f32),
                   jax.ShapeDtypeStruct((1, kvl), f32)],
        compiler_params=_cparams(("arbitrary",)),
    )(h, h, h, cs, gq, gkv, d_qn, d_kvn, d_krd)


def _convln_fn(hc, g, b):
    y = _layer_norm(hc, g, b)
    return y * jax.nn.sigmoid(y)


def _convln_fwd(hconv, g, b):
    t, cw = hconv.shape

    def body(h_ref, g_ref, b_ref, o_ref):
        o_ref[...] = _convln_fn(h_ref[...], g_ref[...], b_ref[...]).astype(bf16)

    return pl.pallas_call(
        body, name="convln_fwd", grid=(t // ROW_TILE,),
        in_specs=[_rows(cw), _whole((1, cw)), _whole((1, cw))], out_specs=_rows(cw),
        out_shape=jax.ShapeDtypeStruct((t, cw), bf16), compiler_params=_cparams(("parallel",)),
    )(hconv, g, b)


def _convln_bwd(hconv, g, b, d_mixed, col_block):
    t, cw = hconv.shape

    def body(h_ref, g_ref, b_ref, dy_ref, dh_ref, dg_ref, db_ref):
        _, vjp = jax.vjp(_convln_fn, h_ref[...], g_ref[...], b_ref[...])
        dh, dg, db = vjp(dy_ref[...])
        dh_ref[...] = dh

        @pl.when(pl.program_id(0) == 0)
        def _():
            dg_ref[...] = jnp.zeros_like(dg_ref)
            db_ref[...] = jnp.zeros_like(db_ref)

        dg_ref[...] += dg
        db_ref[...] += db

    return pl.pallas_call(
        body, name="convln_bwd", grid=(t // ROW_TILE,),
        in_specs=[_rows(cw), _whole((1, cw)), _whole((1, cw)), _rows(cw, col_block)],
        out_specs=[_rows(cw), _whole((1, cw)), _whole((1, cw))],
        out_shape=[jax.ShapeDtypeStruct((t, cw), f32), jax.ShapeDtypeStruct((1, cw), f32),
                   jax.ShapeDtypeStruct((1, cw), f32)],
        compiler_params=_cparams(("arbitrary",)),
    )(hconv, g, b, d_mixed)


SEQ_CT = 128


def _conv_fwd(h, w, b, nb, seq, cw):
    ct, kk = SEQ_CT, w.shape[0]
    ncb = cw // ct

    def body(h_ref, w_ref, b_ref, o_ref):
        blk = h_ref[...]
        a, g = blk[:, :ct], blk[:, ct:]
        hc = a * jax.nn.sigmoid(g)
        acc = jnp.zeros_like(hc)
        for k in range(kk):
            acc = acc + w_ref[k:k + 1, :] * _shift_down_raw(hc, kk - 1 - k)
        o_ref[...] = acc + b_ref[...]

    return pl.pallas_call(
        body, name="conv_fwd", grid=(ncb, nb),
        in_specs=[pl.BlockSpec((seq, 2 * ct), lambda j, bb: (bb, j)), pl.BlockSpec((kk, ct), lambda j, bb: (0, j)),
                  pl.BlockSpec((1, ct), lambda j, bb: (0, j))],
        out_specs=pl.BlockSpec((seq, ct), lambda j, bb: (bb, j)),
        out_shape=jax.ShapeDtypeStruct((nb * seq, cw), f32),
        compiler_params=_cparams(("parallel", "parallel")),
    )(h, w, b)


def _conv_bwd(h, w, d_hconv, nb, seq, cw):
    ct, kk = SEQ_CT, w.shape[0]
    ncb = cw // ct

    def body(h_ref, w_ref, dy_ref, dh_ref, dw_ref, db_ref):
        blk = h_ref[...]
        a, g = blk[:, :ct], blk[:, ct:]
        sg = jax.nn.sigmoid(g)
        hc = a * sg
        dy = dy_ref[...]
        dhc = jnp.zeros_like(hc)

        @pl.when(pl.program_id(1) == 0)
        def _():
            dw_ref[...] = jnp.zeros_like(dw_ref)
            db_ref[...] = jnp.zeros_like(db_ref)

        for k in range(kk):
            dhc = dhc + w_ref[k:k + 1, :] * _shift_up_raw(dy, kk - 1 - k)
            dw_ref[k:k + 1, :] += _colsum(dy * _shift_down_raw(hc, kk - 1 - k))
        db_ref[...] += _colsum(dy)
        dh_ref[:, :ct] = (dhc * sg).astype(bf16)
        dh_ref[:, ct:] = (dhc * a * sg * (1.0 - sg)).astype(bf16)

    return pl.pallas_call(
        body, name="conv_bwd", grid=(ncb, nb),
        in_specs=[pl.BlockSpec((seq, 2 * ct), lambda j, bb: (bb, j)), pl.BlockSpec((kk, ct), lambda j, bb: (0, j)),
                  pl.BlockSpec((seq, ct), lambda j, bb: (bb, j))],
        out_specs=[pl.BlockSpec((seq, 2 * ct), lambda j, bb: (bb, j)), pl.BlockSpec((kk, ct), lambda j, bb: (0, j)),
                   pl.BlockSpec((1, ct), lambda j, bb: (0, j))],
        out_shape=[jax.ShapeDtypeStruct((nb * seq, 2 * cw), bf16), jax.ShapeDtypeStruct((kk, cw), f32),
                   jax.ShapeDtypeStruct((1, cw), f32)],
        compiler_params=_cparams(("parallel", "arbitrary")),
    )(h, w, d_hconv)


def _pool_fn(u, wp, scale, pg):
    seq = u.shape[0]
    t1 = (lax.broadcasted_iota(jnp.int32, (seq, 1), 0) + 1).astype(f32)
    outs = []
    for gi, win in enumerate(POOL_WINDOWS):
        ug = u[:, gi * pg:(gi + 1) * pg]
        acc, span = ug, 1
        while span < win:
            acc = acc + _shift_down(acc, span)
            span *= 2
        d = acc / jnp.minimum(t1, float(win)) - ug
        outs.append(_mm_bf16(d, wp[gi * pg:(gi + 1) * pg, :]) * scale[:, gi * pg:(gi + 1) * pg])
    return outs


def _pool_fwd(h, wp, scale, lay, nb, seq):
    pw, pg = lay["pw"], lay["pg"]

    def body(u_ref, wp_ref, sc_ref, o_ref):
        outs = _pool_fn(u_ref[...], wp_ref[...], sc_ref[...], pg)
        for gi in range(len(POOL_WINDOWS)):
            o_ref[:, gi * pg:(gi + 1) * pg] = outs[gi].astype(bf16)

    return pl.pallas_call(
        body, name="pool_fwd", grid=(nb,),
        in_specs=[pl.BlockSpec((seq, pw), lambda bb: (bb, lay["off_pool"] // pw)), _whole((pw, pg)), _whole((1, pw))],
        out_specs=pl.BlockSpec((seq, pw), lambda bb: (bb, 0)),
        out_shape=jax.ShapeDtypeStruct((nb * seq, pw), bf16),
        compiler_params=_cparams(("parallel",)),
    )(h, wp, scale)


def _pool_bwd(h, wp, scale, d_mixed, col_block, lay, nb, seq):
    pw, pg = lay["pw"], lay["pg"]
    ng = len(POOL_WINDOWS)

    def body(u_ref, wp_ref, sc_ref, dy_ref, du_ref, dwp_ref, dsc_ref):
        _, vjp = jax.vjp(functools.partial(_pool_fn, pg=pg), u_ref[...], wp_ref[...], sc_ref[...])
        dy = dy_ref[...]
        du, dwp, dsc = vjp([dy[:, gi * pg:(gi + 1) * pg] for gi in range(ng)])
        du_ref[...] = du.astype(bf16)

        @pl.when(pl.program_id(0) == 0)
        def _():
            dwp_ref[...] = jnp.zeros_like(dwp_ref)
            dsc_ref[...] = jnp.zeros_like(dsc_ref)

        dwp_ref[...] += dwp
        dsc_ref[...] += dsc

    return pl.pallas_call(
        body, name="pool_bwd", grid=(nb,),
        in_specs=[pl.BlockSpec((seq, pw), lambda bb: (bb, lay["off_pool"] // pw)), _whole((pw, pg)), _whole((1, pw)),
                  pl.BlockSpec((seq, pw), lambda bb: (bb, col_block))],
        out_specs=[pl.BlockSpec((seq, pw), lambda bb: (bb, 0)), _whole((pw, pg)), _whole((1, pw))],
        out_shape=[jax.ShapeDtypeStruct((nb * seq, pw), bf16), jax.ShapeDtypeStruct((pw, pg), f32),
                   jax.ShapeDtypeStruct((1, pw), f32)],
        compiler_params=_cparams(("arbitrary",)),
    )(h, wp, scale, d_mixed)


FFN_ROWS = 32
SUBLANES = 8


def _rows_before(ref, r0, s, n):
    if s == 0:
        return ref[r0:r0 + n, :]
    if r0 == 0:
        x = ref[0:n, :]
        row = lax.broadcasted_iota(jnp.int32, x.shape, 0)
        return jnp.where(row >= s, pltpu.roll(x, s, axis=0), 0.0)
    return ref[pl.ds(r0 - s, n), :]


def _ffn_conv_rows(x_ref, w_ref, b_ref, r0, n):
    kk = w_ref.shape[0]
    xs = [_rows_before(x_ref, r0, s, n) for s in range(kk)]
    c = b_ref[...] + w_ref[kk - 1:kk, :] * xs[0]
    for k in range(kk - 1):
        c = c + w_ref[k:k + 1, :] * xs[kk - 1 - k]
    return c, xs


def _fold_rows(x):
    out = x[0:SUBLANES]
    for i in range(1, x.shape[0] // SUBLANES):
        out = out + x[i * SUBLANES:(i + 1) * SUBLANES]
    return out


def _ffn_specs(seq, ct, kk, nct):
    return [pl.BlockSpec((seq, ct), lambda j, bb: (bb, j)), pl.BlockSpec((seq, ct), lambda j, bb: (bb, nct + j)),
            pl.BlockSpec((kk, ct), lambda j, bb: (0, j)), pl.BlockSpec((kk, ct), lambda j, bb: (0, nct + j)),
            pl.BlockSpec((1, ct), lambda j, bb: (0, j)), pl.BlockSpec((1, ct), lambda j, bb: (0, nct + j))]


def _ffn_act_fwd(up, w, b, nb, seq, dff, comm=None):
    ct, kk = SEQ_CT, w.shape[0]
    rows = min(FFN_ROWS, seq)

    def body(ua_ref, ug_ref, wa_ref, wg_ref, ba_ref, bg_ref, o_ref):
        for r0 in range(0, seq, rows):
            a, _ = _ffn_conv_rows(ua_ref, wa_ref, ba_ref, r0, rows)
            g, _ = _ffn_conv_rows(ug_ref, wg_ref, bg_ref, r0, rows)
            o_ref[r0:r0 + rows, :] = (a * g * jax.nn.sigmoid(g)).astype(bf16)

    (act,), comm_outs = _call(body, "ffn_act_fwd", (dff // ct, nb), _ffn_specs(seq, ct, kk, dff // ct),
                              [pl.BlockSpec((seq, ct), lambda j, bb: (bb, j))],
                              [jax.ShapeDtypeStruct((nb * seq, dff), bf16)], (up, up, w, w, b, b), (),
                              ("parallel", "parallel"), comm)
    return act if comm is None else (act, comm_outs)


def _ffn_act_bwd(up, w, b, d_act, nb, seq, dff, comm=None):
    ct, kk = SEQ_CT, w.shape[0]
    rows = min(FFN_ROWS, seq)

    def body(ua_ref, ug_ref, wa_ref, wg_ref, ba_ref, bg_ref, da_ref, du_ref, dw_ref, db_ref, dc_ref):
        @pl.when(pl.program_id(1) == 0)
        def _():
            dw_ref[...] = jnp.zeros_like(dw_ref)
            db_ref[...] = jnp.zeros_like(db_ref)

        dw_acc = [[jnp.zeros((SUBLANES, ct), f32) for _ in range(kk)] for _ in range(2)]
        db_acc = [jnp.zeros((SUBLANES, ct), f32) for _ in range(2)]
        for r0 in range(0, seq, rows):
            a, xa = _ffn_conv_rows(ua_ref, wa_ref, ba_ref, r0, rows)
            g, xg = _ffn_conv_rows(ug_ref, wg_ref, bg_ref, r0, rows)
            sg = jax.nn.sigmoid(g)
            dact = da_ref[r0:r0 + rows, :]
            dcs = (dact * g * sg, dact * a * sg * (1.0 + g * (1.0 - sg)))
            for hf, (dc, xs) in enumerate(zip(dcs, (xa, xg))):
                dc_ref[hf, r0:r0 + rows, :] = dc
                db_acc[hf] = db_acc[hf] + _fold_rows(dc)
                for k in range(kk):
                    dw_acc[hf][k] = dw_acc[hf][k] + _fold_rows(dc * xs[kk - 1 - k])
        for hf in range(2):
            dc_ref[hf, seq:seq + SUBLANES, :] = jnp.zeros((SUBLANES, ct), f32)
            db_ref[hf] += _colsum(db_acc[hf])
            for k in range(kk):
                dw_ref[hf, k:k + 1, :] += _colsum(dw_acc[hf][k])
        for r0 in range(0, seq, rows):
            for hf, w_ref in enumerate((wa_ref, wg_ref)):
                dsrc = w_ref[kk - 1:kk, :] * dc_ref[hf, r0:r0 + rows, :]
                for k in range(kk - 1):
                    dsrc = dsrc + w_ref[k:k + 1, :] * dc_ref[hf, pl.ds(r0 + kk - 1 - k, rows), :]
                du_ref[hf, r0:r0 + rows, :] = dsrc.astype(bf16)

    (d_up, dw, db), comm_outs = _call(
        body, "ffn_act_bwd", (dff // ct, nb),
        _ffn_specs(seq, ct, kk, dff // ct) + [pl.BlockSpec((seq, ct), lambda j, bb: (bb, j))],
        [pl.BlockSpec((2, seq, ct), lambda j, bb: (0, bb, j)), pl.BlockSpec((2, kk, ct), lambda j, bb: (0, 0, j)),
         pl.BlockSpec((2, 1, ct), lambda j, bb: (0, 0, j))],
        [jax.ShapeDtypeStruct((2, nb * seq, dff), bf16), jax.ShapeDtypeStruct((2, kk, dff), f32),
         jax.ShapeDtypeStruct((2, 1, dff), f32)],
        (up, up, w, w, b, b, d_act), [pltpu.VMEM((2, seq + SUBLANES, ct), f32)], ("parallel", "arbitrary"), comm)
    return (d_up, dw, db) if comm is None else (d_up, dw, db, comm_outs)


def _fill_keys(k_scr, kn_ref, krd_ref):
    @pl.when(pl.program_id(2) == 0)
    def _():
        k_scr[:, :LANES] = kn_ref[...]
        k_scr[:, LANES:] = krd_ref[...]


def _scores(q_ref, qt_ref, k_scr, qblk, tq):
    klen = (qblk + 1) * tq
    q = (q_ref[...] * qt_ref[...]).astype(bf16)
    s = _dot(q, k_scr[0:klen, :], _NT)
    row = qblk * tq + lax.broadcasted_iota(jnp.int32, s.shape, 0)
    col = lax.broadcasted_iota(jnp.int32, s.shape, 1)
    return q, jnp.where(col <= row, s, NEG_INF)


def _per_q_block(nq, fn):
    qi = pl.program_id(2)
    for qblk in range(nq):
        pl.when(qi == qblk)(functools.partial(fn, qblk))


def _attn_fwd(q_ext, qt, kv, krd, nb, seq, heads, comm=None):
    tq = _tile(seq, 512)
    nq = seq // tq

    def body(q_ref, qt_ref, kn_ref, krd_ref, v_ref, o_ref, lse_ref, k_scr):
        _fill_keys(k_scr, kn_ref, krd_ref)

        def work(qblk):
            klen = (qblk + 1) * tq
            _, s = _scores(q_ref, qt_ref, k_scr, qblk, tq)
            m = jnp.max(s, axis=-1, keepdims=True)
            p = jnp.exp(s - m)
            l = jnp.sum(p, axis=-1, keepdims=True)
            o_ref[...] = (_dot(p, v_ref[0:klen, :], _NN) / l).astype(bf16)
            lse_ref[...] = m + jnp.log(l)

        _per_q_block(nq, work)

    (o, lse), comm_outs = _call(
        body, "attn_fwd", (nb, heads, nq),
        [pl.BlockSpec((tq, 2 * LANES), lambda b, h, i: (b * nq + i, h)),
         pl.BlockSpec((tq, 2 * LANES), lambda b, h, i: (b * nq + i, 0)),
         pl.BlockSpec((seq, LANES), lambda b, h, i: (b, 2 * h)),
         pl.BlockSpec((seq, LANES), lambda b, h, i: (b, 0)),
         pl.BlockSpec((seq, LANES), lambda b, h, i: (b, 2 * h + 1))],
        [pl.BlockSpec((tq, LANES), lambda b, h, i: (b * nq + i, h)),
         pl.BlockSpec((None, tq, 1), lambda b, h, i: (b * heads + h, i, 0))],
        [jax.ShapeDtypeStruct((nb * seq, heads * LANES), bf16), jax.ShapeDtypeStruct((nb * heads, seq, 1), f32)],
        (q_ext, qt, kv, krd, kv), [pltpu.VMEM((seq, 2 * LANES), bf16)], ("parallel", "parallel", "arbitrary"), comm)
    return (o, lse) if comm is None else (o, lse, comm_outs)


def _attn_bwd(q_ext, qt, kv, krd, lse, d_mixed, nb, seq, heads, comm=None):
    tq = _tile(seq, 512)
    nq = seq // tq

    def body(q_ref, qt_ref, kn_ref, krd_ref, v_ref, lse_ref, do_ref, dq_ref, dkv_ref, dkrd_ref, dkv_acc, k_scr):
        h, qi = pl.program_id(1), pl.program_id(2)
        _fill_keys(k_scr, kn_ref, krd_ref)

        @pl.when(qi == 0)
        def _():
            dkv_acc[...] = jnp.zeros_like(dkv_acc)

        @pl.when((qi == 0) & (h == 0))
        def _():
            dkrd_ref[...] = jnp.zeros_like(dkrd_ref)

        def work(qblk):
            klen = (qblk + 1) * tq
            q, s = _scores(q_ref, qt_ref, k_scr, qblk, tq)
            p = jnp.exp(s - lse_ref[...])
            do = do_ref[...]
            dp = _dot(do, v_ref[0:klen, :], _NT)
            ds = (p * (dp - jnp.sum(p * dp, axis=-1, keepdims=True))).astype(bf16)
            dq_ref[...] = (_dot(ds, k_scr[0:klen, :], _NN) * qt_ref[...]).astype(bf16)
            dk = _dot(ds, q, _TN)
            dkv_acc[0:klen, :LANES] += dk[:, :LANES]
            dkv_acc[0:klen, LANES:] += _dot(p, do, _TN)
            dkrd_ref[0:klen, :] += dk[:, LANES:]

        _per_q_block(nq, work)

        @pl.when(qi == nq - 1)
        def _():
            dkv_ref[...] = dkv_acc[...].astype(bf16)

    (dq, dkv, dkrd), comm_outs = _call(
        body, "attn_bwd", (nb, heads, nq),
        [pl.BlockSpec((tq, 2 * LANES), lambda b, h, i: (b * nq + i, h)),
         pl.BlockSpec((tq, 2 * LANES), lambda b, h, i: (b * nq + i, 0)),
         pl.BlockSpec((seq, LANES), lambda b, h, i: (b, 2 * h)),
         pl.BlockSpec((seq, LANES), lambda b, h, i: (b, 0)),
         pl.BlockSpec((seq, LANES), lambda b, h, i: (b, 2 * h + 1)),
         pl.BlockSpec((None, tq, 1), lambda b, h, i: (b * heads + h, i, 0)),
         pl.BlockSpec((tq, LANES), lambda b, h, i: (b * nq + i, h))],
        [pl.BlockSpec((tq, 2 * LANES), lambda b, h, i: (b * nq + i, h)),
         pl.BlockSpec((seq, 2 * LANES), lambda b, h, i: (b, h)),
         pl.BlockSpec((seq, LANES), lambda b, h, i: (b, 0))],
        [jax.ShapeDtypeStruct((nb * seq, heads * 2 * LANES), bf16),
         jax.ShapeDtypeStruct((nb * seq, heads * 2 * LANES), bf16),
         jax.ShapeDtypeStruct((nb * seq, LANES), f32)],
        (q_ext, qt, kv, krd, kv, lse, d_mixed), [pltpu.VMEM((seq, 2 * LANES), f32), pltpu.VMEM((seq, 2 * LANES), bf16)],
        ("parallel", "arbitrary", "arbitrary"), comm)
    return (dq, dkv, dkrd) if comm is None else (dq, dkv, dkrd, comm_outs)


def _mesh_pos():
    x, y, c = lax.axis_index("x"), lax.axis_index("y"), lax.axis_index("c")
    return x, y, c, [(1 - x, y), (x, 1 - y), (1 - x, 1 - y)]


def _all_gather(shards, name):
    n = len(shards)

    def body(*refs):
        x_refs, out_refs, (send_sems, recv_sems, local_sems) = refs[:n], refs[n:2 * n], refs[2 * n:]
        x, y, c, chips = _mesh_pos()
        me, sibling = (x, y, c), (x, y, 1 - c)

        def copy(t, k, block, to, from_shard=False):
            px, py, pc = block
            rows = out_refs[t].at[4 * px + 2 * py + pc]
            return pltpu.make_async_remote_copy(
                src_ref=x_refs[t] if from_shard else rows, dst_ref=rows,
                send_sem=send_sems.at[t, k], recv_sem=recv_sems.at[t, k], device_id=to, device_id_type=MESH)

        mine = [pltpu.make_async_copy(x_refs[t], out_refs[t].at[4 * x + 2 * y + c], local_sems.at[t]) for t in range(n)]
        first = [[copy(t, 0, me, sibling, True)] + [copy(t, 1 + j, me, (*chip, c), True) for j, chip in enumerate(chips)]
                 for t in range(n)]
        passed = [[copy(t, 4 + j, (*chip, c), sibling) for j, chip in enumerate(chips)] for t in range(n)]
        for t in range(n):
            mine[t].start()
            for cp in first[t]:
                cp.start()
        for j, chip in enumerate(chips):
            for t in range(n):
                copy(t, 1 + j, (*chip, c), me).wait_recv()
                passed[t][j].start()
        for t in range(n):
            copy(t, 0, sibling, me).wait_recv()
            for j, chip in enumerate(chips):
                copy(t, 4 + j, (*chip, 1 - c), me).wait_recv()
        for t in range(n):
            for cp in first[t] + passed[t]:
                cp.wait_send()
            mine[t].wait()

    return pl.pallas_call(
        body, name=name, out_shape=[jax.ShapeDtypeStruct((N_DEV,) + s.shape, s.dtype) for s in shards],
        in_specs=[_ANY] * n, out_specs=[_ANY] * n,
        scratch_shapes=[pltpu.SemaphoreType.DMA((n, 7)), pltpu.SemaphoreType.DMA((n, 7)), pltpu.SemaphoreType.DMA((n,))],
    )(*shards)


def _gather_own(shards):
    n = len(shards)

    def remote(x_refs, out_refs, sems, arriving):
        send_sems, recv_sems, _ = sems
        x, y, c, chips = _mesh_pos()
        peers = [(x, y, 1 - c)] + [(*chip, c) for chip in chips]
        return [pltpu.make_async_remote_copy(
            src_ref=x_refs[t], dst_ref=out_refs[t].at[4 * px + 2 * py + pc if arriving else 4 * x + 2 * y + c],
            send_sem=send_sems.at[t, k], recv_sem=recv_sems.at[t, k], device_id=(px, py, pc), device_id_type=MESH)
            for t in range(n) for k, (px, py, pc) in enumerate(peers)]

    def local(x_refs, out_refs, sems):
        x, y, c, _ = _mesh_pos()
        return [pltpu.make_async_copy(x_refs[t], out_refs[t].at[4 * x + 2 * y + c], sems[2].at[t]) for t in range(n)]

    def start(x_refs, out_refs, sems):
        for cp in local(x_refs, out_refs, sems) + remote(x_refs, out_refs, sems, False):
            cp.start()

    def finish(x_refs, out_refs, sems):
        for cp in remote(x_refs, out_refs, sems, True):
            cp.wait_recv()
        for cp in remote(x_refs, out_refs, sems, False):
            cp.wait_send()
        for cp in local(x_refs, out_refs, sems):
            cp.wait()

    return _Comm(shards, [jax.ShapeDtypeStruct((N_DEV,) + s.shape, s.dtype) for s in shards],
                 [pltpu.SemaphoreType.DMA((n, 4)), pltpu.SemaphoreType.DMA((n, 4)), pltpu.SemaphoreType.DMA((n,))],
                 start, finish)


def _gather_pass(gathered):
    n = len(gathered)

    def copies(in_refs, out_refs, sems, arriving):
        send_sems, recv_sems = sems
        x, y, c, chips = _mesh_pos()
        return [pltpu.make_async_remote_copy(
            src_ref=in_refs[t].at[4 * px + 2 * py + c],
            dst_ref=out_refs[t].at[4 * px + 2 * py + (1 - c if arriving else c)],
            send_sem=send_sems.at[t, j], recv_sem=recv_sems.at[t, j], device_id=(x, y, 1 - c), device_id_type=MESH)
            for t in range(n) for j, (px, py) in enumerate(chips)]

    def start(in_refs, out_refs, sems):
        for cp in copies(in_refs, out_refs, sems, False):
            cp.start()

    def finish(in_refs, out_refs, sems):
        for cp in copies(in_refs, out_refs, sems, True):
            cp.wait_recv()
        for cp in copies(in_refs, out_refs, sems, False):
            cp.wait_send()

    return _Comm(gathered, [jax.ShapeDtypeStruct(g.shape, g.dtype) for g in gathered],
                 [pltpu.SemaphoreType.DMA((n, 3)), pltpu.SemaphoreType.DMA((n, 3))], start, finish,
                 aliases={t: t for t in range(n)})


def _sibling_swap(slots):
    n = len(slots)

    def start(g_refs, out_refs, sems):
        send_sems, recv_sems = sems
        x, y, c, _ = _mesh_pos()
        for t in range(n):
            for k in range(4):
                pltpu.make_async_remote_copy(
                    src_ref=g_refs[t].at[2 * k + (1 - c)], dst_ref=out_refs[t].at[k], send_sem=send_sems.at[t],
                    recv_sem=recv_sems.at[t], device_id=(x, y, 1 - c), device_id_type=MESH).start()

    def finish(g_refs, out_refs, sems):
        send_sems, recv_sems = sems
        x, y, c, _ = _mesh_pos()
        for t in range(n):
            pltpu.make_async_remote_copy(
                src_ref=g_refs[t].at[pl.ds(0, 4)], dst_ref=out_refs[t], send_sem=send_sems.at[t],
                recv_sem=recv_sems.at[t], device_id=(x, y, 1 - c), device_id_type=MESH).wait()

    return _Comm(slots, [jax.ShapeDtypeStruct((4,) + s.shape[1:], s.dtype) for s in slots],
                 [pltpu.SemaphoreType.DMA((n,)), pltpu.SemaphoreType.DMA((n,))], start, finish)


def _chip_swap(p4s, peers=(0, 1, 2), into=None):
    n = len(p4s)

    def copies(refs, out_refs, sems):
        send_sems, recv_sems = sems
        x, y, c, chips = _mesh_pos()
        return [pltpu.make_async_remote_copy(
            src_ref=refs[t].at[2 * chips[j][0] + chips[j][1]], dst_ref=out_refs[t].at[j], send_sem=send_sems.at[t, j],
            recv_sem=recv_sems.at[t, j], device_id=(*chips[j], c), device_id_type=MESH)
            for t in range(n) for j in peers]

    def start(refs, out_refs, sems):
        for cp in copies(refs, out_refs, sems):
            cp.start()

    def finish(refs, out_refs, sems):
        for cp in copies(refs, out_refs, sems):
            cp.wait()

    out_shapes = [jax.ShapeDtypeStruct((3,) + p.shape[1:], p.dtype) for p in p4s]
    sems = [pltpu.SemaphoreType.DMA((n, 3)), pltpu.SemaphoreType.DMA((n, 3))]
    if into is None:
        return _Comm(p4s, out_shapes, sems, start, finish)
    return _Comm(list(p4s) + list(into), out_shapes, sems, start, finish, aliases={n + t: t for t in range(n)})


def _comm_join(a, b):
    ai, ao, asem = len(a.inputs), len(a.out_shapes), len(a.sems)

    def start(ins, outs, sems):
        a.start(ins[:ai], outs[:ao], sems[:asem])
        b.start(ins[ai:], outs[ao:], sems[asem:])

    def finish(ins, outs, sems):
        a.finish(ins[:ai], outs[:ao], sems[:asem])
        b.finish(ins[ai:], outs[ao:], sems[asem:])

    aliases = dict(a.aliases)
    aliases.update({ai + i: ao + o for i, o in b.aliases.items()})
    return _Comm(a.inputs + b.inputs, a.out_shapes + b.out_shapes, a.sems + b.sems, start, finish, aliases)


def _row_tile(rows, cols, max_bytes=1024 * 1024):
    best = None
    for tr in range(16, rows + 1, 16):
        if rows % tr == 0 and tr * cols * 4 <= max_bytes:
            best = tr
    return best or rows


def _pair_add(slots, theirs, core, name):
    _, rows, cols = slots.shape
    tr = _row_tile(rows, cols, 4 * 1024 * 1024)

    def body(c_ref, a_ref, b_ref, o_ref):
        o_ref[...] = (a_ref[...].astype(f32) + b_ref[...].astype(f32)).astype(bf16)

    return pl.pallas_call(
        body, name=name,
        grid_spec=pltpu.PrefetchScalarGridSpec(
            num_scalar_prefetch=1, grid=(4, rows // tr),
            in_specs=[pl.BlockSpec((None, tr, cols), lambda k, i, c_ref: (2 * k + c_ref[0], i, 0)),
                      pl.BlockSpec((None, tr, cols), lambda k, i, c_ref: (k, i, 0))],
            out_specs=pl.BlockSpec((None, tr, cols), lambda k, i, c_ref: (k, i, 0))),
        out_shape=jax.ShapeDtypeStruct((4, rows, cols), bf16), compiler_params=_cparams(("parallel", "parallel")),
    )(core, slots, theirs)


def _adam_update(g, w, m, v):
    c1 = 1.0 / (1.0 - ADAM_B1 ** ADAM_STEP)
    c2 = 1.0 / (1.0 - ADAM_B2 ** ADAM_STEP)
    nm = ADAM_B1 * m + (1.0 - ADAM_B1) * g
    nv = ADAM_B2 * v + (1.0 - ADAM_B2) * jnp.square(g)
    delta = -ADAM_LR * ((nm * c1) / (jnp.sqrt(nv * c2) + ADAM_EPS) + ADAM_WD * w)
    return delta, nm, nv


def _adamw_layers(p4s, chips, chip_idx, w, m, v, name, comm=None):
    depth = len(p4s)
    _, rows_l, cols = p4s[0].shape
    tr = _row_tile(rows_l, cols)
    nr = rows_l // tr

    def body(idx_ref, *refs):
        p_refs, c_refs = refs[:depth], refs[depth:2 * depth]
        w_ref, m_ref, v_ref, g_ref, d_ref, nm_ref, nv_ref = refs[2 * depth:]
        layer = pl.program_id(0)
        for ll in range(depth):
            @pl.when(layer == ll)
            def _(ll=ll):
                g = p_refs[ll][...].astype(f32)
                for j in range(3):
                    g = g + c_refs[ll][j].astype(f32)
                delta, nm, nv = _adam_update(g, w_ref[...], m_ref[...], v_ref[...])
                g_ref[...] = g
                d_ref[...] = delta
                nm_ref[...] = nm
                nv_ref[...] = nv

    def of_layer(ll):
        return lambda l, i: jnp.where(l == ll, i, 0)

    p_specs = [pl.BlockSpec((None, tr, cols), lambda l, i, idx_ref, f=of_layer(ll): (idx_ref[0], f(l, i), 0))
               for ll in range(depth)]
    c_specs = [pl.BlockSpec((3, tr, cols), lambda l, i, idx_ref, f=of_layer(ll): (0, f(l, i), 0)) for ll in range(depth)]
    spec = pl.BlockSpec((tr, cols), lambda l, i, idx_ref: (l * nr + i, 0))
    out = jax.ShapeDtypeStruct(w.shape, f32)
    outs, comm_outs = _call(body, name, (depth, nr), p_specs + c_specs + [spec, spec, spec], [spec, spec, spec, spec],
                            [out, out, out, out], (*p4s, *chips, w, m, v), (), ("parallel", "parallel"), comm,
                            prefetch=[chip_idx])
    return outs if comm is None else (outs, comm_outs)


def _adamw(parts, w, m, v, name):
    rows, cols = w.shape
    tr = _row_tile(rows, cols, 512 * 1024)
    nparts = len(parts)

    def body(*refs):
        part_refs, (w_ref, m_ref, v_ref, g_ref, d_ref, nm_ref, nv_ref) = refs[:nparts], refs[nparts:]
        g = None
        for pr in part_refs:
            for s in range(pr.shape[0]):
                term = pr[s].astype(f32)
                g = term if g is None else g + term
        delta, nm, nv = _adam_update(g, w_ref[...], m_ref[...], v_ref[...])
        g_ref[...] = g
        d_ref[...] = delta
        nm_ref[...] = nm
        nv_ref[...] = nv

    spec = pl.BlockSpec((tr, cols), lambda i: (i, 0))
    part_specs = [pl.BlockSpec((p.shape[0], tr, cols), lambda i: (0, i, 0)) for p in parts]
    out = jax.ShapeDtypeStruct((rows, cols), f32)
    return pl.pallas_call(
        body, name=name, grid=(rows // tr,), in_specs=part_specs + [spec, spec, spec],
        out_specs=[spec, spec, spec, spec], out_shape=[out, out, out, out],
        compiler_params=_cparams(("parallel",)),
    )(*parts, w, m, v)


def _gathered_to_full(gathered, axis):
    s = gathered.shape[1:]
    full = jnp.moveaxis(gathered, 0, axis)
    return full.reshape(s[:axis] + (N_DEV * s[axis],) + s[axis + 1:])


def _interleave_halves(w, ct):
    n = w.shape[-1] // 2
    t = w.reshape(w.shape[:-1] + (2, n // ct, ct))
    return jnp.swapaxes(t, -3, -2).reshape(w.shape)


def _deinterleave_halves(w, ct):
    n = w.shape[-1] // 2
    t = w.reshape(w.shape[:-1] + (n // ct, 2, ct))
    return jnp.swapaxes(t, -3, -2).reshape(w.shape)


def _rot_cols(w):
    half = QK_ROPE_DIM // 2
    return jnp.concatenate([-w[..., half:], w[..., :half]], axis=-1)


def _rot_cols_t(dw):
    half = QK_ROPE_DIM // 2
    return jnp.concatenate([dw[..., half:], -dw[..., :half]], axis=-1)


def kernel(x, positions, ln_in_g, ln_in_b, w_in, q_norm_g, w_uq, kv_norm_g, w_ukv, conv_w, conv_b, conv_ln_g, conv_ln_b, w_pool, pool_scale, w_out, ln1_g, ln1_b, w_up, ffn_conv_w, ffn_conv_b, w_down, ln2_g, ln2_b, loss_target, m_ln_in_g, m_ln_in_b, m_w_in, m_q_norm_g, m_w_uq, m_kv_norm_g, m_w_ukv, m_conv_w, m_conv_b, m_conv_ln_g, m_conv_ln_b, m_w_pool, m_pool_scale, m_w_out, m_ln1_g, m_ln1_b, m_w_up, m_ffn_conv_w, m_ffn_conv_b, m_w_down, m_ln2_g, m_ln2_b, v_ln_in_g, v_ln_in_b, v_w_in, v_q_norm_g, v_w_uq, v_kv_norm_g, v_w_ukv, v_conv_w, v_conv_b, v_conv_ln_g, v_conv_ln_b, v_w_pool, v_pool_scale, v_w_out, v_ln1_g, v_ln1_b, v_w_up, v_ffn_conv_w, v_ffn_conv_b, v_w_down, v_ln2_g, v_ln2_b):
    weights = dict(ln_in_g=ln_in_g, ln_in_b=ln_in_b, w_in=w_in, q_norm_g=q_norm_g, w_uq=w_uq, kv_norm_g=kv_norm_g,
                   w_ukv=w_ukv, conv_w=conv_w, conv_b=conv_b, conv_ln_g=conv_ln_g, conv_ln_b=conv_ln_b, w_pool=w_pool,
                   pool_scale=pool_scale, w_out=w_out, ln1_g=ln1_g, ln1_b=ln1_b, w_up=w_up, ffn_conv_w=ffn_conv_w,
                   ffn_conv_b=ffn_conv_b, w_down=w_down, ln2_g=ln2_g, ln2_b=ln2_b)
    mom1 = dict(ln_in_g=m_ln_in_g, ln_in_b=m_ln_in_b, w_in=m_w_in, q_norm_g=m_q_norm_g, w_uq=m_w_uq,
                kv_norm_g=m_kv_norm_g, w_ukv=m_w_ukv, conv_w=m_conv_w, conv_b=m_conv_b, conv_ln_g=m_conv_ln_g,
                conv_ln_b=m_conv_ln_b, w_pool=m_w_pool, pool_scale=m_pool_scale, w_out=m_w_out, ln1_g=m_ln1_g,
                ln1_b=m_ln1_b, w_up=m_w_up, ffn_conv_w=m_ffn_conv_w, ffn_conv_b=m_ffn_conv_b, w_down=m_w_down,
                ln2_g=m_ln2_g, ln2_b=m_ln2_b)
    mom2 = dict(ln_in_g=v_ln_in_g, ln_in_b=v_ln_in_b, w_in=v_w_in, q_norm_g=v_q_norm_g, w_uq=v_w_uq,
                kv_norm_g=v_kv_norm_g, w_ukv=v_w_ukv, conv_w=v_conv_w, conv_b=v_conv_b, conv_ln_g=v_conv_ln_g,
                conv_ln_b=v_conv_ln_b, w_pool=v_w_pool, pool_scale=v_pool_scale, w_out=v_w_out, ln1_g=v_ln1_g,
                ln1_b=v_ln1_b, w_up=v_w_up, ffn_conv_w=v_ffn_conv_w, ffn_conv_b=v_ffn_conv_b, w_down=v_w_down,
                ln2_g=v_ln2_g, ln2_b=v_ln2_b)
    names = list(weights)

    nb, seq, d = x.shape
    t = nb * seq
    depth = w_in.shape[0]
    ql, kvl, cw, pw = q_norm_g.shape[1], kv_norm_g.shape[1], conv_b.shape[1], pool_scale.shape[1]
    pg = w_pool.shape[-1]
    heads = w_uq.shape[2]
    dff = w_down.shape[1] * N_DEV
    alpha = (2.0 * depth) ** 0.25
    scale = float(QK_NOPE_DIM + QK_ROPE_DIM) ** -0.5
    lay = dict(ql=ql, kvl=kvl, pw=pw, pg=pg, off_q=2 * cw, off_pool=2 * cw + ql, off_kv=2 * cw + ql + pw,
               off_kr=2 * cw + ql + pw + kvl)
    o1, o2, o3, o4 = ql, ql + kvl, ql + kvl + QK_ROPE_DIM, ql + kvl + QK_ROPE_DIM + 2 * cw
    my_x, my_y, my_c = lax.axis_index("x"), lax.axis_index("y"), lax.axis_index("c")
    my_dev = 4 * my_x + 2 * my_y + my_c

    big = ("w_in", "w_uq", "w_ukv", "w_out", "w_up", "w_down")
    g_conv, g_ffn = _all_gather([conv_w, ffn_conv_w], "ag_conv_taps")
    conv_w_full, ffn_w_full = _gathered_to_full(g_conv, 2), _gathered_to_full(g_ffn, 2)
    w_pool_2d = w_pool.reshape(depth, pw, pg)

    rest = ("w_in", "w_uq", "w_ukv", "w_out")

    def bf16_shards(l, which):
        return [weights[n][l].astype(bf16) for n in which]

    def small_weights(gathered):
        g_in, g_uq, g_ukv, g_out = (gathered[n] for n in rest)
        wi = _gathered_to_full(g_in, 1)
        kr_cols = wi[:, o2:o3]
        w_in_pad = jnp.concatenate([_interleave_halves(wi[:, o3:o4], SEQ_CT), wi[:, :o1], wi[:, o4:], wi[:, o1:o2],
                                    kr_cols, _rot_cols(kr_cols)], axis=-1)
        wq = g_uq.reshape(ql, heads, QK_NOPE_DIM + QK_ROPE_DIM)
        w_uq_ext = jnp.concatenate([wq, _rot_cols(wq[..., QK_NOPE_DIM:])], axis=-1).reshape(ql, heads * 2 * LANES)
        return w_in_pad, w_uq_ext, g_ukv.reshape(kvl, heads * 2 * LANES), g_out.reshape(-1, d)

    half = QK_ROPE_DIM // 2
    inv = 1.0 / (ROPE_THETA ** (jnp.arange(0, QK_ROPE_DIM, 2, dtype=f32) / QK_ROPE_DIM))
    inv_lanes = jnp.tile(inv, LANES // half).reshape(1, LANES)
    cs, qt = _rope_tables(positions.reshape(t, 1), inv_lanes, scale)

    x2 = x.reshape(t, d)
    xs, xs_bf = _ln_fwd(x2, None, ln_in_g.reshape(1, d), ln_in_b.reshape(1, d), 1.0, "ln_in_fwd")
    saved = []
    in_pad = o4 + pw + QK_ROPE_DIM
    gathered = dict(zip(rest, _all_gather(bf16_shards(0, rest), "ag_weights")))
    for l in range(depth):
        nxt = l + 1 < depth
        gq, gkv = q_norm_g[l].reshape(1, ql), kv_norm_g[l].reshape(1, kvl)
        w_in_l, w_uq_l, w_ukv_l, w_out_l = small_weights(gathered)
        h = _matmul(xs_bf, w_in_l, "nn", f32, "mm_in", tm=512, tn=in_pad)
        qn, kvn, krd = _prep_fwd(h, cs, gq, gkv, lay)
        q_ext = _matmul(qn, w_uq_l, "nn", f32, "mm_uq", tn=2048)
        kv = _matmul(kvn, w_ukv_l, "nn", bf16, "mm_ukv", tn=2048)
        riders = ("w_down", "w_up") if l == 0 else ("w_down",)
        y_mla, lse, g_half = _attn_fwd(q_ext, qt, kv, krd, nb, seq, heads, comm=_gather_own(bf16_shards(l, riders)))
        hconv = _conv_fwd(h, conv_w_full[l], conv_b[l].reshape(1, cw), nb, seq, cw)
        y_conv = _convln_fwd(hconv, conv_ln_g[l].reshape(1, cw), conv_ln_b[l].reshape(1, cw))
        y_pool = _pool_fwd(h, w_pool_2d[l], pool_scale[l].reshape(1, pw), lay, nb, seq)
        mixed = jnp.concatenate([y_mla, y_conv, y_pool], axis=-1)
        y1, g_full = _matmul(mixed, w_out_l, "nn", f32, "mm_out", comm=_gather_pass(g_half))
        gathered.update(zip(riders, g_full))
        wl = dict(w_in=w_in_l, w_uq=w_uq_l, w_ukv=w_ukv_l, w_out=w_out_l, w_up=gathered["w_up"],
                  w_down=gathered["w_down"].reshape(dff, d))
        gathered = {}
        x1, x1_bf = _ln_fwd(xs, y1, ln1_g[l].reshape(1, d), ln1_b[l].reshape(1, d), alpha, "ln1_fwd")
        if nxt:
            up, (g_up,) = _matmul_up(x1_bf, wl["w_up"], "mm_up", comm=_gather_own(bf16_shards(l + 1, ("w_up",))))
            act, g_rest = _ffn_act_fwd(up, ffn_w_full[l], ffn_conv_b[l].reshape(1, 2 * dff), nb, seq, dff,
                                       comm=_gather_own(bf16_shards(l + 1, rest[1:])))
            y2, (gathered["w_up"], g_in) = _matmul(
                act, wl["w_down"], "nn", f32, "mm_down", tk=dff // 2,
                comm=_comm_join(_gather_pass([g_up]), _gather_own(bf16_shards(l + 1, rest[:1]))))
            xn, xn_bf, g_rest = _ln_fwd(x1, y2, ln2_g[l].reshape(1, d), ln2_b[l].reshape(1, d), alpha, "ln2_fwd",
                                        comm=_gather_pass([g_in] + g_rest))
            gathered.update(zip(rest, g_rest))
        else:
            up = _matmul_up(x1_bf, wl["w_up"], "mm_up")
            act = _ffn_act_fwd(up, ffn_w_full[l], ffn_conv_b[l].reshape(1, 2 * dff), nb, seq, dff)
            y2 = _matmul(act, wl["w_down"], "nn", f32, "mm_down", tk=dff // 2)
            xn, xn_bf = _ln_fwd(x1, y2, ln2_g[l].reshape(1, d), ln2_b[l].reshape(1, d), alpha, "ln2_fwd")
        saved.append(dict(xs=xs, xs_bf=xs_bf, h=h, qn=qn, kvn=kvn, krd=krd, q_ext=q_ext, kv=kv, lse=lse, hconv=hconv,
                          mixed=mixed, y1=y1, x1=x1, x1_bf=x1_bf, up=up, act=act, y2=y2, wl=wl))
        xs, xs_bf = xn, xn_bf

    d_stream, loss_row = _loss_call(xs, loss_target.reshape(t, d))
    loss = lax.psum(loss_row[0, 0], MESH_AXES)

    gw = {n: [None] * depth for n in names if n not in ("ln_in_g", "ln_in_b")}
    rs_own = {n: [None] * depth for n in big}
    rs_chips = {n: [None] * depth for n in big}
    core_idx = jnp.reshape(my_c, (1,)).astype(jnp.int32)
    chip_idx = jnp.reshape(2 * my_x + my_y, (1,)).astype(jnp.int32)

    def pair_add(l, which, slots, theirs):
        p4s = [_pair_add(slots[n], th, core_idx, "rs_add_" + n) for n, th in zip(which, theirs)]
        for n, p4 in zip(which, p4s):
            rs_own[n][l] = p4
        return p4s

    d_res, d_mm = None, d_stream
    pending = None
    for l in reversed(range(depth)):
        sv = saved[l]
        wl = sv["wl"]
        slots = {}
        gq, gkv = q_norm_g[l].reshape(1, ql), kv_norm_g[l].reshape(1, kvl)
        dz2, dz2_bf, gw["ln2_g"][l], gw["ln2_b"][l] = _ln_bwd(
            d_res, d_mm, sv["x1"], sv["y2"], ln2_g[l].reshape(1, d), ln2_b[l].reshape(1, d), alpha, "ln2_bwd")
        if pending is None:
            dw_down = _matmul(sv["act"], dz2_bf, "tn", bf16, "mm_down_dw", tm=dff // 4)
        else:
            dw_down, theirs = _matmul(sv["act"], dz2_bf, "tn", bf16, "mm_down_dw", tm=dff // 4,
                                      comm=_sibling_swap([pending["slots"][n] for n in rest]))
            rest_p4s = pair_add(pending["layer"], rest, pending["slots"], theirs)
        slots["w_down"] = dw_down.reshape(N_DEV, -1, d)
        d_act, theirs = _matmul(dz2_bf, wl["w_down"], "nt", f32, "mm_down_dx", tn=dff // 4,
                                comm=_sibling_swap([slots["w_down"]]))
        down_p4s = pair_add(l, ("w_down",), slots, theirs)
        ffn_args = (sv["up"], ffn_w_full[l], ffn_conv_b[l].reshape(1, 2 * dff), d_act, nb, seq, dff)
        if pending is None:
            d_up, dffw, dffb = _ffn_act_bwd(*ffn_args)
        else:
            d_up, dffw, dffb, (rs_chips["w_up"][pending["layer"]],) = _ffn_act_bwd(
                *ffn_args, comm=_chip_swap(pending["up_p4s"], peers=(2,), into=pending["up_partial"]))
        gw["ffn_conv_w"][l] = jnp.concatenate([dffw[0], dffw[1]], axis=-1)
        gw["ffn_conv_b"][l] = dffb.reshape(2 * dff)
        d_x1, (rs_chips["w_down"][l],) = _matmul_up_dx(d_up, wl["w_up"], "mm_up_dx", comm=_chip_swap(down_p4s))
        if pending is None:
            slots["w_up"] = _matmul_up_dw(sv["x1_bf"], d_up, N_DEV, "mm_up_dw")
        else:
            slots["w_up"], from_chips = _matmul_up_dw(sv["x1_bf"], d_up, N_DEV, "mm_up_dw", comm=_chip_swap(rest_p4s))
            for n, fc in zip(rest, from_chips):
                rs_chips[n][pending["layer"]] = fc
        dz1, dz1_bf, gw["ln1_g"][l], gw["ln1_b"][l] = _ln_bwd(
            dz2, d_x1, sv["xs"], sv["y1"], ln1_g[l].reshape(1, d), ln1_b[l].reshape(1, d), alpha, "ln1_bwd")
        d_mixed, theirs = _matmul(dz1_bf, wl["w_out"], "nt", f32, "mm_out_dx", comm=_sibling_swap([slots["w_up"]]))
        up_p4s = pair_add(l, ("w_up",), slots, theirs)
        slots["w_out"] = _matmul(sv["mixed"], dz1_bf, "tn", bf16, "mm_out_dw").reshape(N_DEV, -1, d)
        d_upool, dwp, dps = _pool_bwd(sv["h"], w_pool_2d[l], pool_scale[l].reshape(1, pw), d_mixed,
                                      (heads * LANES + cw) // pw, lay, nb, seq)
        gw["w_pool"][l] = dwp.reshape(w_pool.shape[1:])
        gw["pool_scale"][l] = dps.reshape(pw)
        d_hconv, dclg, dclb = _convln_bwd(sv["hconv"], conv_ln_g[l].reshape(1, cw), conv_ln_b[l].reshape(1, cw), d_mixed,
                                          heads * LANES // cw)
        gw["conv_ln_g"][l], gw["conv_ln_b"][l] = dclg.reshape(cw), dclb.reshape(cw)
        d_conv, gw["conv_w"][l], dcb = _conv_bwd(sv["h"], conv_w_full[l], d_hconv, nb, seq, cw)
        gw["conv_b"][l] = dcb.reshape(cw)
        dq_ext, dkv, dkrd, up_partial = _attn_bwd(sv["q_ext"], qt, sv["kv"], sv["krd"], sv["lse"], d_mixed, nb, seq, heads,
                                                  comm=_chip_swap(up_p4s, peers=(0, 1, 2) if l == 0 else (0, 1)))
        if l == 0:
            rs_chips["w_up"][l] = up_partial[0]
        d_qn = _matmul(dq_ext, wl["w_uq"], "nt", f32, "mm_uq_dx")
        dwq = _matmul(sv["qn"], dq_ext, "tn", f32, "mm_uq_dw", tn=2048, tk=1024).reshape(ql, heads, 2 * LANES)
        dwq_rope = dwq[..., QK_NOPE_DIM:QK_NOPE_DIM + QK_ROPE_DIM] + _rot_cols_t(dwq[..., QK_NOPE_DIM + QK_ROPE_DIM:])
        slots["w_uq"] = jnp.concatenate([dwq[..., :QK_NOPE_DIM], dwq_rope], axis=-1).astype(bf16).reshape(
            N_DEV, -1, QK_NOPE_DIM + QK_ROPE_DIM)
        d_kvn = _matmul(dkv, wl["w_ukv"], "nt", f32, "mm_ukv_dx")
        slots["w_ukv"] = _matmul(sv["kvn"], dkv, "tn", bf16, "mm_ukv_dw", tn=2048, tk=1024).reshape(N_DEV, -1, 2 * LANES)
        d_cq, d_ckv, d_kr, dgq, dgkv = _prep_bwd(sv["h"], cs, gq, gkv, d_qn, d_kvn, dkrd, lay)
        gw["q_norm_g"][l], gw["kv_norm_g"][l] = dgq.reshape(ql), dgkv.reshape(kvl)
        d_h = jnp.concatenate([d_conv, d_cq, d_upool, d_ckv, d_kr], axis=-1)
        d_xs = _matmul(d_h, wl["w_in"], "nt", f32, "mm_in_dx", tm=512, tk=in_pad)
        dwi = _matmul(sv["xs_bf"], d_h, "tn", f32, "mm_in_dw", tn=in_pad, tk=1024)
        dkr_cols = dwi[:, lay["off_kr"]:lay["off_kr"] + QK_ROPE_DIM] + _rot_cols_t(dwi[:, lay["off_kr"] + QK_ROPE_DIM:])
        dwi_nat = jnp.concatenate(
            [dwi[:, lay["off_q"]:lay["off_q"] + ql], dwi[:, lay["off_kv"]:lay["off_kv"] + kvl], dkr_cols,
             _deinterleave_halves(dwi[:, :2 * cw], SEQ_CT), dwi[:, lay["off_pool"]:lay["off_pool"] + pw]],
            axis=-1).astype(bf16)
        slots["w_in"] = jnp.moveaxis(dwi_nat.reshape(d, N_DEV, -1), 1, 0)
        pending = dict(layer=l, slots=slots, up_p4s=up_p4s, up_partial=up_partial)
        d_res, d_mm = dz1, d_xs

    grad_x, _, d_ln_in_g, d_ln_in_b = _ln_bwd(d_res, d_mm, x2, None, ln_in_g.reshape(1, d), ln_in_b.reshape(1, d), alpha,
                                               "ln_in_bwd")
    grad_x = grad_x.reshape(x.shape)

    grads, deltas, new_m, new_v = {}, {}, {}, {}

    def finish(n, parts):
        shp = weights[n].shape
        rows = math.prod(shp[:-1]) if len(shp) > 1 else 1
        as2d = lambda a: a.reshape(rows, shp[-1])
        parts = [p.reshape(p.shape[0], rows, shp[-1]) for p in parts]
        g, dl, nm, nv = _adamw(parts, as2d(weights[n]), as2d(mom1[n]), as2d(mom2[n]), "adamw_" + n)
        grads[n], deltas[n], new_m[n], new_v[n] = (a.reshape(shp) for a in (g, dl, nm, nv))

    def finish_big(n, comm=None):
        shp = weights[n].shape
        as2d = lambda a: a.reshape(-1, shp[-1])
        res = _adamw_layers(rs_own[n], rs_chips[n], chip_idx, as2d(weights[n]), as2d(mom1[n]), as2d(mom2[n]),
                            "adamw_" + n, comm)
        outs, comm_outs = res if comm is not None else (res, None)
        grads[n], deltas[n], new_m[n], new_v[n] = (a.reshape(shp) for a in outs)
        return comm_outs

    small = [n for n in names if n not in big]
    partial = {"ln_in_g": d_ln_in_g.reshape(d), "ln_in_b": d_ln_in_b.reshape(d)}
    for n in small:
        if n not in partial:
            partial[n] = jnp.stack(gw[n])
    flat = jnp.concatenate([partial[n].astype(f32).reshape(-1) for n in small]).reshape(-1, LANES)
    *theirs, small_half = finish_big("w_up", _comm_join(_sibling_swap([pending["slots"][n] for n in rest]),
                                                         _gather_own([flat])))
    rest_p4s = pair_add(pending["layer"], rest, pending["slots"], theirs)
    *from_chips, gathered = finish_big("w_down", _comm_join(_chip_swap(rest_p4s), _gather_pass([small_half])))
    for n, fc in zip(rest, from_chips):
        rs_chips[n][pending["layer"]] = fc
    for n in rest:
        finish_big(n)
    gathered = gathered.reshape(N_DEV, -1)
    off = 0
    for n in small:
        size = math.prod(partial[n].shape)
        part = gathered[:, off:off + size].reshape((N_DEV,) + partial[n].shape)
        off += size
        if n in ("conv_w", "ffn_conv_w"):
            width = weights[n].shape[-1]
            part = lax.dynamic_slice_in_dim(part, my_dev * width, width, axis=part.ndim - 1)
        finish(n, [part])

    return (loss, grad_x, *[grads[n] for n in names], *[deltas[n] for n in names], *[new_m[n] for n in names],
            *[new_v[n] for n in names])
```

```python
import functools
import math

import jax
import jax.numpy as jnp
from jax import lax
from jax.experimental import pallas as pl
from jax.experimental.pallas import tpu as pltpu

f32 = jnp.float32
bf16 = jnp.bfloat16

QK_NOPE_DIM = 128
QK_ROPE_DIM = 64
V_HEAD_DIM = 128
CONV_KERNEL = 31
FFN_CONV_KERNEL = 3
POOL_WINDOWS = (2, 4, 8, 16)
ROPE_THETA = 10000.0
LN_EPS = 1e-5
RMS_EPS = 1e-6
ADAM_LR = 0.001
ADAM_B1 = 0.9
ADAM_B2 = 0.999
ADAM_EPS = 1e-08
ADAM_WD = 0.01
ADAM_STEP = 10

N_DEV = 8
MESH_AXES = ("x", "y", "c")
V7X_VMEM_LIMIT_BYTES = 56 * 1024 * 1024
LANES = 128
NEG_INF = -1e30
MESH = pl.DeviceIdType.MESH


def _cparams(sem):
    return pltpu.CompilerParams(dimension_semantics=sem, vmem_limit_bytes=V7X_VMEM_LIMIT_BYTES)


def _tile(dim, pref):
    t = pref
    while t >= LANES:
        if dim % t == 0:
            return t
        t //= 2
    return dim


_ANY = pl.BlockSpec(memory_space=pl.ANY)


class _Comm:
    def __init__(self, inputs, out_shapes, sems, start, finish, aliases=None):
        self.inputs, self.out_shapes, self.sems = list(inputs), list(out_shapes), list(sems)
        self.start, self.finish, self.aliases = start, finish, dict(aliases or {})


def _call(body, name, grid, in_specs, out_specs, out_shape, args, scratch=(), sem=None, comm=None, prefetch=()):
    in_specs, out_specs, out_shape, scratch = list(in_specs), list(out_specs), list(out_shape), list(scratch)
    n_pre, n_in, n_out, n_scr = len(prefetch), len(in_specs), len(out_specs), len(scratch)
    c_in, c_out = (len(comm.inputs), len(comm.out_shapes)) if comm else (0, 0)

    def carrier(*refs):
        refs = list(refs)
        pre, refs = refs[:n_pre], refs[n_pre:]
        ins, refs = refs[:n_in], refs[n_in:]
        c_ins, refs = refs[:c_in], refs[c_in:]
        outs, refs = refs[:n_out], refs[n_out:]
        c_outs, refs = refs[:c_out], refs[c_out:]
        scr, c_sems = refs[:n_scr], refs[n_scr:]
        ids = [pl.program_id(a) for a in range(len(grid))]
        first = functools.reduce(lambda p, q: p & q, [i == 0 for i in ids])
        last = functools.reduce(lambda p, q: p & q, [i == g - 1 for i, g in zip(ids, grid)])
        pl.when(first)(lambda: comm.start(c_ins, c_outs, c_sems))
        body(*pre, *ins, *outs, *scr)
        pl.when(last)(lambda: comm.finish(c_ins, c_outs, c_sems))

    grid_spec = pltpu.PrefetchScalarGridSpec(
        num_scalar_prefetch=n_pre, grid=grid, in_specs=in_specs + [_ANY] * c_in, out_specs=out_specs + [_ANY] * c_out,
        scratch_shapes=scratch + (comm.sems if comm else []))
    if comm is None:
        outs = pl.pallas_call(body, name=name, grid_spec=grid_spec, out_shape=out_shape,
                              compiler_params=_cparams(sem))(*prefetch, *args)
        return list(outs), []
    outs = pl.pallas_call(
        carrier, name=name, grid_spec=grid_spec, out_shape=out_shape + comm.out_shapes,
        input_output_aliases={n_pre + n_in + a: n_out + b for a, b in comm.aliases.items()},
        compiler_params=_cparams(("arbitrary",) * len(grid)),
    )(*prefetch, *args, *comm.inputs)
    return list(outs[:n_out]), list(outs[n_out:])


def _shift_down_raw(x, k):
    if k == 0:
        return x
    row = lax.broadcasted_iota(jnp.int32, x.shape, 0)
    return jnp.where(row >= k, pltpu.roll(x, k, axis=0), 0.0)


def _shift_up_raw(x, k):
    if k == 0:
        return x
    n = x.shape[0]
    row = lax.broadcasted_iota(jnp.int32, x.shape, 0)
    return jnp.where(row < n - k, pltpu.roll(x, n - k, axis=0), 0.0)


@functools.partial(jax.custom_vjp, nondiff_argnums=(1,))
def _shift_down(x, k):
    return _shift_down_raw(x, k)


def _shift_down_fwd(x, k):
    return _shift_down_raw(x, k), None


def _shift_down_bwd(k, _, g):
    return (_shift_up_raw(g, k),)


_shift_down.defvjp(_shift_down_fwd, _shift_down_bwd)


@jax.custom_vjp
def _dup_halves(p):
    return p + pltpu.roll(p, LANES // 2, axis=1)


def _dup_halves_fwd(p):
    return p + pltpu.roll(p, LANES // 2, axis=1), None


def _dup_halves_bwd(_, g):
    return (g + pltpu.roll(g, LANES // 2, axis=1),)


_dup_halves.defvjp(_dup_halves_fwd, _dup_halves_bwd)

_NN = (((1,), (0,)), ((), ()))
_NT = (((1,), (1,)), ((), ()))
_TN = (((0,), (0,)), ((), ()))


def _dot(a, b, dims):
    return lax.dot_general(a.astype(bf16), b.astype(bf16), dims, preferred_element_type=f32)


@jax.custom_vjp
def _mm_bf16(a, b):
    return _dot(a, b, _NN)


def _mm_bf16_fwd(a, b):
    return _dot(a, b, _NN), (a, b)


def _mm_bf16_bwd(res, g):
    a, b = res
    return _dot(g, b, _NT), _dot(a, g, _TN)


_mm_bf16.defvjp(_mm_bf16_fwd, _mm_bf16_bwd)


def _layer_norm(z, g, b):
    mu = jnp.mean(z, axis=-1, keepdims=True)
    var = jnp.mean(jnp.square(z - mu), axis=-1, keepdims=True)
    return (z - mu) * lax.rsqrt(var + LN_EPS) * g + b


def _rms_norm(x, g):
    ms = jnp.mean(jnp.square(x), axis=-1, keepdims=True)
    return x * lax.rsqrt(ms + RMS_EPS) * g


def _colsum(x):
    return jnp.sum(x, axis=0, keepdims=True)


def _matmul_core(a, b, mode, grid, a_spec, b_spec, o_spec, o_shape, tile, out_dtype, name, comm=None):
    nk = grid[2]
    dims = {"nn": _NN, "nt": _NT, "tn": _TN}[mode]
    acc_in_out = out_dtype == f32

    def body(a_ref, b_ref, o_ref, *scratch):
        def prod():
            return _dot(a_ref[...], b_ref[...], dims)

        if nk == 1:
            o_ref[...] = prod().astype(out_dtype)
            return
        acc_ref = o_ref if acc_in_out else scratch[0]
        kk = pl.program_id(2)

        @pl.when(kk == 0)
        def _():
            acc_ref[...] = prod()

        if acc_in_out:
            @pl.when(kk > 0)
            def _():
                acc_ref[...] += prod()
        else:
            @pl.when((kk > 0) & (kk < nk - 1))
            def _():
                acc_ref[...] += prod()

            @pl.when(kk == nk - 1)
            def _():
                o_ref[...] = (acc_ref[...] + prod()).astype(out_dtype)

    scratch = [] if (nk == 1 or acc_in_out) else [pltpu.VMEM(tile, f32)]
    (out,), comm_outs = _call(body, name, grid, [a_spec, b_spec], [o_spec], [jax.ShapeDtypeStruct(o_shape, out_dtype)],
                              (a, b), scratch, ("parallel", "parallel", "arbitrary"), comm)
    return out if comm is None else (out, comm_outs)


def _matmul(a, b, mode, out_dtype, name, tm=1024, tn=1024, tk=2048, comm=None):
    if mode == "nn":
        (m, k), (k2, n) = a.shape, b.shape
    elif mode == "nt":
        (m, k), (n, k2) = a.shape, b.shape
    else:
        (k, m), (k2, n) = a.shape, b.shape
    assert k == k2, (name, a.shape, b.shape)
    tm, tn, tk = _tile(m, tm), _tile(n, tn), _tile(k, tk)
    a_spec = pl.BlockSpec((tk, tm), lambda i, j, kk: (kk, i)) if mode == "tn" else pl.BlockSpec((tm, tk), lambda i, j, kk: (i, kk))
    b_spec = pl.BlockSpec((tn, tk), lambda i, j, kk: (j, kk)) if mode == "nt" else pl.BlockSpec((tk, tn), lambda i, j, kk: (kk, j))
    return _matmul_core(a, b, mode, (m // tm, n // tn, k // tk), a_spec, b_spec,
                        pl.BlockSpec((tm, tn), lambda i, j, kk: (i, j)), (m, n), (tm, tn), out_dtype, name, comm)


def _matmul_up(x_bf, w_slots, name, tm=1024, comm=None):
    m, k = x_bf.shape
    s, _, ns = w_slots.shape
    tm = _tile(m, tm)
    return _matmul_core(x_bf, w_slots, "nn", (m // tm, s, 1), pl.BlockSpec((tm, k), lambda i, j, kk: (i, 0)),
                        pl.BlockSpec((None, k, ns), lambda i, j, kk: (j, 0, 0)),
                        pl.BlockSpec((tm, ns), lambda i, j, kk: (i, j)), (m, s * ns), (tm, ns), f32, name, comm)


def _matmul_up_dx(d3, w_slots, name, tm=1024, tn=1024, comm=None):
    _, m, half = d3.shape
    s, n, ns = w_slots.shape
    per_half = half // ns
    assert 2 * per_half == s, (d3.shape, w_slots.shape)
    tm, tn = _tile(m, tm), _tile(n, tn)
    return _matmul_core(d3, w_slots, "nt", (m // tm, n // tn, s),
                        pl.BlockSpec((None, tm, ns), lambda i, j, kk: (kk // per_half, i, kk % per_half)),
                        pl.BlockSpec((None, tn, ns), lambda i, j, kk: (kk, j, 0)),
                        pl.BlockSpec((tm, tn), lambda i, j, kk: (i, j)), (m, n), (tm, tn), f32, name, comm)


def _matmul_up_dw(x_bf, d3, n_slots, name, tm=1024, tk=2048, comm=None):
    k, m = x_bf.shape
    _, _, half = d3.shape
    ns = 2 * half // n_slots
    per_half = n_slots // 2
    tm, tk = _tile(m, tm), _tile(k, tk)
    return _matmul_core(x_bf, d3, "tn", (m // tm, n_slots, k // tk), pl.BlockSpec((tk, tm), lambda i, j, kk: (kk, i)),
                        pl.BlockSpec((None, tk, ns), lambda i, j, kk: (j // per_half, kk, j % per_half)),
                        pl.BlockSpec((None, tm, ns), lambda i, j, kk: (j, i, 0)), (n_slots, m, ns), (tm, ns), bf16, name,
                        comm)


ROW_TILE = 256


def _rows(width, col_block=0):
    return pl.BlockSpec((ROW_TILE, width), lambda i, cb=col_block: (i, cb))


def _whole(shape):
    return pl.BlockSpec(shape, lambda i: (0,) * len(shape))


def _ln_fwd(x, y, g, b, alpha, name, comm=None):
    t, d = x.shape

    def body(*refs):
        if y is None:
            x_ref, g_ref, b_ref, o_ref, ob_ref = refs
            z = x_ref[...]
        else:
            x_ref, y_ref, g_ref, b_ref, o_ref, ob_ref = refs
            z = alpha * x_ref[...] + y_ref[...]
        out = _layer_norm(z, g_ref[...], b_ref[...])
        o_ref[...] = out
        ob_ref[...] = out.astype(bf16)

    ins = [x] + ([] if y is None else [y]) + [g, b]
    specs = [_rows(d)] + ([] if y is None else [_rows(d)]) + [_whole((1, d)), _whole((1, d))]
    (out, out_bf), comm_outs = _call(
        body, name, (t // ROW_TILE,), specs, [_rows(d), _rows(d)],
        [jax.ShapeDtypeStruct((t, d), f32), jax.ShapeDtypeStruct((t, d), bf16)], ins, (), ("parallel",), comm)
    return (out, out_bf) if comm is None else (out, out_bf, comm_outs)


def _ln_bwd(d_res, d_mm, x, y, g, b, alpha, name, comm=None):
    t, d = x.shape
    has_res, has_mm, has_y = d_res is not None, d_mm is not None, y is not None

    def body(*refs):
        refs = list(refs)
        d_res_ref = refs.pop(0) if has_res else None
        d_mm_ref = refs.pop(0) if has_mm else None
        x_ref = refs.pop(0)
        y_ref = refs.pop(0) if has_y else None
        g_ref, b_ref, dz_ref, dzb_ref, dg_ref, db_ref = refs
        ct = None
        if has_res:
            ct = alpha * d_res_ref[...]
        if has_mm:
            ct = d_mm_ref[...] if ct is None else ct + d_mm_ref[...]
        z = x_ref[...] if not has_y else alpha * x_ref[...] + y_ref[...]
        _, vjp = jax.vjp(_layer_norm, z, g_ref[...], b_ref[...])
        dz, dg, db = vjp(ct)
        dz_ref[...] = dz
        dzb_ref[...] = dz.astype(bf16)

        @pl.when(pl.program_id(0) == 0)
        def _():
            dg_ref[...] = jnp.zeros_like(dg_ref)
            db_ref[...] = jnp.zeros_like(db_ref)

        dg_ref[...] += dg
        db_ref[...] += db

    ins = [a for a in (d_res, d_mm, x, y) if a is not None] + [g, b]
    specs = [_rows(d) for a in (d_res, d_mm, x, y) if a is not None] + [_whole((1, d)), _whole((1, d))]
    outs, comm_outs = _call(
        body, name, (t // ROW_TILE,), specs, [_rows(d), _rows(d), _whole((1, d)), _whole((1, d))],
        [jax.ShapeDtypeStruct((t, d), f32), jax.ShapeDtypeStruct((t, d), bf16),
         jax.ShapeDtypeStruct((1, d), f32), jax.ShapeDtypeStruct((1, d), f32)], ins, (), ("arbitrary",), comm)
    return tuple(outs) if comm is None else (*outs, comm_outs)


def _loss_call(xf, target):
    t, d = xf.shape

    def body(x_ref, t_ref, dx_ref, loss_ref):
        err = x_ref[...] - t_ref[...]
        dx_ref[...] = err * (1.0 / d)

        @pl.when(pl.program_id(0) == 0)
        def _():
            loss_ref[...] = jnp.zeros_like(loss_ref)

        part = 0.5 * jnp.sum(jnp.mean(jnp.square(err), axis=-1, keepdims=True), axis=0, keepdims=True)
        loss_ref[...] += jnp.broadcast_to(part, loss_ref.shape)

    return pl.pallas_call(
        body, name="loss_head", grid=(t // ROW_TILE,), in_specs=[_rows(d), _rows(d)],
        out_specs=[_rows(d), _whole((1, LANES))],
        out_shape=[jax.ShapeDtypeStruct((t, d), f32), jax.ShapeDtypeStruct((1, LANES), f32)],
        compiler_params=_cparams(("arbitrary",)),
    )(xf, target)


def _rope_tables(pos, inv, scale):
    t = pos.shape[0]

    def body(pos_ref, inv_ref, cs_ref, qt_ref):
        ang = pos_ref[...].astype(f32) * inv_ref[...]
        lane = lax.broadcasted_iota(jnp.int32, ang.shape, 1)
        cs = jnp.where(lane < LANES // 2, jnp.cos(ang), jnp.sin(ang))
        cs_ref[...] = cs
        qt_ref[:, :LANES] = jnp.full((ROW_TILE, LANES), scale, f32)
        qt_ref[:, LANES:] = scale * cs

    return pl.pallas_call(
        body, name="rope_tables", grid=(t // ROW_TILE,),
        in_specs=[pl.BlockSpec((ROW_TILE, 1), lambda i: (i, 0)), _whole((1, LANES))],
        out_specs=[_rows(LANES), _rows(2 * LANES)],
        out_shape=[jax.ShapeDtypeStruct((t, LANES), f32), jax.ShapeDtypeStruct((t, 2 * LANES), f32)],
        compiler_params=_cparams(("parallel",)),
    )(pos, inv)


def _prep_fn(cq, ckv, kr, cs, gq, gkv):
    return _rms_norm(cq, gq), _rms_norm(ckv, gkv), _dup_halves(kr * cs)


def _prep_fwd(h, cs, gq, gkv, lay):
    t = h.shape[0]
    ql, kvl = lay["ql"], lay["kvl"]

    def body(cq_ref, ckv_ref, kr_ref, cs_ref, gq_ref, gkv_ref, qn_ref, kvn_ref, krd_ref):
        qn, kvn, krd = _prep_fn(cq_ref[...], ckv_ref[...], kr_ref[...], cs_ref[...], gq_ref[...], gkv_ref[...])
        qn_ref[...] = qn.astype(bf16)
        kvn_ref[...] = kvn.astype(bf16)
        krd_ref[...] = krd.astype(bf16)

    return pl.pallas_call(
        body, name="prep_fwd", grid=(t // ROW_TILE,),
        in_specs=[_rows(ql, lay["off_q"] // ql), _rows(kvl, lay["off_kv"] // kvl), _rows(LANES, lay["off_kr"] // LANES),
                  _rows(LANES), _whole((1, ql)), _whole((1, kvl))],
        out_specs=[_rows(ql), _rows(kvl), _rows(LANES)],
        out_shape=[jax.ShapeDtypeStruct((t, ql), bf16), jax.ShapeDtypeStruct((t, kvl), bf16),
                   jax.ShapeDtypeStruct((t, LANES), bf16)],
        compiler_params=_cparams(("parallel",)),
    )(h, h, h, cs, gq, gkv)


def _prep_bwd(h, cs, gq, gkv, d_qn, d_kvn, d_krd, lay):
    t = h.shape[0]
    ql, kvl = lay["ql"], lay["kvl"]

    def body(cq_ref, ckv_ref, kr_ref, cs_ref, gq_ref, gkv_ref, dqn_ref, dkvn_ref, dkrd_ref,
             dcq_ref, dckv_ref, dkr_ref, dgq_ref, dgkv_ref):
        _, vjp = jax.vjp(_prep_fn, cq_ref[...], ckv_ref[...], kr_ref[...], cs_ref[...], gq_ref[...], gkv_ref[...])
        dcq, dckv, dkr, _, dgq, dgkv = vjp((dqn_ref[...], dkvn_ref[...], dkrd_ref[...].astype(f32)))
        dcq_ref[...] = dcq.astype(bf16)
        dckv_ref[...] = dckv.astype(bf16)
        dkr_ref[...] = dkr.astype(bf16)

        @pl.when(pl.program_id(0) == 0)
        def _():
            dgq_ref[...] = jnp.zeros_like(dgq_ref)
            dgkv_ref[...] = jnp.zeros_like(dgkv_ref)

        dgq_ref[...] += dgq
        dgkv_ref[...] += dgkv

    return pl.pallas_call(
        body, name="prep_bwd", grid=(t // ROW_TILE,),
        in_specs=[_rows(ql, lay["off_q"] // ql), _rows(kvl, lay["off_kv"] // kvl), _rows(LANES, lay["off_kr"] // LANES),
                  _rows(LANES), _whole((1, ql)), _whole((1, kvl)), _rows(ql), _rows(kvl), _rows(LANES)],
        out_specs=[_rows(ql), _rows(kvl), _rows(LANES), _whole((1, ql)), _whole((1, kvl))],
        out_shape=[jax.ShapeDtypeStruct((t, ql), bf16), jax.ShapeDtypeStruct((t, kvl), bf16),
                   jax.ShapeDtypeStruct((t, LANES), bf16), jax.ShapeDtypeStruct((1, ql), f32),
                   jax.ShapeDtypeStruct((1, kvl), f32)],
        compiler_params=_cparams(("arbitrary",)),
    )(h, h, h, cs, gq, gkv, d_qn, d_kvn, d_krd)


def _convln_fn(hc, g, b):
    y = _layer_norm(hc, g, b)
    return y * jax.nn.sigmoid(y)


def _convln_fwd(hconv, g, b):
    t, cw = hconv.shape

    def body(h_ref, g_ref, b_ref, o_ref):
        o_ref[...] = _convln_fn(h_ref[...], g_ref[...], b_ref[...]).astype(bf16)

    return pl.pallas_call(
        body, name="convln_fwd", grid=(t // ROW_TILE,),
        in_specs=[_rows(cw), _whole((1, cw)), _whole((1, cw))], out_specs=_rows(cw),
        out_shape=jax.ShapeDtypeStruct((t, cw), bf16), compiler_params=_cparams(("parallel",)),
    )(hconv, g, b)


def _convln_bwd(hconv, g, b, d_mixed, col_block):
    t, cw = hconv.shape

    def body(h_ref, g_ref, b_ref, dy_ref, dh_ref, dg_ref, db_ref):
        _, vjp = jax.vjp(_convln_fn, h_ref[...], g_ref[...], b_ref[...])
        dh, dg, db = vjp(dy_ref[...])
        dh_ref[...] = dh

        @pl.when(pl.program_id(0) == 0)
        def _():
            dg_ref[...] = jnp.zeros_like(dg_ref)
            db_ref[...] = jnp.zeros_like(db_ref)

        dg_ref[...] += dg
        db_ref[...] += db

    return pl.pallas_call(
        body, name="convln_bwd", grid=(t // ROW_TILE,),
        in_specs=[_rows(cw), _whole((1, cw)), _whole((1, cw)), _rows(cw, col_block)],
        out_specs=[_rows(cw), _whole((1, cw)), _whole((1, cw))],
        out_shape=[jax.ShapeDtypeStruct((t, cw), f32), jax.ShapeDtypeStruct((1, cw), f32),
                   jax.ShapeDtypeStruct((1, cw), f32)],
        compiler_params=_cparams(("arbitrary",)),
    )(hconv, g, b, d_mixed)


SEQ_CT = 128


def _conv_fwd(h, w, b, nb, seq, cw):
    ct, kk = SEQ_CT, w.shape[0]
    ncb = cw // ct

    def body(h_ref, w_ref, b_ref, o_ref):
        blk = h_ref[...]
        a, g = blk[:, :ct], blk[:, ct:]
        hc = a * jax.nn.sigmoid(g)
        acc = jnp.zeros_like(hc)
        for k in range(kk):
            acc = acc + w_ref[k:k + 1, :] * _shift_down_raw(hc, kk - 1 - k)
        o_ref[...] = acc + b_ref[...]

    return pl.pallas_call(
        body, name="conv_fwd", grid=(ncb, nb),
        in_specs=[pl.BlockSpec((seq, 2 * ct), lambda j, bb: (bb, j)), pl.BlockSpec((kk, ct), lambda j, bb: (0, j)),
                  pl.BlockSpec((1, ct), lambda j, bb: (0, j))],
        out_specs=pl.BlockSpec((seq, ct), lambda j, bb: (bb, j)),
        out_shape=jax.ShapeDtypeStruct((nb * seq, cw), f32),
        compiler_params=_cparams(("parallel", "parallel")),
    )(h, w, b)


def _conv_bwd(h, w, d_hconv, nb, seq, cw):
    ct, kk = SEQ_CT, w.shape[0]
    ncb = cw // ct

    def body(h_ref, w_ref, dy_ref, dh_ref, dw_ref, db_ref):
        blk = h_ref[...]
        a, g = blk[:, :ct], blk[:, ct:]
        sg = jax.nn.sigmoid(g)
        hc = a * sg
        dy = dy_ref[...]
        dhc = jnp.zeros_like(hc)

        @pl.when(pl.program_id(1) == 0)
        def _():
            dw_ref[...] = jnp.zeros_like(dw_ref)
            db_ref[...] = jnp.zeros_like(db_ref)

        for k in range(kk):
            dhc = dhc + w_ref[k:k + 1, :] * _shift_up_raw(dy, kk - 1 - k)
            dw_ref[k:k + 1, :] += _colsum(dy * _shift_down_raw(hc, kk - 1 - k))
        db_ref[...] += _colsum(dy)
        dh_ref[:, :ct] = (dhc * sg).astype(bf16)
        dh_ref[:, ct:] = (dhc * a * sg * (1.0 - sg)).astype(bf16)

    return pl.pallas_call(
        body, name="conv_bwd", grid=(ncb, nb),
        in_specs=[pl.BlockSpec((seq, 2 * ct), lambda j, bb: (bb, j)), pl.BlockSpec((kk, ct), lambda j, bb: (0, j)),
                  pl.BlockSpec((seq, ct), lambda j, bb: (bb, j))],
        out_specs=[pl.BlockSpec((seq, 2 * ct), lambda j, bb: (bb, j)), pl.BlockSpec((kk, ct), lambda j, bb: (0, j)),
                   pl.BlockSpec((1, ct), lambda j, bb: (0, j))],
        out_shape=[jax.ShapeDtypeStruct((nb * seq, 2 * cw), bf16), jax.ShapeDtypeStruct((kk, cw), f32),
                   jax.ShapeDtypeStruct((1, cw), f32)],
        compiler_params=_cparams(("parallel", "arbitrary")),
    )(h, w, d_hconv)


def _pool_fn(u, wp, scale, pg):
    seq = u.shape[0]
    t1 = (lax.broadcasted_iota(jnp.int32, (seq, 1), 0) + 1).astype(f32)
    outs = []
    for gi, win in enumerate(POOL_WINDOWS):
        ug = u[:, gi * pg:(gi + 1) * pg]
        acc, span = ug, 1
        while span < win:
            acc = acc + _shift_down(acc, span)
            span *= 2
        d = acc / jnp.minimum(t1, float(win)) - ug
        outs.append(_mm_bf16(d, wp[gi * pg:(gi + 1) * pg, :]) * scale[:, gi * pg:(gi + 1) * pg])
    return outs


def _pool_fwd(h, wp, scale, lay, nb, seq):
    pw, pg = lay["pw"], lay["pg"]

    def body(u_ref, wp_ref, sc_ref, o_ref):
        outs = _pool_fn(u_ref[...], wp_ref[...], sc_ref[...], pg)
        for gi in range(len(POOL_WINDOWS)):
            o_ref[:, gi * pg:(gi + 1) * pg] = outs[gi].astype(bf16)

    return pl.pallas_call(
        body, name="pool_fwd", grid=(nb,),
        in_specs=[pl.BlockSpec((seq, pw), lambda bb: (bb, lay["off_pool"] // pw)), _whole((pw, pg)), _whole((1, pw))],
        out_specs=pl.BlockSpec((seq, pw), lambda bb: (bb, 0)),
        out_shape=jax.ShapeDtypeStruct((nb * seq, pw), bf16),
        compiler_params=_cparams(("parallel",)),
    )(h, wp, scale)


def _pool_bwd(h, wp, scale, d_mixed, col_block, lay, nb, seq):
    pw, pg = lay["pw"], lay["pg"]
    ng = len(POOL_WINDOWS)

    def body(u_ref, wp_ref, sc_ref, dy_ref, du_ref, dwp_ref, dsc_ref):
        _, vjp = jax.vjp(functools.partial(_pool_fn, pg=pg), u_ref[...], wp_ref[...], sc_ref[...])
        dy = dy_ref[...]
        du, dwp, dsc = vjp([dy[:, gi * pg:(gi + 1) * pg] for gi in range(ng)])
        du_ref[...] = du.astype(bf16)

        @pl.when(pl.program_id(0) == 0)
        def _():
            dwp_ref[...] = jnp.zeros_like(dwp_ref)
            dsc_ref[...] = jnp.zeros_like(dsc_ref)

        dwp_ref[...] += dwp
        dsc_ref[...] += dsc

    return pl.pallas_call(
        body, name="pool_bwd", grid=(nb,),
        in_specs=[pl.BlockSpec((seq, pw), lambda bb: (bb, lay["off_pool"] // pw)), _whole((pw, pg)), _whole((1, pw)),
                  pl.BlockSpec((seq, pw), lambda bb: (bb, col_block))],
        out_specs=[pl.BlockSpec((seq, pw), lambda bb: (bb, 0)), _whole((pw, pg)), _whole((1, pw))],
        out_shape=[jax.ShapeDtypeStruct((nb * seq, pw), bf16), jax.ShapeDtypeStruct((pw, pg), f32),
                   jax.ShapeDtypeStruct((1, pw), f32)],
        compiler_params=_cparams(("arbitrary",)),
    )(h, wp, scale, d_mixed)


FFN_ROWS = 32
SUBLANES = 8


def _rows_before(ref, r0, s, n):
    if s == 0:
        return ref[r0:r0 + n, :]
    if r0 == 0:
        x = ref[0:n, :]
        row = lax.broadcasted_iota(jnp.int32, x.shape, 0)
        return jnp.where(row >= s, pltpu.roll(x, s, axis=0), 0.0)
    return ref[pl.ds(r0 - s, n), :]


def _ffn_conv_rows(x_ref, w_ref, b_ref, r0, n):
    kk = w_ref.shape[0]
    xs = [_rows_before(x_ref, r0, s, n) for s in range(kk)]
    c = b_ref[...] + w_ref[kk - 1:kk, :] * xs[0]
    for k in range(kk - 1):
        c = c + w_ref[k:k + 1, :] * xs[kk - 1 - k]
    return c, xs


def _fold_rows(x):
    out = x[0:SUBLANES]
    for i in range(1, x.shape[0] // SUBLANES):
        out = out + x[i * SUBLANES:(i + 1) * SUBLANES]
    return out


def _ffn_specs(seq, ct, kk, nct):
    return [pl.BlockSpec((seq, ct), lambda j, bb: (bb, j)), pl.BlockSpec((seq, ct), lambda j, bb: (bb, nct + j)),
            pl.BlockSpec((kk, ct), lambda j, bb: (0, j)), pl.BlockSpec((kk, ct), lambda j, bb: (0, nct + j)),
            pl.BlockSpec((1, ct), lambda j, bb: (0, j)), pl.BlockSpec((1, ct), lambda j, bb: (0, nct + j))]


def _ffn_act_fwd(up, w, b, nb, seq, dff, comm=None):
    ct, kk = SEQ_CT, w.shape[0]
    rows = min(FFN_ROWS, seq)

    def body(ua_ref, ug_ref, wa_ref, wg_ref, ba_ref, bg_ref, o_ref):
        for r0 in range(0, seq, rows):
            a, _ = _ffn_conv_rows(ua_ref, wa_ref, ba_ref, r0, rows)
            g, _ = _ffn_conv_rows(ug_ref, wg_ref, bg_ref, r0, rows)
            o_ref[r0:r0 + rows, :] = (a * g * jax.nn.sigmoid(g)).astype(bf16)

    (act,), comm_outs = _call(body, "ffn_act_fwd", (dff // ct, nb), _ffn_specs(seq, ct, kk, dff // ct),
                              [pl.BlockSpec((seq, ct), lambda j, bb: (bb, j))],
                              [jax.ShapeDtypeStruct((nb * seq, dff), bf16)], (up, up, w, w, b, b), (),
                              ("parallel", "parallel"), comm)
    return act if comm is None else (act, comm_outs)


def _ffn_act_bwd(up, w, b, d_act, nb, seq, dff, comm=None):
    ct, kk = SEQ_CT, w.shape[0]
    rows = min(FFN_ROWS, seq)

    def body(ua_ref, ug_ref, wa_ref, wg_ref, ba_ref, bg_ref, da_ref, du_ref, dw_ref, db_ref, dc_ref):
        @pl.when(pl.program_id(1) == 0)
        def _():
            dw_ref[...] = jnp.zeros_like(dw_ref)
            db_ref[...] = jnp.zeros_like(db_ref)

        dw_acc = [[jnp.zeros((SUBLANES, ct), f32) for _ in range(kk)] for _ in range(2)]
        db_acc = [jnp.zeros((SUBLANES, ct), f32) for _ in range(2)]
        for r0 in range(0, seq, rows):
            a, xa = _ffn_conv_rows(ua_ref, wa_ref, ba_ref, r0, rows)
            g, xg = _ffn_conv_rows(ug_ref, wg_ref, bg_ref, r0, rows)
            sg = jax.nn.sigmoid(g)
            dact = da_ref[r0:r0 + rows, :]
            dcs = (dact * g * sg, dact * a * sg * (1.0 + g * (1.0 - sg)))
            for hf, (dc, xs) in enumerate(zip(dcs, (xa, xg))):
                dc_ref[hf, r0:r0 + rows, :] = dc
                db_acc[hf] = db_acc[hf] + _fold_rows(dc)
                for k in range(kk):
                    dw_acc[hf][k] = dw_acc[hf][k] + _fold_rows(dc * xs[kk - 1 - k])
        for hf in range(2):
            dc_ref[hf, seq:seq + SUBLANES, :] = jnp.zeros((SUBLANES, ct), f32)
            db_ref[hf] += _colsum(db_acc[hf])
            for k in range(kk):
                dw_ref[hf, k:k + 1, :] += _colsum(dw_acc[hf][k])
        for r0 in range(0, seq, rows):
            for hf, w_ref in enumerate((wa_ref, wg_ref)):
                dsrc = w_ref[kk - 1:kk, :] * dc_ref[hf, r0:r0 + rows, :]
                for k in range(kk - 1):
                    dsrc = dsrc + w_ref[k:k + 1, :] * dc_ref[hf, pl.ds(r0 + kk - 1 - k, rows), :]
                du_ref[hf, r0:r0 + rows, :] = dsrc.astype(bf16)

    (d_up, dw, db), comm_outs = _call(
        body, "ffn_act_bwd", (dff // ct, nb),
        _ffn_specs(seq, ct, kk, dff // ct) + [pl.BlockSpec((seq, ct), lambda j, bb: (bb, j))],
        [pl.BlockSpec((2, seq, ct), lambda j, bb: (0, bb, j)), pl.BlockSpec((2, kk, ct), lambda j, bb: (0, 0, j)),
         pl.BlockSpec((2, 1, ct), lambda j, bb: (0, 0, j))],
        [jax.ShapeDtypeStruct((2, nb * seq, dff), bf16), jax.ShapeDtypeStruct((2, kk, dff), f32),
         jax.ShapeDtypeStruct((2, 1, dff), f32)],
        (up, up, w, w, b, b, d_act), [pltpu.VMEM((2, seq + SUBLANES, ct), f32)], ("parallel", "arbitrary"), comm)
    return (d_up, dw, db) if comm is None else (d_up, dw, db, comm_outs)


def _fill_keys(k_scr, kn_ref, krd_ref):
    @pl.when(pl.program_id(2) == 0)
    def _():
        k_scr[:, :LANES] = kn_ref[...]
        k_scr[:, LANES:] = krd_ref[...]


def _scores(q_ref, qt_ref, k_scr, qblk, tq):
    klen = (qblk + 1) * tq
    q = (q_ref[...] * qt_ref[...]).astype(bf16)
    s = _dot(q, k_scr[0:klen, :], _NT)
    row = qblk * tq + lax.broadcasted_iota(jnp.int32, s.shape, 0)
    col = lax.broadcasted_iota(jnp.int32, s.shape, 1)
    return q, jnp.where(col <= row, s, NEG_INF)


def _per_q_block(nq, fn):
    qi = pl.program_id(2)
    for qblk in range(nq):
        pl.when(qi == qblk)(functools.partial(fn, qblk))


def _attn_fwd(q_ext, qt, kv, krd, nb, seq, heads, comm=None):
    tq = _tile(seq, 512)
    nq = seq // tq

    def body(q_ref, qt_ref, kn_ref, krd_ref, v_ref, o_ref, lse_ref, k_scr):
        _fill_keys(k_scr, kn_ref, krd_ref)

        def work(qblk):
            klen = (qblk + 1) * tq
            _, s = _scores(q_ref, qt_ref, k_scr, qblk, tq)
            m = jnp.max(s, axis=-1, keepdims=True)
            p = jnp.exp(s - m)
            l = jnp.sum(p, axis=-1, keepdims=True)
            o_ref[...] = (_dot(p, v_ref[0:klen, :], _NN) / l).astype(bf16)
            lse_ref[...] = m + jnp.log(l)

        _per_q_block(nq, work)

    (o, lse), comm_outs = _call(
        body, "attn_fwd", (nb, heads, nq),
        [pl.BlockSpec((tq, 2 * LANES), lambda b, h, i: (b * nq + i, h)),
         pl.BlockSpec((tq, 2 * LANES), lambda b, h, i: (b * nq + i, 0)),
         pl.BlockSpec((seq, LANES), lambda b, h, i: (b, 2 * h)),
         pl.BlockSpec((seq, LANES), lambda b, h, i: (b, 0)),
         pl.BlockSpec((seq, LANES), lambda b, h, i: (b, 2 * h + 1))],
        [pl.BlockSpec((tq, LANES), lambda b, h, i: (b * nq + i, h)),
         pl.BlockSpec((None, tq, 1), lambda b, h, i: (b * heads + h, i, 0))],
        [jax.ShapeDtypeStruct((nb * seq, heads * LANES), bf16), jax.ShapeDtypeStruct((nb * heads, seq, 1), f32)],
        (q_ext, qt, kv, krd, kv), [pltpu.VMEM((seq, 2 * LANES), bf16)], ("parallel", "parallel", "arbitrary"), comm)
    return (o, lse) if comm is None else (o, lse, comm_outs)


def _attn_bwd(q_ext, qt, kv, krd, lse, d_mixed, nb, seq, heads, comm=None):
    tq = _tile(seq, 512)
    nq = seq // tq

    def body(q_ref, qt_ref, kn_ref, krd_ref, v_ref, lse_ref, do_ref, dq_ref, dkv_ref, dkrd_ref, dkv_acc, k_scr):
        h, qi = pl.program_id(1), pl.program_id(2)
        _fill_keys(k_scr, kn_ref, krd_ref)

        @pl.when(qi == 0)
        def _():
            dkv_acc[...] = jnp.zeros_like(dkv_acc)

        @pl.when((qi == 0) & (h == 0))
        def _():
            dkrd_ref[...] = jnp.zeros_like(dkrd_ref)

        def work(qblk):
            klen = (qblk + 1) * tq
            q, s = _scores(q_ref, qt_ref, k_scr, qblk, tq)
            p = jnp.exp(s - lse_ref[...])
            do = do_ref[...]
            dp = _dot(do, v_ref[0:klen, :], _NT)
            ds = (p * (dp - jnp.sum(p * dp, axis=-1, keepdims=True))).astype(bf16)
            dq_ref[...] = (_dot(ds, k_scr[0:klen, :], _NN) * qt_ref[...]).astype(bf16)
            dk = _dot(ds, q, _TN)
            dkv_acc[0:klen, :LANES] += dk[:, :LANES]
            dkv_acc[0:klen, LANES:] += _dot(p, do, _TN)
            dkrd_ref[0:klen, :] += dk[:, LANES:]

        _per_q_block(nq, work)

        @pl.when(qi == nq - 1)
        def _():
            dkv_ref[...] = dkv_acc[...].astype(bf16)

    (dq, dkv, dkrd), comm_outs = _call(
        body, "attn_bwd", (nb, heads, nq),
        [pl.BlockSpec((tq, 2 * LANES), lambda b, h, i: (b * nq + i, h)),
         pl.BlockSpec((tq, 2 * LANES), lambda b, h, i: (b * nq + i, 0)),
         pl.BlockSpec((seq, LANES), lambda b, h, i: (b, 2 * h)),
         pl.BlockSpec((seq, LANES), lambda b, h, i: (b, 0)),
         pl.BlockSpec((seq, LANES), lambda b, h, i: (b, 2 * h + 1)),
         pl.BlockSpec((None, tq, 1), lambda b, h, i: (b * heads + h, i, 0)),
         pl.BlockSpec((tq, LANES), lambda b, h, i: (b * nq + i, h))],
        [pl.BlockSpec((tq, 2 * LANES), lambda b, h, i: (b * nq + i, h)),
         pl.BlockSpec((seq, 2 * LANES), lambda b, h, i: (b, h)),
         pl.BlockSpec((seq, LANES), lambda b, h, i: (b, 0))],
        [jax.ShapeDtypeStruct((nb * seq, heads * 2 * LANES), bf16),
         jax.ShapeDtypeStruct((nb * seq, heads * 2 * LANES), bf16),
         jax.ShapeDtypeStruct((nb * seq, LANES), f32)],
        (q_ext, qt, kv, krd, kv, lse, d_mixed), [pltpu.VMEM((seq, 2 * LANES), f32), pltpu.VMEM((seq, 2 * LANES), bf16)],
        ("parallel", "arbitrary", "arbitrary"), comm)
    return (dq, dkv, dkrd) if comm is None else (dq, dkv, dkrd, comm_outs)


def _mesh_pos():
    x, y, c = lax.axis_index("x"), lax.axis_index("y"), lax.axis_index("c")
    return x, y, c, [(1 - x, y), (x, 1 - y), (1 - x, 1 - y)]


def _all_gather(shards, name):
    n = len(shards)

    def body(*refs):
        x_refs, out_refs, (send_sems, recv_sems, local_sems) = refs[:n], refs[n:2 * n], refs[2 * n:]
        x, y, c, chips = _mesh_pos()
        me, sibling = (x, y, c), (x, y, 1 - c)

        def copy(t, k, block, to, from_shard=False):
            px, py, pc = block
            rows = out_refs[t].at[4 * px + 2 * py + pc]
            return pltpu.make_async_remote_copy(
                src_ref=x_refs[t] if from_shard else rows, dst_ref=rows,
                send_sem=send_sems.at[t, k], recv_sem=recv_sems.at[t, k], device_id=to, device_id_type=MESH)

        mine = [pltpu.make_async_copy(x_refs[t], out_refs[t].at[4 * x + 2 * y + c], local_sems.at[t]) for t in range(n)]
        first = [[copy(t, 0, me, sibling, True)] + [copy(t, 1 + j, me, (*chip, c), True) for j, chip in enumerate(chips)]
                 for t in range(n)]
        passed = [[copy(t, 4 + j, (*chip, c), sibling) for j, chip in enumerate(chips)] for t in range(n)]
        for t in range(n):
            mine[t].start()
            for cp in first[t]:
                cp.start()
        for j, chip in enumerate(chips):
            for t in range(n):
                copy(t, 1 + j, (*chip, c), me).wait_recv()
                passed[t][j].start()
        for t in range(n):
            copy(t, 0, sibling, me).wait_recv()
            for j, chip in enumerate(chips):
                copy(t, 4 + j, (*chip, 1 - c), me).wait_recv()
        for t in range(n):
            for cp in first[t] + passed[t]:
                cp.wait_send()
            mine[t].wait()

    return pl.pallas_call(
        body, name=name, out_shape=[jax.ShapeDtypeStruct((N_DEV,) + s.shape, s.dtype) for s in shards],
        in_specs=[_ANY] * n, out_specs=[_ANY] * n,
        scratch_shapes=[pltpu.SemaphoreType.DMA((n, 7)), pltpu.SemaphoreType.DMA((n, 7)), pltpu.SemaphoreType.DMA((n,))],
    )(*shards)


def _gather_own(shards):
    n = len(shards)

    def remote(x_refs, out_refs, sems, arriving):
        send_sems, recv_sems, _ = sems
        x, y, c, chips = _mesh_pos()
        peers = [(x, y, 1 - c)] + [(*chip, c) for chip in chips]
        return [pltpu.make_async_remote_copy(
            src_ref=x_refs[t], dst_ref=out_refs[t].at[4 * px + 2 * py + pc if arriving else 4 * x + 2 * y + c],
            send_sem=send_sems.at[t, k], recv_sem=recv_sems.at[t, k], device_id=(px, py, pc), device_id_type=MESH)
            for t in range(n) for k, (px, py, pc) in enumerate(peers)]

    def local(x_refs, out_refs, sems):
        x, y, c, _ = _mesh_pos()
        return [pltpu.make_async_copy(x_refs[t], out_refs[t].at[4 * x + 2 * y + c], sems[2].at[t]) for t in range(n)]

    def start(x_refs, out_refs, sems):
        for cp in local(x_refs, out_refs, sems) + remote(x_refs, out_refs, sems, False):
            cp.start()

    def finish(x_refs, out_refs, sems):
        for cp in remote(x_refs, out_refs, sems, True):
            cp.wait_recv()
        for cp in remote(x_refs, out_refs, sems, False):
            cp.wait_send()
        for cp in local(x_refs, out_refs, sems):
            cp.wait()

    return _Comm(shards, [jax.ShapeDtypeStruct((N_DEV,) + s.shape, s.dtype) for s in shards],
                 [pltpu.SemaphoreType.DMA((n, 4)), pltpu.SemaphoreType.DMA((n, 4)), pltpu.SemaphoreType.DMA((n,))],
                 start, finish)


def _gather_pass(gathered):
    n = len(gathered)

    def copies(in_refs, out_refs, sems, arriving):
        send_sems, recv_sems = sems
        x, y, c, chips = _mesh_pos()
        return [pltpu.make_async_remote_copy(
            src_ref=in_refs[t].at[4 * px + 2 * py + c],
            dst_ref=out_refs[t].at[4 * px + 2 * py + (1 - c if arriving else c)],
            send_sem=send_sems.at[t, j], recv_sem=recv_sems.at[t, j], device_id=(x, y, 1 - c), device_id_type=MESH)
            for t in range(n) for j, (px, py) in enumerate(chips)]

    def start(in_refs, out_refs, sems):
        for cp in copies(in_refs, out_refs, sems, False):
            cp.start()

    def finish(in_refs, out_refs, sems):
        for cp in copies(in_refs, out_refs, sems, True):
            cp.wait_recv()
        for cp in copies(in_refs, out_refs, sems, False):
            cp.wait_send()

    return _Comm(gathered, [jax.ShapeDtypeStruct(g.shape, g.dtype) for g in gathered],
                 [pltpu.SemaphoreType.DMA((n, 3)), pltpu.SemaphoreType.DMA((n, 3))], start, finish,
                 aliases={t: t for t in range(n)})


def _sibling_swap(slots):
    n = len(slots)

    def start(g_refs, out_refs, sems):
        send_sems, recv_sems = sems
        x, y, c, _ = _mesh_pos()
        for t in range(n):
            for k in range(4):
                pltpu.make_async_remote_copy(
                    src_ref=g_refs[t].at[2 * k + (1 - c)], dst_ref=out_refs[t].at[k], send_sem=send_sems.at[t],
                    recv_sem=recv_sems.at[t], device_id=(x, y, 1 - c), device_id_type=MESH).start()

    def finish(g_refs, out_refs, sems):
        send_sems, recv_sems = sems
        x, y, c, _ = _mesh_pos()
        for t in range(n):
            pltpu.make_async_remote_copy(
                src_ref=g_refs[t].at[pl.ds(0, 4)], dst_ref=out_refs[t], send_sem=send_sems.at[t],
                recv_sem=recv_sems.at[t], device_id=(x, y, 1 - c), device_id_type=MESH).wait()

    return _Comm(slots, [jax.ShapeDtypeStruct((4,) + s.shape[1:], s.dtype) for s in slots],
                 [pltpu.SemaphoreType.DMA((n,)), pltpu.SemaphoreType.DMA((n,))], start, finish)


def _chip_swap(p4s, peers=(0, 1, 2), into=None):
    n = len(p4s)

    def copies(refs, out_refs, sems):
        send_sems, recv_sems = sems
        x, y, c, chips = _mesh_pos()
        return [pltpu.make_async_remote_copy(
            src_ref=refs[t].at[2 * chips[j][0] + chips[j][1]], dst_ref=out_refs[t].at[j], send_sem=send_sems.at[t, j],
            recv_sem=recv_sems.at[t, j], device_id=(*chips[j], c), device_id_type=MESH)
            for t in range(n) for j in peers]

    def start(refs, out_refs, sems):
        for cp in copies(refs, out_refs, sems):
            cp.start()

    def finish(refs, out_refs, sems):
        for cp in copies(refs, out_refs, sems):
            cp.wait()

    out_shapes = [jax.ShapeDtypeStruct((3,) + p.shape[1:], p.dtype) for p in p4s]
    sems = [pltpu.SemaphoreType.DMA((n, 3)), pltpu.SemaphoreType.DMA((n, 3))]
    if into is None:
        return _Comm(p4s, out_shapes, sems, start, finish)
    return _Comm(list(p4s) + list(into), out_shapes, sems, start, finish, aliases={n + t: t for t in range(n)})


def _comm_join(a, b):
    ai, ao, asem = len(a.inputs), len(a.out_shapes), len(a.sems)

    def start(ins, outs, sems):
        a.start(ins[:ai], outs[:ao], sems[:asem])
        b.start(ins[ai:], outs[ao:], sems[asem:])

    def finish(ins, outs, sems):
        a.finish(ins[:ai], outs[:ao], sems[:asem])
        b.finish(ins[ai:], outs[ao:], sems[asem:])

    aliases = dict(a.aliases)
    aliases.update({ai + i: ao + o for i, o in b.aliases.items()})
    return _Comm(a.inputs + b.inputs, a.out_shapes + b.out_shapes, a.sems + b.sems, start, finish, aliases)


def _row_tile(rows, cols, max_bytes=1024 * 1024):
    best = None
    for tr in range(16, rows + 1, 16):
        if rows % tr == 0 and tr * cols * 4 <= max_bytes:
            best = tr
    return best or rows


def _pair_add(slots, theirs, core, name):
    _, rows, cols = slots.shape
    tr = _row_tile(rows, cols, 4 * 1024 * 1024)

    def body(c_ref, a_ref, b_ref, o_ref):
        o_ref[...] = (a_ref[...].astype(f32) + b_ref[...].astype(f32)).astype(bf16)

    return pl.pallas_call(
        body, name=name,
        grid_spec=pltpu.PrefetchScalarGridSpec(
            num_scalar_prefetch=1, grid=(4, rows // tr),
            in_specs=[pl.BlockSpec((None, tr, cols), lambda k, i, c_ref: (2 * k + c_ref[0], i, 0)),
                      pl.BlockSpec((None, tr, cols), lambda k, i, c_ref: (k, i, 0))],
            out_specs=pl.BlockSpec((None, tr, cols), lambda k, i, c_ref: (k, i, 0))),
        out_shape=jax.ShapeDtypeStruct((4, rows, cols), bf16), compiler_params=_cparams(("parallel", "parallel")),
    )(core, slots, theirs)


def _adam_update(g, w, m, v):
    c1 = 1.0 / (1.0 - ADAM_B1 ** ADAM_STEP)
    c2 = 1.0 / (1.0 - ADAM_B2 ** ADAM_STEP)
    nm = ADAM_B1 * m + (1.0 - ADAM_B1) * g
    nv = ADAM_B2 * v + (1.0 - ADAM_B2) * jnp.square(g)
    delta = -ADAM_LR * ((nm * c1) / (jnp.sqrt(nv * c2) + ADAM_EPS) + ADAM_WD * w)
    return delta, nm, nv


def _adamw_layers(p4s, chips, chip_idx, w, m, v, name, comm=None):
    depth = len(p4s)
    _, rows_l, cols = p4s[0].shape
    tr = _row_tile(rows_l, cols)
    nr = rows_l // tr

    def body(idx_ref, *refs):
        p_refs, c_refs = refs[:depth], refs[depth:2 * depth]
        w_ref, m_ref, v_ref, g_ref, d_ref, nm_ref, nv_ref = refs[2 * depth:]
        layer = pl.program_id(0)
        for ll in range(depth):
            @pl.when(layer == ll)
            def _(ll=ll):
                g = p_refs[ll][...].astype(f32)
                for j in range(3):
                    g = g + c_refs[ll][j].astype(f32)
                delta, nm, nv = _adam_update(g, w_ref[...], m_ref[...], v_ref[...])
                g_ref[...] = g
                d_ref[...] = delta
                nm_ref[...] = nm
                nv_ref[...] = nv

    def of_layer(ll):
        return lambda l, i: jnp.where(l == ll, i, 0)

    p_specs = [pl.BlockSpec((None, tr, cols), lambda l, i, idx_ref, f=of_layer(ll): (idx_ref[0], f(l, i), 0))
               for ll in range(depth)]
    c_specs = [pl.BlockSpec((3, tr, cols), lambda l, i, idx_ref, f=of_layer(ll): (0, f(l, i), 0)) for ll in range(depth)]
    spec = pl.BlockSpec((tr, cols), lambda l, i, idx_ref: (l * nr + i, 0))
    out = jax.ShapeDtypeStruct(w.shape, f32)
    outs, comm_outs = _call(body, name, (depth, nr), p_specs + c_specs + [spec, spec, spec], [spec, spec, spec, spec],
                            [out, out, out, out], (*p4s, *chips, w, m, v), (), ("parallel", "parallel"), comm,
                            prefetch=[chip_idx])
    return outs if comm is None else (outs, comm_outs)


def _adamw(parts, w, m, v, name):
    rows, cols = w.shape
    tr = _row_tile(rows, cols, 512 * 1024)
    nparts = len(parts)

    def body(*refs):
        part_refs, (w_ref, m_ref, v_ref, g_ref, d_ref, nm_ref, nv_ref) = refs[:nparts], refs[nparts:]
        g = None
        for pr in part_refs:
            for s in range(pr.shape[0]):
                term = pr[s].astype(f32)
                g = term if g is None else g + term
        delta, nm, nv = _adam_update(g, w_ref[...], m_ref[...], v_ref[...])
        g_ref[...] = g
        d_ref[...] = delta
        nm_ref[...] = nm
        nv_ref[...] = nv

    spec = pl.BlockSpec((tr, cols), lambda i: (i, 0))
    part_specs = [pl.BlockSpec((p.shape[0], tr, cols), lambda i: (0, i, 0)) for p in parts]
    out = jax.ShapeDtypeStruct((rows, cols), f32)
    return pl.pallas_call(
        body, name=name, grid=(rows // tr,), in_specs=part_specs + [spec, spec, spec],
        out_specs=[spec, spec, spec, spec], out_shape=[out, out, out, out],
        compiler_params=_cparams(("parallel",)),
    )(*parts, w, m, v)


def _gathered_to_full(gathered, axis):
    s = gathered.shape[1:]
    full = jnp.moveaxis(gathered, 0, axis)
    return full.reshape(s[:axis] + (N_DEV * s[axis],) + s[axis + 1:])


def _interleave_halves(w, ct):
    n = w.shape[-1] // 2
    t = w.reshape(w.shape[:-1] + (2, n // ct, ct))
    return jnp.swapaxes(t, -3, -2).reshape(w.shape)


def _deinterleave_halves(w, ct):
    n = w.shape[-1] // 2
    t = w.reshape(w.shape[:-1] + (n // ct, 2, ct))
    return jnp.swapaxes(t, -3, -2).reshape(w.shape)


def _rot_cols(w):
    half = QK_ROPE_DIM // 2
    return jnp.concatenate([-w[..., half:], w[..., :half]], axis=-1)


def _rot_cols_t(dw):
    half = QK_ROPE_DIM // 2
    return jnp.concatenate([dw[..., half:], -dw[..., :half]], axis=-1)


def kernel(x, positions, ln_in_g, ln_in_b, w_in, q_norm_g, w_uq, kv_norm_g, w_ukv, conv_w, conv_b, conv_ln_g, conv_ln_b, w_pool, pool_scale, w_out, ln1_g, ln1_b, w_up, ffn_conv_w, ffn_conv_b, w_down, ln2_g, ln2_b, loss_target, m_ln_in_g, m_ln_in_b, m_w_in, m_q_norm_g, m_w_uq, m_kv_norm_g, m_w_ukv, m_conv_w, m_conv_b, m_conv_ln_g, m_conv_ln_b, m_w_pool, m_pool_scale, m_w_out, m_ln1_g, m_ln1_b, m_w_up, m_ffn_conv_w, m_ffn_conv_b, m_w_down, m_ln2_g, m_ln2_b, v_ln_in_g, v_ln_in_b, v_w_in, v_q_norm_g, v_w_uq, v_kv_norm_g, v_w_ukv, v_conv_w, v_conv_b, v_conv_ln_g, v_conv_ln_b, v_w_pool, v_pool_scale, v_w_out, v_ln1_g, v_ln1_b, v_w_up, v_ffn_conv_w, v_ffn_conv_b, v_w_down, v_ln2_g, v_ln2_b):
    weights = dict(ln_in_g=ln_in_g, ln_in_b=ln_in_b, w_in=w_in, q_norm_g=q_norm_g, w_uq=w_uq, kv_norm_g=kv_norm_g,
                   w_ukv=w_ukv, conv_w=conv_w, conv_b=conv_b, conv_ln_g=conv_ln_g, conv_ln_b=conv_ln_b, w_pool=w_pool,
                   pool_scale=pool_scale, w_out=w_out, ln1_g=ln1_g, ln1_b=ln1_b, w_up=w_up, ffn_conv_w=ffn_conv_w,
                   ffn_conv_b=ffn_conv_b, w_down=w_down, ln2_g=ln2_g, ln2_b=ln2_b)
    mom1 = dict(ln_in_g=m_ln_in_g, ln_in_b=m_ln_in_b, w_in=m_w_in, q_norm_g=m_q_norm_g, w_uq=m_w_uq,
                kv_norm_g=m_kv_norm_g, w_ukv=m_w_ukv, conv_w=m_conv_w, conv_b=m_conv_b, conv_ln_g=m_conv_ln_g,
                conv_ln_b=m_conv_ln_b, w_pool=m_w_pool, pool_scale=m_pool_scale, w_out=m_w_out, ln1_g=m_ln1_g,
                ln1_b=m_ln1_b, w_up=m_w_up, ffn_conv_w=m_ffn_conv_w, ffn_conv_b=m_ffn_conv_b, w_down=m_w_down,
                ln2_g=m_ln2_g, ln2_b=m_ln2_b)
    mom2 = dict(ln_in_g=v_ln_in_g, ln_in_b=v_ln_in_b, w_in=v_w_in, q_norm_g=v_q_norm_g, w_uq=v_w_uq,
                kv_norm_g=v_kv_norm_g, w_ukv=v_w_ukv, conv_w=v_conv_w, conv_b=v_conv_b, conv_ln_g=v_conv_ln_g,
                conv_ln_b=v_conv_ln_b, w_pool=v_w_pool, pool_scale=v_pool_scale, w_out=v_w_out, ln1_g=v_ln1_g,
                ln1_b=v_ln1_b, w_up=v_w_up, ffn_conv_w=v_ffn_conv_w, ffn_conv_b=v_ffn_conv_b, w_down=v_w_down,
                ln2_g=v_ln2_g, ln2_b=v_ln2_b)
    names = list(weights)

    nb, seq, d = x.shape
    t = nb * seq
    depth = w_in.shape[0]
    ql, kvl, cw, pw = q_norm_g.shape[1], kv_norm_g.shape[1], conv_b.shape[1], pool_scale.shape[1]
    pg = w_pool.shape[-1]
    heads = w_uq.shape[2]
    dff = w_down.shape[1] * N_DEV
    alpha = (2.0 * depth) ** 0.25
    scale = float(QK_NOPE_DIM + QK_ROPE_DIM) ** -0.5
    lay = dict(ql=ql, kvl=kvl, pw=pw, pg=pg, off_q=2 * cw, off_pool=2 * cw + ql, off_kv=2 * cw + ql + pw,
               off_kr=2 * cw + ql + pw + kvl)
    o1, o2, o3, o4 = ql, ql + kvl, ql + kvl + QK_ROPE_DIM, ql + kvl + QK_ROPE_DIM + 2 * cw
    my_x, my_y, my_c = lax.axis_index("x"), lax.axis_index("y"), lax.axis_index("c")
    my_dev = 4 * my_x + 2 * my_y + my_c

    big = ("w_in", "w_uq", "w_ukv", "w_out", "w_up", "w_down")
    g_conv, g_ffn = _all_gather([conv_w, ffn_conv_w], "ag_conv_taps")
    conv_w_full, ffn_w_full = _gathered_to_full(g_conv, 2), _gathered_to_full(g_ffn, 2)
    w_pool_2d = w_pool.reshape(depth, pw, pg)

    rest = ("w_in", "w_uq", "w_ukv", "w_out")

    def bf16_shards(l, which):
        return [weights[n][l].astype(bf16) for n in which]

    def small_weights(gathered):
        g_in, g_uq, g_ukv, g_out = (gathered[n] for n in rest)
        wi = _gathered_to_full(g_in, 1)
        kr_cols = wi[:, o2:o3]
        w_in_pad = jnp.concatenate([_interleave_halves(wi[:, o3:o4], SEQ_CT), wi[:, :o1], wi[:, o4:], wi[:, o1:o2],
                                    kr_cols, _rot_cols(kr_cols)], axis=-1)
        wq = g_uq.reshape(ql, heads, QK_NOPE_DIM + QK_ROPE_DIM)
        w_uq_ext = jnp.concatenate([wq, _rot_cols(wq[..., QK_NOPE_DIM:])], axis=-1).reshape(ql, heads * 2 * LANES)
        return w_in_pad, w_uq_ext, g_ukv.reshape(kvl, heads * 2 * LANES), g_out.reshape(-1, d)

    half = QK_ROPE_DIM // 2
    inv = 1.0 / (ROPE_THETA ** (jnp.arange(0, QK_ROPE_DIM, 2, dtype=f32) / QK_ROPE_DIM))
    inv_lanes = jnp.tile(inv, LANES // half).reshape(1, LANES)
    cs, qt = _rope_tables(positions.reshape(t, 1), inv_lanes, scale)

    x2 = x.reshape(t, d)
    xs, xs_bf = _ln_fwd(x2, None, ln_in_g.reshape(1, d), ln_in_b.reshape(1, d), 1.0, "ln_in_fwd")
    saved = []
    in_pad = o4 + pw + QK_ROPE_DIM
    gathered = dict(zip(rest, _all_gather(bf16_shards(0, rest), "ag_weights")))
    for l in range(depth):
        nxt = l + 1 < depth
        gq, gkv = q_norm_g[l].reshape(1, ql), kv_norm_g[l].reshape(1, kvl)
        w_in_l, w_uq_l, w_ukv_l, w_out_l = small_weights(gathered)
        h = _matmul(xs_bf, w_in_l, "nn", f32, "mm_in", tm=512, tn=in_pad)
        qn, kvn, krd = _prep_fwd(h, cs, gq, gkv, lay)
        q_ext = _matmul(qn, w_uq_l, "nn", f32, "mm_uq", tn=2048)
        kv = _matmul(kvn, w_ukv_l, "nn", bf16, "mm_ukv", tn=2048)
        riders = ("w_down", "w_up") if l == 0 else ("w_down",)
        y_mla, lse, g_half = _attn_fwd(q_ext, qt, kv, krd, nb, seq, heads, comm=_gather_own(bf16_shards(l, riders)))
        hconv = _conv_fwd(h, conv_w_full[l], conv_b[l].reshape(1, cw), nb, seq, cw)
        y_conv = _convln_fwd(hconv, conv_ln_g[l].reshape(1, cw), conv_ln_b[l].reshape(1, cw))
        y_pool = _pool_fwd(h, w_pool_2d[l], pool_scale[l].reshape(1, pw), lay, nb, seq)
        mixed = jnp.concatenate([y_mla, y_conv, y_pool], axis=-1)
        y1, g_full = _matmul(mixed, w_out_l, "nn", f32, "mm_out", comm=_gather_pass(g_half))
        gathered.update(zip(riders, g_full))
        wl = dict(w_in=w_in_l, w_uq=w_uq_l, w_ukv=w_ukv_l, w_out=w_out_l, w_up=gathered["w_up"],
                  w_down=gathered["w_down"].reshape(dff, d))
        gathered = {}
        x1, x1_bf = _ln_fwd(xs, y1, ln1_g[l].reshape(1, d), ln1_b[l].reshape(1, d), alpha, "ln1_fwd")
        if nxt:
            up, (g_up,) = _matmul_up(x1_bf, wl["w_up"], "mm_up", comm=_gather_own(bf16_shards(l + 1, ("w_up",))))
            act, g_rest = _ffn_act_fwd(up, ffn_w_full[l], ffn_conv_b[l].reshape(1, 2 * dff), nb, seq, dff,
                                       comm=_gather_own(bf16_shards(l + 1, rest[1:])))
            y2, (gathered["w_up"], g_in) = _matmul(
                act, wl["w_down"], "nn", f32, "mm_down", tk=dff // 2,
                comm=_comm_join(_gather_pass([g_up]), _gather_own(bf16_shards(l + 1, rest[:1]))))
            xn, xn_bf, g_rest = _ln_fwd(x1, y2, ln2_g[l].reshape(1, d), ln2_b[l].reshape(1, d), alpha, "ln2_fwd",
                                        comm=_gather_pass([g_in] + g_rest))
            gathered.update(zip(rest, g_rest))
        else:
            up = _matmul_up(x1_bf, wl["w_up"], "mm_up")
            act = _ffn_act_fwd(up, ffn_w_full[l], ffn_conv_b[l].reshape(1, 2 * dff), nb, seq, dff)
            y2 = _matmul(act, wl["w_down"], "nn", f32, "mm_down", tk=dff // 2)
            xn, xn_bf = _ln_fwd(x1, y2, ln2_g[l].reshape(1, d), ln2_b[l].reshape(1, d), alpha, "ln2_fwd")
        saved.append(dict(xs=xs, xs_bf=xs_bf, h=h, qn=qn, kvn=kvn, krd=krd, q_ext=q_ext, kv=kv, lse=lse, hconv=hconv,
                          mixed=mixed, y1=y1, x1=x1, x1_bf=x1_bf, up=up, act=act, y2=y2, wl=wl))
        xs, xs_bf = xn, xn_bf

    d_stream, loss_row = _loss_call(xs, loss_target.reshape(t, d))
    loss = lax.psum(loss_row[0, 0], MESH_AXES)

    gw = {n: [None] * depth for n in names if n not in ("ln_in_g", "ln_in_b")}
    small = [n for n in gw if n not in big]
    rs_own = {n: [None] * depth for n in big}
    rs_chips = {n: [None] * depth for n in big}
    core_idx = jnp.reshape(my_c, (1,)).astype(jnp.int32)
    chip_idx = jnp.reshape(2 * my_x + my_y, (1,)).astype(jnp.int32)

    def pair_add(l, which, slots, theirs):
        p4s = [_pair_add(slots[n], th, core_idx, "rs_add_" + n) for n, th in zip(which, theirs)]
        for n, p4 in zip(which, p4s):
            rs_own[n][l] = p4
        return p4s

    d_res, d_mm = None, d_stream
    pending = None
    for l in reversed(range(depth)):
        sv = saved[l]
        wl = sv["wl"]
        slots = {}
        gq, gkv = q_norm_g[l].reshape(1, ql), kv_norm_g[l].reshape(1, kvl)
        dz2, dz2_bf, gw["ln2_g"][l], gw["ln2_b"][l] = _ln_bwd(
            d_res, d_mm, sv["x1"], sv["y2"], ln2_g[l].reshape(1, d), ln2_b[l].reshape(1, d), alpha, "ln2_bwd")
        if pending is None:
            dw_down = _matmul(sv["act"], dz2_bf, "tn", bf16, "mm_down_dw", tm=dff // 4)
        else:
            dw_down, theirs = _matmul(sv["act"], dz2_bf, "tn", bf16, "mm_down_dw", tm=dff // 4,
                                      comm=_sibling_swap([pending["slots"][n] for n in rest]))
            rest_p4s = pair_add(pending["layer"], rest, pending["slots"], theirs)
        slots["w_down"] = dw_down.reshape(N_DEV, -1, d)
        d_act, theirs = _matmul(dz2_bf, wl["w_down"], "nt", f32, "mm_down_dx", tn=dff // 4,
                                comm=_sibling_swap([slots["w_down"]]))
        down_p4s = pair_add(l, ("w_down",), slots, theirs)
        ffn_args = (sv["up"], ffn_w_full[l], ffn_conv_b[l].reshape(1, 2 * dff), d_act, nb, seq, dff)
        if pending is None:
            d_up, dffw, dffb = _ffn_act_bwd(*ffn_args)
        else:
            d_up, dffw, dffb, (rs_chips["w_up"][pending["layer"]],) = _ffn_act_bwd(
                *ffn_args, comm=_chip_swap(pending["up_p4s"], peers=(2,), into=pending["up_partial"]))
        gw["ffn_conv_w"][l] = jnp.concatenate([dffw[0], dffw[1]], axis=-1)
        gw["ffn_conv_b"][l] = dffb.reshape(2 * dff)
        d_x1, (rs_chips["w_down"][l],) = _matmul_up_dx(d_up, wl["w_up"], "mm_up_dx", comm=_chip_swap(down_p4s))
        if pending is None:
            slots["w_up"] = _matmul_up_dw(sv["x1_bf"], d_up, N_DEV, "mm_up_dw")
        else:
            slots["w_up"], from_chips = _matmul_up_dw(sv["x1_bf"], d_up, N_DEV, "mm_up_dw", comm=_chip_swap(rest_p4s))
            for n, fc in zip(rest, from_chips):
                rs_chips[n][pending["layer"]] = fc
        dz1, dz1_bf, gw["ln1_g"][l], gw["ln1_b"][l] = _ln_bwd(
            dz2, d_x1, sv["xs"], sv["y1"], ln1_g[l].reshape(1, d), ln1_b[l].reshape(1, d), alpha, "ln1_bwd")
        d_mixed, theirs = _matmul(dz1_bf, wl["w_out"], "nt", f32, "mm_out_dx", comm=_sibling_swap([slots["w_up"]]))
        up_p4s = pair_add(l, ("w_up",), slots, theirs)
        slots["w_out"] = _matmul(sv["mixed"], dz1_bf, "tn", bf16, "mm_out_dw").reshape(N_DEV, -1, d)
        d_upool, dwp, dps = _pool_bwd(sv["h"], w_pool_2d[l], pool_scale[l].reshape(1, pw), d_mixed,
                                      (heads * LANES + cw) // pw, lay, nb, seq)
        gw["w_pool"][l] = dwp.reshape(w_pool.shape[1:])
        gw["pool_scale"][l] = dps.reshape(pw)
        d_hconv, dclg, dclb = _convln_bwd(sv["hconv"], conv_ln_g[l].reshape(1, cw), conv_ln_b[l].reshape(1, cw), d_mixed,
                                          heads * LANES // cw)
        gw["conv_ln_g"][l], gw["conv_ln_b"][l] = dclg.reshape(cw), dclb.reshape(cw)
        d_conv, gw["conv_w"][l], dcb = _conv_bwd(sv["h"], conv_w_full[l], d_hconv, nb, seq, cw)
        gw["conv_b"][l] = dcb.reshape(cw)
        dq_ext, dkv, dkrd, up_partial = _attn_bwd(sv["q_ext"], qt, sv["kv"], sv["krd"], sv["lse"], d_mixed, nb, seq, heads,
                                                  comm=_chip_swap(up_p4s, peers=(0, 1, 2) if l == 0 else (0, 1)))
        if l == 0:
            rs_chips["w_up"][l] = up_partial[0]
        d_qn = _matmul(dq_ext, wl["w_uq"], "nt", f32, "mm_uq_dx")
        dwq = _matmul(sv["qn"], dq_ext, "tn", f32, "mm_uq_dw", tn=2048, tk=1024).reshape(ql, heads, 2 * LANES)
        dwq_rope = dwq[..., QK_NOPE_DIM:QK_NOPE_DIM + QK_ROPE_DIM] + _rot_cols_t(dwq[..., QK_NOPE_DIM + QK_ROPE_DIM:])
        slots["w_uq"] = jnp.concatenate([dwq[..., :QK_NOPE_DIM], dwq_rope], axis=-1).astype(bf16).reshape(
            N_DEV, -1, QK_NOPE_DIM + QK_ROPE_DIM)
        d_kvn = _matmul(dkv, wl["w_ukv"], "nt", f32, "mm_ukv_dx")
        slots["w_ukv"] = _matmul(sv["kvn"], dkv, "tn", bf16, "mm_ukv_dw", tn=2048, tk=1024).reshape(N_DEV, -1, 2 * LANES)
        d_cq, d_ckv, d_kr, dgq, dgkv = _prep_bwd(sv["h"], cs, gq, gkv, d_qn, d_kvn, dkrd, lay)
        gw["q_norm_g"][l], gw["kv_norm_g"][l] = dgq.reshape(ql), dgkv.reshape(kvl)
        d_h = jnp.concatenate([d_conv, d_cq, d_upool, d_ckv, d_kr], axis=-1)
        if l > 0:
            d_xs = _matmul(d_h, wl["w_in"], "nt", f32, "mm_in_dx", tm=512, tk=in_pad)
            dwi = _matmul(sv["xs_bf"], d_h, "tn", f32, "mm_in_dw", tn=in_pad, tk=1024)
        else:
            flat_small = jnp.concatenate([jnp.stack(gw[n]).astype(f32).reshape(-1) for n in small]).reshape(-1, LANES)
            d_xs, small_half = _matmul(d_h, wl["w_in"], "nt", f32, "mm_in_dx", tm=512, tk=in_pad,
                                       comm=_gather_own([flat_small]))
            dwi, (gathered_small,) = _matmul(sv["xs_bf"], d_h, "tn", f32, "mm_in_dw", tn=in_pad, tk=1024,
                                             comm=_gather_pass(small_half))
        dkr_cols = dwi[:, lay["off_kr"]:lay["off_kr"] + QK_ROPE_DIM] + _rot_cols_t(dwi[:, lay["off_kr"] + QK_ROPE_DIM:])
        dwi_nat = jnp.concatenate(
            [dwi[:, lay["off_q"]:lay["off_q"] + ql], dwi[:, lay["off_kv"]:lay["off_kv"] + kvl], dkr_cols,
             _deinterleave_halves(dwi[:, :2 * cw], SEQ_CT), dwi[:, lay["off_pool"]:lay["off_pool"] + pw]],
            axis=-1).astype(bf16)
        slots["w_in"] = jnp.moveaxis(dwi_nat.reshape(d, N_DEV, -1), 1, 0)
        pending = dict(layer=l, slots=slots, up_p4s=up_p4s, up_partial=up_partial)
        d_res, d_mm = dz1, d_xs

    grad_x, _, d_ln_in_g, d_ln_in_b, theirs = _ln_bwd(
        d_res, d_mm, x2, None, ln_in_g.reshape(1, d), ln_in_b.reshape(1, d), alpha, "ln_in_bwd",
        comm=_sibling_swap([pending["slots"][n] for n in rest]))
    grad_x = grad_x.reshape(x.shape)
    rest_p4s = pair_add(pending["layer"], rest, pending["slots"], theirs)

    grads, deltas, new_m, new_v = {}, {}, {}, {}

    def finish(n, parts):
        shp = weights[n].shape
        rows = math.prod(shp[:-1]) if len(shp) > 1 else 1
        as2d = lambda a: a.reshape(rows, shp[-1])
        parts = [p.reshape(p.shape[0], rows, shp[-1]) for p in parts]
        g, dl, nm, nv = _adamw(parts, as2d(weights[n]), as2d(mom1[n]), as2d(mom2[n]), "adamw_" + n)
        grads[n], deltas[n], new_m[n], new_v[n] = (a.reshape(shp) for a in (g, dl, nm, nv))

    def finish_big(n, comm=None):
        shp = weights[n].shape
        as2d = lambda a: a.reshape(-1, shp[-1])
        res = _adamw_layers(rs_own[n], rs_chips[n], chip_idx, as2d(weights[n]), as2d(mom1[n]), as2d(mom2[n]),
                            "adamw_" + n, comm)
        outs, comm_outs = res if comm is not None else (res, None)
        grads[n], deltas[n], new_m[n], new_v[n] = (a.reshape(shp) for a in outs)
        return comm_outs

    ln_in = ("ln_in_g", "ln_in_b")
    flat_ln = jnp.concatenate([d_ln_in_g.reshape(-1), d_ln_in_b.reshape(-1)]).reshape(-1, LANES)
    gathered_ln = _all_gather([flat_ln], "ag_ln_in_grads")[0].reshape(N_DEV, 2, d)
    finish_big("w_up")
    from_chips = finish_big("w_down", _chip_swap(rest_p4s))
    for n, fc in zip(rest, from_chips):
        rs_chips[n][pending["layer"]] = fc
    for n in rest:
        finish_big(n)
    for i, n in enumerate(ln_in):
        finish(n, [gathered_ln[:, i]])
    gathered_small = gathered_small.reshape(N_DEV, -1)
    off = 0
    for n in small:
        shape = (depth,) + gw[n][0].shape
        size = math.prod(shape)
        part = gathered_small[:, off:off + size].reshape((N_DEV,) + shape)
        off += size
        if n in ("conv_w", "ffn_conv_w"):
            width = weights[n].shape[-1]
            part = lax.dynamic_slice_in_dim(part, my_dev * width, width, axis=part.ndim - 1)
        finish(n, [part])

    return (loss, grad_x, *[grads[n] for n in names], *[deltas[n] for n in names], *[new_m[n] for n in names],
            *[new_v[n] for n in names])
```

```python
import functools
import math

import jax
import jax.numpy as jnp
from jax import lax
from jax.experimental import pallas as pl
from jax.experimental.pallas import tpu as pltpu

f32 = jnp.float32
bf16 = jnp.bfloat16

QK_NOPE_DIM = 128
QK_ROPE_DIM = 64
V_HEAD_DIM = 128
CONV_KERNEL = 31
FFN_CONV_KERNEL = 3
POOL_WINDOWS = (2, 4, 8, 16)
ROPE_THETA = 10000.0
LN_EPS = 1e-5
RMS_EPS = 1e-6
ADAM_LR = 0.001
ADAM_B1 = 0.9
ADAM_B2 = 0.999
ADAM_EPS = 1e-08
ADAM_WD = 0.01
ADAM_STEP = 10

N_DEV = 8
MESH_AXES = ("x", "y", "c")
V7X_VMEM_LIMIT_BYTES = 56 * 1024 * 1024
LANES = 128
NEG_INF = -1e30
MESH = pl.DeviceIdType.MESH


def _cparams(sem):
    return pltpu.CompilerParams(dimension_semantics=sem, vmem_limit_bytes=V7X_VMEM_LIMIT_BYTES)


def _tile(dim, pref):
    t = pref
    while t >= LANES:
        if dim % t == 0:
            return t
        t //= 2
    return dim


_ANY = pl.BlockSpec(memory_space=pl.ANY)


class _Comm:
    def __init__(self, inputs, out_shapes, sems, start, finish, aliases=None):
        self.inputs, self.out_shapes, self.sems = list(inputs), list(out_shapes), list(sems)
        self.start, self.finish, self.aliases = start, finish, dict(aliases or {})


def _call(body, name, grid, in_specs, out_specs, out_shape, args, scratch=(), sem=None, comm=None, prefetch=()):
    in_specs, out_specs, out_shape, scratch = list(in_specs), list(out_specs), list(out_shape), list(scratch)
    n_pre, n_in, n_out, n_scr = len(prefetch), len(in_specs), len(out_specs), len(scratch)
    c_in, c_out = (len(comm.inputs), len(comm.out_shapes)) if comm else (0, 0)

    def carrier(*refs):
        refs = list(refs)
        pre, refs = refs[:n_pre], refs[n_pre:]
        ins, refs = refs[:n_in], refs[n_in:]
        c_ins, refs = refs[:c_in], refs[c_in:]
        outs, refs = refs[:n_out], refs[n_out:]
        c_outs, refs = refs[:c_out], refs[c_out:]
        scr, c_sems = refs[:n_scr], refs[n_scr:]
        ids = [pl.program_id(a) for a in range(len(grid))]
        first = functools.reduce(lambda p, q: p & q, [i == 0 for i in ids])
        last = functools.reduce(lambda p, q: p & q, [i == g - 1 for i, g in zip(ids, grid)])
        pl.when(first)(lambda: comm.start(c_ins, c_outs, c_sems))
        body(*pre, *ins, *outs, *scr)
        pl.when(last)(lambda: comm.finish(c_ins, c_outs, c_sems))

    grid_spec = pltpu.PrefetchScalarGridSpec(
        num_scalar_prefetch=n_pre, grid=grid, in_specs=in_specs + [_ANY] * c_in, out_specs=out_specs + [_ANY] * c_out,
        scratch_shapes=scratch + (comm.sems if comm else []))
    if comm is None:
        outs = pl.pallas_call(body, name=name, grid_spec=grid_spec, out_shape=out_shape,
                              compiler_params=_cparams(sem))(*prefetch, *args)
        return list(outs), []
    outs = pl.pallas_call(
        carrier, name=name, grid_spec=grid_spec, out_shape=out_shape + comm.out_shapes,
        input_output_aliases={n_pre + n_in + a: n_out + b for a, b in comm.aliases.items()},
        compiler_params=_cparams(("arbitrary",) * len(grid)),
    )(*prefetch, *args, *comm.inputs)
    return list(outs[:n_out]), list(outs[n_out:])


def _shift_down_raw(x, k):
    if k == 0:
        return x
    row = lax.broadcasted_iota(jnp.int32, x.shape, 0)
    return jnp.where(row >= k, pltpu.roll(x, k, axis=0), 0.0)


def _shift_up_raw(x, k):
    if k == 0:
        return x
    n = x.shape[0]
    row = lax.broadcasted_iota(jnp.int32, x.shape, 0)
    return jnp.where(row < n - k, pltpu.roll(x, n - k, axis=0), 0.0)


@functools.partial(jax.custom_vjp, nondiff_argnums=(1,))
def _shift_down(x, k):
    return _shift_down_raw(x, k)


def _shift_down_fwd(x, k):
    return _shift_down_raw(x, k), None


def _shift_down_bwd(k, _, g):
    return (_shift_up_raw(g, k),)


_shift_down.defvjp(_shift_down_fwd, _shift_down_bwd)


@jax.custom_vjp
def _dup_halves(p):
    return p + pltpu.roll(p, LANES // 2, axis=1)


def _dup_halves_fwd(p):
    return p + pltpu.roll(p, LANES // 2, axis=1), None


def _dup_halves_bwd(_, g):
    return (g + pltpu.roll(g, LANES // 2, axis=1),)


_dup_halves.defvjp(_dup_halves_fwd, _dup_halves_bwd)

_NN = (((1,), (0,)), ((), ()))
_NT = (((1,), (1,)), ((), ()))
_TN = (((0,), (0,)), ((), ()))


def _dot(a, b, dims):
    return lax.dot_general(a.astype(bf16), b.astype(bf16), dims, preferred_element_type=f32)


@jax.custom_vjp
def _mm_bf16(a, b):
    return _dot(a, b, _NN)


def _mm_bf16_fwd(a, b):
    return _dot(a, b, _NN), (a, b)


def _mm_bf16_bwd(res, g):
    a, b = res
    return _dot(g, b, _NT), _dot(a, g, _TN)


_mm_bf16.defvjp(_mm_bf16_fwd, _mm_bf16_bwd)


def _layer_norm(z, g, b):
    mu = jnp.mean(z, axis=-1, keepdims=True)
    var = jnp.mean(jnp.square(z - mu), axis=-1, keepdims=True)
    return (z - mu) * lax.rsqrt(var + LN_EPS) * g + b


def _rms_norm(x, g):
    ms = jnp.mean(jnp.square(x), axis=-1, keepdims=True)
    return x * lax.rsqrt(ms + RMS_EPS) * g


def _colsum(x):
    return jnp.sum(x, axis=0, keepdims=True)


def _matmul_core(a, b, mode, grid, a_spec, b_spec, o_spec, o_shape, tile, out_dtype, name, comm=None):
    nk = grid[2]
    dims = {"nn": _NN, "nt": _NT, "tn": _TN}[mode]
    acc_in_out = out_dtype == f32

    def body(a_ref, b_ref, o_ref, *scratch):
        def prod():
            return _dot(a_ref[...], b_ref[...], dims)

        if nk == 1:
            o_ref[...] = prod().astype(out_dtype)
            return
        acc_ref = o_ref if acc_in_out else scratch[0]
        kk = pl.program_id(2)

        @pl.when(kk == 0)
        def _():
            acc_ref[...] = prod()

        if acc_in_out:
            @pl.when(kk > 0)
            def _():
                acc_ref[...] += prod()
        else:
            @pl.when((kk > 0) & (kk < nk - 1))
            def _():
                acc_ref[...] += prod()

            @pl.when(kk == nk - 1)
            def _():
                o_ref[...] = (acc_ref[...] + prod()).astype(out_dtype)

    scratch = [] if (nk == 1 or acc_in_out) else [pltpu.VMEM(tile, f32)]
    (out,), comm_outs = _call(body, name, grid, [a_spec, b_spec], [o_spec], [jax.ShapeDtypeStruct(o_shape, out_dtype)],
                              (a, b), scratch, ("parallel", "parallel", "arbitrary"), comm)
    return out if comm is None else (out, comm_outs)


def _matmul(a, b, mode, out_dtype, name, tm=1024, tn=1024, tk=2048, comm=None):
    if mode == "nn":
        (m, k), (k2, n) = a.shape, b.shape
    elif mode == "nt":
        (m, k), (n, k2) = a.shape, b.shape
    else:
        (k, m), (k2, n) = a.shape, b.shape
    assert k == k2, (name, a.shape, b.shape)
    tm, tn, tk = _tile(m, tm), _tile(n, tn), _tile(k, tk)
    a_spec = pl.BlockSpec((tk, tm), lambda i, j, kk: (kk, i)) if mode == "tn" else pl.BlockSpec((tm, tk), lambda i, j, kk: (i, kk))
    b_spec = pl.BlockSpec((tn, tk), lambda i, j, kk: (j, kk)) if mode == "nt" else pl.BlockSpec((tk, tn), lambda i, j, kk: (kk, j))
    return _matmul_core(a, b, mode, (m // tm, n // tn, k // tk), a_spec, b_spec,
                        pl.BlockSpec((tm, tn), lambda i, j, kk: (i, j)), (m, n), (tm, tn), out_dtype, name, comm)


def _matmul_up(x_bf, w_slots, name, tm=1024, comm=None):
    m, k = x_bf.shape
    s, _, ns = w_slots.shape
    tm = _tile(m, tm)
    return _matmul_core(x_bf, w_slots, "nn", (m // tm, s, 1), pl.BlockSpec((tm, k), lambda i, j, kk: (i, 0)),
                        pl.BlockSpec((None, k, ns), lambda i, j, kk: (j, 0, 0)),
                        pl.BlockSpec((tm, ns), lambda i, j, kk: (i, j)), (m, s * ns), (tm, ns), f32, name, comm)


def _matmul_up_dx(d3, w_slots, name, tm=1024, tn=1024, comm=None):
    _, m, half = d3.shape
    s, n, ns = w_slots.shape
    per_half = half // ns
    assert 2 * per_half == s, (d3.shape, w_slots.shape)
    tm, tn = _tile(m, tm), _tile(n, tn)
    return _matmul_core(d3, w_slots, "nt", (m // tm, n // tn, s),
                        pl.BlockSpec((None, tm, ns), lambda i, j, kk: (kk // per_half, i, kk % per_half)),
                        pl.BlockSpec((None, tn, ns), lambda i, j, kk: (kk, j, 0)),
                        pl.BlockSpec((tm, tn), lambda i, j, kk: (i, j)), (m, n), (tm, tn), f32, name, comm)


def _matmul_up_dw(x_bf, d3, n_slots, name, tm=1024, tk=2048, comm=None):
    k, m = x_bf.shape
    _, _, half = d3.shape
    ns = 2 * half // n_slots
    per_half = n_slots // 2
    tm, tk = _tile(m, tm), _tile(k, tk)
    return _matmul_core(x_bf, d3, "tn", (m // tm, n_slots, k // tk), pl.BlockSpec((tk, tm), lambda i, j, kk: (kk, i)),
                        pl.BlockSpec((None, tk, ns), lambda i, j, kk: (j // per_half, kk, j % per_half)),
                        pl.BlockSpec((None, tm, ns), lambda i, j, kk: (j, i, 0)), (n_slots, m, ns), (tm, ns), bf16, name,
                        comm)


ROW_TILE = 256


def _rows(width, col_block=0):
    return pl.BlockSpec((ROW_TILE, width), lambda i, cb=col_block: (i, cb))


def _whole(shape):
    return pl.BlockSpec(shape, lambda i: (0,) * len(shape))


def _ln_fwd(x, y, g, b, alpha, name, comm=None):
    t, d = x.shape

    def body(*refs):
        if y is None:
            x_ref, g_ref, b_ref, o_ref, ob_ref = refs
            z = x_ref[...]
        else:
            x_ref, y_ref, g_ref, b_ref, o_ref, ob_ref = refs
            z = alpha * x_ref[...] + y_ref[...]
        out = _layer_norm(z, g_ref[...], b_ref[...])
        o_ref[...] = out
        ob_ref[...] = out.astype(bf16)

    ins = [x] + ([] if y is None else [y]) + [g, b]
    specs = [_rows(d)] + ([] if y is None else [_rows(d)]) + [_whole((1, d)), _whole((1, d))]
    (out, out_bf), comm_outs = _call(
        body, name, (t // ROW_TILE,), specs, [_rows(d), _rows(d)],
        [jax.ShapeDtypeStruct((t, d), f32), jax.ShapeDtypeStruct((t, d), bf16)], ins, (), ("parallel",), comm)
    return (out, out_bf) if comm is None else (out, out_bf, comm_outs)


def _ln_bwd(d_res, d_mm, x, y, g, b, alpha, name, comm=None):
    t, d = x.shape
    has_res, has_mm, has_y = d_res is not None, d_mm is not None, y is not None

    def body(*refs):
        refs = list(refs)
        d_res_ref = refs.pop(0) if has_res else None
        d_mm_ref = refs.pop(0) if has_mm else None
        x_ref = refs.pop(0)
        y_ref = refs.pop(0) if has_y else None
        g_ref, b_ref, dz_ref, dzb_ref, dg_ref, db_ref = refs
        ct = None
        if has_res:
            ct = alpha * d_res_ref[...]
        if has_mm:
            ct = d_mm_ref[...] if ct is None else ct + d_mm_ref[...]
        z = x_ref[...] if not has_y else alpha * x_ref[...] + y_ref[...]
        _, vjp = jax.vjp(_layer_norm, z, g_ref[...], b_ref[...])
        dz, dg, db = vjp(ct)
        dz_ref[...] = dz
        dzb_ref[...] = dz.astype(bf16)

        @pl.when(pl.program_id(0) == 0)
        def _():
            dg_ref[...] = jnp.zeros_like(dg_ref)
            db_ref[...] = jnp.zeros_like(db_ref)

        dg_ref[...] += dg
        db_ref[...] += db

    ins = [a for a in (d_res, d_mm, x, y) if a is not None] + [g, b]
    specs = [_rows(d) for a in (d_res, d_mm, x, y) if a is not None] + [_whole((1, d)), _whole((1, d))]
    outs, comm_outs = _call(
        body, name, (t // ROW_TILE,), specs, [_rows(d), _rows(d), _whole((1, d)), _whole((1, d))],
        [jax.ShapeDtypeStruct((t, d), f32), jax.ShapeDtypeStruct((t, d), bf16),
         jax.ShapeDtypeStruct((1, d), f32), jax.ShapeDtypeStruct((1, d), f32)], ins, (), ("arbitrary",), comm)
    return tuple(outs) if comm is None else (*outs, comm_outs)


def _loss_call(xf, target):
    t, d = xf.shape

    def body(x_ref, t_ref, dx_ref, loss_ref):
        err = x_ref[...] - t_ref[...]
        dx_ref[...] = err * (1.0 / d)

        @pl.when(pl.program_id(0) == 0)
        def _():
            loss_ref[...] = jnp.zeros_like(loss_ref)

        part = 0.5 * jnp.sum(jnp.mean(jnp.square(err), axis=-1, keepdims=True), axis=0, keepdims=True)
        loss_ref[...] += jnp.broadcast_to(part, loss_ref.shape)

    return pl.pallas_call(
        body, name="loss_head", grid=(t // ROW_TILE,), in_specs=[_rows(d), _rows(d)],
        out_specs=[_rows(d), _whole((1, LANES))],
        out_shape=[jax.ShapeDtypeStruct((t, d), f32), jax.ShapeDtypeStruct((1, LANES), f32)],
        compiler_params=_cparams(("arbitrary",)),
    )(xf, target)


def _rope_tables(pos, inv, scale):
    t = pos.shape[0]

    def body(pos_ref, inv_ref, cs_ref, qt_ref):
        ang = pos_ref[...].astype(f32) * inv_ref[...]
        lane = lax.broadcasted_iota(jnp.int32, ang.shape, 1)
        cs = jnp.where(lane < LANES // 2, jnp.cos(ang), jnp.sin(ang))
        cs_ref[...] = cs
        qt_ref[:, :LANES] = jnp.full((ROW_TILE, LANES), scale, f32)
        qt_ref[:, LANES:] = scale * cs

    return pl.pallas_call(
        body, name="rope_tables", grid=(t // ROW_TILE,),
        in_specs=[pl.BlockSpec((ROW_TILE, 1), lambda i: (i, 0)), _whole((1, LANES))],
        out_specs=[_rows(LANES), _rows(2 * LANES)],
        out_shape=[jax.ShapeDtypeStruct((t, LANES), f32), jax.ShapeDtypeStruct((t, 2 * LANES), f32)],
        compiler_params=_cparams(("parallel",)),
    )(pos, inv)


def _prep_fn(cq, ckv, kr, cs, gq, gkv):
    return _rms_norm(cq, gq), _rms_norm(ckv, gkv), _dup_halves(kr * cs)


def _prep_fwd(h, cs, gq, gkv, lay):
    t = h.shape[0]
    ql, kvl = lay["ql"], lay["kvl"]

    def body(cq_ref, ckv_ref, kr_ref, cs_ref, gq_ref, gkv_ref, qn_ref, kvn_ref, krd_ref):
        qn, kvn, krd = _prep_fn(cq_ref[...], ckv_ref[...], kr_ref[...], cs_ref[...], gq_ref[...], gkv_ref[...])
        qn_ref[...] = qn.astype(bf16)
        kvn_ref[...] = kvn.astype(bf16)
        krd_ref[...] = krd.astype(bf16)

    return pl.pallas_call(
        body, name="prep_fwd", grid=(t // ROW_TILE,),
        in_specs=[_rows(ql, lay["off_q"] // ql), _rows(kvl, lay["off_kv"] // kvl), _rows(LANES, lay["off_kr"] // LANES),
                  _rows(LANES), _whole((1, ql)), _whole((1, kvl))],
        out_specs=[_rows(ql), _rows(kvl), _rows(LANES)],
        out_shape=[jax.ShapeDtypeStruct((t, ql), bf16), jax.ShapeDtypeStruct((t, kvl), bf16),
                   jax.ShapeDtypeStruct((t, LANES), bf16)],
        compiler_params=_cparams(("parallel",)),
    )(h, h, h, cs, gq, gkv)


def _prep_bwd(h, cs, gq, gkv, d_qn, d_kvn, d_krd, lay):
    t = h.shape[0]
    ql, kvl = lay["ql"], lay["kvl"]

    def body(cq_ref, ckv_ref, kr_ref, cs_ref, gq_ref, gkv_ref, dqn_ref, dkvn_ref, dkrd_ref,
             dcq_ref, dckv_ref, dkr_ref, dgq_ref, dgkv_ref):
        _, vjp = jax.vjp(_prep_fn, cq_ref[...], ckv_ref[...], kr_ref[...], cs_ref[...], gq_ref[...], gkv_ref[...])
        dcq, dckv, dkr, _, dgq, dgkv = vjp((dqn_ref[...], dkvn_ref[...], dkrd_ref[...].astype(f32)))
        dcq_ref[...] = dcq.astype(bf16)
        dckv_ref[...] = dckv.astype(bf16)
        dkr_ref[...] = dkr.astype(bf16)

        @pl.when(pl.program_id(0) == 0)
        def _():
            dgq_ref[...] = jnp.zeros_like(dgq_ref)
            dgkv_ref[...] = jnp.zeros_like(dgkv_ref)

        dgq_ref[...] += dgq
        dgkv_ref[...] += dgkv

    return pl.pallas_call(
        body, name="prep_bwd", grid=(t // ROW_TILE,),
        in_specs=[_rows(ql, lay["off_q"] // ql), _rows(kvl, lay["off_kv"] // kvl), _rows(LANES, lay["off_kr"] // LANES),
                  _rows(LANES), _whole((1, ql)), _whole((1, kvl)), _rows(ql), _rows(kvl), _rows(LANES)],
        out_specs=[_rows(ql), _rows(kvl), _rows(LANES), _whole((1, ql)), _whole((1, kvl))],
        out_shape=[jax.ShapeDtypeStruct((t, ql), bf16), jax.ShapeDtypeStruct((t, kvl), bf16),
                   jax.ShapeDtypeStruct((t, LANES), bf16), jax.ShapeDtypeStruct((1, ql), f32),
                   jax.ShapeDtypeStruct((1, kvl), f32)],
        compiler_params=_cparams(("arbitrary",)),
    )(h, h, h, cs, gq, gkv, d_qn, d_kvn, d_krd)


def _convln_fn(hc, g, b):
    y = _layer_norm(hc, g, b)
    return y * jax.nn.sigmoid(y)


def _convln_fwd(hconv, g, b):
    t, cw = hconv.shape

    def body(h_ref, g_ref, b_ref, o_ref):
        o_ref[...] = _convln_fn(h_ref[...], g_ref[...], b_ref[...]).astype(bf16)

    return pl.pallas_call(
        body, name="convln_fwd", grid=(t // ROW_TILE,),
        in_specs=[_rows(cw), _whole((1, cw)), _whole((1, cw))], out_specs=_rows(cw),
        out_shape=jax.ShapeDtypeStruct((t, cw), bf16), compiler_params=_cparams(("parallel",)),
    )(hconv, g, b)


def _convln_bwd(hconv, g, b, d_mixed, col_block):
    t, cw = hconv.shape

    def body(h_ref, g_ref, b_ref, dy_ref, dh_ref, dg_ref, db_ref):
        _, vjp = jax.vjp(_convln_fn, h_ref[...], g_ref[...], b_ref[...])
        dh, dg, db = vjp(dy_ref[...])
        dh_ref[...] = dh

        @pl.when(pl.program_id(0) == 0)
        def _():
            dg_ref[...] = jnp.zeros_like(dg_ref)
            db_ref[...] = jnp.zeros_like(db_ref)

        dg_ref[...] += dg
        db_ref[...] += db

    return pl.pallas_call(
        body, name="convln_bwd", grid=(t // ROW_TILE,),
        in_specs=[_rows(cw), _whole((1, cw)), _whole((1, cw)), _rows(cw, col_block)],
        out_specs=[_rows(cw), _whole((1, cw)), _whole((1, cw))],
        out_shape=[jax.ShapeDtypeStruct((t, cw), f32), jax.ShapeDtypeStruct((1, cw), f32),
                   jax.ShapeDtypeStruct((1, cw), f32)],
        compiler_params=_cparams(("arbitrary",)),
    )(hconv, g, b, d_mixed)


SEQ_CT = 128


def _conv_fwd(h, w, b, nb, seq, cw):
    ct, kk = SEQ_CT, w.shape[0]
    ncb = cw // ct

    def body(h_ref, w_ref, b_ref, o_ref):
        blk = h_ref[...]
        a, g = blk[:, :ct], blk[:, ct:]
        hc = a * jax.nn.sigmoid(g)
        acc = jnp.zeros_like(hc)
        for k in range(kk):
            acc = acc + w_ref[k:k + 1, :] * _shift_down_raw(hc, kk - 1 - k)
        o_ref[...] = acc + b_ref[...]

    return pl.pallas_call(
        body, name="conv_fwd", grid=(ncb, nb),
        in_specs=[pl.BlockSpec((seq, 2 * ct), lambda j, bb: (bb, j)), pl.BlockSpec((kk, ct), lambda j, bb: (0, j)),
                  pl.BlockSpec((1, ct), lambda j, bb: (0, j))],
        out_specs=pl.BlockSpec((seq, ct), lambda j, bb: (bb, j)),
        out_shape=jax.ShapeDtypeStruct((nb * seq, cw), f32),
        compiler_params=_cparams(("parallel", "parallel")),
    )(h, w, b)


def _conv_bwd(h, w, d_hconv, nb, seq, cw):
    ct, kk = SEQ_CT, w.shape[0]
    ncb = cw // ct

    def body(h_ref, w_ref, dy_ref, dh_ref, dw_ref, db_ref):
        blk = h_ref[...]
        a, g = blk[:, :ct], blk[:, ct:]
        sg = jax.nn.sigmoid(g)
        hc = a * sg
        dy = dy_ref[...]
        dhc = jnp.zeros_like(hc)

        @pl.when(pl.program_id(1) == 0)
        def _():
            dw_ref[...] = jnp.zeros_like(dw_ref)
            db_ref[...] = jnp.zeros_like(db_ref)

        for k in range(kk):
            dhc = dhc + w_ref[k:k + 1, :] * _shift_up_raw(dy, kk - 1 - k)
            dw_ref[k:k + 1, :] += _colsum(dy * _shift_down_raw(hc, kk - 1 - k))
        db_ref[...] += _colsum(dy)
        dh_ref[:, :ct] = (dhc * sg).astype(bf16)
        dh_ref[:, ct:] = (dhc * a * sg * (1.0 - sg)).astype(bf16)

    return pl.pallas_call(
        body, name="conv_bwd", grid=(ncb, nb),
        in_specs=[pl.BlockSpec((seq, 2 * ct), lambda j, bb: (bb, j)), pl.BlockSpec((kk, ct), lambda j, bb: (0, j)),
                  pl.BlockSpec((seq, ct), lambda j, bb: (bb, j))],
        out_specs=[pl.BlockSpec((seq, 2 * ct), lambda j, bb: (bb, j)), pl.BlockSpec((kk, ct), lambda j, bb: (0, j)),
                   pl.BlockSpec((1, ct), lambda j, bb: (0, j))],
        out_shape=[jax.ShapeDtypeStruct((nb * seq, 2 * cw), bf16), jax.ShapeDtypeStruct((kk, cw), f32),
                   jax.ShapeDtypeStruct((1, cw), f32)],
        compiler_params=_cparams(("parallel", "arbitrary")),
    )(h, w, d_hconv)


def _pool_fn(u, wp, scale, pg):
    seq = u.shape[0]
    t1 = (lax.broadcasted_iota(jnp.int32, (seq, 1), 0) + 1).astype(f32)
    outs = []
    for gi, win in enumerate(POOL_WINDOWS):
        ug = u[:, gi * pg:(gi + 1) * pg]
        acc, span = ug, 1
        while span < win:
            acc = acc + _shift_down(acc, span)
            span *= 2
        d = acc / jnp.minimum(t1, float(win)) - ug
        outs.append(_mm_bf16(d, wp[gi * pg:(gi + 1) * pg, :]) * scale[:, gi * pg:(gi + 1) * pg])
    return outs


def _pool_fwd(h, wp, scale, lay, nb, seq):
    pw, pg = lay["pw"], lay["pg"]

    def body(u_ref, wp_ref, sc_ref, o_ref):
        outs = _pool_fn(u_ref[...], wp_ref[...], sc_ref[...], pg)
        for gi in range(len(POOL_WINDOWS)):
            o_ref[:, gi * pg:(gi + 1) * pg] = outs[gi].astype(bf16)

    return pl.pallas_call(
        body, name="pool_fwd", grid=(nb,),
        in_specs=[pl.BlockSpec((seq, pw), lambda bb: (bb, lay["off_pool"] // pw)), _whole((pw, pg)), _whole((1, pw))],
        out_specs=pl.BlockSpec((seq, pw), lambda bb: (bb, 0)),
        out_shape=jax.ShapeDtypeStruct((nb * seq, pw), bf16),
        compiler_params=_cparams(("parallel",)),
    )(h, wp, scale)


def _pool_bwd(h, wp, scale, d_mixed, col_block, lay, nb, seq):
    pw, pg = lay["pw"], lay["pg"]
    ng = len(POOL_WINDOWS)

    def body(u_ref, wp_ref, sc_ref, dy_ref, du_ref, dwp_ref, dsc_ref):
        _, vjp = jax.vjp(functools.partial(_pool_fn, pg=pg), u_ref[...], wp_ref[...], sc_ref[...])
        dy = dy_ref[...]
        du, dwp, dsc = vjp([dy[:, gi * pg:(gi + 1) * pg] for gi in range(ng)])
        du_ref[...] = du.astype(bf16)

        @pl.when(pl.program_id(0) == 0)
        def _():
            dwp_ref[...] = jnp.zeros_like(dwp_ref)
            dsc_ref[...] = jnp.zeros_like(dsc_ref)

        dwp_ref[...] += dwp
        dsc_ref[...] += dsc

    return pl.pallas_call(
        body, name="pool_bwd", grid=(nb,),
        in_specs=[pl.BlockSpec((seq, pw), lambda bb: (bb, lay["off_pool"] // pw)), _whole((pw, pg)), _whole((1, pw)),
                  pl.BlockSpec((seq, pw), lambda bb: (bb, col_block))],
        out_specs=[pl.BlockSpec((seq, pw), lambda bb: (bb, 0)), _whole((pw, pg)), _whole((1, pw))],
        out_shape=[jax.ShapeDtypeStruct((nb * seq, pw), bf16), jax.ShapeDtypeStruct((pw, pg), f32),
                   jax.ShapeDtypeStruct((1, pw), f32)],
        compiler_params=_cparams(("arbitrary",)),
    )(h, wp, scale, d_mixed)


FFN_ROWS = 32
FFN_CT = 256
SUBLANES = 8


def _rows_before(ref, r0, s, n, cols):
    if s == 0:
        return ref[r0:r0 + n, cols]
    if r0 == 0:
        x = ref[0:n, cols]
        row = lax.broadcasted_iota(jnp.int32, x.shape, 0)
        return jnp.where(row >= s, pltpu.roll(x, s, axis=0), 0.0)
    return ref[pl.ds(r0 - s, n), cols]


def _ffn_conv_rows(x_ref, w_ref, b_ref, r0, n, cols):
    kk = w_ref.shape[0]
    xs = [_rows_before(x_ref, r0, s, n, cols) for s in range(kk)]
    c = b_ref[:, cols] + w_ref[kk - 1:kk, cols] * xs[0]
    for k in range(kk - 1):
        c = c + w_ref[k:k + 1, cols] * xs[kk - 1 - k]
    return c, xs


def _lane_tiles(width):
    return [slice(c0, c0 + LANES) for c0 in range(0, width, LANES)]


def _fold_rows(x):
    out = x[0:SUBLANES]
    for i in range(1, x.shape[0] // SUBLANES):
        out = out + x[i * SUBLANES:(i + 1) * SUBLANES]
    return out


def _ffn_specs(seq, ct, kk, nct):
    return [pl.BlockSpec((seq, ct), lambda j, bb: (bb, j)), pl.BlockSpec((seq, ct), lambda j, bb: (bb, nct + j)),
            pl.BlockSpec((kk, ct), lambda j, bb: (0, j)), pl.BlockSpec((kk, ct), lambda j, bb: (0, nct + j)),
            pl.BlockSpec((1, ct), lambda j, bb: (0, j)), pl.BlockSpec((1, ct), lambda j, bb: (0, nct + j))]


def _ffn_act_fwd(up, w, b, nb, seq, dff, comm=None):
    ct, kk = _tile(dff, FFN_CT), w.shape[0]
    rows = min(FFN_ROWS, seq)

    def body(ua_ref, ug_ref, wa_ref, wg_ref, ba_ref, bg_ref, o_ref):
        for cols in _lane_tiles(ct):
            for r0 in range(0, seq, rows):
                a, _ = _ffn_conv_rows(ua_ref, wa_ref, ba_ref, r0, rows, cols)
                g, _ = _ffn_conv_rows(ug_ref, wg_ref, bg_ref, r0, rows, cols)
                o_ref[r0:r0 + rows, cols] = (a * g * jax.nn.sigmoid(g)).astype(bf16)

    (act,), comm_outs = _call(body, "ffn_act_fwd", (dff // ct, nb), _ffn_specs(seq, ct, kk, dff // ct),
                              [pl.BlockSpec((seq, ct), lambda j, bb: (bb, j))],
                              [jax.ShapeDtypeStruct((nb * seq, dff), bf16)], (up, up, w, w, b, b), (),
                              ("parallel", "parallel"), comm)
    return act if comm is None else (act, comm_outs)


def _ffn_act_bwd(up, w, b, d_act, nb, seq, dff, comm=None):
    ct, kk = _tile(dff, FFN_CT), w.shape[0]
    rows = min(FFN_ROWS, seq)

    def body(ua_ref, ug_ref, wa_ref, wg_ref, ba_ref, bg_ref, da_ref, du_ref, dw_ref, db_ref, dc_ref):
        @pl.when(pl.program_id(1) == 0)
        def _():
            dw_ref[...] = jnp.zeros_like(dw_ref)
            db_ref[...] = jnp.zeros_like(db_ref)

        for cols in _lane_tiles(ct):
            dw_acc = [[jnp.zeros((SUBLANES, LANES), f32) for _ in range(kk)] for _ in range(2)]
            db_acc = [jnp.zeros((SUBLANES, LANES), f32) for _ in range(2)]
            for r0 in range(0, seq, rows):
                a, xa = _ffn_conv_rows(ua_ref, wa_ref, ba_ref, r0, rows, cols)
                g, xg = _ffn_conv_rows(ug_ref, wg_ref, bg_ref, r0, rows, cols)
                sg = jax.nn.sigmoid(g)
                dact = da_ref[r0:r0 + rows, cols]
                dcs = (dact * g * sg, dact * a * sg * (1.0 + g * (1.0 - sg)))
                for hf, (dc, xs) in enumerate(zip(dcs, (xa, xg))):
                    dc_ref[hf, r0:r0 + rows, cols] = dc
                    db_acc[hf] = db_acc[hf] + _fold_rows(dc)
                    for k in range(kk):
                        dw_acc[hf][k] = dw_acc[hf][k] + _fold_rows(dc * xs[kk - 1 - k])
            for hf in range(2):
                dc_ref[hf, seq:seq + SUBLANES, cols] = jnp.zeros((SUBLANES, LANES), f32)
                db_ref[hf, :, cols] += _colsum(db_acc[hf])
                for k in range(kk):
                    dw_ref[hf, k:k + 1, cols] += _colsum(dw_acc[hf][k])
            for r0 in range(0, seq, rows):
                for hf, w_ref in enumerate((wa_ref, wg_ref)):
                    dsrc = w_ref[kk - 1:kk, cols] * dc_ref[hf, r0:r0 + rows, cols]
                    for k in range(kk - 1):
                        dsrc = dsrc + w_ref[k:k + 1, cols] * dc_ref[hf, pl.ds(r0 + kk - 1 - k, rows), cols]
                    du_ref[hf, r0:r0 + rows, cols] = dsrc.astype(bf16)

    (d_up, dw, db), comm_outs = _call(
        body, "ffn_act_bwd", (dff // ct, nb),
        _ffn_specs(seq, ct, kk, dff // ct) + [pl.BlockSpec((seq, ct), lambda j, bb: (bb, j))],
        [pl.BlockSpec((2, seq, ct), lambda j, bb: (0, bb, j)), pl.BlockSpec((2, kk, ct), lambda j, bb: (0, 0, j)),
         pl.BlockSpec((2, 1, ct), lambda j, bb: (0, 0, j))],
        [jax.ShapeDtypeStruct((2, nb * seq, dff), bf16), jax.ShapeDtypeStruct((2, kk, dff), f32),
         jax.ShapeDtypeStruct((2, 1, dff), f32)],
        (up, up, w, w, b, b, d_act), [pltpu.VMEM((2, seq + SUBLANES, ct), f32)], ("parallel", "arbitrary"), comm)
    return (d_up, dw, db) if comm is None else (d_up, dw, db, comm_outs)


def _fill_keys(k_scr, kn_ref, krd_ref):
    @pl.when(pl.program_id(2) == 0)
    def _():
        k_scr[:, :LANES] = kn_ref[...]
        k_scr[:, LANES:] = krd_ref[...]


def _scores(q_ref, qt_ref, k_scr, qblk, tq):
    klen = (qblk + 1) * tq
    q = (q_ref[...] * qt_ref[...]).astype(bf16)
    s = _dot(q, k_scr[0:klen, :], _NT)
    row = qblk * tq + lax.broadcasted_iota(jnp.int32, s.shape, 0)
    col = lax.broadcasted_iota(jnp.int32, s.shape, 1)
    return q, jnp.where(col <= row, s, NEG_INF)


def _per_q_block(nq, fn):
    qi = pl.program_id(2)
    for qblk in range(nq):
        pl.when(qi == qblk)(functools.partial(fn, qblk))


def _attn_fwd(q_ext, qt, kv, krd, nb, seq, heads, comm=None):
    tq = _tile(seq, 512)
    nq = seq // tq

    def body(q_ref, qt_ref, kn_ref, krd_ref, v_ref, o_ref, lse_ref, k_scr):
        _fill_keys(k_scr, kn_ref, krd_ref)

        def work(qblk):
            klen = (qblk + 1) * tq
            _, s = _scores(q_ref, qt_ref, k_scr, qblk, tq)
            m = jnp.max(s, axis=-1, keepdims=True)
            p = jnp.exp(s - m)
            l = jnp.sum(p, axis=-1, keepdims=True)
            o_ref[...] = (_dot(p, v_ref[0:klen, :], _NN) / l).astype(bf16)
            lse_ref[...] = m + jnp.log(l)

        _per_q_block(nq, work)

    (o, lse), comm_outs = _call(
        body, "attn_fwd", (nb, heads, nq),
        [pl.BlockSpec((tq, 2 * LANES), lambda b, h, i: (b * nq + i, h)),
         pl.BlockSpec((tq, 2 * LANES), lambda b, h, i: (b * nq + i, 0)),
         pl.BlockSpec((seq, LANES), lambda b, h, i: (b, 2 * h)),
         pl.BlockSpec((seq, LANES), lambda b, h, i: (b, 0)),
         pl.BlockSpec((seq, LANES), lambda b, h, i: (b, 2 * h + 1))],
        [pl.BlockSpec((tq, LANES), lambda b, h, i: (b * nq + i, h)),
         pl.BlockSpec((None, tq, 1), lambda b, h, i: (b * heads + h, i, 0))],
        [jax.ShapeDtypeStruct((nb * seq, heads * LANES), bf16), jax.ShapeDtypeStruct((nb * heads, seq, 1), f32)],
        (q_ext, qt, kv, krd, kv), [pltpu.VMEM((seq, 2 * LANES), bf16)], ("parallel", "parallel", "arbitrary"), comm)
    return (o, lse) if comm is None else (o, lse, comm_outs)


def _attn_bwd(q_ext, qt, kv, krd, lse, d_mixed, nb, seq, heads, comm=None):
    tq = _tile(seq, 512)
    nq = seq // tq

    def body(q_ref, qt_ref, kn_ref, krd_ref, v_ref, lse_ref, do_ref, dq_ref, dkv_ref, dkrd_ref, dkv_acc, k_scr):
        h, qi = pl.program_id(1), pl.program_id(2)
        _fill_keys(k_scr, kn_ref, krd_ref)

        @pl.when(qi == 0)
        def _():
            dkv_acc[...] = jnp.zeros_like(dkv_acc)

        @pl.when((qi == 0) & (h == 0))
        def _():
            dkrd_ref[...] = jnp.zeros_like(dkrd_ref)

        def work(qblk):
            klen = (qblk + 1) * tq
            q, s = _scores(q_ref, qt_ref, k_scr, qblk, tq)
            p = jnp.exp(s - lse_ref[...])
            do = do_ref[...]
            dp = _dot(do, v_ref[0:klen, :], _NT)
            ds = (p * (dp - jnp.sum(p * dp, axis=-1, keepdims=True))).astype(bf16)
            dq_ref[...] = (_dot(ds, k_scr[0:klen, :], _NN) * qt_ref[...]).astype(bf16)
            dk = _dot(ds, q, _TN)
            dkv_acc[0:klen, :LANES] += dk[:, :LANES]
            dkv_acc[0:klen, LANES:] += _dot(p, do, _TN)
            dkrd_ref[0:klen, :] += dk[:, LANES:]

        _per_q_block(nq, work)

        @pl.when(qi == nq - 1)
        def _():
            dkv_ref[...] = dkv_acc[...].astype(bf16)

    (dq, dkv, dkrd), comm_outs = _call(
        body, "attn_bwd", (nb, heads, nq),
        [pl.BlockSpec((tq, 2 * LANES), lambda b, h, i: (b * nq + i, h)),
         pl.BlockSpec((tq, 2 * LANES), lambda b, h, i: (b * nq + i, 0)),
         pl.BlockSpec((seq, LANES), lambda b, h, i: (b, 2 * h)),
         pl.BlockSpec((seq, LANES), lambda b, h, i: (b, 0)),
         pl.BlockSpec((seq, LANES), lambda b, h, i: (b, 2 * h + 1)),
         pl.BlockSpec((None, tq, 1), lambda b, h, i: (b * heads + h, i, 0)),
         pl.BlockSpec((tq, LANES), lambda b, h, i: (b * nq + i, h))],
        [pl.BlockSpec((tq, 2 * LANES), lambda b, h, i: (b * nq + i, h)),
         pl.BlockSpec((seq, 2 * LANES), lambda b, h, i: (b, h)),
         pl.BlockSpec((seq, LANES), lambda b, h, i: (b, 0))],
        [jax.ShapeDtypeStruct((nb * seq, heads * 2 * LANES), bf16),
         jax.ShapeDtypeStruct((nb * seq, heads * 2 * LANES), bf16),
         jax.ShapeDtypeStruct((nb * seq, LANES), f32)],
        (q_ext, qt, kv, krd, kv, lse, d_mixed), [pltpu.VMEM((seq, 2 * LANES), f32), pltpu.VMEM((seq, 2 * LANES), bf16)],
        ("parallel", "arbitrary", "arbitrary"), comm)
    return (dq, dkv, dkrd) if comm is None else (dq, dkv, dkrd, comm_outs)


def _mesh_pos():
    x, y, c = lax.axis_index("x"), lax.axis_index("y"), lax.axis_index("c")
    return x, y, c, [(1 - x, y), (x, 1 - y), (1 - x, 1 - y)]


def _all_gather(shards, name):
    n = len(shards)

    def body(*refs):
        x_refs, out_refs, (send_sems, recv_sems, local_sems) = refs[:n], refs[n:2 * n], refs[2 * n:]
        x, y, c, chips = _mesh_pos()
        me, sibling = (x, y, c), (x, y, 1 - c)

        def copy(t, k, block, to, from_shard=False):
            px, py, pc = block
            rows = out_refs[t].at[4 * px + 2 * py + pc]
            return pltpu.make_async_remote_copy(
                src_ref=x_refs[t] if from_shard else rows, dst_ref=rows,
                send_sem=send_sems.at[t, k], recv_sem=recv_sems.at[t, k], device_id=to, device_id_type=MESH)

        mine = [pltpu.make_async_copy(x_refs[t], out_refs[t].at[4 * x + 2 * y + c], local_sems.at[t]) for t in range(n)]
        first = [[copy(t, 0, me, sibling, True)] + [copy(t, 1 + j, me, (*chip, c), True) for j, chip in enumerate(chips)]
                 for t in range(n)]
        passed = [[copy(t, 4 + j, (*chip, c), sibling) for j, chip in enumerate(chips)] for t in range(n)]
        for t in range(n):
            mine[t].start()
            for cp in first[t]:
                cp.start()
        for j, chip in enumerate(chips):
            for t in range(n):
                copy(t, 1 + j, (*chip, c), me).wait_recv()
                passed[t][j].start()
        for t in range(n):
            copy(t, 0, sibling, me).wait_recv()
            for j, chip in enumerate(chips):
                copy(t, 4 + j, (*chip, 1 - c), me).wait_recv()
        for t in range(n):
            for cp in first[t] + passed[t]:
                cp.wait_send()
            mine[t].wait()

    return pl.pallas_call(
        body, name=name, out_shape=[jax.ShapeDtypeStruct((N_DEV,) + s.shape, s.dtype) for s in shards],
        in_specs=[_ANY] * n, out_specs=[_ANY] * n,
        scratch_shapes=[pltpu.SemaphoreType.DMA((n, 7)), pltpu.SemaphoreType.DMA((n, 7)), pltpu.SemaphoreType.DMA((n,))],
    )(*shards)


def _gather_own(shards):
    n = len(shards)

    def remote(x_refs, out_refs, sems, arriving):
        send_sems, recv_sems, _ = sems
        x, y, c, chips = _mesh_pos()
        peers = [(x, y, 1 - c)] + [(*chip, c) for chip in chips]
        return [pltpu.make_async_remote_copy(
            src_ref=x_refs[t], dst_ref=out_refs[t].at[4 * px + 2 * py + pc if arriving else 4 * x + 2 * y + c],
            send_sem=send_sems.at[t, k], recv_sem=recv_sems.at[t, k], device_id=(px, py, pc), device_id_type=MESH)
            for t in range(n) for k, (px, py, pc) in enumerate(peers)]

    def local(x_refs, out_refs, sems):
        x, y, c, _ = _mesh_pos()
        return [pltpu.make_async_copy(x_refs[t], out_refs[t].at[4 * x + 2 * y + c], sems[2].at[t]) for t in range(n)]

    def start(x_refs, out_refs, sems):
        for cp in local(x_refs, out_refs, sems) + remote(x_refs, out_refs, sems, False):
            cp.start()

    def finish(x_refs, out_refs, sems):
        for cp in remote(x_refs, out_refs, sems, True):
            cp.wait_recv()
        for cp in remote(x_refs, out_refs, sems, False):
            cp.wait_send()
        for cp in local(x_refs, out_refs, sems):
            cp.wait()

    return _Comm(shards, [jax.ShapeDtypeStruct((N_DEV,) + s.shape, s.dtype) for s in shards],
                 [pltpu.SemaphoreType.DMA((n, 4)), pltpu.SemaphoreType.DMA((n, 4)), pltpu.SemaphoreType.DMA((n,))],
                 start, finish)


def _gather_pass(gathered):
    n = len(gathered)

    def copies(in_refs, out_refs, sems, arriving):
        send_sems, recv_sems = sems
        x, y, c, chips = _mesh_pos()
        return [pltpu.make_async_remote_copy(
            src_ref=in_refs[t].at[4 * px + 2 * py + c],
            dst_ref=out_refs[t].at[4 * px + 2 * py + (1 - c if arriving else c)],
            send_sem=send_sems.at[t, j], recv_sem=recv_sems.at[t, j], device_id=(x, y, 1 - c), device_id_type=MESH)
            for t in range(n) for j, (px, py) in enumerate(chips)]

    def start(in_refs, out_refs, sems):
        for cp in copies(in_refs, out_refs, sems, False):
            cp.start()

    def finish(in_refs, out_refs, sems):
        for cp in copies(in_refs, out_refs, sems, True):
            cp.wait_recv()
        for cp in copies(in_refs, out_refs, sems, False):
            cp.wait_send()

    return _Comm(gathered, [jax.ShapeDtypeStruct(g.shape, g.dtype) for g in gathered],
                 [pltpu.SemaphoreType.DMA((n, 3)), pltpu.SemaphoreType.DMA((n, 3))], start, finish,
                 aliases={t: t for t in range(n)})


def _sibling_swap(slots):
    n = len(slots)

    def start(g_refs, out_refs, sems):
        send_sems, recv_sems = sems
        x, y, c, _ = _mesh_pos()
        for t in range(n):
            for k in range(4):
                pltpu.make_async_remote_copy(
                    src_ref=g_refs[t].at[2 * k + (1 - c)], dst_ref=out_refs[t].at[k], send_sem=send_sems.at[t],
                    recv_sem=recv_sems.at[t], device_id=(x, y, 1 - c), device_id_type=MESH).start()

    def finish(g_refs, out_refs, sems):
        send_sems, recv_sems = sems
        x, y, c, _ = _mesh_pos()
        for t in range(n):
            pltpu.make_async_remote_copy(
                src_ref=g_refs[t].at[pl.ds(0, 4)], dst_ref=out_refs[t], send_sem=send_sems.at[t],
                recv_sem=recv_sems.at[t], device_id=(x, y, 1 - c), device_id_type=MESH).wait()

    return _Comm(slots, [jax.ShapeDtypeStruct((4,) + s.shape[1:], s.dtype) for s in slots],
                 [pltpu.SemaphoreType.DMA((n,)), pltpu.SemaphoreType.DMA((n,))], start, finish)


def _chip_swap(p4s, peers=(0, 1, 2), into=None):
    n = len(p4s)

    def copies(refs, out_refs, sems):
        send_sems, recv_sems = sems
        x, y, c, chips = _mesh_pos()
        return [pltpu.make_async_remote_copy(
            src_ref=refs[t].at[2 * chips[j][0] + chips[j][1]], dst_ref=out_refs[t].at[j], send_sem=send_sems.at[t, j],
            recv_sem=recv_sems.at[t, j], device_id=(*chips[j], c), device_id_type=MESH)
            for t in range(n) for j in peers]

    def start(refs, out_refs, sems):
        for cp in copies(refs, out_refs, sems):
            cp.start()

    def finish(refs, out_refs, sems):
        for cp in copies(refs, out_refs, sems):
            cp.wait()

    out_shapes = [jax.ShapeDtypeStruct((3,) + p.shape[1:], p.dtype) for p in p4s]
    sems = [pltpu.SemaphoreType.DMA((n, 3)), pltpu.SemaphoreType.DMA((n, 3))]
    if into is None:
        return _Comm(p4s, out_shapes, sems, start, finish)
    return _Comm(list(p4s) + list(into), out_shapes, sems, start, finish, aliases={n + t: t for t in range(n)})


def _comm_join(a, b):
    ai, ao, asem = len(a.inputs), len(a.out_shapes), len(a.sems)

    def start(ins, outs, sems):
        a.start(ins[:ai], outs[:ao], sems[:asem])
        b.start(ins[ai:], outs[ao:], sems[asem:])

    def finish(ins, outs, sems):
        a.finish(ins[:ai], outs[:ao], sems[:asem])
        b.finish(ins[ai:], outs[ao:], sems[asem:])

    aliases = dict(a.aliases)
    aliases.update({ai + i: ao + o for i, o in b.aliases.items()})
    return _Comm(a.inputs + b.inputs, a.out_shapes + b.out_shapes, a.sems + b.sems, start, finish, aliases)


def _row_tile(rows, cols, max_bytes=1024 * 1024):
    best = None
    for tr in range(16, rows + 1, 16):
        if rows % tr == 0 and tr * cols * 4 <= max_bytes:
            best = tr
    return best or rows


def _pair_add(slots, theirs, core, name):
    _, rows, cols = slots.shape
    tr = _row_tile(rows, cols, 4 * 1024 * 1024)

    def body(c_ref, a_ref, b_ref, o_ref):
        o_ref[...] = (a_ref[...].astype(f32) + b_ref[...].astype(f32)).astype(bf16)

    return pl.pallas_call(
        body, name=name,
        grid_spec=pltpu.PrefetchScalarGridSpec(
            num_scalar_prefetch=1, grid=(4, rows // tr),
            in_specs=[pl.BlockSpec((None, tr, cols), lambda k, i, c_ref: (2 * k + c_ref[0], i, 0)),
                      pl.BlockSpec((None, tr, cols), lambda k, i, c_ref: (k, i, 0))],
            out_specs=pl.BlockSpec((None, tr, cols), lambda k, i, c_ref: (k, i, 0))),
        out_shape=jax.ShapeDtypeStruct((4, rows, cols), bf16), compiler_params=_cparams(("parallel", "parallel")),
    )(core, slots, theirs)


def _adam_update(g, w, m, v):
    c1 = 1.0 / (1.0 - ADAM_B1 ** ADAM_STEP)
    c2 = 1.0 / (1.0 - ADAM_B2 ** ADAM_STEP)
    nm = ADAM_B1 * m + (1.0 - ADAM_B1) * g
    nv = ADAM_B2 * v + (1.0 - ADAM_B2) * jnp.square(g)
    delta = -ADAM_LR * ((nm * c1) / (jnp.sqrt(nv * c2) + ADAM_EPS) + ADAM_WD * w)
    return delta, nm, nv


def _adamw_layers(p4s, chips, chip_idx, w, m, v, name, comm=None):
    depth = len(p4s)
    _, rows_l, cols = p4s[0].shape
    tr = _row_tile(rows_l, cols)
    nr = rows_l // tr

    def body(idx_ref, *refs):
        p_refs, c_refs = refs[:depth], refs[depth:2 * depth]
        w_ref, m_ref, v_ref, g_ref, d_ref, nm_ref, nv_ref = refs[2 * depth:]
        layer = pl.program_id(0)
        for ll in range(depth):
            @pl.when(layer == ll)
            def _(ll=ll):
                g = p_refs[ll][...].astype(f32)
                for j in range(3):
                    g = g + c_refs[ll][j].astype(f32)
                delta, nm, nv = _adam_update(g, w_ref[...], m_ref[...], v_ref[...])
                g_ref[...] = g
                d_ref[...] = delta
                nm_ref[...] = nm
                nv_ref[...] = nv

    def of_layer(ll):
        return lambda l, i: jnp.where(l == ll, i, 0)

    p_specs = [pl.BlockSpec((None, tr, cols), lambda l, i, idx_ref, f=of_layer(ll): (idx_ref[0], f(l, i), 0))
               for ll in range(depth)]
    c_specs = [pl.BlockSpec((3, tr, cols), lambda l, i, idx_ref, f=of_layer(ll): (0, f(l, i), 0)) for ll in range(depth)]
    spec = pl.BlockSpec((tr, cols), lambda l, i, idx_ref: (l * nr + i, 0))
    out = jax.ShapeDtypeStruct(w.shape, f32)
    outs, comm_outs = _call(body, name, (depth, nr), p_specs + c_specs + [spec, spec, spec], [spec, spec, spec, spec],
                            [out, out, out, out], (*p4s, *chips, w, m, v), (), ("parallel", "parallel"), comm,
                            prefetch=[chip_idx])
    return outs if comm is None else (outs, comm_outs)


def _adamw(parts, w, m, v, name):
    rows, cols = w.shape
    tr = _row_tile(rows, cols, 512 * 1024)
    nparts = len(parts)

    def body(*refs):
        part_refs, (w_ref, m_ref, v_ref, g_ref, d_ref, nm_ref, nv_ref) = refs[:nparts], refs[nparts:]
        g = None
        for pr in part_refs:
            for s in range(pr.shape[0]):
                term = pr[s].astype(f32)
                g = term if g is None else g + term
        delta, nm, nv = _adam_update(g, w_ref[...], m_ref[...], v_ref[...])
        g_ref[...] = g
        d_ref[...] = delta
        nm_ref[...] = nm
        nv_ref[...] = nv

    spec = pl.BlockSpec((tr, cols), lambda i: (i, 0))
    part_specs = [pl.BlockSpec((p.shape[0], tr, cols), lambda i: (0, i, 0)) for p in parts]
    out = jax.ShapeDtypeStruct((rows, cols), f32)
    return pl.pallas_call(
        body, name=name, grid=(rows // tr,), in_specs=part_specs + [spec, spec, spec],
        out_specs=[spec, spec, spec, spec], out_shape=[out, out, out, out],
        compiler_params=_cparams(("parallel",)),
    )(*parts, w, m, v)


def _gathered_to_full(gathered, axis):
    s = gathered.shape[1:]
    full = jnp.moveaxis(gathered, 0, axis)
    return full.reshape(s[:axis] + (N_DEV * s[axis],) + s[axis + 1:])


def _interleave_halves(w, ct):
    n = w.shape[-1] // 2
    t = w.reshape(w.shape[:-1] + (2, n // ct, ct))
    return jnp.swapaxes(t, -3, -2).reshape(w.shape)


def _deinterleave_halves(w, ct):
    n = w.shape[-1] // 2
    t = w.reshape(w.shape[:-1] + (n // ct, 2, ct))
    return jnp.swapaxes(t, -3, -2).reshape(w.shape)


def _rot_cols(w):
    half = QK_ROPE_DIM // 2
    return jnp.concatenate([-w[..., half:], w[..., :half]], axis=-1)


def _rot_cols_t(dw):
    half = QK_ROPE_DIM // 2
    return jnp.concatenate([dw[..., half:], -dw[..., :half]], axis=-1)


def kernel(x, positions, ln_in_g, ln_in_b, w_in, q_norm_g, w_uq, kv_norm_g, w_ukv, conv_w, conv_b, conv_ln_g, conv_ln_b, w_pool, pool_scale, w_out, ln1_g, ln1_b, w_up, ffn_conv_w, ffn_conv_b, w_down, ln2_g, ln2_b, loss_target, m_ln_in_g, m_ln_in_b, m_w_in, m_q_norm_g, m_w_uq, m_kv_norm_g, m_w_ukv, m_conv_w, m_conv_b, m_conv_ln_g, m_conv_ln_b, m_w_pool, m_pool_scale, m_w_out, m_ln1_g, m_ln1_b, m_w_up, m_ffn_conv_w, m_ffn_conv_b, m_w_down, m_ln2_g, m_ln2_b, v_ln_in_g, v_ln_in_b, v_w_in, v_q_norm_g, v_w_uq, v_kv_norm_g, v_w_ukv, v_conv_w, v_conv_b, v_conv_ln_g, v_conv_ln_b, v_w_pool, v_pool_scale, v_w_out, v_ln1_g, v_ln1_b, v_w_up, v_ffn_conv_w, v_ffn_conv_b, v_w_down, v_ln2_g, v_ln2_b):
    weights = dict(ln_in_g=ln_in_g, ln_in_b=ln_in_b, w_in=w_in, q_norm_g=q_norm_g, w_uq=w_uq, kv_norm_g=kv_norm_g,
                   w_ukv=w_ukv, conv_w=conv_w, conv_b=conv_b, conv_ln_g=conv_ln_g, conv_ln_b=conv_ln_b, w_pool=w_pool,
                   pool_scale=pool_scale, w_out=w_out, ln1_g=ln1_g, ln1_b=ln1_b, w_up=w_up, ffn_conv_w=ffn_conv_w,
                   ffn_conv_b=ffn_conv_b, w_down=w_down, ln2_g=ln2_g, ln2_b=ln2_b)
    mom1 = dict(ln_in_g=m_ln_in_g, ln_in_b=m_ln_in_b, w_in=m_w_in, q_norm_g=m_q_norm_g, w_uq=m_w_uq,
                kv_norm_g=m_kv_norm_g, w_ukv=m_w_ukv, conv_w=m_conv_w, conv_b=m_conv_b, conv_ln_g=m_conv_ln_g,
                conv_ln_b=m_conv_ln_b, w_pool=m_w_pool, pool_scale=m_pool_scale, w_out=m_w_out, ln1_g=m_ln1_g,
                ln1_b=m_ln1_b, w_up=m_w_up, ffn_conv_w=m_ffn_conv_w, ffn_conv_b=m_ffn_conv_b, w_down=m_w_down,
                ln2_g=m_ln2_g, ln2_b=m_ln2_b)
    mom2 = dict(ln_in_g=v_ln_in_g, ln_in_b=v_ln_in_b, w_in=v_w_in, q_norm_g=v_q_norm_g, w_uq=v_w_uq,
                kv_norm_g=v_kv_norm_g, w_ukv=v_w_ukv, conv_w=v_conv_w, conv_b=v_conv_b, conv_ln_g=v_conv_ln_g,
                conv_ln_b=v_conv_ln_b, w_pool=v_w_pool, pool_scale=v_pool_scale, w_out=v_w_out, ln1_g=v_ln1_g,
                ln1_b=v_ln1_b, w_up=v_w_up, ffn_conv_w=v_ffn_conv_w, ffn_conv_b=v_ffn_conv_b, w_down=v_w_down,
                ln2_g=v_ln2_g, ln2_b=v_ln2_b)
    names = list(weights)

    nb, seq, d = x.shape
    t = nb * seq
    depth = w_in.shape[0]
    ql, kvl, cw, pw = q_norm_g.shape[1], kv_norm_g.shape[1], conv_b.shape[1], pool_scale.shape[1]
    pg = w_pool.shape[-1]
    heads = w_uq.shape[2]
    dff = w_down.shape[1] * N_DEV
    alpha = (2.0 * depth) ** 0.25
    scale = float(QK_NOPE_DIM + QK_ROPE_DIM) ** -0.5
    lay = dict(ql=ql, kvl=kvl, pw=pw, pg=pg, off_q=2 * cw, off_pool=2 * cw + ql, off_kv=2 * cw + ql + pw,
               off_kr=2 * cw + ql + pw + kvl)
    o1, o2, o3, o4 = ql, ql + kvl, ql + kvl + QK_ROPE_DIM, ql + kvl + QK_ROPE_DIM + 2 * cw
    my_x, my_y, my_c = lax.axis_index("x"), lax.axis_index("y"), lax.axis_index("c")
    my_dev = 4 * my_x + 2 * my_y + my_c

    big = ("w_in", "w_uq", "w_ukv", "w_out", "w_up", "w_down")
    g_conv, g_ffn = _all_gather([conv_w, ffn_conv_w], "ag_conv_taps")
    conv_w_full, ffn_w_full = _gathered_to_full(g_conv, 2), _gathered_to_full(g_ffn, 2)
    w_pool_2d = w_pool.reshape(depth, pw, pg)

    rest = ("w_in", "w_uq", "w_ukv", "w_out")

    def bf16_shards(l, which):
        return [weights[n][l].astype(bf16) for n in which]

    def small_weights(gathered):
        g_in, g_uq, g_ukv, g_out = (gathered[n] for n in rest)
        wi = _gathered_to_full(g_in, 1)
        kr_cols = wi[:, o2:o3]
        w_in_pad = jnp.concatenate([_interleave_halves(wi[:, o3:o4], SEQ_CT), wi[:, :o1], wi[:, o4:], wi[:, o1:o2],
                                    kr_cols, _rot_cols(kr_cols)], axis=-1)
        wq = g_uq.reshape(ql, heads, QK_NOPE_DIM + QK_ROPE_DIM)
        w_uq_ext = jnp.concatenate([wq, _rot_cols(wq[..., QK_NOPE_DIM:])], axis=-1).reshape(ql, heads * 2 * LANES)
        return w_in_pad, w_uq_ext, g_ukv.reshape(kvl, heads * 2 * LANES), g_out.reshape(-1, d)

    half = QK_ROPE_DIM // 2
    inv = 1.0 / (ROPE_THETA ** (jnp.arange(0, QK_ROPE_DIM, 2, dtype=f32) / QK_ROPE_DIM))
    inv_lanes = jnp.tile(inv, LANES // half).reshape(1, LANES)
    cs, qt = _rope_tables(positions.reshape(t, 1), inv_lanes, scale)

    x2 = x.reshape(t, d)
    xs, xs_bf = _ln_fwd(x2, None, ln_in_g.reshape(1, d), ln_in_b.reshape(1, d), 1.0, "ln_in_fwd")
    saved = []
    in_pad = o4 + pw + QK_ROPE_DIM
    gathered = dict(zip(rest, _all_gather(bf16_shards(0, rest), "ag_weights")))
    for l in range(depth):
        nxt = l + 1 < depth
        gq, gkv = q_norm_g[l].reshape(1, ql), kv_norm_g[l].reshape(1, kvl)
        w_in_l, w_uq_l, w_ukv_l, w_out_l = small_weights(gathered)
        h = _matmul(xs_bf, w_in_l, "nn", f32, "mm_in", tm=512, tn=in_pad)
        qn, kvn, krd = _prep_fwd(h, cs, gq, gkv, lay)
        q_ext = _matmul(qn, w_uq_l, "nn", f32, "mm_uq", tn=2048)
        kv = _matmul(kvn, w_ukv_l, "nn", bf16, "mm_ukv", tn=2048)
        riders = ("w_down", "w_up") if l == 0 else ("w_down",)
        y_mla, lse, g_half = _attn_fwd(q_ext, qt, kv, krd, nb, seq, heads, comm=_gather_own(bf16_shards(l, riders)))
        hconv = _conv_fwd(h, conv_w_full[l], conv_b[l].reshape(1, cw), nb, seq, cw)
        y_conv = _convln_fwd(hconv, conv_ln_g[l].reshape(1, cw), conv_ln_b[l].reshape(1, cw))
        y_pool = _pool_fwd(h, w_pool_2d[l], pool_scale[l].reshape(1, pw), lay, nb, seq)
        mixed = jnp.concatenate([y_mla, y_conv, y_pool], axis=-1)
        y1, g_full = _matmul(mixed, w_out_l, "nn", f32, "mm_out", comm=_gather_pass(g_half))
        gathered.update(zip(riders, g_full))
        wl = dict(w_in=w_in_l, w_uq=w_uq_l, w_ukv=w_ukv_l, w_out=w_out_l, w_up=gathered["w_up"],
                  w_down=gathered["w_down"].reshape(dff, d))
        gathered = {}
        x1, x1_bf = _ln_fwd(xs, y1, ln1_g[l].reshape(1, d), ln1_b[l].reshape(1, d), alpha, "ln1_fwd")
        if nxt:
            up, (g_up,) = _matmul_up(x1_bf, wl["w_up"], "mm_up", comm=_gather_own(bf16_shards(l + 1, ("w_up",))))
            act, g_rest = _ffn_act_fwd(up, ffn_w_full[l], ffn_conv_b[l].reshape(1, 2 * dff), nb, seq, dff,
                                       comm=_gather_own(bf16_shards(l + 1, rest[1:])))
            y2, (gathered["w_up"], g_in) = _matmul(
                act, wl["w_down"], "nn", f32, "mm_down", tk=dff // 2,
                comm=_comm_join(_gather_pass([g_up]), _gather_own(bf16_shards(l + 1, rest[:1]))))
            xn, xn_bf, g_rest = _ln_fwd(x1, y2, ln2_g[l].reshape(1, d), ln2_b[l].reshape(1, d), alpha, "ln2_fwd",
                                        comm=_gather_pass([g_in] + g_rest))
            gathered.update(zip(rest, g_rest))
        else:
            up = _matmul_up(x1_bf, wl["w_up"], "mm_up")
            act = _ffn_act_fwd(up, ffn_w_full[l], ffn_conv_b[l].reshape(1, 2 * dff), nb, seq, dff)
            y2 = _matmul(act, wl["w_down"], "nn", f32, "mm_down", tk=dff // 2)
            xn, xn_bf = _ln_fwd(x1, y2, ln2_g[l].reshape(1, d), ln2_b[l].reshape(1, d), alpha, "ln2_fwd")
        saved.append(dict(xs=xs, xs_bf=xs_bf, h=h, qn=qn, kvn=kvn, krd=krd, q_ext=q_ext, kv=kv, lse=lse, hconv=hconv,
                          mixed=mixed, y1=y1, x1=x1, x1_bf=x1_bf, up=up, act=act, y2=y2, wl=wl))
        xs, xs_bf = xn, xn_bf

    d_stream, loss_row = _loss_call(xs, loss_target.reshape(t, d))
    loss = lax.psum(loss_row[0, 0], MESH_AXES)

    gw = {n: [None] * depth for n in names if n not in ("ln_in_g", "ln_in_b")}
    small = [n for n in gw if n not in big]
    rs_own = {n: [None] * depth for n in big}
    rs_chips = {n: [None] * depth for n in big}
    core_idx = jnp.reshape(my_c, (1,)).astype(jnp.int32)
    chip_idx = jnp.reshape(2 * my_x + my_y, (1,)).astype(jnp.int32)

    def pair_add(l, which, slots, theirs):
        p4s = [_pair_add(slots[n], th, core_idx, "rs_add_" + n) for n, th in zip(which, theirs)]
        for n, p4 in zip(which, p4s):
            rs_own[n][l] = p4
        return p4s

    d_res, d_mm = None, d_stream
    pending = None
    for l in reversed(range(depth)):
        sv = saved[l]
        wl = sv["wl"]
        slots = {}
        gq, gkv = q_norm_g[l].reshape(1, ql), kv_norm_g[l].reshape(1, kvl)
        dz2, dz2_bf, gw["ln2_g"][l], gw["ln2_b"][l] = _ln_bwd(
            d_res, d_mm, sv["x1"], sv["y2"], ln2_g[l].reshape(1, d), ln2_b[l].reshape(1, d), alpha, "ln2_bwd")
        if pending is None:
            dw_down = _matmul(sv["act"], dz2_bf, "tn", bf16, "mm_down_dw", tm=dff // 4)
        else:
            dw_down, theirs = _matmul(sv["act"], dz2_bf, "tn", bf16, "mm_down_dw", tm=dff // 4,
                                      comm=_sibling_swap([pending["slots"][n] for n in rest]))
            rest_p4s = pair_add(pending["layer"], rest, pending["slots"], theirs)
        slots["w_down"] = dw_down.reshape(N_DEV, -1, d)
        d_act, theirs = _matmul(dz2_bf, wl["w_down"], "nt", f32, "mm_down_dx", tn=dff // 4,
                                comm=_sibling_swap([slots["w_down"]]))
        down_p4s = pair_add(l, ("w_down",), slots, theirs)
        ffn_args = (sv["up"], ffn_w_full[l], ffn_conv_b[l].reshape(1, 2 * dff), d_act, nb, seq, dff)
        if pending is None:
            d_up, dffw, dffb = _ffn_act_bwd(*ffn_args)
        else:
            d_up, dffw, dffb, (rs_chips["w_up"][pending["layer"]],) = _ffn_act_bwd(
                *ffn_args, comm=_chip_swap(pending["up_p4s"], peers=(2,), into=pending["up_partial"]))
        gw["ffn_conv_w"][l] = jnp.concatenate([dffw[0], dffw[1]], axis=-1)
        gw["ffn_conv_b"][l] = dffb.reshape(2 * dff)
        d_x1, (rs_chips["w_down"][l],) = _matmul_up_dx(d_up, wl["w_up"], "mm_up_dx", comm=_chip_swap(down_p4s))
        if pending is None:
            slots["w_up"] = _matmul_up_dw(sv["x1_bf"], d_up, N_DEV, "mm_up_dw")
        else:
            slots["w_up"], from_chips = _matmul_up_dw(sv["x1_bf"], d_up, N_DEV, "mm_up_dw", comm=_chip_swap(rest_p4s))
            for n, fc in zip(rest, from_chips):
                rs_chips[n][pending["layer"]] = fc
        dz1, dz1_bf, gw["ln1_g"][l], gw["ln1_b"][l] = _ln_bwd(
            dz2, d_x1, sv["xs"], sv["y1"], ln1_g[l].reshape(1, d), ln1_b[l].reshape(1, d), alpha, "ln1_bwd")
        d_mixed, theirs = _matmul(dz1_bf, wl["w_out"], "nt", f32, "mm_out_dx", comm=_sibling_swap([slots["w_up"]]))
        up_p4s = pair_add(l, ("w_up",), slots, theirs)
        slots["w_out"] = _matmul(sv["mixed"], dz1_bf, "tn", bf16, "mm_out_dw").reshape(N_DEV, -1, d)
        d_upool, dwp, dps = _pool_bwd(sv["h"], w_pool_2d[l], pool_scale[l].reshape(1, pw), d_mixed,
                                      (heads * LANES + cw) // pw, lay, nb, seq)
        gw["w_pool"][l] = dwp.reshape(w_pool.shape[1:])
        gw["pool_scale"][l] = dps.reshape(pw)
        d_hconv, dclg, dclb = _convln_bwd(sv["hconv"], conv_ln_g[l].reshape(1, cw), conv_ln_b[l].reshape(1, cw), d_mixed,
                                          heads * LANES // cw)
        gw["conv_ln_g"][l], gw["conv_ln_b"][l] = dclg.reshape(cw), dclb.reshape(cw)
        d_conv, gw["conv_w"][l], dcb = _conv_bwd(sv["h"], conv_w_full[l], d_hconv, nb, seq, cw)
        gw["conv_b"][l] = dcb.reshape(cw)
        dq_ext, dkv, dkrd, up_partial = _attn_bwd(sv["q_ext"], qt, sv["kv"], sv["krd"], sv["lse"], d_mixed, nb, seq, heads,
                                                  comm=_chip_swap(up_p4s, peers=(0, 1, 2) if l == 0 else (0, 1)))
        if l == 0:
            rs_chips["w_up"][l] = up_partial[0]
        d_qn = _matmul(dq_ext, wl["w_uq"], "nt", f32, "mm_uq_dx")
        dwq = _matmul(sv["qn"], dq_ext, "tn", f32, "mm_uq_dw", tn=2048, tk=1024).reshape(ql, heads, 2 * LANES)
        dwq_rope = dwq[..., QK_NOPE_DIM:QK_NOPE_DIM + QK_ROPE_DIM] + _rot_cols_t(dwq[..., QK_NOPE_DIM + QK_ROPE_DIM:])
        slots["w_uq"] = jnp.concatenate([dwq[..., :QK_NOPE_DIM], dwq_rope], axis=-1).astype(bf16).reshape(
            N_DEV, -1, QK_NOPE_DIM + QK_ROPE_DIM)
        d_kvn = _matmul(dkv, wl["w_ukv"], "nt", f32, "mm_ukv_dx")
        slots["w_ukv"] = _matmul(sv["kvn"], dkv, "tn", bf16, "mm_ukv_dw", tn=2048, tk=1024).reshape(N_DEV, -1, 2 * LANES)
        d_cq, d_ckv, d_kr, dgq, dgkv = _prep_bwd(sv["h"], cs, gq, gkv, d_qn, d_kvn, dkrd, lay)
        gw["q_norm_g"][l], gw["kv_norm_g"][l] = dgq.reshape(ql), dgkv.reshape(kvl)
        d_h = jnp.concatenate([d_conv, d_cq, d_upool, d_ckv, d_kr], axis=-1)
        if l > 0:
            d_xs = _matmul(d_h, wl["w_in"], "nt", f32, "mm_in_dx", tm=512, tk=in_pad)
            dwi = _matmul(sv["xs_bf"], d_h, "tn", f32, "mm_in_dw", tn=in_pad, tk=1024)
        else:
            flat_small = jnp.concatenate([jnp.stack(gw[n]).astype(f32).reshape(-1) for n in small]).reshape(-1, LANES)
            d_xs, small_half = _matmul(d_h, wl["w_in"], "nt", f32, "mm_in_dx", tm=512, tk=in_pad,
                                       comm=_gather_own([flat_small]))
            dwi, (gathered_small,) = _matmul(sv["xs_bf"], d_h, "tn", f32, "mm_in_dw", tn=in_pad, tk=1024,
                                             comm=_gather_pass(small_half))
        dkr_cols = dwi[:, lay["off_kr"]:lay["off_kr"] + QK_ROPE_DIM] + _rot_cols_t(dwi[:, lay["off_kr"] + QK_ROPE_DIM:])
        dwi_nat = jnp.concatenate(
            [dwi[:, lay["off_q"]:lay["off_q"] + ql], dwi[:, lay["off_kv"]:lay["off_kv"] + kvl], dkr_cols,
             _deinterleave_halves(dwi[:, :2 * cw], SEQ_CT), dwi[:, lay["off_pool"]:lay["off_pool"] + pw]],
            axis=-1).astype(bf16)
        slots["w_in"] = jnp.moveaxis(dwi_nat.reshape(d, N_DEV, -1), 1, 0)
        pending = dict(layer=l, slots=slots, up_p4s=up_p4s, up_partial=up_partial)
        d_res, d_mm = dz1, d_xs

    grad_x, _, d_ln_in_g, d_ln_in_b, theirs = _ln_bwd(
        d_res, d_mm, x2, None, ln_in_g.reshape(1, d), ln_in_b.reshape(1, d), alpha, "ln_in_bwd",
        comm=_sibling_swap([pending["slots"][n] for n in rest]))
    grad_x = grad_x.reshape(x.shape)
    rest_p4s = pair_add(pending["layer"], rest, pending["slots"], theirs)

    grads, deltas, new_m, new_v = {}, {}, {}, {}

    def finish(n, parts):
        shp = weights[n].shape
        rows = math.prod(shp[:-1]) if len(shp) > 1 else 1
        as2d = lambda a: a.reshape(rows, shp[-1])
        parts = [p.reshape(p.shape[0], rows, shp[-1]) for p in parts]
        g, dl, nm, nv = _adamw(parts, as2d(weights[n]), as2d(mom1[n]), as2d(mom2[n]), "adamw_" + n)
        grads[n], deltas[n], new_m[n], new_v[n] = (a.reshape(shp) for a in (g, dl, nm, nv))

    def finish_big(n, comm=None):
        shp = weights[n].shape
        as2d = lambda a: a.reshape(-1, shp[-1])
        res = _adamw_layers(rs_own[n], rs_chips[n], chip_idx, as2d(weights[n]), as2d(mom1[n]), as2d(mom2[n]),
                            "adamw_" + n, comm)
        outs, comm_outs = res if comm is not None else (res, None)
        grads[n], deltas[n], new_m[n], new_v[n] = (a.reshape(shp) for a in outs)
        return comm_outs

    ln_in = ("ln_in_g", "ln_in_b")
    flat_ln = jnp.concatenate([d_ln_in_g.reshape(-1), d_ln_in_b.reshape(-1)]).reshape(-1, LANES)
    gathered_ln = _all_gather([flat_ln], "ag_ln_in_grads")[0].reshape(N_DEV, 2, d)
    finish_big("w_up")
    from_chips = finish_big("w_down", _chip_swap(rest_p4s))
    for n, fc in zip(rest, from_chips):
        rs_chips[n][pending["layer"]] = fc
    for n in rest:
        finish_big(n)
    for i, n in enumerate(ln_in):
        finish(n, [gathered_ln[:, i]])
    gathered_small = gathered_small.reshape(N_DEV, -1)
    off = 0
    for n in small:
        shape = (depth,) + gw[n][0].shape
        size = math.prod(shape)
        part = gathered_small[:, off:off + size].reshape((N_DEV,) + shape)
        off += size
        if n in ("conv_w", "ffn_conv_w"):
            width = weights[n].shape[-1]
            part = lax.dynamic_slice_in_dim(part, my_dev * width, width, axis=part.ndim - 1)
        finish(n, [part])

    return (loss, grad_x, *[grads[n] for n in names], *[deltas[n] for n in names], *[new_m[n] for n in names],
            *[new_v[n] for n in names])
```

```python
import functools
import math

import jax
import jax.numpy as jnp
from jax import lax
from jax.experimental import pallas as pl
from jax.experimental.pallas import tpu as pltpu

f32 = jnp.float32
bf16 = jnp.bfloat16

QK_NOPE_DIM = 128
QK_ROPE_DIM = 64
V_HEAD_DIM = 128
CONV_KERNEL = 31
FFN_CONV_KERNEL = 3
POOL_WINDOWS = (2, 4, 8, 16)
ROPE_THETA = 10000.0
LN_EPS = 1e-5
RMS_EPS = 1e-6
ADAM_LR = 0.001
ADAM_B1 = 0.9
ADAM_B2 = 0.999
ADAM_EPS = 1e-08
ADAM_WD = 0.01
ADAM_STEP = 10

N_DEV = 8
MESH_AXES = ("x", "y", "c")
V7X_VMEM_LIMIT_BYTES = 56 * 1024 * 1024
LANES = 128
NEG_INF = -1e30
MESH = pl.DeviceIdType.MESH


def _cparams(sem):
    return pltpu.CompilerParams(dimension_semantics=sem, vmem_limit_bytes=V7X_VMEM_LIMIT_BYTES)


def _tile(dim, pref):
    t = pref
    while t >= LANES:
        if dim % t == 0:
            return t
        t //= 2
    return dim


_ANY = pl.BlockSpec(memory_space=pl.ANY)


class _Comm:
    def __init__(self, inputs, out_shapes, sems, start, finish, aliases=None):
        self.inputs, self.out_shapes, self.sems = list(inputs), list(out_shapes), list(sems)
        self.start, self.finish, self.aliases = start, finish, dict(aliases or {})


def _call(body, name, grid, in_specs, out_specs, out_shape, args, scratch=(), sem=None, comm=None, prefetch=()):
    in_specs, out_specs, out_shape, scratch = list(in_specs), list(out_specs), list(out_shape), list(scratch)
    n_pre, n_in, n_out, n_scr = len(prefetch), len(in_specs), len(out_specs), len(scratch)
    c_in, c_out = (len(comm.inputs), len(comm.out_shapes)) if comm else (0, 0)

    def carrier(*refs):
        refs = list(refs)
        pre, refs = refs[:n_pre], refs[n_pre:]
        ins, refs = refs[:n_in], refs[n_in:]
        c_ins, refs = refs[:c_in], refs[c_in:]
        outs, refs = refs[:n_out], refs[n_out:]
        c_outs, refs = refs[:c_out], refs[c_out:]
        scr, c_sems = refs[:n_scr], refs[n_scr:]
        ids = [pl.program_id(a) for a in range(len(grid))]
        first = functools.reduce(lambda p, q: p & q, [i == 0 for i in ids])
        last = functools.reduce(lambda p, q: p & q, [i == g - 1 for i, g in zip(ids, grid)])
        pl.when(first)(lambda: comm.start(c_ins, c_outs, c_sems))
        body(*pre, *ins, *outs, *scr)
        pl.when(last)(lambda: comm.finish(c_ins, c_outs, c_sems))

    grid_spec = pltpu.PrefetchScalarGridSpec(
        num_scalar_prefetch=n_pre, grid=grid, in_specs=in_specs + [_ANY] * c_in, out_specs=out_specs + [_ANY] * c_out,
        scratch_shapes=scratch + (comm.sems if comm else []))
    if comm is None:
        outs = pl.pallas_call(body, name=name, grid_spec=grid_spec, out_shape=out_shape,
                              compiler_params=_cparams(sem))(*prefetch, *args)
        return list(outs), []
    outs = pl.pallas_call(
        carrier, name=name, grid_spec=grid_spec, out_shape=out_shape + comm.out_shapes,
        input_output_aliases={n_pre + n_in + a: n_out + b for a, b in comm.aliases.items()},
        compiler_params=_cparams(("arbitrary",) * len(grid)),
    )(*prefetch, *args, *comm.inputs)
    return list(outs[:n_out]), list(outs[n_out:])


def _shift_down_raw(x, k):
    if k == 0:
        return x
    row = lax.broadcasted_iota(jnp.int32, x.shape, 0)
    return jnp.where(row >= k, pltpu.roll(x, k, axis=0), 0.0)


def _shift_up_raw(x, k):
    if k == 0:
        return x
    n = x.shape[0]
    row = lax.broadcasted_iota(jnp.int32, x.shape, 0)
    return jnp.where(row < n - k, pltpu.roll(x, n - k, axis=0), 0.0)


@functools.partial(jax.custom_vjp, nondiff_argnums=(1,))
def _shift_down(x, k):
    return _shift_down_raw(x, k)


def _shift_down_fwd(x, k):
    return _shift_down_raw(x, k), None


def _shift_down_bwd(k, _, g):
    return (_shift_up_raw(g, k),)


_shift_down.defvjp(_shift_down_fwd, _shift_down_bwd)


@jax.custom_vjp
def _dup_halves(p):
    return p + pltpu.roll(p, LANES // 2, axis=1)


def _dup_halves_fwd(p):
    return p + pltpu.roll(p, LANES // 2, axis=1), None


def _dup_halves_bwd(_, g):
    return (g + pltpu.roll(g, LANES // 2, axis=1),)


_dup_halves.defvjp(_dup_halves_fwd, _dup_halves_bwd)

_NN = (((1,), (0,)), ((), ()))
_NT = (((1,), (1,)), ((), ()))
_TN = (((0,), (0,)), ((), ()))


def _dot(a, b, dims):
    return lax.dot_general(a.astype(bf16), b.astype(bf16), dims, preferred_element_type=f32)


@jax.custom_vjp
def _mm_bf16(a, b):
    return _dot(a, b, _NN)


def _mm_bf16_fwd(a, b):
    return _dot(a, b, _NN), (a, b)


def _mm_bf16_bwd(res, g):
    a, b = res
    return _dot(g, b, _NT), _dot(a, g, _TN)


_mm_bf16.defvjp(_mm_bf16_fwd, _mm_bf16_bwd)


def _layer_norm(z, g, b):
    mu = jnp.mean(z, axis=-1, keepdims=True)
    var = jnp.mean(jnp.square(z - mu), axis=-1, keepdims=True)
    return (z - mu) * lax.rsqrt(var + LN_EPS) * g + b


def _rms_norm(x, g):
    ms = jnp.mean(jnp.square(x), axis=-1, keepdims=True)
    return x * lax.rsqrt(ms + RMS_EPS) * g


def _colsum(x):
    return jnp.sum(x, axis=0, keepdims=True)


def _matmul_core(a, b, mode, grid, a_spec, b_spec, o_spec, o_shape, tile, out_dtype, name, comm=None):
    nk = grid[2]
    dims = {"nn": _NN, "nt": _NT, "tn": _TN}[mode]
    acc_in_out = out_dtype == f32

    def body(a_ref, b_ref, o_ref, *scratch):
        def prod():
            return _dot(a_ref[...], b_ref[...], dims)

        if nk == 1:
            o_ref[...] = prod().astype(out_dtype)
            return
        acc_ref = o_ref if acc_in_out else scratch[0]
        kk = pl.program_id(2)

        @pl.when(kk == 0)
        def _():
            acc_ref[...] = prod()

        if acc_in_out:
            @pl.when(kk > 0)
            def _():
                acc_ref[...] += prod()
        else:
            @pl.when((kk > 0) & (kk < nk - 1))
            def _():
                acc_ref[...] += prod()

            @pl.when(kk == nk - 1)
            def _():
                o_ref[...] = (acc_ref[...] + prod()).astype(out_dtype)

    scratch = [] if (nk == 1 or acc_in_out) else [pltpu.VMEM(tile, f32)]
    (out,), comm_outs = _call(body, name, grid, [a_spec, b_spec], [o_spec], [jax.ShapeDtypeStruct(o_shape, out_dtype)],
                              (a, b), scratch, ("parallel", "parallel", "arbitrary"), comm)
    return out if comm is None else (out, comm_outs)


def _matmul(a, b, mode, out_dtype, name, tm=1024, tn=1024, tk=2048, comm=None):
    if mode == "nn":
        (m, k), (k2, n) = a.shape, b.shape
    elif mode == "nt":
        (m, k), (n, k2) = a.shape, b.shape
    else:
        (k, m), (k2, n) = a.shape, b.shape
    assert k == k2, (name, a.shape, b.shape)
    tm, tn, tk = _tile(m, tm), _tile(n, tn), _tile(k, tk)
    a_spec = pl.BlockSpec((tk, tm), lambda i, j, kk: (kk, i)) if mode == "tn" else pl.BlockSpec((tm, tk), lambda i, j, kk: (i, kk))
    b_spec = pl.BlockSpec((tn, tk), lambda i, j, kk: (j, kk)) if mode == "nt" else pl.BlockSpec((tk, tn), lambda i, j, kk: (kk, j))
    return _matmul_core(a, b, mode, (m // tm, n // tn, k // tk), a_spec, b_spec,
                        pl.BlockSpec((tm, tn), lambda i, j, kk: (i, j)), (m, n), (tm, tn), out_dtype, name, comm)


def _matmul_up(x_bf, w_slots, name, tm=1024, comm=None):
    m, k = x_bf.shape
    s, _, ns = w_slots.shape
    tm = _tile(m, tm)
    return _matmul_core(x_bf, w_slots, "nn", (m // tm, s, 1), pl.BlockSpec((tm, k), lambda i, j, kk: (i, 0)),
                        pl.BlockSpec((None, k, ns), lambda i, j, kk: (j, 0, 0)),
                        pl.BlockSpec((tm, ns), lambda i, j, kk: (i, j)), (m, s * ns), (tm, ns), f32, name, comm)


def _matmul_up_dx(d3, w_slots, name, tm=1024, tn=1024, comm=None):
    _, m, half = d3.shape
    s, n, ns = w_slots.shape
    per_half = half // ns
    assert 2 * per_half == s, (d3.shape, w_slots.shape)
    tm, tn = _tile(m, tm), _tile(n, tn)
    return _matmul_core(d3, w_slots, "nt", (m // tm, n // tn, s),
                        pl.BlockSpec((None, tm, ns), lambda i, j, kk: (kk // per_half, i, kk % per_half)),
                        pl.BlockSpec((None, tn, ns), lambda i, j, kk: (kk, j, 0)),
                        pl.BlockSpec((tm, tn), lambda i, j, kk: (i, j)), (m, n), (tm, tn), f32, name, comm)


def _matmul_up_dw(x_bf, d3, n_slots, name, tm=1024, tk=2048, comm=None):
    k, m = x_bf.shape
    _, _, half = d3.shape
    ns = 2 * half // n_slots
    per_half = n_slots // 2
    tm, tk = _tile(m, tm), _tile(k, tk)
    return _matmul_core(x_bf, d3, "tn", (m // tm, n_slots, k // tk), pl.BlockSpec((tk, tm), lambda i, j, kk: (kk, i)),
                        pl.BlockSpec((None, tk, ns), lambda i, j, kk: (j // per_half, kk, j % per_half)),
                        pl.BlockSpec((None, tm, ns), lambda i, j, kk: (j, i, 0)), (n_slots, m, ns), (tm, ns), bf16, name,
                        comm)


ROW_TILE = 256


def _rows(width, col_block=0):
    return pl.BlockSpec((ROW_TILE, width), lambda i, cb=col_block: (i, cb))


def _whole(shape):
    return pl.BlockSpec(shape, lambda i: (0,) * len(shape))


def _ln_fwd(x, y, g, b, alpha, name, comm=None):
    t, d = x.shape

    def body(*refs):
        if y is None:
            x_ref, g_ref, b_ref, o_ref, ob_ref = refs
            z = x_ref[...]
        else:
            x_ref, y_ref, g_ref, b_ref, o_ref, ob_ref = refs
            z = alpha * x_ref[...] + y_ref[...]
        out = _layer_norm(z, g_ref[...], b_ref[...])
        o_ref[...] = out
        ob_ref[...] = out.astype(bf16)

    ins = [x] + ([] if y is None else [y]) + [g, b]
    specs = [_rows(d)] + ([] if y is None else [_rows(d)]) + [_whole((1, d)), _whole((1, d))]
    (out, out_bf), comm_outs = _call(
        body, name, (t // ROW_TILE,), specs, [_rows(d), _rows(d)],
        [jax.ShapeDtypeStruct((t, d), f32), jax.ShapeDtypeStruct((t, d), bf16)], ins, (), ("parallel",), comm)
    return (out, out_bf) if comm is None else (out, out_bf, comm_outs)


def _ln_bwd(d_res, d_mm, x, y, g, b, alpha, name, comm=None):
    t, d = x.shape
    has_res, has_mm, has_y = d_res is not None, d_mm is not None, y is not None

    def body(*refs):
        refs = list(refs)
        d_res_ref = refs.pop(0) if has_res else None
        d_mm_ref = refs.pop(0) if has_mm else None
        x_ref = refs.pop(0)
        y_ref = refs.pop(0) if has_y else None
        g_ref, b_ref, dz_ref, dzb_ref, dg_ref, db_ref = refs
        ct = None
        if has_res:
            ct = alpha * d_res_ref[...]
        if has_mm:
            ct = d_mm_ref[...] if ct is None else ct + d_mm_ref[...]
        z = x_ref[...] if not has_y else alpha * x_ref[...] + y_ref[...]
        _, vjp = jax.vjp(_layer_norm, z, g_ref[...], b_ref[...])
        dz, dg, db = vjp(ct)
        dz_ref[...] = dz
        dzb_ref[...] = dz.astype(bf16)

        @pl.when(pl.program_id(0) == 0)
        def _():
            dg_ref[...] = jnp.zeros_like(dg_ref)
            db_ref[...] = jnp.zeros_like(db_ref)

        dg_ref[...] += dg
        db_ref[...] += db

    ins = [a for a in (d_res, d_mm, x, y) if a is not None] + [g, b]
    specs = [_rows(d) for a in (d_res, d_mm, x, y) if a is not None] + [_whole((1, d)), _whole((1, d))]
    outs, comm_outs = _call(
        body, name, (t // ROW_TILE,), specs, [_rows(d), _rows(d), _whole((1, d)), _whole((1, d))],
        [jax.ShapeDtypeStruct((t, d), f32), jax.ShapeDtypeStruct((t, d), bf16),
         jax.ShapeDtypeStruct((1, d), f32), jax.ShapeDtypeStruct((1, d), f32)], ins, (), ("arbitrary",), comm)
    return tuple(outs) if comm is None else (*outs, comm_outs)


def _loss_call(xf, target):
    t, d = xf.shape

    def body(x_ref, t_ref, dx_ref, loss_ref):
        err = x_ref[...] - t_ref[...]
        dx_ref[...] = err * (1.0 / d)

        @pl.when(pl.program_id(0) == 0)
        def _():
            loss_ref[...] = jnp.zeros_like(loss_ref)

        part = 0.5 * jnp.sum(jnp.mean(jnp.square(err), axis=-1, keepdims=True), axis=0, keepdims=True)
        loss_ref[...] += jnp.broadcast_to(part, loss_ref.shape)

    return pl.pallas_call(
        body, name="loss_head", grid=(t // ROW_TILE,), in_specs=[_rows(d), _rows(d)],
        out_specs=[_rows(d), _whole((1, LANES))],
        out_shape=[jax.ShapeDtypeStruct((t, d), f32), jax.ShapeDtypeStruct((1, LANES), f32)],
        compiler_params=_cparams(("arbitrary",)),
    )(xf, target)


def _rope_tables(pos, inv, scale):
    t = pos.shape[0]

    def body(pos_ref, inv_ref, cs_ref, qt_ref):
        ang = pos_ref[...].astype(f32) * inv_ref[...]
        lane = lax.broadcasted_iota(jnp.int32, ang.shape, 1)
        cs = jnp.where(lane < LANES // 2, jnp.cos(ang), jnp.sin(ang))
        cs_ref[...] = cs
        qt_ref[:, :LANES] = jnp.full((ROW_TILE, LANES), scale, f32)
        qt_ref[:, LANES:] = scale * cs

    return pl.pallas_call(
        body, name="rope_tables", grid=(t // ROW_TILE,),
        in_specs=[pl.BlockSpec((ROW_TILE, 1), lambda i: (i, 0)), _whole((1, LANES))],
        out_specs=[_rows(LANES), _rows(2 * LANES)],
        out_shape=[jax.ShapeDtypeStruct((t, LANES), f32), jax.ShapeDtypeStruct((t, 2 * LANES), f32)],
        compiler_params=_cparams(("parallel",)),
    )(pos, inv)


def _prep_fn(cq, ckv, kr, cs, gq, gkv):
    return _rms_norm(cq, gq), _rms_norm(ckv, gkv), _dup_halves(kr * cs)


def _prep_fwd(h, cs, gq, gkv, lay):
    t = h.shape[0]
    ql, kvl = lay["ql"], lay["kvl"]

    def body(cq_ref, ckv_ref, kr_ref, cs_ref, gq_ref, gkv_ref, qn_ref, kvn_ref, krd_ref):
        qn, kvn, krd = _prep_fn(cq_ref[...], ckv_ref[...], kr_ref[...], cs_ref[...], gq_ref[...], gkv_ref[...])
        qn_ref[...] = qn.astype(bf16)
        kvn_ref[...] = kvn.astype(bf16)
        krd_ref[...] = krd.astype(bf16)

    return pl.pallas_call(
        body, name="prep_fwd", grid=(t // ROW_TILE,),
        in_specs=[_rows(ql, lay["off_q"] // ql), _rows(kvl, lay["off_kv"] // kvl), _rows(LANES, lay["off_kr"] // LANES),
                  _rows(LANES), _whole((1, ql)), _whole((1, kvl))],
        out_specs=[_rows(ql), _rows(kvl), _rows(LANES)],
        out_shape=[jax.ShapeDtypeStruct((t, ql), bf16), jax.ShapeDtypeStruct((t, kvl), bf16),
                   jax.ShapeDtypeStruct((t, LANES), bf16)],
        compiler_params=_cparams(("parallel",)),
    )(h, h, h, cs, gq, gkv)


def _prep_bwd(h, cs, gq, gkv, d_qn, d_kvn, d_krd, lay):
    t = h.shape[0]
    ql, kvl = lay["ql"], lay["kvl"]

    def body(cq_ref, ckv_ref, kr_ref, cs_ref, gq_ref, gkv_ref, dqn_ref, dkvn_ref, dkrd_ref,
             dcq_ref, dckv_ref, dkr_ref, dgq_ref, dgkv_ref):
        _, vjp = jax.vjp(_prep_fn, cq_ref[...], ckv_ref[...], kr_ref[...], cs_ref[...], gq_ref[...], gkv_ref[...])
        dcq, dckv, dkr, _, dgq, dgkv = vjp((dqn_ref[...], dkvn_ref[...], dkrd_ref[...].astype(f32)))
        dcq_ref[...] = dcq.astype(bf16)
        dckv_ref[...] = dckv.astype(bf16)
        dkr_ref[...] = dkr.astype(bf16)

        @pl.when(pl.program_id(0) == 0)
        def _():
            dgq_ref[...] = jnp.zeros_like(dgq_ref)
            dgkv_ref[...] = jnp.zeros_like(dgkv_ref)

        dgq_ref[...] += dgq
        dgkv_ref[...] += dgkv

    return pl.pallas_call(
        body, name="prep_bwd", grid=(t // ROW_TILE,),
        in_specs=[_rows(ql, lay["off_q"] // ql), _rows(kvl, lay["off_kv"] // kvl), _rows(LANES, lay["off_kr"] // LANES),
                  _rows(LANES), _whole((1, ql)), _whole((1, kvl)), _rows(ql), _rows(kvl), _rows(LANES)],
        out_specs=[_rows(ql), _rows(kvl), _rows(LANES), _whole((1, ql)), _whole((1, kvl))],
        out_shape=[jax.ShapeDtypeStruct((t, ql), bf16), jax.ShapeDtypeStruct((t, kvl), bf16),
                   jax.ShapeDtypeStruct((t, LANES), bf16), jax.ShapeDtypeStruct((1, ql), f32),
                   jax.ShapeDtypeStruct((1, kvl), f32)],
        compiler_params=_cparams(("arbitrary",)),
    )(h, h, h, cs, gq, gkv, d_qn, d_kvn, d_krd)


def _convln_fn(hc, g, b):
    y = _layer_norm(hc, g, b)
    return y * jax.nn.sigmoid(y)


def _convln_fwd(hconv, g, b):
    t, cw = hconv.shape

    def body(h_ref, g_ref, b_ref, o_ref):
        o_ref[...] = _convln_fn(h_ref[...], g_ref[...], b_ref[...]).astype(bf16)

    return pl.pallas_call(
        body, name="convln_fwd", grid=(t // ROW_TILE,),
        in_specs=[_rows(cw), _whole((1, cw)), _whole((1, cw))], out_specs=_rows(cw),
        out_shape=jax.ShapeDtypeStruct((t, cw), bf16), compiler_params=_cparams(("parallel",)),
    )(hconv, g, b)


def _convln_bwd(hconv, g, b, d_mixed, col_block):
    t, cw = hconv.shape

    def body(h_ref, g_ref, b_ref, dy_ref, dh_ref, dg_ref, db_ref):
        _, vjp = jax.vjp(_convln_fn, h_ref[...], g_ref[...], b_ref[...])
        dh, dg, db = vjp(dy_ref[...])
        dh_ref[...] = dh

        @pl.when(pl.program_id(0) == 0)
        def _():
            dg_ref[...] = jnp.zeros_like(dg_ref)
            db_ref[...] = jnp.zeros_like(db_ref)

        dg_ref[...] += dg
        db_ref[...] += db

    return pl.pallas_call(
        body, name="convln_bwd", grid=(t // ROW_TILE,),
        in_specs=[_rows(cw), _whole((1, cw)), _whole((1, cw)), _rows(cw, col_block)],
        out_specs=[_rows(cw), _whole((1, cw)), _whole((1, cw))],
        out_shape=[jax.ShapeDtypeStruct((t, cw), f32), jax.ShapeDtypeStruct((1, cw), f32),
                   jax.ShapeDtypeStruct((1, cw), f32)],
        compiler_params=_cparams(("arbitrary",)),
    )(hconv, g, b, d_mixed)


SEQ_CT = 128


def _conv_fwd(h, w, b, nb, seq, cw):
    ct, kk = SEQ_CT, w.shape[0]
    ncb = cw // ct

    def body(h_ref, w_ref, b_ref, o_ref):
        blk = h_ref[...]
        a, g = blk[:, :ct], blk[:, ct:]
        hc = a * jax.nn.sigmoid(g)
        acc = jnp.zeros_like(hc)
        for k in range(kk):
            acc = acc + w_ref[k:k + 1, :] * _shift_down_raw(hc, kk - 1 - k)
        o_ref[...] = acc + b_ref[...]

    return pl.pallas_call(
        body, name="conv_fwd", grid=(ncb, nb),
        in_specs=[pl.BlockSpec((seq, 2 * ct), lambda j, bb: (bb, j)), pl.BlockSpec((kk, ct), lambda j, bb: (0, j)),
                  pl.BlockSpec((1, ct), lambda j, bb: (0, j))],
        out_specs=pl.BlockSpec((seq, ct), lambda j, bb: (bb, j)),
        out_shape=jax.ShapeDtypeStruct((nb * seq, cw), f32),
        compiler_params=_cparams(("parallel", "parallel")),
    )(h, w, b)


def _conv_bwd(h, w, d_hconv, nb, seq, cw):
    ct, kk = SEQ_CT, w.shape[0]
    ncb = cw // ct

    def body(h_ref, w_ref, dy_ref, dh_ref, dw_ref, db_ref):
        blk = h_ref[...]
        a, g = blk[:, :ct], blk[:, ct:]
        sg = jax.nn.sigmoid(g)
        hc = a * sg
        dy = dy_ref[...]
        dhc = jnp.zeros_like(hc)

        @pl.when(pl.program_id(1) == 0)
        def _():
            dw_ref[...] = jnp.zeros_like(dw_ref)
            db_ref[...] = jnp.zeros_like(db_ref)

        for k in range(kk):
            dhc = dhc + w_ref[k:k + 1, :] * _shift_up_raw(dy, kk - 1 - k)
            dw_ref[k:k + 1, :] += _colsum(dy * _shift_down_raw(hc, kk - 1 - k))
        db_ref[...] += _colsum(dy)
        dh_ref[:, :ct] = (dhc * sg).astype(bf16)
        dh_ref[:, ct:] = (dhc * a * sg * (1.0 - sg)).astype(bf16)

    return pl.pallas_call(
        body, name="conv_bwd", grid=(ncb, nb),
        in_specs=[pl.BlockSpec((seq, 2 * ct), lambda j, bb: (bb, j)), pl.BlockSpec((kk, ct), lambda j, bb: (0, j)),
                  pl.BlockSpec((seq, ct), lambda j, bb: (bb, j))],
        out_specs=[pl.BlockSpec((seq, 2 * ct), lambda j, bb: (bb, j)), pl.BlockSpec((kk, ct), lambda j, bb: (0, j)),
                   pl.BlockSpec((1, ct), lambda j, bb: (0, j))],
        out_shape=[jax.ShapeDtypeStruct((nb * seq, 2 * cw), bf16), jax.ShapeDtypeStruct((kk, cw), f32),
                   jax.ShapeDtypeStruct((1, cw), f32)],
        compiler_params=_cparams(("parallel", "arbitrary")),
    )(h, w, d_hconv)


def _pool_fn(u, wp, scale, pg):
    seq = u.shape[0]
    t1 = (lax.broadcasted_iota(jnp.int32, (seq, 1), 0) + 1).astype(f32)
    outs = []
    for gi, win in enumerate(POOL_WINDOWS):
        ug = u[:, gi * pg:(gi + 1) * pg]
        acc, span = ug, 1
        while span < win:
            acc = acc + _shift_down(acc, span)
            span *= 2
        d = acc / jnp.minimum(t1, float(win)) - ug
        outs.append(_mm_bf16(d, wp[gi * pg:(gi + 1) * pg, :]) * scale[:, gi * pg:(gi + 1) * pg])
    return outs


def _pool_fwd(h, wp, scale, lay, nb, seq):
    pw, pg = lay["pw"], lay["pg"]

    def body(u_ref, wp_ref, sc_ref, o_ref):
        outs = _pool_fn(u_ref[...], wp_ref[...], sc_ref[...], pg)
        for gi in range(len(POOL_WINDOWS)):
            o_ref[:, gi * pg:(gi + 1) * pg] = outs[gi].astype(bf16)

    return pl.pallas_call(
        body, name="pool_fwd", grid=(nb,),
        in_specs=[pl.BlockSpec((seq, pw), lambda bb: (bb, lay["off_pool"] // pw)), _whole((pw, pg)), _whole((1, pw))],
        out_specs=pl.BlockSpec((seq, pw), lambda bb: (bb, 0)),
        out_shape=jax.ShapeDtypeStruct((nb * seq, pw), bf16),
        compiler_params=_cparams(("parallel",)),
    )(h, wp, scale)


def _pool_bwd(h, wp, scale, d_mixed, col_block, lay, nb, seq):
    pw, pg = lay["pw"], lay["pg"]
    ng = len(POOL_WINDOWS)

    def body(u_ref, wp_ref, sc_ref, dy_ref, du_ref, dwp_ref, dsc_ref):
        _, vjp = jax.vjp(functools.partial(_pool_fn, pg=pg), u_ref[...], wp_ref[...], sc_ref[...])
        dy = dy_ref[...]
        du, dwp, dsc = vjp([dy[:, gi * pg:(gi + 1) * pg] for gi in range(ng)])
        du_ref[...] = du.astype(bf16)

        @pl.when(pl.program_id(0) == 0)
        def _():
            dwp_ref[...] = jnp.zeros_like(dwp_ref)
            dsc_ref[...] = jnp.zeros_like(dsc_ref)

        dwp_ref[...] += dwp
        dsc_ref[...] += dsc

    return pl.pallas_call(
        body, name="pool_bwd", grid=(nb,),
        in_specs=[pl.BlockSpec((seq, pw), lambda bb: (bb, lay["off_pool"] // pw)), _whole((pw, pg)), _whole((1, pw)),
                  pl.BlockSpec((seq, pw), lambda bb: (bb, col_block))],
        out_specs=[pl.BlockSpec((seq, pw), lambda bb: (bb, 0)), _whole((pw, pg)), _whole((1, pw))],
        out_shape=[jax.ShapeDtypeStruct((nb * seq, pw), bf16), jax.ShapeDtypeStruct((pw, pg), f32),
                   jax.ShapeDtypeStruct((1, pw), f32)],
        compiler_params=_cparams(("arbitrary",)),
    )(h, wp, scale, d_mixed)


FFN_ROWS = 32
FFN_FWD_CT = 256
FFN_BWD_CT = 128
SUBLANES = 8


def _rows_before(ref, r0, s, n, cols):
    if s == 0:
        return ref[r0:r0 + n, cols]
    if r0 == 0:
        x = ref[0:n, cols]
        row = lax.broadcasted_iota(jnp.int32, x.shape, 0)
        return jnp.where(row >= s, pltpu.roll(x, s, axis=0), 0.0)
    return ref[pl.ds(r0 - s, n), cols]


def _ffn_conv_rows(x_ref, w_ref, b_ref, r0, n, cols):
    kk = w_ref.shape[0]
    xs = [_rows_before(x_ref, r0, s, n, cols) for s in range(kk)]
    c = b_ref[:, cols] + w_ref[kk - 1:kk, cols] * xs[0]
    for k in range(kk - 1):
        c = c + w_ref[k:k + 1, cols] * xs[kk - 1 - k]
    return c, xs


def _lane_tiles(width):
    return [slice(c0, c0 + LANES) for c0 in range(0, width, LANES)]


def _fold_rows(x):
    out = x[0:SUBLANES]
    for i in range(1, x.shape[0] // SUBLANES):
        out = out + x[i * SUBLANES:(i + 1) * SUBLANES]
    return out


def _ffn_specs(seq, ct, kk, nct):
    return [pl.BlockSpec((seq, ct), lambda j, bb: (bb, j)), pl.BlockSpec((seq, ct), lambda j, bb: (bb, nct + j)),
            pl.BlockSpec((kk, ct), lambda j, bb: (0, j)), pl.BlockSpec((kk, ct), lambda j, bb: (0, nct + j)),
            pl.BlockSpec((1, ct), lambda j, bb: (0, j)), pl.BlockSpec((1, ct), lambda j, bb: (0, nct + j))]


def _ffn_act_fwd(up, w, b, nb, seq, dff, comm=None):
    ct, kk = _tile(dff, FFN_FWD_CT), w.shape[0]
    rows = min(FFN_ROWS, seq)

    def body(ua_ref, ug_ref, wa_ref, wg_ref, ba_ref, bg_ref, o_ref):
        for cols in _lane_tiles(ct):
            for r0 in range(0, seq, rows):
                a, _ = _ffn_conv_rows(ua_ref, wa_ref, ba_ref, r0, rows, cols)
                g, _ = _ffn_conv_rows(ug_ref, wg_ref, bg_ref, r0, rows, cols)
                o_ref[r0:r0 + rows, cols] = (a * g * jax.nn.sigmoid(g)).astype(bf16)

    (act,), comm_outs = _call(body, "ffn_act_fwd", (dff // ct, nb), _ffn_specs(seq, ct, kk, dff // ct),
                              [pl.BlockSpec((seq, ct), lambda j, bb: (bb, j))],
                              [jax.ShapeDtypeStruct((nb * seq, dff), bf16)], (up, up, w, w, b, b), (),
                              ("parallel", "parallel"), comm)
    return act if comm is None else (act, comm_outs)


def _ffn_act_bwd(up, w, b, d_act, nb, seq, dff, comm=None):
    ct, kk = _tile(dff, FFN_BWD_CT), w.shape[0]
    rows = min(FFN_ROWS, seq)

    def body(ua_ref, ug_ref, wa_ref, wg_ref, ba_ref, bg_ref, da_ref, du_ref, dw_ref, db_ref, dc_ref):
        @pl.when(pl.program_id(1) == 0)
        def _():
            dw_ref[...] = jnp.zeros_like(dw_ref)
            db_ref[...] = jnp.zeros_like(db_ref)

        for cols in _lane_tiles(ct):
            dw_acc = [[jnp.zeros((SUBLANES, LANES), f32) for _ in range(kk)] for _ in range(2)]
            db_acc = [jnp.zeros((SUBLANES, LANES), f32) for _ in range(2)]
            for r0 in range(0, seq, rows):
                a, xa = _ffn_conv_rows(ua_ref, wa_ref, ba_ref, r0, rows, cols)
                g, xg = _ffn_conv_rows(ug_ref, wg_ref, bg_ref, r0, rows, cols)
                sg = jax.nn.sigmoid(g)
                dact = da_ref[r0:r0 + rows, cols]
                dcs = (dact * g * sg, dact * a * sg * (1.0 + g * (1.0 - sg)))
                for hf, (dc, xs) in enumerate(zip(dcs, (xa, xg))):
                    dc_ref[hf, r0:r0 + rows, cols] = dc
                    db_acc[hf] = db_acc[hf] + _fold_rows(dc)
                    for k in range(kk):
                        dw_acc[hf][k] = dw_acc[hf][k] + _fold_rows(dc * xs[kk - 1 - k])
            for hf in range(2):
                dc_ref[hf, seq:seq + SUBLANES, cols] = jnp.zeros((SUBLANES, LANES), f32)
                db_ref[hf, :, cols] += _colsum(db_acc[hf])
                for k in range(kk):
                    dw_ref[hf, k:k + 1, cols] += _colsum(dw_acc[hf][k])
            for r0 in range(0, seq, rows):
                for hf, w_ref in enumerate((wa_ref, wg_ref)):
                    dsrc = w_ref[kk - 1:kk, cols] * dc_ref[hf, r0:r0 + rows, cols]
                    for k in range(kk - 1):
                        dsrc = dsrc + w_ref[k:k + 1, cols] * dc_ref[hf, pl.ds(r0 + kk - 1 - k, rows), cols]
                    du_ref[hf, r0:r0 + rows, cols] = dsrc.astype(bf16)

    (d_up, dw, db), comm_outs = _call(
        body, "ffn_act_bwd", (dff // ct, nb),
        _ffn_specs(seq, ct, kk, dff // ct) + [pl.BlockSpec((seq, ct), lambda j, bb: (bb, j))],
        [pl.BlockSpec((2, seq, ct), lambda j, bb: (0, bb, j)), pl.BlockSpec((2, kk, ct), lambda j, bb: (0, 0, j)),
         pl.BlockSpec((2, 1, ct), lambda j, bb: (0, 0, j))],
        [jax.ShapeDtypeStruct((2, nb * seq, dff), bf16), jax.ShapeDtypeStruct((2, kk, dff), f32),
         jax.ShapeDtypeStruct((2, 1, dff), f32)],
        (up, up, w, w, b, b, d_act), [pltpu.VMEM((2, seq + SUBLANES, ct), f32)], ("parallel", "arbitrary"), comm)
    return (d_up, dw, db) if comm is None else (d_up, dw, db, comm_outs)


def _fill_keys(k_scr, kn_ref, krd_ref):
    @pl.when(pl.program_id(2) == 0)
    def _():
        k_scr[:, :LANES] = kn_ref[...]
        k_scr[:, LANES:] = krd_ref[...]


def _scores(q_ref, qt_ref, k_scr, qblk, tq):
    klen = (qblk + 1) * tq
    q = (q_ref[...] * qt_ref[...]).astype(bf16)
    s = _dot(q, k_scr[0:klen, :], _NT)
    row = qblk * tq + lax.broadcasted_iota(jnp.int32, s.shape, 0)
    col = lax.broadcasted_iota(jnp.int32, s.shape, 1)
    return q, jnp.where(col <= row, s, NEG_INF)


def _per_q_block(nq, fn):
    qi = pl.program_id(2)
    for qblk in range(nq):
        pl.when(qi == qblk)(functools.partial(fn, qblk))


def _attn_fwd(q_ext, qt, kv, krd, nb, seq, heads, comm=None):
    tq = _tile(seq, 512)
    nq = seq // tq

    def body(q_ref, qt_ref, kn_ref, krd_ref, v_ref, o_ref, lse_ref, k_scr):
        _fill_keys(k_scr, kn_ref, krd_ref)

        def work(qblk):
            klen = (qblk + 1) * tq
            _, s = _scores(q_ref, qt_ref, k_scr, qblk, tq)
            m = jnp.max(s, axis=-1, keepdims=True)
            p = jnp.exp(s - m)
            l = jnp.sum(p, axis=-1, keepdims=True)
            o_ref[...] = (_dot(p, v_ref[0:klen, :], _NN) / l).astype(bf16)
            lse_ref[...] = m + jnp.log(l)

        _per_q_block(nq, work)

    (o, lse), comm_outs = _call(
        body, "attn_fwd", (nb, heads, nq),
        [pl.BlockSpec((tq, 2 * LANES), lambda b, h, i: (b * nq + i, h)),
         pl.BlockSpec((tq, 2 * LANES), lambda b, h, i: (b * nq + i, 0)),
         pl.BlockSpec((seq, LANES), lambda b, h, i: (b, 2 * h)),
         pl.BlockSpec((seq, LANES), lambda b, h, i: (b, 0)),
         pl.BlockSpec((seq, LANES), lambda b, h, i: (b, 2 * h + 1))],
        [pl.BlockSpec((tq, LANES), lambda b, h, i: (b * nq + i, h)),
         pl.BlockSpec((None, tq, 1), lambda b, h, i: (b * heads + h, i, 0))],
        [jax.ShapeDtypeStruct((nb * seq, heads * LANES), bf16), jax.ShapeDtypeStruct((nb * heads, seq, 1), f32)],
        (q_ext, qt, kv, krd, kv), [pltpu.VMEM((seq, 2 * LANES), bf16)], ("parallel", "parallel", "arbitrary"), comm)
    return (o, lse) if comm is None else (o, lse, comm_outs)


def _attn_bwd(q_ext, qt, kv, krd, lse, d_mixed, nb, seq, heads, comm=None):
    tq = _tile(seq, 512)
    nq = seq // tq

    def body(q_ref, qt_ref, kn_ref, krd_ref, v_ref, lse_ref, do_ref, dq_ref, dkv_ref, dkrd_ref, dkv_acc, k_scr):
        h, qi = pl.program_id(1), pl.program_id(2)
        _fill_keys(k_scr, kn_ref, krd_ref)

        @pl.when(qi == 0)
        def _():
            dkv_acc[...] = jnp.zeros_like(dkv_acc)

        @pl.when((qi == 0) & (h == 0))
        def _():
            dkrd_ref[...] = jnp.zeros_like(dkrd_ref)

        def work(qblk):
            klen = (qblk + 1) * tq
            q, s = _scores(q_ref, qt_ref, k_scr, qblk, tq)
            p = jnp.exp(s - lse_ref[...])
            do = do_ref[...]
            dp = _dot(do, v_ref[0:klen, :], _NT)
            ds = (p * (dp - jnp.sum(p * dp, axis=-1, keepdims=True))).astype(bf16)
            dq_ref[...] = (_dot(ds, k_scr[0:klen, :], _NN) * qt_ref[...]).astype(bf16)
            dk = _dot(ds, q, _TN)
            dkv_acc[0:klen, :LANES] += dk[:, :LANES]
            dkv_acc[0:klen, LANES:] += _dot(p, do, _TN)
            dkrd_ref[0:klen, :] += dk[:, LANES:]

        _per_q_block(nq, work)

        @pl.when(qi == nq - 1)
        def _():
            dkv_ref[...] = dkv_acc[...].astype(bf16)

    (dq, dkv, dkrd), comm_outs = _call(
        body, "attn_bwd", (nb, heads, nq),
        [pl.BlockSpec((tq, 2 * LANES), lambda b, h, i: (b * nq + i, h)),
         pl.BlockSpec((tq, 2 * LANES), lambda b, h, i: (b * nq + i, 0)),
         pl.BlockSpec((seq, LANES), lambda b, h, i: (b, 2 * h)),
         pl.BlockSpec((seq, LANES), lambda b, h, i: (b, 0)),
         pl.BlockSpec((seq, LANES), lambda b, h, i: (b, 2 * h + 1)),
         pl.BlockSpec((None, tq, 1), lambda b, h, i: (b * heads + h, i, 0)),
         pl.BlockSpec((tq, LANES), lambda b, h, i: (b * nq + i, h))],
        [pl.BlockSpec((tq, 2 * LANES), lambda b, h, i: (b * nq + i, h)),
         pl.BlockSpec((seq, 2 * LANES), lambda b, h, i: (b, h)),
         pl.BlockSpec((seq, LANES), lambda b, h, i: (b, 0))],
        [jax.ShapeDtypeStruct((nb * seq, heads * 2 * LANES), bf16),
         jax.ShapeDtypeStruct((nb * seq, heads * 2 * LANES), bf16),
         jax.ShapeDtypeStruct((nb * seq, LANES), f32)],
        (q_ext, qt, kv, krd, kv, lse, d_mixed), [pltpu.VMEM((seq, 2 * LANES), f32), pltpu.VMEM((seq, 2 * LANES), bf16)],
        ("parallel", "arbitrary", "arbitrary"), comm)
    return (dq, dkv, dkrd) if comm is None else (dq, dkv, dkrd, comm_outs)


def _mesh_pos():
    x, y, c = lax.axis_index("x"), lax.axis_index("y"), lax.axis_index("c")
    return x, y, c, [(1 - x, y), (x, 1 - y), (1 - x, 1 - y)]


def _all_gather(shards, name):
    n = len(shards)

    def body(*refs):
        x_refs, out_refs, (send_sems, recv_sems, local_sems) = refs[:n], refs[n:2 * n], refs[2 * n:]
        x, y, c, chips = _mesh_pos()
        me, sibling = (x, y, c), (x, y, 1 - c)

        def copy(t, k, block, to, from_shard=False):
            px, py, pc = block
            rows = out_refs[t].at[4 * px + 2 * py + pc]
            return pltpu.make_async_remote_copy(
                src_ref=x_refs[t] if from_shard else rows, dst_ref=rows,
                send_sem=send_sems.at[t, k], recv_sem=recv_sems.at[t, k], device_id=to, device_id_type=MESH)

        mine = [pltpu.make_async_copy(x_refs[t], out_refs[t].at[4 * x + 2 * y + c], local_sems.at[t]) for t in range(n)]
        first = [[copy(t, 0, me, sibling, True)] + [copy(t, 1 + j, me, (*chip, c), True) for j, chip in enumerate(chips)]
                 for t in range(n)]
        passed = [[copy(t, 4 + j, (*chip, c), sibling) for j, chip in enumerate(chips)] for t in range(n)]
        for t in range(n):
            mine[t].start()
            for cp in first[t]:
                cp.start()
        for j, chip in enumerate(chips):
            for t in range(n):
                copy(t, 1 + j, (*chip, c), me).wait_recv()
                passed[t][j].start()
        for t in range(n):
            copy(t, 0, sibling, me).wait_recv()
            for j, chip in enumerate(chips):
                copy(t, 4 + j, (*chip, 1 - c), me).wait_recv()
        for t in range(n):
            for cp in first[t] + passed[t]:
                cp.wait_send()
            mine[t].wait()

    return pl.pallas_call(
        body, name=name, out_shape=[jax.ShapeDtypeStruct((N_DEV,) + s.shape, s.dtype) for s in shards],
        in_specs=[_ANY] * n, out_specs=[_ANY] * n,
        scratch_shapes=[pltpu.SemaphoreType.DMA((n, 7)), pltpu.SemaphoreType.DMA((n, 7)), pltpu.SemaphoreType.DMA((n,))],
    )(*shards)


def _gather_own(shards):
    n = len(shards)

    def remote(x_refs, out_refs, sems, arriving):
        send_sems, recv_sems, _ = sems
        x, y, c, chips = _mesh_pos()
        peers = [(x, y, 1 - c)] + [(*chip, c) for chip in chips]
        return [pltpu.make_async_remote_copy(
            src_ref=x_refs[t], dst_ref=out_refs[t].at[4 * px + 2 * py + pc if arriving else 4 * x + 2 * y + c],
            send_sem=send_sems.at[t, k], recv_sem=recv_sems.at[t, k], device_id=(px, py, pc), device_id_type=MESH)
            for t in range(n) for k, (px, py, pc) in enumerate(peers)]

    def local(x_refs, out_refs, sems):
        x, y, c, _ = _mesh_pos()
        return [pltpu.make_async_copy(x_refs[t], out_refs[t].at[4 * x + 2 * y + c], sems[2].at[t]) for t in range(n)]

    def start(x_refs, out_refs, sems):
        for cp in local(x_refs, out_refs, sems) + remote(x_refs, out_refs, sems, False):
            cp.start()

    def finish(x_refs, out_refs, sems):
        for cp in remote(x_refs, out_refs, sems, True):
            cp.wait_recv()
        for cp in remote(x_refs, out_refs, sems, False):
            cp.wait_send()
        for cp in local(x_refs, out_refs, sems):
            cp.wait()

    return _Comm(shards, [jax.ShapeDtypeStruct((N_DEV,) + s.shape, s.dtype) for s in shards],
                 [pltpu.SemaphoreType.DMA((n, 4)), pltpu.SemaphoreType.DMA((n, 4)), pltpu.SemaphoreType.DMA((n,))],
                 start, finish)


def _gather_pass(gathered):
    n = len(gathered)

    def copies(in_refs, out_refs, sems, arriving):
        send_sems, recv_sems = sems
        x, y, c, chips = _mesh_pos()
        return [pltpu.make_async_remote_copy(
            src_ref=in_refs[t].at[4 * px + 2 * py + c],
            dst_ref=out_refs[t].at[4 * px + 2 * py + (1 - c if arriving else c)],
            send_sem=send_sems.at[t, j], recv_sem=recv_sems.at[t, j], device_id=(x, y, 1 - c), device_id_type=MESH)
            for t in range(n) for j, (px, py) in enumerate(chips)]

    def start(in_refs, out_refs, sems):
        for cp in copies(in_refs, out_refs, sems, False):
            cp.start()

    def finish(in_refs, out_refs, sems):
        for cp in copies(in_refs, out_refs, sems, True):
            cp.wait_recv()
        for cp in copies(in_refs, out_refs, sems, False):
            cp.wait_send()

    return _Comm(gathered, [jax.ShapeDtypeStruct(g.shape, g.dtype) for g in gathered],
                 [pltpu.SemaphoreType.DMA((n, 3)), pltpu.SemaphoreType.DMA((n, 3))], start, finish,
                 aliases={t: t for t in range(n)})


def _sibling_swap(slots):
    n = len(slots)

    def start(g_refs, out_refs, sems):
        send_sems, recv_sems = sems
        x, y, c, _ = _mesh_pos()
        for t in range(n):
            for k in range(4):
                pltpu.make_async_remote_copy(
                    src_ref=g_refs[t].at[2 * k + (1 - c)], dst_ref=out_refs[t].at[k], send_sem=send_sems.at[t],
                    recv_sem=recv_sems.at[t], device_id=(x, y, 1 - c), device_id_type=MESH).start()

    def finish(g_refs, out_refs, sems):
        send_sems, recv_sems = sems
        x, y, c, _ = _mesh_pos()
        for t in range(n):
            pltpu.make_async_remote_copy(
                src_ref=g_refs[t].at[pl.ds(0, 4)], dst_ref=out_refs[t], send_sem=send_sems.at[t],
                recv_sem=recv_sems.at[t], device_id=(x, y, 1 - c), device_id_type=MESH).wait()

    return _Comm(slots, [jax.ShapeDtypeStruct((4,) + s.shape[1:], s.dtype) for s in slots],
                 [pltpu.SemaphoreType.DMA((n,)), pltpu.SemaphoreType.DMA((n,))], start, finish)


def _chip_swap(p4s, peers=(0, 1, 2), into=None):
    n = len(p4s)

    def copies(refs, out_refs, sems):
        send_sems, recv_sems = sems
        x, y, c, chips = _mesh_pos()
        return [pltpu.make_async_remote_copy(
            src_ref=refs[t].at[2 * chips[j][0] + chips[j][1]], dst_ref=out_refs[t].at[j], send_sem=send_sems.at[t, j],
            recv_sem=recv_sems.at[t, j], device_id=(*chips[j], c), device_id_type=MESH)
            for t in range(n) for j in peers]

    def start(refs, out_refs, sems):
        for cp in copies(refs, out_refs, sems):
            cp.start()

    def finish(refs, out_refs, sems):
        for cp in copies(refs, out_refs, sems):
            cp.wait()

    out_shapes = [jax.ShapeDtypeStruct((3,) + p.shape[1:], p.dtype) for p in p4s]
    sems = [pltpu.SemaphoreType.DMA((n, 3)), pltpu.SemaphoreType.DMA((n, 3))]
    if into is None:
        return _Comm(p4s, out_shapes, sems, start, finish)
    return _Comm(list(p4s) + list(into), out_shapes, sems, start, finish, aliases={n + t: t for t in range(n)})


def _comm_join(a, b):
    ai, ao, asem = len(a.inputs), len(a.out_shapes), len(a.sems)

    def start(ins, outs, sems):
        a.start(ins[:ai], outs[:ao], sems[:asem])
        b.start(ins[ai:], outs[ao:], sems[asem:])

    def finish(ins, outs, sems):
        a.finish(ins[:ai], outs[:ao], sems[:asem])
        b.finish(ins[ai:], outs[ao:], sems[asem:])

    aliases = dict(a.aliases)
    aliases.update({ai + i: ao + o for i, o in b.aliases.items()})
    return _Comm(a.inputs + b.inputs, a.out_shapes + b.out_shapes, a.sems + b.sems, start, finish, aliases)


def _row_tile(rows, cols, max_bytes=1024 * 1024):
    best = None
    for tr in range(16, rows + 1, 16):
        if rows % tr == 0 and tr * cols * 4 <= max_bytes:
            best = tr
    return best or rows


def _pair_add(slots, theirs, core, name):
    _, rows, cols = slots.shape
    tr = _row_tile(rows, cols, 4 * 1024 * 1024)

    def body(c_ref, a_ref, b_ref, o_ref):
        o_ref[...] = (a_ref[...].astype(f32) + b_ref[...].astype(f32)).astype(bf16)

    return pl.pallas_call(
        body, name=name,
        grid_spec=pltpu.PrefetchScalarGridSpec(
            num_scalar_prefetch=1, grid=(4, rows // tr),
            in_specs=[pl.BlockSpec((None, tr, cols), lambda k, i, c_ref: (2 * k + c_ref[0], i, 0)),
                      pl.BlockSpec((None, tr, cols), lambda k, i, c_ref: (k, i, 0))],
            out_specs=pl.BlockSpec((None, tr, cols), lambda k, i, c_ref: (k, i, 0))),
        out_shape=jax.ShapeDtypeStruct((4, rows, cols), bf16), compiler_params=_cparams(("parallel", "parallel")),
    )(core, slots, theirs)


def _adam_update(g, w, m, v):
    c1 = 1.0 / (1.0 - ADAM_B1 ** ADAM_STEP)
    c2 = 1.0 / (1.0 - ADAM_B2 ** ADAM_STEP)
    nm = ADAM_B1 * m + (1.0 - ADAM_B1) * g
    nv = ADAM_B2 * v + (1.0 - ADAM_B2) * jnp.square(g)
    delta = -ADAM_LR * ((nm * c1) / (jnp.sqrt(nv * c2) + ADAM_EPS) + ADAM_WD * w)
    return delta, nm, nv


def _adamw_layers(p4s, chips, chip_idx, w, m, v, name, comm=None):
    depth = len(p4s)
    _, rows_l, cols = p4s[0].shape
    tr = _row_tile(rows_l, cols)
    nr = rows_l // tr

    def body(idx_ref, *refs):
        p_refs, c_refs = refs[:depth], refs[depth:2 * depth]
        w_ref, m_ref, v_ref, g_ref, d_ref, nm_ref, nv_ref = refs[2 * depth:]
        layer = pl.program_id(0)
        for ll in range(depth):
            @pl.when(layer == ll)
            def _(ll=ll):
                g = p_refs[ll][...].astype(f32)
                for j in range(3):
                    g = g + c_refs[ll][j].astype(f32)
                delta, nm, nv = _adam_update(g, w_ref[...], m_ref[...], v_ref[...])
                g_ref[...] = g
                d_ref[...] = delta
                nm_ref[...] = nm
                nv_ref[...] = nv

    def of_layer(ll):
        return lambda l, i: jnp.where(l == ll, i, 0)

    p_specs = [pl.BlockSpec((None, tr, cols), lambda l, i, idx_ref, f=of_layer(ll): (idx_ref[0], f(l, i), 0))
               for ll in range(depth)]
    c_specs = [pl.BlockSpec((3, tr, cols), lambda l, i, idx_ref, f=of_layer(ll): (0, f(l, i), 0)) for ll in range(depth)]
    spec = pl.BlockSpec((tr, cols), lambda l, i, idx_ref: (l * nr + i, 0))
    out = jax.ShapeDtypeStruct(w.shape, f32)
    outs, comm_outs = _call(body, name, (depth, nr), p_specs + c_specs + [spec, spec, spec], [spec, spec, spec, spec],
                            [out, out, out, out], (*p4s, *chips, w, m, v), (), ("parallel", "parallel"), comm,
                            prefetch=[chip_idx])
    return outs if comm is None else (outs, comm_outs)


def _adamw(parts, w, m, v, name):
    rows, cols = w.shape
    tr = _row_tile(rows, cols, 512 * 1024)
    nparts = len(parts)

    def body(*refs):
        part_refs, (w_ref, m_ref, v_ref, g_ref, d_ref, nm_ref, nv_ref) = refs[:nparts], refs[nparts:]
        g = None
        for pr in part_refs:
            for s in range(pr.shape[0]):
                term = pr[s].astype(f32)
                g = term if g is None else g + term
        delta, nm, nv = _adam_update(g, w_ref[...], m_ref[...], v_ref[...])
        g_ref[...] = g
        d_ref[...] = delta
        nm_ref[...] = nm
        nv_ref[...] = nv

    spec = pl.BlockSpec((tr, cols), lambda i: (i, 0))
    part_specs = [pl.BlockSpec((p.shape[0], tr, cols), lambda i: (0, i, 0)) for p in parts]
    out = jax.ShapeDtypeStruct((rows, cols), f32)
    return pl.pallas_call(
        body, name=name, grid=(rows // tr,), in_specs=part_specs + [spec, spec, spec],
        out_specs=[spec, spec, spec, spec], out_shape=[out, out, out, out],
        compiler_params=_cparams(("parallel",)),
    )(*parts, w, m, v)


def _gathered_to_full(gathered, axis):
    s = gathered.shape[1:]
    full = jnp.moveaxis(gathered, 0, axis)
    return full.reshape(s[:axis] + (N_DEV * s[axis],) + s[axis + 1:])


def _interleave_halves(w, ct):
    n = w.shape[-1] // 2
    t = w.reshape(w.shape[:-1] + (2, n // ct, ct))
    return jnp.swapaxes(t, -3, -2).reshape(w.shape)


def _deinterleave_halves(w, ct):
    n = w.shape[-1] // 2
    t = w.reshape(w.shape[:-1] + (n // ct, 2, ct))
    return jnp.swapaxes(t, -3, -2).reshape(w.shape)


def _rot_cols(w):
    half = QK_ROPE_DIM // 2
    return jnp.concatenate([-w[..., half:], w[..., :half]], axis=-1)


def _rot_cols_t(dw):
    half = QK_ROPE_DIM // 2
    return jnp.concatenate([dw[..., half:], -dw[..., :half]], axis=-1)


def kernel(x, positions, ln_in_g, ln_in_b, w_in, q_norm_g, w_uq, kv_norm_g, w_ukv, conv_w, conv_b, conv_ln_g, conv_ln_b, w_pool, pool_scale, w_out, ln1_g, ln1_b, w_up, ffn_conv_w, ffn_conv_b, w_down, ln2_g, ln2_b, loss_target, m_ln_in_g, m_ln_in_b, m_w_in, m_q_norm_g, m_w_uq, m_kv_norm_g, m_w_ukv, m_conv_w, m_conv_b, m_conv_ln_g, m_conv_ln_b, m_w_pool, m_pool_scale, m_w_out, m_ln1_g, m_ln1_b, m_w_up, m_ffn_conv_w, m_ffn_conv_b, m_w_down, m_ln2_g, m_ln2_b, v_ln_in_g, v_ln_in_b, v_w_in, v_q_norm_g, v_w_uq, v_kv_norm_g, v_w_ukv, v_conv_w, v_conv_b, v_conv_ln_g, v_conv_ln_b, v_w_pool, v_pool_scale, v_w_out, v_ln1_g, v_ln1_b, v_w_up, v_ffn_conv_w, v_ffn_conv_b, v_w_down, v_ln2_g, v_ln2_b):
    weights = dict(ln_in_g=ln_in_g, ln_in_b=ln_in_b, w_in=w_in, q_norm_g=q_norm_g, w_uq=w_uq, kv_norm_g=kv_norm_g,
                   w_ukv=w_ukv, conv_w=conv_w, conv_b=conv_b, conv_ln_g=conv_ln_g, conv_ln_b=conv_ln_b, w_pool=w_pool,
                   pool_scale=pool_scale, w_out=w_out, ln1_g=ln1_g, ln1_b=ln1_b, w_up=w_up, ffn_conv_w=ffn_conv_w,
                   ffn_conv_b=ffn_conv_b, w_down=w_down, ln2_g=ln2_g, ln2_b=ln2_b)
    mom1 = dict(ln_in_g=m_ln_in_g, ln_in_b=m_ln_in_b, w_in=m_w_in, q_norm_g=m_q_norm_g, w_uq=m_w_uq,
                kv_norm_g=m_kv_norm_g, w_ukv=m_w_ukv, conv_w=m_conv_w, conv_b=m_conv_b, conv_ln_g=m_conv_ln_g,
                conv_ln_b=m_conv_ln_b, w_pool=m_w_pool, pool_scale=m_pool_scale, w_out=m_w_out, ln1_g=m_ln1_g,
                ln1_b=m_ln1_b, w_up=m_w_up, ffn_conv_w=m_ffn_conv_w, ffn_conv_b=m_ffn_conv_b, w_down=m_w_down,
                ln2_g=m_ln2_g, ln2_b=m_ln2_b)
    mom2 = dict(ln_in_g=v_ln_in_g, ln_in_b=v_ln_in_b, w_in=v_w_in, q_norm_g=v_q_norm_g, w_uq=v_w_uq,
                kv_norm_g=v_kv_norm_g, w_ukv=v_w_ukv, conv_w=v_conv_w, conv_b=v_conv_b, conv_ln_g=v_conv_ln_g,
                conv_ln_b=v_conv_ln_b, w_pool=v_w_pool, pool_scale=v_pool_scale, w_out=v_w_out, ln1_g=v_ln1_g,
                ln1_b=v_ln1_b, w_up=v_w_up, ffn_conv_w=v_ffn_conv_w, ffn_conv_b=v_ffn_conv_b, w_down=v_w_down,
                ln2_g=v_ln2_g, ln2_b=v_ln2_b)
    names = list(weights)

    nb, seq, d = x.shape
    t = nb * seq
    depth = w_in.shape[0]
    ql, kvl, cw, pw = q_norm_g.shape[1], kv_norm_g.shape[1], conv_b.shape[1], pool_scale.shape[1]
    pg = w_pool.shape[-1]
    heads = w_uq.shape[2]
    dff = w_down.shape[1] * N_DEV
    alpha = (2.0 * depth) ** 0.25
    scale = float(QK_NOPE_DIM + QK_ROPE_DIM) ** -0.5
    lay = dict(ql=ql, kvl=kvl, pw=pw, pg=pg, off_q=2 * cw, off_pool=2 * cw + ql, off_kv=2 * cw + ql + pw,
               off_kr=2 * cw + ql + pw + kvl)
    o1, o2, o3, o4 = ql, ql + kvl, ql + kvl + QK_ROPE_DIM, ql + kvl + QK_ROPE_DIM + 2 * cw
    my_x, my_y, my_c = lax.axis_index("x"), lax.axis_index("y"), lax.axis_index("c")
    my_dev = 4 * my_x + 2 * my_y + my_c

    big = ("w_in", "w_uq", "w_ukv", "w_out", "w_up", "w_down")
    g_conv, g_ffn = _all_gather([conv_w, ffn_conv_w], "ag_conv_taps")
    conv_w_full, ffn_w_full = _gathered_to_full(g_conv, 2), _gathered_to_full(g_ffn, 2)
    w_pool_2d = w_pool.reshape(depth, pw, pg)

    rest = ("w_in", "w_uq", "w_ukv", "w_out")

    def bf16_shards(l, which):
        return [weights[n][l].astype(bf16) for n in which]

    def small_weights(gathered):
        g_in, g_uq, g_ukv, g_out = (gathered[n] for n in rest)
        wi = _gathered_to_full(g_in, 1)
        kr_cols = wi[:, o2:o3]
        w_in_pad = jnp.concatenate([_interleave_halves(wi[:, o3:o4], SEQ_CT), wi[:, :o1], wi[:, o4:], wi[:, o1:o2],
                                    kr_cols, _rot_cols(kr_cols)], axis=-1)
        wq = g_uq.reshape(ql, heads, QK_NOPE_DIM + QK_ROPE_DIM)
        w_uq_ext = jnp.concatenate([wq, _rot_cols(wq[..., QK_NOPE_DIM:])], axis=-1).reshape(ql, heads * 2 * LANES)
        return w_in_pad, w_uq_ext, g_ukv.reshape(kvl, heads * 2 * LANES), g_out.reshape(-1, d)

    half = QK_ROPE_DIM // 2
    inv = 1.0 / (ROPE_THETA ** (jnp.arange(0, QK_ROPE_DIM, 2, dtype=f32) / QK_ROPE_DIM))
    inv_lanes = jnp.tile(inv, LANES // half).reshape(1, LANES)
    cs, qt = _rope_tables(positions.reshape(t, 1), inv_lanes, scale)

    x2 = x.reshape(t, d)
    xs, xs_bf = _ln_fwd(x2, None, ln_in_g.reshape(1, d), ln_in_b.reshape(1, d), 1.0, "ln_in_fwd")
    saved = []
    in_pad = o4 + pw + QK_ROPE_DIM
    gathered = dict(zip(rest, _all_gather(bf16_shards(0, rest), "ag_weights")))
    for l in range(depth):
        nxt = l + 1 < depth
        gq, gkv = q_norm_g[l].reshape(1, ql), kv_norm_g[l].reshape(1, kvl)
        w_in_l, w_uq_l, w_ukv_l, w_out_l = small_weights(gathered)
        h = _matmul(xs_bf, w_in_l, "nn", f32, "mm_in", tm=512, tn=in_pad)
        qn, kvn, krd = _prep_fwd(h, cs, gq, gkv, lay)
        q_ext = _matmul(qn, w_uq_l, "nn", f32, "mm_uq", tn=2048)
        kv = _matmul(kvn, w_ukv_l, "nn", bf16, "mm_ukv", tn=2048)
        riders = ("w_down", "w_up") if l == 0 else ("w_down",)
        y_mla, lse, g_half = _attn_fwd(q_ext, qt, kv, krd, nb, seq, heads, comm=_gather_own(bf16_shards(l, riders)))
        hconv = _conv_fwd(h, conv_w_full[l], conv_b[l].reshape(1, cw), nb, seq, cw)
        y_conv = _convln_fwd(hconv, conv_ln_g[l].reshape(1, cw), conv_ln_b[l].reshape(1, cw))
        y_pool = _pool_fwd(h, w_pool_2d[l], pool_scale[l].reshape(1, pw), lay, nb, seq)
        mixed = jnp.concatenate([y_mla, y_conv, y_pool], axis=-1)
        y1, g_full = _matmul(mixed, w_out_l, "nn", f32, "mm_out", comm=_gather_pass(g_half))
        gathered.update(zip(riders, g_full))
        wl = dict(w_in=w_in_l, w_uq=w_uq_l, w_ukv=w_ukv_l, w_out=w_out_l, w_up=gathered["w_up"],
                  w_down=gathered["w_down"].reshape(dff, d))
        gathered = {}
        x1, x1_bf = _ln_fwd(xs, y1, ln1_g[l].reshape(1, d), ln1_b[l].reshape(1, d), alpha, "ln1_fwd")
        if nxt:
            up, (g_up,) = _matmul_up(x1_bf, wl["w_up"], "mm_up", comm=_gather_own(bf16_shards(l + 1, ("w_up",))))
            act, g_rest = _ffn_act_fwd(up, ffn_w_full[l], ffn_conv_b[l].reshape(1, 2 * dff), nb, seq, dff,
                                       comm=_gather_own(bf16_shards(l + 1, rest[1:])))
            y2, (gathered["w_up"], g_in) = _matmul(
                act, wl["w_down"], "nn", f32, "mm_down", tk=dff // 2,
                comm=_comm_join(_gather_pass([g_up]), _gather_own(bf16_shards(l + 1, rest[:1]))))
            xn, xn_bf, g_rest = _ln_fwd(x1, y2, ln2_g[l].reshape(1, d), ln2_b[l].reshape(1, d), alpha, "ln2_fwd",
                                        comm=_gather_pass([g_in] + g_rest))
            gathered.update(zip(rest, g_rest))
        else:
            up = _matmul_up(x1_bf, wl["w_up"], "mm_up")
            act = _ffn_act_fwd(up, ffn_w_full[l], ffn_conv_b[l].reshape(1, 2 * dff), nb, seq, dff)
            y2 = _matmul(act, wl["w_down"], "nn", f32, "mm_down", tk=dff // 2)
            xn, xn_bf = _ln_fwd(x1, y2, ln2_g[l].reshape(1, d), ln2_b[l].reshape(1, d), alpha, "ln2_fwd")
        saved.append(dict(xs=xs, xs_bf=xs_bf, h=h, qn=qn, kvn=kvn, krd=krd, q_ext=q_ext, kv=kv, lse=lse, hconv=hconv,
                          mixed=mixed, y1=y1, x1=x1, x1_bf=x1_bf, up=up, act=act, y2=y2, wl=wl))
        xs, xs_bf = xn, xn_bf

    d_stream, loss_row = _loss_call(xs, loss_target.reshape(t, d))
    loss = lax.psum(loss_row[0, 0], MESH_AXES)

    gw = {n: [None] * depth for n in names if n not in ("ln_in_g", "ln_in_b")}
    small = [n for n in gw if n not in big]
    rs_own = {n: [None] * depth for n in big}
    rs_chips = {n: [None] * depth for n in big}
    core_idx = jnp.reshape(my_c, (1,)).astype(jnp.int32)
    chip_idx = jnp.reshape(2 * my_x + my_y, (1,)).astype(jnp.int32)

    def pair_add(l, which, slots, theirs):
        p4s = [_pair_add(slots[n], th, core_idx, "rs_add_" + n) for n, th in zip(which, theirs)]
        for n, p4 in zip(which, p4s):
            rs_own[n][l] = p4
        return p4s

    d_res, d_mm = None, d_stream
    pending = None
    for l in reversed(range(depth)):
        sv = saved[l]
        wl = sv["wl"]
        slots = {}
        gq, gkv = q_norm_g[l].reshape(1, ql), kv_norm_g[l].reshape(1, kvl)
        dz2, dz2_bf, gw["ln2_g"][l], gw["ln2_b"][l] = _ln_bwd(
            d_res, d_mm, sv["x1"], sv["y2"], ln2_g[l].reshape(1, d), ln2_b[l].reshape(1, d), alpha, "ln2_bwd")
        if pending is None:
            dw_down = _matmul(sv["act"], dz2_bf, "tn", bf16, "mm_down_dw", tm=dff // 4)
        else:
            dw_down, theirs = _matmul(sv["act"], dz2_bf, "tn", bf16, "mm_down_dw", tm=dff // 4,
                                      comm=_sibling_swap([pending["slots"][n] for n in rest]))
            rest_p4s = pair_add(pending["layer"], rest, pending["slots"], theirs)
        slots["w_down"] = dw_down.reshape(N_DEV, -1, d)
        d_act, theirs = _matmul(dz2_bf, wl["w_down"], "nt", f32, "mm_down_dx", tn=dff // 4,
                                comm=_sibling_swap([slots["w_down"]]))
        down_p4s = pair_add(l, ("w_down",), slots, theirs)
        ffn_args = (sv["up"], ffn_w_full[l], ffn_conv_b[l].reshape(1, 2 * dff), d_act, nb, seq, dff)
        if pending is None:
            d_up, dffw, dffb = _ffn_act_bwd(*ffn_args)
        else:
            d_up, dffw, dffb, (rs_chips["w_up"][pending["layer"]],) = _ffn_act_bwd(
                *ffn_args, comm=_chip_swap(pending["up_p4s"], peers=(2,), into=pending["up_partial"]))
        gw["ffn_conv_w"][l] = jnp.concatenate([dffw[0], dffw[1]], axis=-1)
        gw["ffn_conv_b"][l] = dffb.reshape(2 * dff)
        d_x1, (rs_chips["w_down"][l],) = _matmul_up_dx(d_up, wl["w_up"], "mm_up_dx", comm=_chip_swap(down_p4s))
        if pending is None:
            slots["w_up"] = _matmul_up_dw(sv["x1_bf"], d_up, N_DEV, "mm_up_dw")
        else:
            slots["w_up"], from_chips = _matmul_up_dw(sv["x1_bf"], d_up, N_DEV, "mm_up_dw", comm=_chip_swap(rest_p4s))
            for n, fc in zip(rest, from_chips):
                rs_chips[n][pending["layer"]] = fc
        dz1, dz1_bf, gw["ln1_g"][l], gw["ln1_b"][l] = _ln_bwd(
            dz2, d_x1, sv["xs"], sv["y1"], ln1_g[l].reshape(1, d), ln1_b[l].reshape(1, d), alpha, "ln1_bwd")
        d_mixed, theirs = _matmul(dz1_bf, wl["w_out"], "nt", f32, "mm_out_dx", comm=_sibling_swap([slots["w_up"]]))
        up_p4s = pair_add(l, ("w_up",), slots, theirs)
        slots["w_out"] = _matmul(sv["mixed"], dz1_bf, "tn", bf16, "mm_out_dw").reshape(N_DEV, -1, d)
        d_upool, dwp, dps = _pool_bwd(sv["h"], w_pool_2d[l], pool_scale[l].reshape(1, pw), d_mixed,
                                      (heads * LANES + cw) // pw, lay, nb, seq)
        gw["w_pool"][l] = dwp.reshape(w_pool.shape[1:])
        gw["pool_scale"][l] = dps.reshape(pw)
        d_hconv, dclg, dclb = _convln_bwd(sv["hconv"], conv_ln_g[l].reshape(1, cw), conv_ln_b[l].reshape(1, cw), d_mixed,
                                          heads * LANES // cw)
        gw["conv_ln_g"][l], gw["conv_ln_b"][l] = dclg.reshape(cw), dclb.reshape(cw)
        d_conv, gw["conv_w"][l], dcb = _conv_bwd(sv["h"], conv_w_full[l], d_hconv, nb, seq, cw)
        gw["conv_b"][l] = dcb.reshape(cw)
        dq_ext, dkv, dkrd, up_partial = _attn_bwd(sv["q_ext"], qt, sv["kv"], sv["krd"], sv["lse"], d_mixed, nb, seq, heads,
                                                  comm=_chip_swap(up_p4s, peers=(0, 1, 2) if l == 0 else (0, 1)))
        if l == 0:
            rs_chips["w_up"][l] = up_partial[0]
        d_qn = _matmul(dq_ext, wl["w_uq"], "nt", f32, "mm_uq_dx")
        dwq = _matmul(sv["qn"], dq_ext, "tn", f32, "mm_uq_dw", tn=2048, tk=1024).reshape(ql, heads, 2 * LANES)
        dwq_rope = dwq[..., QK_NOPE_DIM:QK_NOPE_DIM + QK_ROPE_DIM] + _rot_cols_t(dwq[..., QK_NOPE_DIM + QK_ROPE_DIM:])
        slots["w_uq"] = jnp.concatenate([dwq[..., :QK_NOPE_DIM], dwq_rope], axis=-1).astype(bf16).reshape(
            N_DEV, -1, QK_NOPE_DIM + QK_ROPE_DIM)
        d_kvn = _matmul(dkv, wl["w_ukv"], "nt", f32, "mm_ukv_dx")
        slots["w_ukv"] = _matmul(sv["kvn"], dkv, "tn", bf16, "mm_ukv_dw", tn=2048, tk=1024).reshape(N_DEV, -1, 2 * LANES)
        d_cq, d_ckv, d_kr, dgq, dgkv = _prep_bwd(sv["h"], cs, gq, gkv, d_qn, d_kvn, dkrd, lay)
        gw["q_norm_g"][l], gw["kv_norm_g"][l] = dgq.reshape(ql), dgkv.reshape(kvl)
        d_h = jnp.concatenate([d_conv, d_cq, d_upool, d_ckv, d_kr], axis=-1)
        if l > 0:
            d_xs = _matmul(d_h, wl["w_in"], "nt", f32, "mm_in_dx", tm=512, tk=in_pad)
            dwi = _matmul(sv["xs_bf"], d_h, "tn", f32, "mm_in_dw", tn=in_pad, tk=1024)
        else:
            flat_small = jnp.concatenate([jnp.stack(gw[n]).astype(f32).reshape(-1) for n in small]).reshape(-1, LANES)
            d_xs, small_half = _matmul(d_h, wl["w_in"], "nt", f32, "mm_in_dx", tm=512, tk=in_pad,
                                       comm=_gather_own([flat_small]))
            dwi, (gathered_small,) = _matmul(sv["xs_bf"], d_h, "tn", f32, "mm_in_dw", tn=in_pad, tk=1024,
                                             comm=_gather_pass(small_half))
        dkr_cols = dwi[:, lay["off_kr"]:lay["off_kr"] + QK_ROPE_DIM] + _rot_cols_t(dwi[:, lay["off_kr"] + QK_ROPE_DIM:])
        dwi_nat = jnp.concatenate(
            [dwi[:, lay["off_q"]:lay["off_q"] + ql], dwi[:, lay["off_kv"]:lay["off_kv"] + kvl], dkr_cols,
             _deinterleave_halves(dwi[:, :2 * cw], SEQ_CT), dwi[:, lay["off_pool"]:lay["off_pool"] + pw]],
            axis=-1).astype(bf16)
        slots["w_in"] = jnp.moveaxis(dwi_nat.reshape(d, N_DEV, -1), 1, 0)
        pending = dict(layer=l, slots=slots, up_p4s=up_p4s, up_partial=up_partial)
        d_res, d_mm = dz1, d_xs

    grad_x, _, d_ln_in_g, d_ln_in_b, theirs = _ln_bwd(
        d_res, d_mm, x2, None, ln_in_g.reshape(1, d), ln_in_b.reshape(1, d), alpha, "ln_in_bwd",
        comm=_sibling_swap([pending["slots"][n] for n in rest]))
    grad_x = grad_x.reshape(x.shape)
    rest_p4s = pair_add(pending["layer"], rest, pending["slots"], theirs)

    grads, deltas, new_m, new_v = {}, {}, {}, {}

    def finish(n, parts):
        shp = weights[n].shape
        rows = math.prod(shp[:-1]) if len(shp) > 1 else 1
        as2d = lambda a: a.reshape(rows, shp[-1])
        parts = [p.reshape(p.shape[0], rows, shp[-1]) for p in parts]
        g, dl, nm, nv = _adamw(parts, as2d(weights[n]), as2d(mom1[n]), as2d(mom2[n]), "adamw_" + n)
        grads[n], deltas[n], new_m[n], new_v[n] = (a.reshape(shp) for a in (g, dl, nm, nv))

    def finish_big(n, comm=None):
        shp = weights[n].shape
        as2d = lambda a: a.reshape(-1, shp[-1])
        res = _adamw_layers(rs_own[n], rs_chips[n], chip_idx, as2d(weights[n]), as2d(mom1[n]), as2d(mom2[n]),
                            "adamw_" + n, comm)
        outs, comm_outs = res if comm is not None else (res, None)
        grads[n], deltas[n], new_m[n], new_v[n] = (a.reshape(shp) for a in outs)
        return comm_outs

    ln_in = ("ln_in_g", "ln_in_b")
    flat_ln = jnp.concatenate([d_ln_in_g.reshape(-1), d_ln_in_b.reshape(-1)]).reshape(-1, LANES)
    gathered_ln = _all_gather([flat_ln], "ag_ln_in_grads")[0].reshape(N_DEV, 2, d)
    finish_big("w_up")
    from_chips = finish_big("w_down", _chip_swap(rest_p4s))
    for n, fc in zip(rest, from_chips):
        rs_chips[n][pending["layer"]] = fc
    for n in rest:
        finish_big(n)
    for i, n in enumerate(ln_in):
        finish(n, [gathered_ln[:, i]])
    gathered_small = gathered_small.reshape(N_DEV, -1)
    off = 0
    for n in small:
        shape = (depth,) + gw[n][0].shape
        size = math.prod(shape)
        part = gathered_small[:, off:off + size].reshape((N_DEV,) + shape)
        off += size
        if n in ("conv_w", "ffn_conv_w"):
            width = weights[n].shape[-1]
            part = lax.dynamic_slice_in_dim(part, my_dev * width, width, axis=part.ndim - 1)
        finish(n, [part])

    return (loss, grad_x, *[grads[n] for n in names], *[deltas[n] for n in names], *[new_m[n] for n in names],
            *[new_v[n] for n in names])
```

```python
import functools
import math

import jax
import jax.numpy as jnp
from jax import lax
from jax.experimental import pallas as pl
from jax.experimental.pallas import tpu as pltpu

f32 = jnp.float32
bf16 = jnp.bfloat16

QK_NOPE_DIM = 128
QK_ROPE_DIM = 64
V_HEAD_DIM = 128
CONV_KERNEL = 31
FFN_CONV_KERNEL = 3
POOL_WINDOWS = (2, 4, 8, 16)
ROPE_THETA = 10000.0
LN_EPS = 1e-5
RMS_EPS = 1e-6
ADAM_LR = 0.001
ADAM_B1 = 0.9
ADAM_B2 = 0.999
ADAM_EPS = 1e-08
ADAM_WD = 0.01
ADAM_STEP = 10

N_DEV = 8
MESH_AXES = ("x", "y", "c")
V7X_VMEM_LIMIT_BYTES = 56 * 1024 * 1024
LANES = 128
NEG_INF = -1e30
MESH = pl.DeviceIdType.MESH


def _cparams(sem):
    return pltpu.CompilerParams(dimension_semantics=sem, vmem_limit_bytes=V7X_VMEM_LIMIT_BYTES)


def _tile(dim, pref):
    t = pref
    while t >= LANES:
        if dim % t == 0:
            return t
        t //= 2
    return dim


_ANY = pl.BlockSpec(memory_space=pl.ANY)


class _Comm:
    def __init__(self, inputs, out_shapes, sems, start, finish, aliases=None):
        self.inputs, self.out_shapes, self.sems = list(inputs), list(out_shapes), list(sems)
        self.start, self.finish, self.aliases = start, finish, dict(aliases or {})


def _call(body, name, grid, in_specs, out_specs, out_shape, args, scratch=(), sem=None, comm=None, prefetch=()):
    in_specs, out_specs, out_shape, scratch = list(in_specs), list(out_specs), list(out_shape), list(scratch)
    n_pre, n_in, n_out, n_scr = len(prefetch), len(in_specs), len(out_specs), len(scratch)
    c_in, c_out = (len(comm.inputs), len(comm.out_shapes)) if comm else (0, 0)

    def carrier(*refs):
        refs = list(refs)
        pre, refs = refs[:n_pre], refs[n_pre:]
        ins, refs = refs[:n_in], refs[n_in:]
        c_ins, refs = refs[:c_in], refs[c_in:]
        outs, refs = refs[:n_out], refs[n_out:]
        c_outs, refs = refs[:c_out], refs[c_out:]
        scr, c_sems = refs[:n_scr], refs[n_scr:]
        ids = [pl.program_id(a) for a in range(len(grid))]
        first = functools.reduce(lambda p, q: p & q, [i == 0 for i in ids])
        last = functools.reduce(lambda p, q: p & q, [i == g - 1 for i, g in zip(ids, grid)])
        pl.when(first)(lambda: comm.start(c_ins, c_outs, c_sems))
        body(*pre, *ins, *outs, *scr)
        pl.when(last)(lambda: comm.finish(c_ins, c_outs, c_sems))

    grid_spec = pltpu.PrefetchScalarGridSpec(
        num_scalar_prefetch=n_pre, grid=grid, in_specs=in_specs + [_ANY] * c_in, out_specs=out_specs + [_ANY] * c_out,
        scratch_shapes=scratch + (comm.sems if comm else []))
    if comm is None:
        outs = pl.pallas_call(body, name=name, grid_spec=grid_spec, out_shape=out_shape,
                              compiler_params=_cparams(sem))(*prefetch, *args)
        return list(outs), []
    outs = pl.pallas_call(
        carrier, name=name, grid_spec=grid_spec, out_shape=out_shape + comm.out_shapes,
        input_output_aliases={n_pre + n_in + a: n_out + b for a, b in comm.aliases.items()},
        compiler_params=_cparams(("arbitrary",) * len(grid)),
    )(*prefetch, *args, *comm.inputs)
    return list(outs[:n_out]), list(outs[n_out:])


def _shift_down_raw(x, k):
    if k == 0:
        return x
    row = lax.broadcasted_iota(jnp.int32, x.shape, 0)
    return jnp.where(row >= k, pltpu.roll(x, k, axis=0), 0.0)


def _shift_up_raw(x, k):
    if k == 0:
        return x
    n = x.shape[0]
    row = lax.broadcasted_iota(jnp.int32, x.shape, 0)
    return jnp.where(row < n - k, pltpu.roll(x, n - k, axis=0), 0.0)


@functools.partial(jax.custom_vjp, nondiff_argnums=(1,))
def _shift_down(x, k):
    return _shift_down_raw(x, k)


def _shift_down_fwd(x, k):
    return _shift_down_raw(x, k), None


def _shift_down_bwd(k, _, g):
    return (_shift_up_raw(g, k),)


_shift_down.defvjp(_shift_down_fwd, _shift_down_bwd)


@jax.custom_vjp
def _dup_halves(p):
    return p + pltpu.roll(p, LANES // 2, axis=1)


def _dup_halves_fwd(p):
    return p + pltpu.roll(p, LANES // 2, axis=1), None


def _dup_halves_bwd(_, g):
    return (g + pltpu.roll(g, LANES // 2, axis=1),)


_dup_halves.defvjp(_dup_halves_fwd, _dup_halves_bwd)

_NN = (((1,), (0,)), ((), ()))
_NT = (((1,), (1,)), ((), ()))
_TN = (((0,), (0,)), ((), ()))


def _dot(a, b, dims):
    return lax.dot_general(a.astype(bf16), b.astype(bf16), dims, preferred_element_type=f32)


@jax.custom_vjp
def _mm_bf16(a, b):
    return _dot(a, b, _NN)


def _mm_bf16_fwd(a, b):
    return _dot(a, b, _NN), (a, b)


def _mm_bf16_bwd(res, g):
    a, b = res
    return _dot(g, b, _NT), _dot(a, g, _TN)


_mm_bf16.defvjp(_mm_bf16_fwd, _mm_bf16_bwd)


def _layer_norm(z, g, b):
    mu = jnp.mean(z, axis=-1, keepdims=True)
    var = jnp.mean(jnp.square(z - mu), axis=-1, keepdims=True)
    return (z - mu) * lax.rsqrt(var + LN_EPS) * g + b


def _rms_norm(x, g):
    ms = jnp.mean(jnp.square(x), axis=-1, keepdims=True)
    return x * lax.rsqrt(ms + RMS_EPS) * g


def _colsum(x):
    return jnp.sum(x, axis=0, keepdims=True)


def _matmul_core(a, b, mode, grid, a_spec, b_spec, o_spec, o_shape, tile, out_dtype, name, comm=None):
    nk = grid[2]
    dims = {"nn": _NN, "nt": _NT, "tn": _TN}[mode]
    acc_in_out = out_dtype == f32

    def body(a_ref, b_ref, o_ref, *scratch):
        def prod():
            return _dot(a_ref[...], b_ref[...], dims)

        if nk == 1:
            o_ref[...] = prod().astype(out_dtype)
            return
        acc_ref = o_ref if acc_in_out else scratch[0]
        kk = pl.program_id(2)

        @pl.when(kk == 0)
        def _():
            acc_ref[...] = prod()

        if acc_in_out:
            @pl.when(kk > 0)
            def _():
                acc_ref[...] += prod()
        else:
            @pl.when((kk > 0) & (kk < nk - 1))
            def _():
                acc_ref[...] += prod()

            @pl.when(kk == nk - 1)
            def _():
                o_ref[...] = (acc_ref[...] + prod()).astype(out_dtype)

    scratch = [] if (nk == 1 or acc_in_out) else [pltpu.VMEM(tile, f32)]
    (out,), comm_outs = _call(body, name, grid, [a_spec, b_spec], [o_spec], [jax.ShapeDtypeStruct(o_shape, out_dtype)],
                              (a, b), scratch, ("parallel", "parallel", "arbitrary"), comm)
    return out if comm is None else (out, comm_outs)


def _matmul(a, b, mode, out_dtype, name, tm=1024, tn=1024, tk=2048, comm=None):
    if mode == "nn":
        (m, k), (k2, n) = a.shape, b.shape
    elif mode == "nt":
        (m, k), (n, k2) = a.shape, b.shape
    else:
        (k, m), (k2, n) = a.shape, b.shape
    assert k == k2, (name, a.shape, b.shape)
    tm, tn, tk = _tile(m, tm), _tile(n, tn), _tile(k, tk)
    a_spec = pl.BlockSpec((tk, tm), lambda i, j, kk: (kk, i)) if mode == "tn" else pl.BlockSpec((tm, tk), lambda i, j, kk: (i, kk))
    b_spec = pl.BlockSpec((tn, tk), lambda i, j, kk: (j, kk)) if mode == "nt" else pl.BlockSpec((tk, tn), lambda i, j, kk: (kk, j))
    return _matmul_core(a, b, mode, (m // tm, n // tn, k // tk), a_spec, b_spec,
                        pl.BlockSpec((tm, tn), lambda i, j, kk: (i, j)), (m, n), (tm, tn), out_dtype, name, comm)


def _matmul_up(x_bf, w_slots, name, tm=1024, comm=None):
    m, k = x_bf.shape
    s, _, ns = w_slots.shape
    tm = _tile(m, tm)
    return _matmul_core(x_bf, w_slots, "nn", (m // tm, s, 1), pl.BlockSpec((tm, k), lambda i, j, kk: (i, 0)),
                        pl.BlockSpec((None, k, ns), lambda i, j, kk: (j, 0, 0)),
                        pl.BlockSpec((tm, ns), lambda i, j, kk: (i, j)), (m, s * ns), (tm, ns), f32, name, comm)


def _matmul_up_dx(d3, w_slots, name, tm=1024, tn=1024, comm=None):
    _, m, half = d3.shape
    s, n, ns = w_slots.shape
    per_half = half // ns
    assert 2 * per_half == s, (d3.shape, w_slots.shape)
    tm, tn = _tile(m, tm), _tile(n, tn)
    return _matmul_core(d3, w_slots, "nt", (m // tm, n // tn, s),
                        pl.BlockSpec((None, tm, ns), lambda i, j, kk: (kk // per_half, i, kk % per_half)),
                        pl.BlockSpec((None, tn, ns), lambda i, j, kk: (kk, j, 0)),
                        pl.BlockSpec((tm, tn), lambda i, j, kk: (i, j)), (m, n), (tm, tn), f32, name, comm)


def _matmul_up_dw(x_bf, d3, n_slots, name, tm=1024, tk=2048, comm=None):
    k, m = x_bf.shape
    _, _, half = d3.shape
    ns = 2 * half // n_slots
    per_half = n_slots // 2
    tm, tk = _tile(m, tm), _tile(k, tk)
    return _matmul_core(x_bf, d3, "tn", (m // tm, n_slots, k // tk), pl.BlockSpec((tk, tm), lambda i, j, kk: (kk, i)),
                        pl.BlockSpec((None, tk, ns), lambda i, j, kk: (j // per_half, kk, j % per_half)),
                        pl.BlockSpec((None, tm, ns), lambda i, j, kk: (j, i, 0)), (n_slots, m, ns), (tm, ns), bf16, name,
                        comm)


ROW_TILE = 256


def _rows(width, col_block=0):
    return pl.BlockSpec((ROW_TILE, width), lambda i, cb=col_block: (i, cb))


def _whole(shape):
    return pl.BlockSpec(shape, lambda i: (0,) * len(shape))


def _ln_fwd(x, y, g, b, alpha, name, comm=None):
    t, d = x.shape

    def body(*refs):
        if y is None:
            x_ref, g_ref, b_ref, o_ref, ob_ref = refs
            z = x_ref[...]
        else:
            x_ref, y_ref, g_ref, b_ref, o_ref, ob_ref = refs
            z = alpha * x_ref[...] + y_ref[...]
        out = _layer_norm(z, g_ref[...], b_ref[...])
        o_ref[...] = out
        ob_ref[...] = out.astype(bf16)

    ins = [x] + ([] if y is None else [y]) + [g, b]
    specs = [_rows(d)] + ([] if y is None else [_rows(d)]) + [_whole((1, d)), _whole((1, d))]
    (out, out_bf), comm_outs = _call(
        body, name, (t // ROW_TILE,), specs, [_rows(d), _rows(d)],
        [jax.ShapeDtypeStruct((t, d), f32), jax.ShapeDtypeStruct((t, d), bf16)], ins, (), ("parallel",), comm)
    return (out, out_bf) if comm is None else (out, out_bf, comm_outs)


def _ln_bwd(d_res, d_mm, x, y, g, b, alpha, name, comm=None):
    t, d = x.shape
    has_res, has_mm, has_y = d_res is not None, d_mm is not None, y is not None

    def body(*refs):
        refs = list(refs)
        d_res_ref = refs.pop(0) if has_res else None
        d_mm_ref = refs.pop(0) if has_mm else None
        x_ref = refs.pop(0)
        y_ref = refs.pop(0) if has_y else None
        g_ref, b_ref, dz_ref, dzb_ref, dg_ref, db_ref = refs
        ct = None
        if has_res:
            ct = alpha * d_res_ref[...]
        if has_mm:
            ct = d_mm_ref[...] if ct is None else ct + d_mm_ref[...]
        z = x_ref[...] if not has_y else alpha * x_ref[...] + y_ref[...]
        _, vjp = jax.vjp(_layer_norm, z, g_ref[...], b_ref[...])
        dz, dg, db = vjp(ct)
        dz_ref[...] = dz
        dzb_ref[...] = dz.astype(bf16)

        @pl.when(pl.program_id(0) == 0)
        def _():
            dg_ref[...] = jnp.zeros_like(dg_ref)
            db_ref[...] = jnp.zeros_like(db_ref)

        dg_ref[...] += dg
        db_ref[...] += db

    ins = [a for a in (d_res, d_mm, x, y) if a is not None] + [g, b]
    specs = [_rows(d) for a in (d_res, d_mm, x, y) if a is not None] + [_whole((1, d)), _whole((1, d))]
    outs, comm_outs = _call(
        body, name, (t // ROW_TILE,), specs, [_rows(d), _rows(d), _whole((1, d)), _whole((1, d))],
        [jax.ShapeDtypeStruct((t, d), f32), jax.ShapeDtypeStruct((t, d), bf16),
         jax.ShapeDtypeStruct((1, d), f32), jax.ShapeDtypeStruct((1, d), f32)], ins, (), ("arbitrary",), comm)
    return tuple(outs) if comm is None else (*outs, comm_outs)


def _loss_call(xf, target):
    t, d = xf.shape

    def body(x_ref, t_ref, dx_ref, loss_ref):
        err = x_ref[...] - t_ref[...]
        dx_ref[...] = err * (1.0 / d)

        @pl.when(pl.program_id(0) == 0)
        def _():
            loss_ref[...] = jnp.zeros_like(loss_ref)

        part = 0.5 * jnp.sum(jnp.mean(jnp.square(err), axis=-1, keepdims=True), axis=0, keepdims=True)
        loss_ref[...] += jnp.broadcast_to(part, loss_ref.shape)

    return pl.pallas_call(
        body, name="loss_head", grid=(t // ROW_TILE,), in_specs=[_rows(d), _rows(d)],
        out_specs=[_rows(d), _whole((1, LANES))],
        out_shape=[jax.ShapeDtypeStruct((t, d), f32), jax.ShapeDtypeStruct((1, LANES), f32)],
        compiler_params=_cparams(("arbitrary",)),
    )(xf, target)


def _rope_tables(pos, inv, scale):
    t = pos.shape[0]

    def body(pos_ref, inv_ref, cs_ref, qt_ref):
        ang = pos_ref[...].astype(f32) * inv_ref[...]
        lane = lax.broadcasted_iota(jnp.int32, ang.shape, 1)
        cs = jnp.where(lane < LANES // 2, jnp.cos(ang), jnp.sin(ang))
        cs_ref[...] = cs
        qt_ref[:, :LANES] = jnp.full((ROW_TILE, LANES), scale, f32)
        qt_ref[:, LANES:] = scale * cs

    return pl.pallas_call(
        body, name="rope_tables", grid=(t // ROW_TILE,),
        in_specs=[pl.BlockSpec((ROW_TILE, 1), lambda i: (i, 0)), _whole((1, LANES))],
        out_specs=[_rows(LANES), _rows(2 * LANES)],
        out_shape=[jax.ShapeDtypeStruct((t, LANES), f32), jax.ShapeDtypeStruct((t, 2 * LANES), f32)],
        compiler_params=_cparams(("parallel",)),
    )(pos, inv)


def _prep_fn(cq, ckv, kr, cs, gq, gkv):
    return _rms_norm(cq, gq), _rms_norm(ckv, gkv), _dup_halves(kr * cs)


def _prep_fwd(h, cs, gq, gkv, lay):
    t = h.shape[0]
    ql, kvl = lay["ql"], lay["kvl"]

    def body(cq_ref, ckv_ref, kr_ref, cs_ref, gq_ref, gkv_ref, qn_ref, kvn_ref, krd_ref):
        qn, kvn, krd = _prep_fn(cq_ref[...], ckv_ref[...], kr_ref[...], cs_ref[...], gq_ref[...], gkv_ref[...])
        qn_ref[...] = qn.astype(bf16)
        kvn_ref[...] = kvn.astype(bf16)
        krd_ref[...] = krd.astype(bf16)

    return pl.pallas_call(
        body, name="prep_fwd", grid=(t // ROW_TILE,),
        in_specs=[_rows(ql, lay["off_q"] // ql), _rows(kvl, lay["off_kv"] // kvl), _rows(LANES, lay["off_kr"] // LANES),
                  _rows(LANES), _whole((1, ql)), _whole((1, kvl))],
        out_specs=[_rows(ql), _rows(kvl), _rows(LANES)],
        out_shape=[jax.ShapeDtypeStruct((t, ql), bf16), jax.ShapeDtypeStruct((t, kvl), bf16),
                   jax.ShapeDtypeStruct((t, LANES), bf16)],
        compiler_params=_cparams(("parallel",)),
    )(h, h, h, cs, gq, gkv)


def _prep_bwd(h, cs, gq, gkv, d_qn, d_kvn, d_krd, lay):
    t = h.shape[0]
    ql, kvl = lay["ql"], lay["kvl"]

    def body(cq_ref, ckv_ref, kr_ref, cs_ref, gq_ref, gkv_ref, dqn_ref, dkvn_ref, dkrd_ref,
             dcq_ref, dckv_ref, dkr_ref, dgq_ref, dgkv_ref):
        _, vjp = jax.vjp(_prep_fn, cq_ref[...], ckv_ref[...], kr_ref[...], cs_ref[...], gq_ref[...], gkv_ref[...])
        dcq, dckv, dkr, _, dgq, dgkv = vjp((dqn_ref[...], dkvn_ref[...], dkrd_ref[...].astype(f32)))
        dcq_ref[...] = dcq.astype(bf16)
        dckv_ref[...] = dckv.astype(bf16)
        dkr_ref[...] = dkr.astype(bf16)

        @pl.when(pl.program_id(0) == 0)
        def _():
            dgq_ref[...] = jnp.zeros_like(dgq_ref)
            dgkv_ref[...] = jnp.zeros_like(dgkv_ref)

        dgq_ref[...] += dgq
        dgkv_ref[...] += dgkv

    return pl.pallas_call(
        body, name="prep_bwd", grid=(t // ROW_TILE,),
        in_specs=[_rows(ql, lay["off_q"] // ql), _rows(kvl, lay["off_kv"] // kvl), _rows(LANES, lay["off_kr"] // LANES),
                  _rows(LANES), _whole((1, ql)), _whole((1, kvl)), _rows(ql), _rows(kvl), _rows(LANES)],
        out_specs=[_rows(ql), _rows(kvl), _rows(LANES), _whole((1, ql)), _whole((1, kvl))],
        out_shape=[jax.ShapeDtypeStruct((t, ql), bf16), jax.ShapeDtypeStruct((t, kvl), bf16),
                   jax.ShapeDtypeStruct((t, LANES), bf16), jax.ShapeDtypeStruct((1, ql), f32),
                   jax.ShapeDtypeStruct((1, kvl), f32)],
        compiler_params=_cparams(("arbitrary",)),
    )(h, h, h, cs, gq, gkv, d_qn, d_kvn, d_krd)


def _convln_fn(hc, g, b):
    y = _layer_norm(hc, g, b)
    return y * jax.nn.sigmoid(y)


def _convln_fwd(hconv, g, b):
    t, cw = hconv.shape

    def body(h_ref, g_ref, b_ref, o_ref):
        o_ref[...] = _convln_fn(h_ref[...], g_ref[...], b_ref[...]).astype(bf16)

    return pl.pallas_call(
        body, name="convln_fwd", grid=(t // ROW_TILE,),
        in_specs=[_rows(cw), _whole((1, cw)), _whole((1, cw))], out_specs=_rows(cw),
        out_shape=jax.ShapeDtypeStruct((t, cw), bf16), compiler_params=_cparams(("parallel",)),
    )(hconv, g, b)


def _convln_bwd(hconv, g, b, d_mixed, col_block):
    t, cw = hconv.shape

    def body(h_ref, g_ref, b_ref, dy_ref, dh_ref, dg_ref, db_ref):
        _, vjp = jax.vjp(_convln_fn, h_ref[...], g_ref[...], b_ref[...])
        dh, dg, db = vjp(dy_ref[...])
        dh_ref[...] = dh

        @pl.when(pl.program_id(0) == 0)
        def _():
            dg_ref[...] = jnp.zeros_like(dg_ref)
            db_ref[...] = jnp.zeros_like(db_ref)

        dg_ref[...] += dg
        db_ref[...] += db

    return pl.pallas_call(
        body, name="convln_bwd", grid=(t // ROW_TILE,),
        in_specs=[_rows(cw), _whole((1, cw)), _whole((1, cw)), _rows(cw, col_block)],
        out_specs=[_rows(cw), _whole((1, cw)), _whole((1, cw))],
        out_shape=[jax.ShapeDtypeStruct((t, cw), f32), jax.ShapeDtypeStruct((1, cw), f32),
                   jax.ShapeDtypeStruct((1, cw), f32)],
        compiler_params=_cparams(("arbitrary",)),
    )(hconv, g, b, d_mixed)


SEQ_CT = 128


def _conv_fwd(h, w, b, nb, seq, cw):
    ct, kk = SEQ_CT, w.shape[0]
    ncb = cw // ct

    def body(h_ref, w_ref, b_ref, o_ref):
        blk = h_ref[...]
        a, g = blk[:, :ct], blk[:, ct:]
        hc = a * jax.nn.sigmoid(g)
        acc = jnp.zeros_like(hc)
        for k in range(kk):
            acc = acc + w_ref[k:k + 1, :] * _shift_down_raw(hc, kk - 1 - k)
        o_ref[...] = acc + b_ref[...]

    return pl.pallas_call(
        body, name="conv_fwd", grid=(ncb, nb),
        in_specs=[pl.BlockSpec((seq, 2 * ct), lambda j, bb: (bb, j)), pl.BlockSpec((kk, ct), lambda j, bb: (0, j)),
                  pl.BlockSpec((1, ct), lambda j, bb: (0, j))],
        out_specs=pl.BlockSpec((seq, ct), lambda j, bb: (bb, j)),
        out_shape=jax.ShapeDtypeStruct((nb * seq, cw), f32),
        compiler_params=_cparams(("parallel", "parallel")),
    )(h, w, b)


def _conv_bwd(h, w, d_hconv, nb, seq, cw):
    ct, kk = SEQ_CT, w.shape[0]
    ncb = cw // ct

    def body(h_ref, w_ref, dy_ref, dh_ref, dw_ref, db_ref):
        blk = h_ref[...]
        a, g = blk[:, :ct], blk[:, ct:]
        sg = jax.nn.sigmoid(g)
        hc = a * sg
        dy = dy_ref[...]
        dhc = jnp.zeros_like(hc)

        @pl.when(pl.program_id(1) == 0)
        def _():
            dw_ref[...] = jnp.zeros_like(dw_ref)
            db_ref[...] = jnp.zeros_like(db_ref)

        for k in range(kk):
            dhc = dhc + w_ref[k:k + 1, :] * _shift_up_raw(dy, kk - 1 - k)
            dw_ref[k:k + 1, :] += _colsum(dy * _shift_down_raw(hc, kk - 1 - k))
        db_ref[...] += _colsum(dy)
        dh_ref[:, :ct] = (dhc * sg).astype(bf16)
        dh_ref[:, ct:] = (dhc * a * sg * (1.0 - sg)).astype(bf16)

    return pl.pallas_call(
        body, name="conv_bwd", grid=(ncb, nb),
        in_specs=[pl.BlockSpec((seq, 2 * ct), lambda j, bb: (bb, j)), pl.BlockSpec((kk, ct), lambda j, bb: (0, j)),
                  pl.BlockSpec((seq, ct), lambda j, bb: (bb, j))],
        out_specs=[pl.BlockSpec((seq, 2 * ct), lambda j, bb: (bb, j)), pl.BlockSpec((kk, ct), lambda j, bb: (0, j)),
                   pl.BlockSpec((1, ct), lambda j, bb: (0, j))],
        out_shape=[jax.ShapeDtypeStruct((nb * seq, 2 * cw), bf16), jax.ShapeDtypeStruct((kk, cw), f32),
                   jax.ShapeDtypeStruct((1, cw), f32)],
        compiler_params=_cparams(("parallel", "arbitrary")),
    )(h, w, d_hconv)


def _pool_fn(u, wp, scale, pg):
    seq = u.shape[0]
    t1 = (lax.broadcasted_iota(jnp.int32, (seq, 1), 0) + 1).astype(f32)
    outs = []
    for gi, win in enumerate(POOL_WINDOWS):
        ug = u[:, gi * pg:(gi + 1) * pg]
        acc, span = ug, 1
        while span < win:
            acc = acc + _shift_down(acc, span)
            span *= 2
        d = acc / jnp.minimum(t1, float(win)) - ug
        outs.append(_mm_bf16(d, wp[gi * pg:(gi + 1) * pg, :]) * scale[:, gi * pg:(gi + 1) * pg])
    return outs


def _pool_fwd(h, wp, scale, lay, nb, seq):
    pw, pg = lay["pw"], lay["pg"]

    def body(u_ref, wp_ref, sc_ref, o_ref):
        outs = _pool_fn(u_ref[...], wp_ref[...], sc_ref[...], pg)
        for gi in range(len(POOL_WINDOWS)):
            o_ref[:, gi * pg:(gi + 1) * pg] = outs[gi].astype(bf16)

    return pl.pallas_call(
        body, name="pool_fwd", grid=(nb,),
        in_specs=[pl.BlockSpec((seq, pw), lambda bb: (bb, lay["off_pool"] // pw)), _whole((pw, pg)), _whole((1, pw))],
        out_specs=pl.BlockSpec((seq, pw), lambda bb: (bb, 0)),
        out_shape=jax.ShapeDtypeStruct((nb * seq, pw), bf16),
        compiler_params=_cparams(("parallel",)),
    )(h, wp, scale)


def _pool_bwd(h, wp, scale, d_mixed, col_block, lay, nb, seq):
    pw, pg = lay["pw"], lay["pg"]
    ng = len(POOL_WINDOWS)

    def body(u_ref, wp_ref, sc_ref, dy_ref, du_ref, dwp_ref, dsc_ref):
        _, vjp = jax.vjp(functools.partial(_pool_fn, pg=pg), u_ref[...], wp_ref[...], sc_ref[...])
        dy = dy_ref[...]
        du, dwp, dsc = vjp([dy[:, gi * pg:(gi + 1) * pg] for gi in range(ng)])
        du_ref[...] = du.astype(bf16)

        @pl.when(pl.program_id(0) == 0)
        def _():
            dwp_ref[...] = jnp.zeros_like(dwp_ref)
            dsc_ref[...] = jnp.zeros_like(dsc_ref)

        dwp_ref[...] += dwp
        dsc_ref[...] += dsc

    return pl.pallas_call(
        body, name="pool_bwd", grid=(nb,),
        in_specs=[pl.BlockSpec((seq, pw), lambda bb: (bb, lay["off_pool"] // pw)), _whole((pw, pg)), _whole((1, pw)),
                  pl.BlockSpec((seq, pw), lambda bb: (bb, col_block))],
        out_specs=[pl.BlockSpec((seq, pw), lambda bb: (bb, 0)), _whole((pw, pg)), _whole((1, pw))],
        out_shape=[jax.ShapeDtypeStruct((nb * seq, pw), bf16), jax.ShapeDtypeStruct((pw, pg), f32),
                   jax.ShapeDtypeStruct((1, pw), f32)],
        compiler_params=_cparams(("arbitrary",)),
    )(h, wp, scale, d_mixed)


FFN_ROWS = 32
FFN_FWD_CT = 256
FFN_BWD_CT = 128
SUBLANES = 8


def _rows_before(ref, r0, s, n, cols):
    if s == 0:
        return ref[r0:r0 + n, cols]
    if r0 == 0:
        x = ref[0:n, cols]
        row = lax.broadcasted_iota(jnp.int32, x.shape, 0)
        return jnp.where(row >= s, pltpu.roll(x, s, axis=0), 0.0)
    return ref[pl.ds(r0 - s, n), cols]


def _ffn_conv_rows(x_ref, w_ref, b_ref, r0, n, cols):
    kk = w_ref.shape[0]
    xs = [_rows_before(x_ref, r0, s, n, cols) for s in range(kk)]
    c = b_ref[:, cols] + w_ref[kk - 1:kk, cols] * xs[0]
    for k in range(kk - 1):
        c = c + w_ref[k:k + 1, cols] * xs[kk - 1 - k]
    return c, xs


def _lane_tiles(width):
    return [slice(c0, c0 + LANES) for c0 in range(0, width, LANES)]


def _fold_rows(x):
    out = x[0:SUBLANES]
    for i in range(1, x.shape[0] // SUBLANES):
        out = out + x[i * SUBLANES:(i + 1) * SUBLANES]
    return out


def _ffn_specs(seq, ct, kk, nct):
    return [pl.BlockSpec((seq, ct), lambda j, bb: (bb, j)), pl.BlockSpec((seq, ct), lambda j, bb: (bb, nct + j)),
            pl.BlockSpec((kk, ct), lambda j, bb: (0, j)), pl.BlockSpec((kk, ct), lambda j, bb: (0, nct + j)),
            pl.BlockSpec((1, ct), lambda j, bb: (0, j)), pl.BlockSpec((1, ct), lambda j, bb: (0, nct + j))]


def _ffn_act_fwd(up, w, b, nb, seq, dff, comm=None):
    ct, kk = _tile(dff, FFN_FWD_CT), w.shape[0]
    rows = min(FFN_ROWS, seq)

    def body(ua_ref, ug_ref, wa_ref, wg_ref, ba_ref, bg_ref, o_ref):
        for cols in _lane_tiles(ct):
            for r0 in range(0, seq, rows):
                a, _ = _ffn_conv_rows(ua_ref, wa_ref, ba_ref, r0, rows, cols)
                g, _ = _ffn_conv_rows(ug_ref, wg_ref, bg_ref, r0, rows, cols)
                o_ref[r0:r0 + rows, cols] = (a * g * jax.nn.sigmoid(g)).astype(bf16)

    (act,), comm_outs = _call(body, "ffn_act_fwd", (dff // ct, nb), _ffn_specs(seq, ct, kk, dff // ct),
                              [pl.BlockSpec((seq, ct), lambda j, bb: (bb, j))],
                              [jax.ShapeDtypeStruct((nb * seq, dff), bf16)], (up, up, w, w, b, b), (),
                              ("parallel", "parallel"), comm)
    return act if comm is None else (act, comm_outs)


def _ffn_act_bwd(up, w, b, d_act, nb, seq, dff, comm=None):
    ct, kk = _tile(dff, FFN_BWD_CT), w.shape[0]
    rows = min(FFN_ROWS, seq)

    def body(ua_ref, ug_ref, wa_ref, wg_ref, ba_ref, bg_ref, da_ref, du_ref, dw_ref, db_ref, dc_ref):
        @pl.when(pl.program_id(1) == 0)
        def _():
            dw_ref[...] = jnp.zeros_like(dw_ref)
            db_ref[...] = jnp.zeros_like(db_ref)

        for cols in _lane_tiles(ct):
            dw_acc = [[jnp.zeros((SUBLANES, LANES), f32) for _ in range(kk)] for _ in range(2)]
            db_acc = [jnp.zeros((SUBLANES, LANES), f32) for _ in range(2)]
            for r0 in range(0, seq, rows):
                a, xa = _ffn_conv_rows(ua_ref, wa_ref, ba_ref, r0, rows, cols)
                g, xg = _ffn_conv_rows(ug_ref, wg_ref, bg_ref, r0, rows, cols)
                sg = jax.nn.sigmoid(g)
                dact = da_ref[r0:r0 + rows, cols]
                dcs = (dact * g * sg, dact * a * sg * (1.0 + g * (1.0 - sg)))
                for hf, (dc, xs) in enumerate(zip(dcs, (xa, xg))):
                    dc_ref[hf, r0:r0 + rows, cols] = dc
                    db_acc[hf] = db_acc[hf] + _fold_rows(dc)
                    for k in range(kk):
                        dw_acc[hf][k] = dw_acc[hf][k] + _fold_rows(dc * xs[kk - 1 - k])
            for hf in range(2):
                dc_ref[hf, seq:seq + SUBLANES, cols] = jnp.zeros((SUBLANES, LANES), f32)
                db_ref[hf, :, cols] += _colsum(db_acc[hf])
                for k in range(kk):
                    dw_ref[hf, k:k + 1, cols] += _colsum(dw_acc[hf][k])
            for r0 in range(0, seq, rows):
                for hf, w_ref in enumerate((wa_ref, wg_ref)):
                    dsrc = w_ref[kk - 1:kk, cols] * dc_ref[hf, r0:r0 + rows, cols]
                    for k in range(kk - 1):
                        dsrc = dsrc + w_ref[k:k + 1, cols] * dc_ref[hf, pl.ds(r0 + kk - 1 - k, rows), cols]
                    du_ref[hf, r0:r0 + rows, cols] = dsrc.astype(bf16)

    (d_up, dw, db), comm_outs = _call(
        body, "ffn_act_bwd", (dff // ct, nb),
        _ffn_specs(seq, ct, kk, dff // ct) + [pl.BlockSpec((seq, ct), lambda j, bb: (bb, j))],
        [pl.BlockSpec((2, seq, ct), lambda j, bb: (0, bb, j)), pl.BlockSpec((2, kk, ct), lambda j, bb: (0, 0, j)),
         pl.BlockSpec((2, 1, ct), lambda j, bb: (0, 0, j))],
        [jax.ShapeDtypeStruct((2, nb * seq, dff), bf16), jax.ShapeDtypeStruct((2, kk, dff), f32),
         jax.ShapeDtypeStruct((2, 1, dff), f32)],
        (up, up, w, w, b, b, d_act), [pltpu.VMEM((2, seq + SUBLANES, ct), f32)], ("parallel", "arbitrary"), comm)
    return (d_up, dw, db) if comm is None else (d_up, dw, db, comm_outs)


def _fill_keys(k_scr, kn_ref, krd_ref):
    @pl.when(pl.program_id(2) == 0)
    def _():
        k_scr[:, :LANES] = kn_ref[...]
        k_scr[:, LANES:] = krd_ref[...]


def _scores(q_ref, qt_ref, k_scr, qblk, tq):
    klen = (qblk + 1) * tq
    q = (q_ref[...] * qt_ref[...]).astype(bf16)
    s = _dot(q, k_scr[0:klen, :], _NT)
    row = qblk * tq + lax.broadcasted_iota(jnp.int32, s.shape, 0)
    col = lax.broadcasted_iota(jnp.int32, s.shape, 1)
    return q, jnp.where(col <= row, s, NEG_INF)


def _per_q_block(nq, fn):
    qi = pl.program_id(2)
    for qblk in range(nq):
        pl.when(qi == qblk)(functools.partial(fn, qblk))


def _attn_fwd(q_ext, qt, kv, krd, nb, seq, heads, comm=None):
    tq = _tile(seq, 512)
    nq = seq // tq

    def body(q_ref, qt_ref, kn_ref, krd_ref, v_ref, o_ref, lse_ref, k_scr):
        _fill_keys(k_scr, kn_ref, krd_ref)

        def work(qblk):
            klen = (qblk + 1) * tq
            _, s = _scores(q_ref, qt_ref, k_scr, qblk, tq)
            m = jnp.max(s, axis=-1, keepdims=True)
            p = jnp.exp(s - m)
            l = jnp.sum(p, axis=-1, keepdims=True)
            o_ref[...] = (_dot(p, v_ref[0:klen, :], _NN) / l).astype(bf16)
            lse_ref[...] = m + jnp.log(l)

        _per_q_block(nq, work)

    (o, lse), comm_outs = _call(
        body, "attn_fwd", (nb, heads, nq),
        [pl.BlockSpec((tq, 2 * LANES), lambda b, h, i: (b * nq + i, h)),
         pl.BlockSpec((tq, 2 * LANES), lambda b, h, i: (b * nq + i, 0)),
         pl.BlockSpec((seq, LANES), lambda b, h, i: (b, 2 * h)),
         pl.BlockSpec((seq, LANES), lambda b, h, i: (b, 0)),
         pl.BlockSpec((seq, LANES), lambda b, h, i: (b, 2 * h + 1))],
        [pl.BlockSpec((tq, LANES), lambda b, h, i: (b * nq + i, h)),
         pl.BlockSpec((None, tq, 1), lambda b, h, i: (b * heads + h, i, 0))],
        [jax.ShapeDtypeStruct((nb * seq, heads * LANES), bf16), jax.ShapeDtypeStruct((nb * heads, seq, 1), f32)],
        (q_ext, qt, kv, krd, kv), [pltpu.VMEM((seq, 2 * LANES), bf16)], ("parallel", "parallel", "arbitrary"), comm)
    return (o, lse) if comm is None else (o, lse, comm_outs)


def _attn_bwd(q_ext, qt, kv, krd, lse, d_mixed, nb, seq, heads, comm=None):
    tq = _tile(seq, 512)
    nq = seq // tq

    def body(q_ref, qt_ref, kn_ref, krd_ref, v_ref, lse_ref, do_ref, dq_ref, dkv_ref, dkrd_ref, dkv_acc, k_scr):
        h, qi = pl.program_id(1), pl.program_id(2)
        _fill_keys(k_scr, kn_ref, krd_ref)

        @pl.when(qi == 0)
        def _():
            dkv_acc[...] = jnp.zeros_like(dkv_acc)

        @pl.when((qi == 0) & (h == 0))
        def _():
            dkrd_ref[...] = jnp.zeros_like(dkrd_ref)

        def work(qblk):
            klen = (qblk + 1) * tq
            q, s = _scores(q_ref, qt_ref, k_scr, qblk, tq)
            p = jnp.exp(s - lse_ref[...])
            do = do_ref[...]
            dp = _dot(do, v_ref[0:klen, :], _NT)
            ds = (p * (dp - jnp.sum(p * dp, axis=-1, keepdims=True))).astype(bf16)
            dq_ref[...] = (_dot(ds, k_scr[0:klen, :], _NN) * qt_ref[...]).astype(bf16)
            dk = _dot(ds, q, _TN)
            dkv_acc[0:klen, :LANES] += dk[:, :LANES]
            dkv_acc[0:klen, LANES:] += _dot(p, do, _TN)
            dkrd_ref[0:klen, :] += dk[:, LANES:]

        _per_q_block(nq, work)

        @pl.when(qi == nq - 1)
        def _():
            dkv_ref[...] = dkv_acc[...].astype(bf16)

    (dq, dkv, dkrd), comm_outs = _call(
        body, "attn_bwd", (nb, heads, nq),
        [pl.BlockSpec((tq, 2 * LANES), lambda b, h, i: (b * nq + i, h)),
         pl.BlockSpec((tq, 2 * LANES), lambda b, h, i: (b * nq + i, 0)),
         pl.BlockSpec((seq, LANES), lambda b, h, i: (b, 2 * h)),
         pl.BlockSpec((seq, LANES), lambda b, h, i: (b, 0)),
         pl.BlockSpec((seq, LANES), lambda b, h, i: (b, 2 * h + 1)),
         pl.BlockSpec((None, tq, 1), lambda b, h, i: (b * heads + h, i, 0)),
         pl.BlockSpec((tq, LANES), lambda b, h, i: (b * nq + i, h))],
        [pl.BlockSpec((tq, 2 * LANES), lambda b, h, i: (b * nq + i, h)),
         pl.BlockSpec((seq, 2 * LANES), lambda b, h, i: (b, h)),
         pl.BlockSpec((seq, LANES), lambda b, h, i: (b, 0))],
        [jax.ShapeDtypeStruct((nb * seq, heads * 2 * LANES), bf16),
         jax.ShapeDtypeStruct((nb * seq, heads * 2 * LANES), bf16),
         jax.ShapeDtypeStruct((nb * seq, LANES), f32)],
        (q_ext, qt, kv, krd, kv, lse, d_mixed), [pltpu.VMEM((seq, 2 * LANES), f32), pltpu.VMEM((seq, 2 * LANES), bf16)],
        ("parallel", "arbitrary", "arbitrary"), comm)
    return (dq, dkv, dkrd) if comm is None else (dq, dkv, dkrd, comm_outs)


def _mesh_pos():
    x, y, c = lax.axis_index("x"), lax.axis_index("y"), lax.axis_index("c")
    return x, y, c, [(1 - x, y), (x, 1 - y), (1 - x, 1 - y)]


def _all_gather(shards, name):
    n = len(shards)

    def body(*refs):
        x_refs, out_refs, (send_sems, recv_sems, local_sems) = refs[:n], refs[n:2 * n], refs[2 * n:]
        x, y, c, chips = _mesh_pos()
        me, sibling = (x, y, c), (x, y, 1 - c)

        def copy(t, k, block, to, from_shard=False):
            px, py, pc = block
            rows = out_refs[t].at[4 * px + 2 * py + pc]
            return pltpu.make_async_remote_copy(
                src_ref=x_refs[t] if from_shard else rows, dst_ref=rows,
                send_sem=send_sems.at[t, k], recv_sem=recv_sems.at[t, k], device_id=to, device_id_type=MESH)

        mine = [pltpu.make_async_copy(x_refs[t], out_refs[t].at[4 * x + 2 * y + c], local_sems.at[t]) for t in range(n)]
        first = [[copy(t, 0, me, sibling, True)] + [copy(t, 1 + j, me, (*chip, c), True) for j, chip in enumerate(chips)]
                 for t in range(n)]
        passed = [[copy(t, 4 + j, (*chip, c), sibling) for j, chip in enumerate(chips)] for t in range(n)]
        for t in range(n):
            mine[t].start()
            for cp in first[t]:
                cp.start()
        for j, chip in enumerate(chips):
            for t in range(n):
                copy(t, 1 + j, (*chip, c), me).wait_recv()
                passed[t][j].start()
        for t in range(n):
            copy(t, 0, sibling, me).wait_recv()
            for j, chip in enumerate(chips):
                copy(t, 4 + j, (*chip, 1 - c), me).wait_recv()
        for t in range(n):
            for cp in first[t] + passed[t]:
                cp.wait_send()
            mine[t].wait()

    return pl.pallas_call(
        body, name=name, out_shape=[jax.ShapeDtypeStruct((N_DEV,) + s.shape, s.dtype) for s in shards],
        in_specs=[_ANY] * n, out_specs=[_ANY] * n,
        scratch_shapes=[pltpu.SemaphoreType.DMA((n, 7)), pltpu.SemaphoreType.DMA((n, 7)), pltpu.SemaphoreType.DMA((n,))],
    )(*shards)


def _gather_own(shards, peers=(0, 1, 2, 3), into=None):
    n = len(shards)

    def remote(x_refs, out_refs, sems, arriving):
        send_sems, recv_sems, _ = sems
        x, y, c, chips = _mesh_pos()
        targets = [(x, y, 1 - c)] + [(*chip, c) for chip in chips]
        return [pltpu.make_async_remote_copy(
            src_ref=x_refs[t],
            dst_ref=out_refs[t].at[4 * targets[k][0] + 2 * targets[k][1] + targets[k][2] if arriving else 4 * x + 2 * y + c],
            send_sem=send_sems.at[t, k], recv_sem=recv_sems.at[t, k], device_id=targets[k], device_id_type=MESH)
            for t in range(n) for k in peers]

    def local(x_refs, out_refs, sems):
        if into is not None:
            return []
        x, y, c, _ = _mesh_pos()
        return [pltpu.make_async_copy(x_refs[t], out_refs[t].at[4 * x + 2 * y + c], sems[2].at[t]) for t in range(n)]

    def start(x_refs, out_refs, sems):
        for cp in local(x_refs, out_refs, sems) + remote(x_refs, out_refs, sems, False):
            cp.start()

    def finish(x_refs, out_refs, sems):
        for cp in remote(x_refs, out_refs, sems, True):
            cp.wait_recv()
        for cp in remote(x_refs, out_refs, sems, False):
            cp.wait_send()
        for cp in local(x_refs, out_refs, sems):
            cp.wait()

    out_shapes = [jax.ShapeDtypeStruct((N_DEV,) + s.shape, s.dtype) for s in shards]
    sems = [pltpu.SemaphoreType.DMA((n, 4)), pltpu.SemaphoreType.DMA((n, 4)), pltpu.SemaphoreType.DMA((n,))]
    if into is None:
        return _Comm(shards, out_shapes, sems, start, finish)
    return _Comm(list(shards) + list(into), out_shapes, sems, start, finish, aliases={n + t: t for t in range(n)})


def _gather_pass(gathered, chips_served=(0, 1, 2)):
    n = len(gathered)

    def copies(in_refs, out_refs, sems, arriving):
        send_sems, recv_sems = sems
        x, y, c, chips = _mesh_pos()
        return [pltpu.make_async_remote_copy(
            src_ref=in_refs[t].at[4 * chips[j][0] + 2 * chips[j][1] + c],
            dst_ref=out_refs[t].at[4 * chips[j][0] + 2 * chips[j][1] + (1 - c if arriving else c)],
            send_sem=send_sems.at[t, j], recv_sem=recv_sems.at[t, j], device_id=(x, y, 1 - c), device_id_type=MESH)
            for t in range(n) for j in chips_served]

    def start(in_refs, out_refs, sems):
        for cp in copies(in_refs, out_refs, sems, False):
            cp.start()

    def finish(in_refs, out_refs, sems):
        for cp in copies(in_refs, out_refs, sems, True):
            cp.wait_recv()
        for cp in copies(in_refs, out_refs, sems, False):
            cp.wait_send()

    return _Comm(gathered, [jax.ShapeDtypeStruct(g.shape, g.dtype) for g in gathered],
                 [pltpu.SemaphoreType.DMA((n, 3)), pltpu.SemaphoreType.DMA((n, 3))], start, finish,
                 aliases={t: t for t in range(n)})


def _sibling_swap(slots):
    n = len(slots)

    def start(g_refs, out_refs, sems):
        send_sems, recv_sems = sems
        x, y, c, _ = _mesh_pos()
        for t in range(n):
            for k in range(4):
                pltpu.make_async_remote_copy(
                    src_ref=g_refs[t].at[2 * k + (1 - c)], dst_ref=out_refs[t].at[k], send_sem=send_sems.at[t],
                    recv_sem=recv_sems.at[t], device_id=(x, y, 1 - c), device_id_type=MESH).start()

    def finish(g_refs, out_refs, sems):
        send_sems, recv_sems = sems
        x, y, c, _ = _mesh_pos()
        for t in range(n):
            pltpu.make_async_remote_copy(
                src_ref=g_refs[t].at[pl.ds(0, 4)], dst_ref=out_refs[t], send_sem=send_sems.at[t],
                recv_sem=recv_sems.at[t], device_id=(x, y, 1 - c), device_id_type=MESH).wait()

    return _Comm(slots, [jax.ShapeDtypeStruct((4,) + s.shape[1:], s.dtype) for s in slots],
                 [pltpu.SemaphoreType.DMA((n,)), pltpu.SemaphoreType.DMA((n,))], start, finish)


def _chip_swap(p4s, peers=(0, 1, 2), into=None):
    n = len(p4s)

    def copies(refs, out_refs, sems):
        send_sems, recv_sems = sems
        x, y, c, chips = _mesh_pos()
        return [pltpu.make_async_remote_copy(
            src_ref=refs[t].at[2 * chips[j][0] + chips[j][1]], dst_ref=out_refs[t].at[j], send_sem=send_sems.at[t, j],
            recv_sem=recv_sems.at[t, j], device_id=(*chips[j], c), device_id_type=MESH)
            for t in range(n) for j in peers]

    def start(refs, out_refs, sems):
        for cp in copies(refs, out_refs, sems):
            cp.start()

    def finish(refs, out_refs, sems):
        for cp in copies(refs, out_refs, sems):
            cp.wait()

    out_shapes = [jax.ShapeDtypeStruct((3,) + p.shape[1:], p.dtype) for p in p4s]
    sems = [pltpu.SemaphoreType.DMA((n, 3)), pltpu.SemaphoreType.DMA((n, 3))]
    if into is None:
        return _Comm(p4s, out_shapes, sems, start, finish)
    return _Comm(list(p4s) + list(into), out_shapes, sems, start, finish, aliases={n + t: t for t in range(n)})


def _comm_join(a, b):
    ai, ao, asem = len(a.inputs), len(a.out_shapes), len(a.sems)

    def start(ins, outs, sems):
        a.start(ins[:ai], outs[:ao], sems[:asem])
        b.start(ins[ai:], outs[ao:], sems[asem:])

    def finish(ins, outs, sems):
        a.finish(ins[:ai], outs[:ao], sems[:asem])
        b.finish(ins[ai:], outs[ao:], sems[asem:])

    aliases = dict(a.aliases)
    aliases.update({ai + i: ao + o for i, o in b.aliases.items()})
    return _Comm(a.inputs + b.inputs, a.out_shapes + b.out_shapes, a.sems + b.sems, start, finish, aliases)


def _row_tile(rows, cols, max_bytes=1024 * 1024):
    best = None
    for tr in range(16, rows + 1, 16):
        if rows % tr == 0 and tr * cols * 4 <= max_bytes:
            best = tr
    return best or rows


def _pair_add(slots, theirs, core, name):
    _, rows, cols = slots.shape
    tr = _row_tile(rows, cols, 4 * 1024 * 1024)

    def body(c_ref, a_ref, b_ref, o_ref):
        o_ref[...] = (a_ref[...].astype(f32) + b_ref[...].astype(f32)).astype(bf16)

    return pl.pallas_call(
        body, name=name,
        grid_spec=pltpu.PrefetchScalarGridSpec(
            num_scalar_prefetch=1, grid=(4, rows // tr),
            in_specs=[pl.BlockSpec((None, tr, cols), lambda k, i, c_ref: (2 * k + c_ref[0], i, 0)),
                      pl.BlockSpec((None, tr, cols), lambda k, i, c_ref: (k, i, 0))],
            out_specs=pl.BlockSpec((None, tr, cols), lambda k, i, c_ref: (k, i, 0))),
        out_shape=jax.ShapeDtypeStruct((4, rows, cols), bf16), compiler_params=_cparams(("parallel", "parallel")),
    )(core, slots, theirs)


def _adam_update(g, w, m, v):
    c1 = 1.0 / (1.0 - ADAM_B1 ** ADAM_STEP)
    c2 = 1.0 / (1.0 - ADAM_B2 ** ADAM_STEP)
    nm = ADAM_B1 * m + (1.0 - ADAM_B1) * g
    nv = ADAM_B2 * v + (1.0 - ADAM_B2) * jnp.square(g)
    delta = -ADAM_LR * ((nm * c1) / (jnp.sqrt(nv * c2) + ADAM_EPS) + ADAM_WD * w)
    return delta, nm, nv


def _adamw_layers(p4s, chips, chip_idx, w, m, v, name, comm=None):
    depth = len(p4s)
    _, rows_l, cols = p4s[0].shape
    tr = _row_tile(rows_l, cols)
    nr = rows_l // tr

    def body(idx_ref, *refs):
        p_refs, c_refs = refs[:depth], refs[depth:2 * depth]
        w_ref, m_ref, v_ref, g_ref, d_ref, nm_ref, nv_ref = refs[2 * depth:]
        layer = pl.program_id(0)
        for ll in range(depth):
            @pl.when(layer == ll)
            def _(ll=ll):
                g = p_refs[ll][...].astype(f32)
                for j in range(3):
                    g = g + c_refs[ll][j].astype(f32)
                delta, nm, nv = _adam_update(g, w_ref[...], m_ref[...], v_ref[...])
                g_ref[...] = g
                d_ref[...] = delta
                nm_ref[...] = nm
                nv_ref[...] = nv

    def of_layer(ll):
        return lambda l, i: jnp.where(l == ll, i, 0)

    p_specs = [pl.BlockSpec((None, tr, cols), lambda l, i, idx_ref, f=of_layer(ll): (idx_ref[0], f(l, i), 0))
               for ll in range(depth)]
    c_specs = [pl.BlockSpec((3, tr, cols), lambda l, i, idx_ref, f=of_layer(ll): (0, f(l, i), 0)) for ll in range(depth)]
    spec = pl.BlockSpec((tr, cols), lambda l, i, idx_ref: (l * nr + i, 0))
    out = jax.ShapeDtypeStruct(w.shape, f32)
    outs, comm_outs = _call(body, name, (depth, nr), p_specs + c_specs + [spec, spec, spec], [spec, spec, spec, spec],
                            [out, out, out, out], (*p4s, *chips, w, m, v), (), ("parallel", "parallel"), comm,
                            prefetch=[chip_idx])
    return outs if comm is None else (outs, comm_outs)


def _adamw(parts, w, m, v, name):
    rows, cols = w.shape
    tr = _row_tile(rows, cols, 512 * 1024)
    nparts = len(parts)

    def body(*refs):
        part_refs, (w_ref, m_ref, v_ref, g_ref, d_ref, nm_ref, nv_ref) = refs[:nparts], refs[nparts:]
        g = None
        for pr in part_refs:
            for s in range(pr.shape[0]):
                term = pr[s].astype(f32)
                g = term if g is None else g + term
        delta, nm, nv = _adam_update(g, w_ref[...], m_ref[...], v_ref[...])
        g_ref[...] = g
        d_ref[...] = delta
        nm_ref[...] = nm
        nv_ref[...] = nv

    spec = pl.BlockSpec((tr, cols), lambda i: (i, 0))
    part_specs = [pl.BlockSpec((p.shape[0], tr, cols), lambda i: (0, i, 0)) for p in parts]
    out = jax.ShapeDtypeStruct((rows, cols), f32)
    return pl.pallas_call(
        body, name=name, grid=(rows // tr,), in_specs=part_specs + [spec, spec, spec],
        out_specs=[spec, spec, spec, spec], out_shape=[out, out, out, out],
        compiler_params=_cparams(("parallel",)),
    )(*parts, w, m, v)


def _gathered_to_full(gathered, axis):
    s = gathered.shape[1:]
    full = jnp.moveaxis(gathered, 0, axis)
    return full.reshape(s[:axis] + (N_DEV * s[axis],) + s[axis + 1:])


def _interleave_halves(w, ct):
    n = w.shape[-1] // 2
    t = w.reshape(w.shape[:-1] + (2, n // ct, ct))
    return jnp.swapaxes(t, -3, -2).reshape(w.shape)


def _deinterleave_halves(w, ct):
    n = w.shape[-1] // 2
    t = w.reshape(w.shape[:-1] + (n // ct, 2, ct))
    return jnp.swapaxes(t, -3, -2).reshape(w.shape)


def _rot_cols(w):
    half = QK_ROPE_DIM // 2
    return jnp.concatenate([-w[..., half:], w[..., :half]], axis=-1)


def _rot_cols_t(dw):
    half = QK_ROPE_DIM // 2
    return jnp.concatenate([dw[..., half:], -dw[..., :half]], axis=-1)


def kernel(x, positions, ln_in_g, ln_in_b, w_in, q_norm_g, w_uq, kv_norm_g, w_ukv, conv_w, conv_b, conv_ln_g, conv_ln_b, w_pool, pool_scale, w_out, ln1_g, ln1_b, w_up, ffn_conv_w, ffn_conv_b, w_down, ln2_g, ln2_b, loss_target, m_ln_in_g, m_ln_in_b, m_w_in, m_q_norm_g, m_w_uq, m_kv_norm_g, m_w_ukv, m_conv_w, m_conv_b, m_conv_ln_g, m_conv_ln_b, m_w_pool, m_pool_scale, m_w_out, m_ln1_g, m_ln1_b, m_w_up, m_ffn_conv_w, m_ffn_conv_b, m_w_down, m_ln2_g, m_ln2_b, v_ln_in_g, v_ln_in_b, v_w_in, v_q_norm_g, v_w_uq, v_kv_norm_g, v_w_ukv, v_conv_w, v_conv_b, v_conv_ln_g, v_conv_ln_b, v_w_pool, v_pool_scale, v_w_out, v_ln1_g, v_ln1_b, v_w_up, v_ffn_conv_w, v_ffn_conv_b, v_w_down, v_ln2_g, v_ln2_b):
    weights = dict(ln_in_g=ln_in_g, ln_in_b=ln_in_b, w_in=w_in, q_norm_g=q_norm_g, w_uq=w_uq, kv_norm_g=kv_norm_g,
                   w_ukv=w_ukv, conv_w=conv_w, conv_b=conv_b, conv_ln_g=conv_ln_g, conv_ln_b=conv_ln_b, w_pool=w_pool,
                   pool_scale=pool_scale, w_out=w_out, ln1_g=ln1_g, ln1_b=ln1_b, w_up=w_up, ffn_conv_w=ffn_conv_w,
                   ffn_conv_b=ffn_conv_b, w_down=w_down, ln2_g=ln2_g, ln2_b=ln2_b)
    mom1 = dict(ln_in_g=m_ln_in_g, ln_in_b=m_ln_in_b, w_in=m_w_in, q_norm_g=m_q_norm_g, w_uq=m_w_uq,
                kv_norm_g=m_kv_norm_g, w_ukv=m_w_ukv, conv_w=m_conv_w, conv_b=m_conv_b, conv_ln_g=m_conv_ln_g,
                conv_ln_b=m_conv_ln_b, w_pool=m_w_pool, pool_scale=m_pool_scale, w_out=m_w_out, ln1_g=m_ln1_g,
                ln1_b=m_ln1_b, w_up=m_w_up, ffn_conv_w=m_ffn_conv_w, ffn_conv_b=m_ffn_conv_b, w_down=m_w_down,
                ln2_g=m_ln2_g, ln2_b=m_ln2_b)
    mom2 = dict(ln_in_g=v_ln_in_g, ln_in_b=v_ln_in_b, w_in=v_w_in, q_norm_g=v_q_norm_g, w_uq=v_w_uq,
                kv_norm_g=v_kv_norm_g, w_ukv=v_w_ukv, conv_w=v_conv_w, conv_b=v_conv_b, conv_ln_g=v_conv_ln_g,
                conv_ln_b=v_conv_ln_b, w_pool=v_w_pool, pool_scale=v_pool_scale, w_out=v_w_out, ln1_g=v_ln1_g,
                ln1_b=v_ln1_b, w_up=v_w_up, ffn_conv_w=v_ffn_conv_w, ffn_conv_b=v_ffn_conv_b, w_down=v_w_down,
                ln2_g=v_ln2_g, ln2_b=v_ln2_b)
    names = list(weights)

    nb, seq, d = x.shape
    t = nb * seq
    depth = w_in.shape[0]
    ql, kvl, cw, pw = q_norm_g.shape[1], kv_norm_g.shape[1], conv_b.shape[1], pool_scale.shape[1]
    pg = w_pool.shape[-1]
    heads = w_uq.shape[2]
    dff = w_down.shape[1] * N_DEV
    alpha = (2.0 * depth) ** 0.25
    scale = float(QK_NOPE_DIM + QK_ROPE_DIM) ** -0.5
    lay = dict(ql=ql, kvl=kvl, pw=pw, pg=pg, off_q=2 * cw, off_pool=2 * cw + ql, off_kv=2 * cw + ql + pw,
               off_kr=2 * cw + ql + pw + kvl)
    o1, o2, o3, o4 = ql, ql + kvl, ql + kvl + QK_ROPE_DIM, ql + kvl + QK_ROPE_DIM + 2 * cw
    my_x, my_y, my_c = lax.axis_index("x"), lax.axis_index("y"), lax.axis_index("c")
    my_dev = 4 * my_x + 2 * my_y + my_c

    big = ("w_in", "w_uq", "w_ukv", "w_out", "w_up", "w_down")
    g_conv, g_ffn = _all_gather([conv_w, ffn_conv_w], "ag_conv_taps")
    conv_w_full, ffn_w_full = _gathered_to_full(g_conv, 2), _gathered_to_full(g_ffn, 2)
    w_pool_2d = w_pool.reshape(depth, pw, pg)

    rest = ("w_in", "w_uq", "w_ukv", "w_out")

    def bf16_shards(l, which):
        return [weights[n][l].astype(bf16) for n in which]

    def small_weights(gathered):
        g_in, g_uq, g_ukv, g_out = (gathered[n] for n in rest)
        wi = _gathered_to_full(g_in, 1)
        kr_cols = wi[:, o2:o3]
        w_in_pad = jnp.concatenate([_interleave_halves(wi[:, o3:o4], SEQ_CT), wi[:, :o1], wi[:, o4:], wi[:, o1:o2],
                                    kr_cols, _rot_cols(kr_cols)], axis=-1)
        wq = g_uq.reshape(ql, heads, QK_NOPE_DIM + QK_ROPE_DIM)
        w_uq_ext = jnp.concatenate([wq, _rot_cols(wq[..., QK_NOPE_DIM:])], axis=-1).reshape(ql, heads * 2 * LANES)
        return w_in_pad, w_uq_ext, g_ukv.reshape(kvl, heads * 2 * LANES), g_out.reshape(-1, d)

    half = QK_ROPE_DIM // 2
    inv = 1.0 / (ROPE_THETA ** (jnp.arange(0, QK_ROPE_DIM, 2, dtype=f32) / QK_ROPE_DIM))
    inv_lanes = jnp.tile(inv, LANES // half).reshape(1, LANES)
    cs, qt = _rope_tables(positions.reshape(t, 1), inv_lanes, scale)

    x2 = x.reshape(t, d)
    xs, xs_bf = _ln_fwd(x2, None, ln_in_g.reshape(1, d), ln_in_b.reshape(1, d), 1.0, "ln_in_fwd")
    saved = []
    in_pad = o4 + pw + QK_ROPE_DIM
    gathered = dict(zip(rest, _all_gather(bf16_shards(0, rest), "ag_weights")))
    for l in range(depth):
        nxt = l + 1 < depth
        gq, gkv = q_norm_g[l].reshape(1, ql), kv_norm_g[l].reshape(1, kvl)
        w_in_l, w_uq_l, w_ukv_l, w_out_l = small_weights(gathered)
        if nxt:
            h, g_rest = _matmul(xs_bf, w_in_l, "nn", f32, "mm_in", tm=512, tn=in_pad,
                                comm=_gather_own(bf16_shards(l + 1, rest[1:])))
        else:
            h = _matmul(xs_bf, w_in_l, "nn", f32, "mm_in", tm=512, tn=in_pad)
        qn, kvn, krd = _prep_fwd(h, cs, gq, gkv, lay)
        q_ext = _matmul(qn, w_uq_l, "nn", f32, "mm_uq", tn=2048)
        kv = _matmul(kvn, w_ukv_l, "nn", bf16, "mm_ukv", tn=2048)
        riders = ("w_down", "w_up") if l == 0 else ("w_down",)
        y_mla, lse, g_half = _attn_fwd(q_ext, qt, kv, krd, nb, seq, heads, comm=_gather_own(bf16_shards(l, riders)))
        hconv = _conv_fwd(h, conv_w_full[l], conv_b[l].reshape(1, cw), nb, seq, cw)
        y_conv = _convln_fwd(hconv, conv_ln_g[l].reshape(1, cw), conv_ln_b[l].reshape(1, cw))
        y_pool = _pool_fwd(h, w_pool_2d[l], pool_scale[l].reshape(1, pw), lay, nb, seq)
        mixed = jnp.concatenate([y_mla, y_conv, y_pool], axis=-1)
        y1, g_full = _matmul(mixed, w_out_l, "nn", f32, "mm_out", comm=_gather_pass(g_half))
        gathered.update(zip(riders, g_full))
        wl = dict(w_in=w_in_l, w_uq=w_uq_l, w_ukv=w_ukv_l, w_out=w_out_l, w_up=gathered["w_up"],
                  w_down=gathered["w_down"].reshape(dff, d))
        gathered = {}
        x1, x1_bf = _ln_fwd(xs, y1, ln1_g[l].reshape(1, d), ln1_b[l].reshape(1, d), alpha, "ln1_fwd")
        if nxt:
            up_shard = bf16_shards(l + 1, ("w_up",))
            up, g_up = _matmul_up(x1_bf, wl["w_up"], "mm_up", comm=_gather_own(up_shard, peers=(0, 1, 2)))
            act, g_up = _ffn_act_fwd(up, ffn_w_full[l], ffn_conv_b[l].reshape(1, 2 * dff), nb, seq, dff,
                                     comm=_gather_own(up_shard, peers=(3,), into=g_up))
            y2, (g_up, g_in) = _matmul(
                act, wl["w_down"], "nn", f32, "mm_down", tk=dff // 2,
                comm=_comm_join(_gather_pass([g_up[0]], (0, 1)), _gather_own(bf16_shards(l + 1, rest[:1]))))
            xn, xn_bf, (gathered["w_up"], *g_rest) = _ln_fwd(
                x1, y2, ln2_g[l].reshape(1, d), ln2_b[l].reshape(1, d), alpha, "ln2_fwd",
                comm=_comm_join(_gather_pass([g_up], (2,)), _gather_pass([g_in] + g_rest)))
            gathered.update(zip(rest, g_rest))
        else:
            up = _matmul_up(x1_bf, wl["w_up"], "mm_up")
            act = _ffn_act_fwd(up, ffn_w_full[l], ffn_conv_b[l].reshape(1, 2 * dff), nb, seq, dff)
            y2 = _matmul(act, wl["w_down"], "nn", f32, "mm_down", tk=dff // 2)
            xn, xn_bf = _ln_fwd(x1, y2, ln2_g[l].reshape(1, d), ln2_b[l].reshape(1, d), alpha, "ln2_fwd")
        saved.append(dict(xs=xs, xs_bf=xs_bf, h=h, qn=qn, kvn=kvn, krd=krd, q_ext=q_ext, kv=kv, lse=lse, hconv=hconv,
                          mixed=mixed, y1=y1, x1=x1, x1_bf=x1_bf, up=up, act=act, y2=y2, wl=wl))
        xs, xs_bf = xn, xn_bf

    d_stream, loss_row = _loss_call(xs, loss_target.reshape(t, d))
    loss = lax.psum(loss_row[0, 0], MESH_AXES)

    gw = {n: [None] * depth for n in names if n not in ("ln_in_g", "ln_in_b")}
    small = [n for n in gw if n not in big]
    rs_own = {n: [None] * depth for n in big}
    rs_chips = {n: [None] * depth for n in big}
    core_idx = jnp.reshape(my_c, (1,)).astype(jnp.int32)
    chip_idx = jnp.reshape(2 * my_x + my_y, (1,)).astype(jnp.int32)

    def pair_add(l, which, slots, theirs):
        p4s = [_pair_add(slots[n], th, core_idx, "rs_add_" + n) for n, th in zip(which, theirs)]
        for n, p4 in zip(which, p4s):
            rs_own[n][l] = p4
        return p4s

    d_res, d_mm = None, d_stream
    pending = None
    for l in reversed(range(depth)):
        sv = saved[l]
        wl = sv["wl"]
        slots = {}
        gq, gkv = q_norm_g[l].reshape(1, ql), kv_norm_g[l].reshape(1, kvl)
        dz2, dz2_bf, gw["ln2_g"][l], gw["ln2_b"][l] = _ln_bwd(
            d_res, d_mm, sv["x1"], sv["y2"], ln2_g[l].reshape(1, d), ln2_b[l].reshape(1, d), alpha, "ln2_bwd")
        if pending is None:
            dw_down = _matmul(sv["act"], dz2_bf, "tn", bf16, "mm_down_dw", tm=dff // 4)
        else:
            dw_down, theirs = _matmul(sv["act"], dz2_bf, "tn", bf16, "mm_down_dw", tm=dff // 4,
                                      comm=_sibling_swap([pending["slots"][n] for n in rest]))
            rest_p4s = pair_add(pending["layer"], rest, pending["slots"], theirs)
        slots["w_down"] = dw_down.reshape(N_DEV, -1, d)
        d_act, theirs = _matmul(dz2_bf, wl["w_down"], "nt", f32, "mm_down_dx", tn=dff // 4,
                                comm=_sibling_swap([slots["w_down"]]))
        down_p4s = pair_add(l, ("w_down",), slots, theirs)
        ffn_args = (sv["up"], ffn_w_full[l], ffn_conv_b[l].reshape(1, 2 * dff), d_act, nb, seq, dff)
        if pending is None:
            d_up, dffw, dffb = _ffn_act_bwd(*ffn_args)
        else:
            d_up, dffw, dffb, (rs_chips["w_up"][pending["layer"]],) = _ffn_act_bwd(
                *ffn_args, comm=_chip_swap(pending["up_p4s"], peers=(2,), into=pending["up_partial"]))
        gw["ffn_conv_w"][l] = jnp.concatenate([dffw[0], dffw[1]], axis=-1)
        gw["ffn_conv_b"][l] = dffb.reshape(2 * dff)
        d_x1, (rs_chips["w_down"][l],) = _matmul_up_dx(d_up, wl["w_up"], "mm_up_dx", comm=_chip_swap(down_p4s))
        if pending is None:
            slots["w_up"] = _matmul_up_dw(sv["x1_bf"], d_up, N_DEV, "mm_up_dw")
        else:
            slots["w_up"], from_chips = _matmul_up_dw(sv["x1_bf"], d_up, N_DEV, "mm_up_dw", comm=_chip_swap(rest_p4s))
            for n, fc in zip(rest, from_chips):
                rs_chips[n][pending["layer"]] = fc
        dz1, dz1_bf, gw["ln1_g"][l], gw["ln1_b"][l] = _ln_bwd(
            dz2, d_x1, sv["xs"], sv["y1"], ln1_g[l].reshape(1, d), ln1_b[l].reshape(1, d), alpha, "ln1_bwd")
        d_mixed, theirs = _matmul(dz1_bf, wl["w_out"], "nt", f32, "mm_out_dx", comm=_sibling_swap([slots["w_up"]]))
        up_p4s = pair_add(l, ("w_up",), slots, theirs)
        slots["w_out"] = _matmul(sv["mixed"], dz1_bf, "tn", bf16, "mm_out_dw").reshape(N_DEV, -1, d)
        d_upool, dwp, dps = _pool_bwd(sv["h"], w_pool_2d[l], pool_scale[l].reshape(1, pw), d_mixed,
                                      (heads * LANES + cw) // pw, lay, nb, seq)
        gw["w_pool"][l] = dwp.reshape(w_pool.shape[1:])
        gw["pool_scale"][l] = dps.reshape(pw)
        d_hconv, dclg, dclb = _convln_bwd(sv["hconv"], conv_ln_g[l].reshape(1, cw), conv_ln_b[l].reshape(1, cw), d_mixed,
                                          heads * LANES // cw)
        gw["conv_ln_g"][l], gw["conv_ln_b"][l] = dclg.reshape(cw), dclb.reshape(cw)
        d_conv, gw["conv_w"][l], dcb = _conv_bwd(sv["h"], conv_w_full[l], d_hconv, nb, seq, cw)
        gw["conv_b"][l] = dcb.reshape(cw)
        dq_ext, dkv, dkrd, up_partial = _attn_bwd(sv["q_ext"], qt, sv["kv"], sv["krd"], sv["lse"], d_mixed, nb, seq, heads,
                                                  comm=_chip_swap(up_p4s, peers=(0, 1, 2) if l == 0 else (0, 1)))
        if l == 0:
            rs_chips["w_up"][l] = up_partial[0]
        d_qn = _matmul(dq_ext, wl["w_uq"], "nt", f32, "mm_uq_dx")
        dwq = _matmul(sv["qn"], dq_ext, "tn", f32, "mm_uq_dw", tn=2048, tk=1024).reshape(ql, heads, 2 * LANES)
        dwq_rope = dwq[..., QK_NOPE_DIM:QK_NOPE_DIM + QK_ROPE_DIM] + _rot_cols_t(dwq[..., QK_NOPE_DIM + QK_ROPE_DIM:])
        slots["w_uq"] = jnp.concatenate([dwq[..., :QK_NOPE_DIM], dwq_rope], axis=-1).astype(bf16).reshape(
            N_DEV, -1, QK_NOPE_DIM + QK_ROPE_DIM)
        d_kvn = _matmul(dkv, wl["w_ukv"], "nt", f32, "mm_ukv_dx")
        slots["w_ukv"] = _matmul(sv["kvn"], dkv, "tn", bf16, "mm_ukv_dw", tn=2048, tk=1024).reshape(N_DEV, -1, 2 * LANES)
        d_cq, d_ckv, d_kr, dgq, dgkv = _prep_bwd(sv["h"], cs, gq, gkv, d_qn, d_kvn, dkrd, lay)
        gw["q_norm_g"][l], gw["kv_norm_g"][l] = dgq.reshape(ql), dgkv.reshape(kvl)
        d_h = jnp.concatenate([d_conv, d_cq, d_upool, d_ckv, d_kr], axis=-1)
        if l > 0:
            d_xs = _matmul(d_h, wl["w_in"], "nt", f32, "mm_in_dx", tm=512, tk=in_pad)
            dwi = _matmul(sv["xs_bf"], d_h, "tn", f32, "mm_in_dw", tn=in_pad, tk=1024)
        else:
            flat_small = jnp.concatenate([jnp.stack(gw[n]).astype(f32).reshape(-1) for n in small]).reshape(-1, LANES)
            d_xs, small_half = _matmul(d_h, wl["w_in"], "nt", f32, "mm_in_dx", tm=512, tk=in_pad,
                                       comm=_gather_own([flat_small]))
            dwi, (gathered_small,) = _matmul(sv["xs_bf"], d_h, "tn", f32, "mm_in_dw", tn=in_pad, tk=1024,
                                             comm=_gather_pass(small_half))
        dkr_cols = dwi[:, lay["off_kr"]:lay["off_kr"] + QK_ROPE_DIM] + _rot_cols_t(dwi[:, lay["off_kr"] + QK_ROPE_DIM:])
        dwi_nat = jnp.concatenate(
            [dwi[:, lay["off_q"]:lay["off_q"] + ql], dwi[:, lay["off_kv"]:lay["off_kv"] + kvl], dkr_cols,
             _deinterleave_halves(dwi[:, :2 * cw], SEQ_CT), dwi[:, lay["off_pool"]:lay["off_pool"] + pw]],
            axis=-1).astype(bf16)
        slots["w_in"] = jnp.moveaxis(dwi_nat.reshape(d, N_DEV, -1), 1, 0)
        pending = dict(layer=l, slots=slots, up_p4s=up_p4s, up_partial=up_partial)
        d_res, d_mm = dz1, d_xs

    grad_x, _, d_ln_in_g, d_ln_in_b, theirs = _ln_bwd(
        d_res, d_mm, x2, None, ln_in_g.reshape(1, d), ln_in_b.reshape(1, d), alpha, "ln_in_bwd",
        comm=_sibling_swap([pending["slots"][n] for n in rest]))
    grad_x = grad_x.reshape(x.shape)
    rest_p4s = pair_add(pending["layer"], rest, pending["slots"], theirs)

    grads, deltas, new_m, new_v = {}, {}, {}, {}

    def finish(n, parts):
        shp = weights[n].shape
        rows = math.prod(shp[:-1]) if len(shp) > 1 else 1
        as2d = lambda a: a.reshape(rows, shp[-1])
        parts = [p.reshape(p.shape[0], rows, shp[-1]) for p in parts]
        g, dl, nm, nv = _adamw(parts, as2d(weights[n]), as2d(mom1[n]), as2d(mom2[n]), "adamw_" + n)
        grads[n], deltas[n], new_m[n], new_v[n] = (a.reshape(shp) for a in (g, dl, nm, nv))

    def finish_big(n, comm=None):
        shp = weights[n].shape
        as2d = lambda a: a.reshape(-1, shp[-1])
        res = _adamw_layers(rs_own[n], rs_chips[n], chip_idx, as2d(weights[n]), as2d(mom1[n]), as2d(mom2[n]),
                            "adamw_" + n, comm)
        outs, comm_outs = res if comm is not None else (res, None)
        grads[n], deltas[n], new_m[n], new_v[n] = (a.reshape(shp) for a in outs)
        return comm_outs

    ln_in = ("ln_in_g", "ln_in_b")
    flat_ln = jnp.concatenate([d_ln_in_g.reshape(-1), d_ln_in_b.reshape(-1)]).reshape(-1, LANES)
    gathered_ln = _all_gather([flat_ln], "ag_ln_in_grads")[0].reshape(N_DEV, 2, d)
    finish_big("w_up")
    from_chips = finish_big("w_down", _chip_swap(rest_p4s))
    for n, fc in zip(rest, from_chips):
        rs_chips[n][pending["layer"]] = fc
    for n in rest:
        finish_big(n)
    for i, n in enumerate(ln_in):
        finish(n, [gathered_ln[:, i]])
    gathered_small = gathered_small.reshape(N_DEV, -1)
    off = 0
    for n in small:
        shape = (depth,) + gw[n][0].shape
        size = math.prod(shape)
        part = gathered_small[:, off:off + size].reshape((N_DEV,) + shape)
        off += size
        if n in ("conv_w", "ffn_conv_w"):
            width = weights[n].shape[-1]
            part = lax.dynamic_slice_in_dim(part, my_dev * width, width, axis=part.ndim - 1)
        finish(n, [part])

    return (loss, grad_x, *[grads[n] for n in names], *[deltas[n] for n in names], *[new_m[n] for n in names],
            *[new_v[n] for n in names])
```

```python
import functools
import math

import jax
import jax.numpy as jnp
from jax import lax
from jax.experimental import pallas as pl
from jax.experimental.pallas import tpu as pltpu

f32 = jnp.float32
bf16 = jnp.bfloat16

QK_NOPE_DIM = 128
QK_ROPE_DIM = 64
V_HEAD_DIM = 128
CONV_KERNEL = 31
FFN_CONV_KERNEL = 3
POOL_WINDOWS = (2, 4, 8, 16)
ROPE_THETA = 10000.0
LN_EPS = 1e-5
RMS_EPS = 1e-6
ADAM_LR = 0.001
ADAM_B1 = 0.9
ADAM_B2 = 0.999
ADAM_EPS = 1e-08
ADAM_WD = 0.01
ADAM_STEP = 10

N_DEV = 8
MESH_AXES = ("x", "y", "c")
V7X_VMEM_LIMIT_BYTES = 56 * 1024 * 1024
LANES = 128
NEG_INF = -1e30
MESH = pl.DeviceIdType.MESH


def _cparams(sem):
    return pltpu.CompilerParams(dimension_semantics=sem, vmem_limit_bytes=V7X_VMEM_LIMIT_BYTES)


def _tile(dim, pref):
    t = pref
    while t >= LANES:
        if dim % t == 0:
            return t
        t //= 2
    return dim


_ANY = pl.BlockSpec(memory_space=pl.ANY)


class _Comm:
    def __init__(self, inputs, out_shapes, sems, start, finish, aliases=None):
        self.inputs, self.out_shapes, self.sems = list(inputs), list(out_shapes), list(sems)
        self.start, self.finish, self.aliases = start, finish, dict(aliases or {})


def _call(body, name, grid, in_specs, out_specs, out_shape, args, scratch=(), sem=None, comm=None, prefetch=()):
    in_specs, out_specs, out_shape, scratch = list(in_specs), list(out_specs), list(out_shape), list(scratch)
    n_pre, n_in, n_out, n_scr = len(prefetch), len(in_specs), len(out_specs), len(scratch)
    c_in, c_out = (len(comm.inputs), len(comm.out_shapes)) if comm else (0, 0)

    def carrier(*refs):
        refs = list(refs)
        pre, refs = refs[:n_pre], refs[n_pre:]
        ins, refs = refs[:n_in], refs[n_in:]
        c_ins, refs = refs[:c_in], refs[c_in:]
        outs, refs = refs[:n_out], refs[n_out:]
        c_outs, refs = refs[:c_out], refs[c_out:]
        scr, c_sems = refs[:n_scr], refs[n_scr:]
        ids = [pl.program_id(a) for a in range(len(grid))]
        first = functools.reduce(lambda p, q: p & q, [i == 0 for i in ids])
        last = functools.reduce(lambda p, q: p & q, [i == g - 1 for i, g in zip(ids, grid)])
        pl.when(first)(lambda: comm.start(c_ins, c_outs, c_sems))
        body(*pre, *ins, *outs, *scr)
        pl.when(last)(lambda: comm.finish(c_ins, c_outs, c_sems))

    grid_spec = pltpu.PrefetchScalarGridSpec(
        num_scalar_prefetch=n_pre, grid=grid, in_specs=in_specs + [_ANY] * c_in, out_specs=out_specs + [_ANY] * c_out,
        scratch_shapes=scratch + (comm.sems if comm else []))
    if comm is None:
        outs = pl.pallas_call(body, name=name, grid_spec=grid_spec, out_shape=out_shape,
                              compiler_params=_cparams(sem))(*prefetch, *args)
        return list(outs), []
    outs = pl.pallas_call(
        carrier, name=name, grid_spec=grid_spec, out_shape=out_shape + comm.out_shapes,
        input_output_aliases={n_pre + n_in + a: n_out + b for a, b in comm.aliases.items()},
        compiler_params=_cparams(("arbitrary",) * len(grid)),
    )(*prefetch, *args, *comm.inputs)
    return list(outs[:n_out]), list(outs[n_out:])


def _shift_down_raw(x, k):
    if k == 0:
        return x
    row = lax.broadcasted_iota(jnp.int32, x.shape, 0)
    return jnp.where(row >= k, pltpu.roll(x, k, axis=0), 0.0)


def _shift_up_raw(x, k):
    if k == 0:
        return x
    n = x.shape[0]
    row = lax.broadcasted_iota(jnp.int32, x.shape, 0)
    return jnp.where(row < n - k, pltpu.roll(x, n - k, axis=0), 0.0)


@functools.partial(jax.custom_vjp, nondiff_argnums=(1,))
def _shift_down(x, k):
    return _shift_down_raw(x, k)


def _shift_down_fwd(x, k):
    return _shift_down_raw(x, k), None


def _shift_down_bwd(k, _, g):
    return (_shift_up_raw(g, k),)


_shift_down.defvjp(_shift_down_fwd, _shift_down_bwd)


@jax.custom_vjp
def _dup_halves(p):
    return p + pltpu.roll(p, LANES // 2, axis=1)


def _dup_halves_fwd(p):
    return p + pltpu.roll(p, LANES // 2, axis=1), None


def _dup_halves_bwd(_, g):
    return (g + pltpu.roll(g, LANES // 2, axis=1),)


_dup_halves.defvjp(_dup_halves_fwd, _dup_halves_bwd)

_NN = (((1,), (0,)), ((), ()))
_NT = (((1,), (1,)), ((), ()))
_TN = (((0,), (0,)), ((), ()))


def _dot(a, b, dims):
    return lax.dot_general(a.astype(bf16), b.astype(bf16), dims, preferred_element_type=f32)


@jax.custom_vjp
def _mm_bf16(a, b):
    return _dot(a, b, _NN)


def _mm_bf16_fwd(a, b):
    return _dot(a, b, _NN), (a, b)


def _mm_bf16_bwd(res, g):
    a, b = res
    return _dot(g, b, _NT), _dot(a, g, _TN)


_mm_bf16.defvjp(_mm_bf16_fwd, _mm_bf16_bwd)


def _layer_norm(z, g, b):
    mu = jnp.mean(z, axis=-1, keepdims=True)
    var = jnp.mean(jnp.square(z - mu), axis=-1, keepdims=True)
    return (z - mu) * lax.rsqrt(var + LN_EPS) * g + b


def _rms_norm(x, g):
    ms = jnp.mean(jnp.square(x), axis=-1, keepdims=True)
    return x * lax.rsqrt(ms + RMS_EPS) * g


def _colsum(x):
    return jnp.sum(x, axis=0, keepdims=True)


def _matmul_core(a, b, mode, grid, a_spec, b_spec, o_spec, o_shape, tile, out_dtype, name, comm=None):
    nk = grid[2]
    dims = {"nn": _NN, "nt": _NT, "tn": _TN}[mode]
    acc_in_out = out_dtype == f32

    def body(a_ref, b_ref, o_ref, *scratch):
        def prod():
            return _dot(a_ref[...], b_ref[...], dims)

        if nk == 1:
            o_ref[...] = prod().astype(out_dtype)
            return
        acc_ref = o_ref if acc_in_out else scratch[0]
        kk = pl.program_id(2)

        @pl.when(kk == 0)
        def _():
            acc_ref[...] = prod()

        if acc_in_out:
            @pl.when(kk > 0)
            def _():
                acc_ref[...] += prod()
        else:
            @pl.when((kk > 0) & (kk < nk - 1))
            def _():
                acc_ref[...] += prod()

            @pl.when(kk == nk - 1)
            def _():
                o_ref[...] = (acc_ref[...] + prod()).astype(out_dtype)

    scratch = [] if (nk == 1 or acc_in_out) else [pltpu.VMEM(tile, f32)]
    (out,), comm_outs = _call(body, name, grid, [a_spec, b_spec], [o_spec], [jax.ShapeDtypeStruct(o_shape, out_dtype)],
                              (a, b), scratch, ("parallel", "parallel", "arbitrary"), comm)
    return out if comm is None else (out, comm_outs)


def _matmul(a, b, mode, out_dtype, name, tm=1024, tn=1024, tk=2048, comm=None):
    if mode == "nn":
        (m, k), (k2, n) = a.shape, b.shape
    elif mode == "nt":
        (m, k), (n, k2) = a.shape, b.shape
    else:
        (k, m), (k2, n) = a.shape, b.shape
    assert k == k2, (name, a.shape, b.shape)
    tm, tn, tk = _tile(m, tm), _tile(n, tn), _tile(k, tk)
    a_spec = pl.BlockSpec((tk, tm), lambda i, j, kk: (kk, i)) if mode == "tn" else pl.BlockSpec((tm, tk), lambda i, j, kk: (i, kk))
    b_spec = pl.BlockSpec((tn, tk), lambda i, j, kk: (j, kk)) if mode == "nt" else pl.BlockSpec((tk, tn), lambda i, j, kk: (kk, j))
    return _matmul_core(a, b, mode, (m // tm, n // tn, k // tk), a_spec, b_spec,
                        pl.BlockSpec((tm, tn), lambda i, j, kk: (i, j)), (m, n), (tm, tn), out_dtype, name, comm)


def _matmul_up(x_bf, w_slots, name, tm=1024, comm=None):
    m, k = x_bf.shape
    s, _, ns = w_slots.shape
    tm = _tile(m, tm)
    return _matmul_core(x_bf, w_slots, "nn", (m // tm, s, 1), pl.BlockSpec((tm, k), lambda i, j, kk: (i, 0)),
                        pl.BlockSpec((None, k, ns), lambda i, j, kk: (j, 0, 0)),
                        pl.BlockSpec((tm, ns), lambda i, j, kk: (i, j)), (m, s * ns), (tm, ns), f32, name, comm)


def _matmul_up_dx(d3, w_slots, name, tm=1024, tn=1024, comm=None):
    _, m, half = d3.shape
    s, n, ns = w_slots.shape
    per_half = half // ns
    assert 2 * per_half == s, (d3.shape, w_slots.shape)
    tm, tn = _tile(m, tm), _tile(n, tn)
    return _matmul_core(d3, w_slots, "nt", (m // tm, n // tn, s),
                        pl.BlockSpec((None, tm, ns), lambda i, j, kk: (kk // per_half, i, kk % per_half)),
                        pl.BlockSpec((None, tn, ns), lambda i, j, kk: (kk, j, 0)),
                        pl.BlockSpec((tm, tn), lambda i, j, kk: (i, j)), (m, n), (tm, tn), f32, name, comm)


def _matmul_up_dw(x_bf, d3, n_slots, name, tm=1024, tk=2048, comm=None):
    k, m = x_bf.shape
    _, _, half = d3.shape
    ns = 2 * half // n_slots
    per_half = n_slots // 2
    tm, tk = _tile(m, tm), _tile(k, tk)
    return _matmul_core(x_bf, d3, "tn", (m // tm, n_slots, k // tk), pl.BlockSpec((tk, tm), lambda i, j, kk: (kk, i)),
                        pl.BlockSpec((None, tk, ns), lambda i, j, kk: (j // per_half, kk, j % per_half)),
                        pl.BlockSpec((None, tm, ns), lambda i, j, kk: (j, i, 0)), (n_slots, m, ns), (tm, ns), bf16, name,
                        comm)


ROW_TILE = 256


def _rows(width, col_block=0):
    return pl.BlockSpec((ROW_TILE, width), lambda i, cb=col_block: (i, cb))


def _whole(shape):
    return pl.BlockSpec(shape, lambda i: (0,) * len(shape))


def _ln_fwd(x, y, g, b, alpha, name, comm=None):
    t, d = x.shape

    def body(*refs):
        if y is None:
            x_ref, g_ref, b_ref, o_ref, ob_ref = refs
            z = x_ref[...]
        else:
            x_ref, y_ref, g_ref, b_ref, o_ref, ob_ref = refs
            z = alpha * x_ref[...] + y_ref[...]
        out = _layer_norm(z, g_ref[...], b_ref[...])
        o_ref[...] = out
        ob_ref[...] = out.astype(bf16)

    ins = [x] + ([] if y is None else [y]) + [g, b]
    specs = [_rows(d)] + ([] if y is None else [_rows(d)]) + [_whole((1, d)), _whole((1, d))]
    (out, out_bf), comm_outs = _call(
        body, name, (t // ROW_TILE,), specs, [_rows(d), _rows(d)],
        [jax.ShapeDtypeStruct((t, d), f32), jax.ShapeDtypeStruct((t, d), bf16)], ins, (), ("parallel",), comm)
    return (out, out_bf) if comm is None else (out, out_bf, comm_outs)


def _ln_bwd(d_res, d_mm, x, y, g, b, alpha, name, comm=None):
    t, d = x.shape
    has_res, has_mm, has_y = d_res is not None, d_mm is not None, y is not None

    def body(*refs):
        refs = list(refs)
        d_res_ref = refs.pop(0) if has_res else None
        d_mm_ref = refs.pop(0) if has_mm else None
        x_ref = refs.pop(0)
        y_ref = refs.pop(0) if has_y else None
        g_ref, b_ref, dz_ref, dzb_ref, dg_ref, db_ref = refs
        ct = None
        if has_res:
            ct = alpha * d_res_ref[...]
        if has_mm:
            ct = d_mm_ref[...] if ct is None else ct + d_mm_ref[...]
        z = x_ref[...] if not has_y else alpha * x_ref[...] + y_ref[...]
        _, vjp = jax.vjp(_layer_norm, z, g_ref[...], b_ref[...])
        dz, dg, db = vjp(ct)
        dz_ref[...] = dz
        dzb_ref[...] = dz.astype(bf16)

        @pl.when(pl.program_id(0) == 0)
        def _():
            dg_ref[...] = jnp.zeros_like(dg_ref)
            db_ref[...] = jnp.zeros_like(db_ref)

        dg_ref[...] += dg
        db_ref[...] += db

    ins = [a for a in (d_res, d_mm, x, y) if a is not None] + [g, b]
    specs = [_rows(d) for a in (d_res, d_mm, x, y) if a is not None] + [_whole((1, d)), _whole((1, d))]
    outs, comm_outs = _call(
        body, name, (t // ROW_TILE,), specs, [_rows(d), _rows(d), _whole((1, d)), _whole((1, d))],
        [jax.ShapeDtypeStruct((t, d), f32), jax.ShapeDtypeStruct((t, d), bf16),
         jax.ShapeDtypeStruct((1, d), f32), jax.ShapeDtypeStruct((1, d), f32)], ins, (), ("arbitrary",), comm)
    return tuple(outs) if comm is None else (*outs, comm_outs)


def _loss_call(xf, target):
    t, d = xf.shape

    def body(x_ref, t_ref, dx_ref, loss_ref):
        err = x_ref[...] - t_ref[...]
        dx_ref[...] = err * (1.0 / d)

        @pl.when(pl.program_id(0) == 0)
        def _():
            loss_ref[...] = jnp.zeros_like(loss_ref)

        part = 0.5 * jnp.sum(jnp.mean(jnp.square(err), axis=-1, keepdims=True), axis=0, keepdims=True)
        loss_ref[...] += jnp.broadcast_to(part, loss_ref.shape)

    return pl.pallas_call(
        body, name="loss_head", grid=(t // ROW_TILE,), in_specs=[_rows(d), _rows(d)],
        out_specs=[_rows(d), _whole((1, LANES))],
        out_shape=[jax.ShapeDtypeStruct((t, d), f32), jax.ShapeDtypeStruct((1, LANES), f32)],
        compiler_params=_cparams(("arbitrary",)),
    )(xf, target)


def _rope_tables(pos, inv, scale):
    t = pos.shape[0]

    def body(pos_ref, inv_ref, cs_ref, qt_ref):
        ang = pos_ref[...].astype(f32) * inv_ref[...]
        lane = lax.broadcasted_iota(jnp.int32, ang.shape, 1)
        cs = jnp.where(lane < LANES // 2, jnp.cos(ang), jnp.sin(ang))
        cs_ref[...] = cs
        qt_ref[:, :LANES] = jnp.full((ROW_TILE, LANES), scale, f32)
        qt_ref[:, LANES:] = scale * cs

    return pl.pallas_call(
        body, name="rope_tables", grid=(t // ROW_TILE,),
        in_specs=[pl.BlockSpec((ROW_TILE, 1), lambda i: (i, 0)), _whole((1, LANES))],
        out_specs=[_rows(LANES), _rows(2 * LANES)],
        out_shape=[jax.ShapeDtypeStruct((t, LANES), f32), jax.ShapeDtypeStruct((t, 2 * LANES), f32)],
        compiler_params=_cparams(("parallel",)),
    )(pos, inv)


def _prep_fn(cq, ckv, kr, cs, gq, gkv):
    return _rms_norm(cq, gq), _rms_norm(ckv, gkv), _dup_halves(kr * cs)


def _prep_fwd(h, cs, gq, gkv, lay):
    t = h.shape[0]
    ql, kvl = lay["ql"], lay["kvl"]

    def body(cq_ref, ckv_ref, kr_ref, cs_ref, gq_ref, gkv_ref, qn_ref, kvn_ref, krd_ref):
        qn, kvn, krd = _prep_fn(cq_ref[...], ckv_ref[...], kr_ref[...], cs_ref[...], gq_ref[...], gkv_ref[...])
        qn_ref[...] = qn.astype(bf16)
        kvn_ref[...] = kvn.astype(bf16)
        krd_ref[...] = krd.astype(bf16)

    return pl.pallas_call(
        body, name="prep_fwd", grid=(t // ROW_TILE,),
        in_specs=[_rows(ql, lay["off_q"] // ql), _rows(kvl, lay["off_kv"] // kvl), _rows(LANES, lay["off_kr"] // LANES),
                  _rows(LANES), _whole((1, ql)), _whole((1, kvl))],
        out_specs=[_rows(ql), _rows(kvl), _rows(LANES)],
        out_shape=[jax.ShapeDtypeStruct((t, ql), bf16), jax.ShapeDtypeStruct((t, kvl), bf16),
                   jax.ShapeDtypeStruct((t, LANES), bf16)],
        compiler_params=_cparams(("parallel",)),
    )(h, h, h, cs, gq, gkv)


def _prep_bwd(h, cs, gq, gkv, d_qn, d_kvn, d_krd, lay):
    t = h.shape[0]
    ql, kvl = lay["ql"], lay["kvl"]

    def body(cq_ref, ckv_ref, kr_ref, cs_ref, gq_ref, gkv_ref, dqn_ref, dkvn_ref, dkrd_ref,
             dcq_ref, dckv_ref, dkr_ref, dgq_ref, dgkv_ref):
        _, vjp = jax.vjp(_prep_fn, cq_ref[...], ckv_ref[...], kr_ref[...], cs_ref[...], gq_ref[...], gkv_ref[...])
        dcq, dckv, dkr, _, dgq, dgkv = vjp((dqn_ref[...], dkvn_ref[...], dkrd_ref[...].astype(f32)))
        dcq_ref[...] = dcq.astype(bf16)
        dckv_ref[...] = dckv.astype(bf16)
        dkr_ref[...] = dkr.astype(bf16)

        @pl.when(pl.program_id(0) == 0)
        def _():
            dgq_ref[...] = jnp.zeros_like(dgq_ref)
            dgkv_ref[...] = jnp.zeros_like(dgkv_ref)

        dgq_ref[...] += dgq
        dgkv_ref[...] += dgkv

    return pl.pallas_call(
        body, name="prep_bwd", grid=(t // ROW_TILE,),
        in_specs=[_rows(ql, lay["off_q"] // ql), _rows(kvl, lay["off_kv"] // kvl), _rows(LANES, lay["off_kr"] // LANES),
                  _rows(LANES), _whole((1, ql)), _whole((1, kvl)), _rows(ql), _rows(kvl), _rows(LANES)],
        out_specs=[_rows(ql), _rows(kvl), _rows(LANES), _whole((1, ql)), _whole((1, kvl))],
        out_shape=[jax.ShapeDtypeStruct((t, ql), bf16), jax.ShapeDtypeStruct((t, kvl), bf16),
                   jax.ShapeDtypeStruct((t, LANES), bf16), jax.ShapeDtypeStruct((1, ql), f32),
                   jax.ShapeDtypeStruct((1, kvl), f32)],
        compiler_params=_cparams(("arbitrary",)),
    )(h, h, h, cs, gq, gkv, d_qn, d_kvn, d_krd)


def _convln_fn(hc, g, b):
    y = _layer_norm(hc, g, b)
    return y * jax.nn.sigmoid(y)


def _convln_fwd(hconv, g, b):
    t, cw = hconv.shape

    def body(h_ref, g_ref, b_ref, o_ref):
        o_ref[...] = _convln_fn(h_ref[...], g_ref[...], b_ref[...]).astype(bf16)

    return pl.pallas_call(
        body, name="convln_fwd", grid=(t // ROW_TILE,),
        in_specs=[_rows(cw), _whole((1, cw)), _whole((1, cw))], out_specs=_rows(cw),
        out_shape=jax.ShapeDtypeStruct((t, cw), bf16), compiler_params=_cparams(("parallel",)),
    )(hconv, g, b)


def _convln_bwd(hconv, g, b, d_mixed, col_block):
    t, cw = hconv.shape

    def body(h_ref, g_ref, b_ref, dy_ref, dh_ref, dg_ref, db_ref):
        _, vjp = jax.vjp(_convln_fn, h_ref[...], g_ref[...], b_ref[...])
        dh, dg, db = vjp(dy_ref[...])
        dh_ref[...] = dh

        @pl.when(pl.program_id(0) == 0)
        def _():
            dg_ref[...] = jnp.zeros_like(dg_ref)
            db_ref[...] = jnp.zeros_like(db_ref)

        dg_ref[...] += dg
        db_ref[...] += db

    return pl.pallas_call(
        body, name="convln_bwd", grid=(t // ROW_TILE,),
        in_specs=[_rows(cw), _whole((1, cw)), _whole((1, cw)), _rows(cw, col_block)],
        out_specs=[_rows(cw), _whole((1, cw)), _whole((1, cw))],
        out_shape=[jax.ShapeDtypeStruct((t, cw), f32), jax.ShapeDtypeStruct((1, cw), f32),
                   jax.ShapeDtypeStruct((1, cw), f32)],
        compiler_params=_cparams(("arbitrary",)),
    )(hconv, g, b, d_mixed)


SEQ_CT = 128


def _conv_fwd(h, w, b, nb, seq, cw):
    ct, kk = SEQ_CT, w.shape[0]
    ncb = cw // ct

    def body(h_ref, w_ref, b_ref, o_ref):
        blk = h_ref[...]
        a, g = blk[:, :ct], blk[:, ct:]
        hc = a * jax.nn.sigmoid(g)
        acc = jnp.zeros_like(hc)
        for k in range(kk):
            acc = acc + w_ref[k:k + 1, :] * _shift_down_raw(hc, kk - 1 - k)
        o_ref[...] = acc + b_ref[...]

    return pl.pallas_call(
        body, name="conv_fwd", grid=(ncb, nb),
        in_specs=[pl.BlockSpec((seq, 2 * ct), lambda j, bb: (bb, j)), pl.BlockSpec((kk, ct), lambda j, bb: (0, j)),
                  pl.BlockSpec((1, ct), lambda j, bb: (0, j))],
        out_specs=pl.BlockSpec((seq, ct), lambda j, bb: (bb, j)),
        out_shape=jax.ShapeDtypeStruct((nb * seq, cw), f32),
        compiler_params=_cparams(("parallel", "parallel")),
    )(h, w, b)


def _conv_bwd(h, w, d_hconv, nb, seq, cw):
    ct, kk = SEQ_CT, w.shape[0]
    ncb = cw // ct

    def body(h_ref, w_ref, dy_ref, dh_ref, dw_ref, db_ref):
        blk = h_ref[...]
        a, g = blk[:, :ct], blk[:, ct:]
        sg = jax.nn.sigmoid(g)
        hc = a * sg
        dy = dy_ref[...]
        dhc = jnp.zeros_like(hc)

        @pl.when(pl.program_id(1) == 0)
        def _():
            dw_ref[...] = jnp.zeros_like(dw_ref)
            db_ref[...] = jnp.zeros_like(db_ref)

        for k in range(kk):
            dhc = dhc + w_ref[k:k + 1, :] * _shift_up_raw(dy, kk - 1 - k)
            dw_ref[k:k + 1, :] += _colsum(dy * _shift_down_raw(hc, kk - 1 - k))
        db_ref[...] += _colsum(dy)
        dh_ref[:, :ct] = (dhc * sg).astype(bf16)
        dh_ref[:, ct:] = (dhc * a * sg * (1.0 - sg)).astype(bf16)

    return pl.pallas_call(
        body, name="conv_bwd", grid=(ncb, nb),
        in_specs=[pl.BlockSpec((seq, 2 * ct), lambda j, bb: (bb, j)), pl.BlockSpec((kk, ct), lambda j, bb: (0, j)),
                  pl.BlockSpec((seq, ct), lambda j, bb: (bb, j))],
        out_specs=[pl.BlockSpec((seq, 2 * ct), lambda j, bb: (bb, j)), pl.BlockSpec((kk, ct), lambda j, bb: (0, j)),
                   pl.BlockSpec((1, ct), lambda j, bb: (0, j))],
        out_shape=[jax.ShapeDtypeStruct((nb * seq, 2 * cw), bf16), jax.ShapeDtypeStruct((kk, cw), f32),
                   jax.ShapeDtypeStruct((1, cw), f32)],
        compiler_params=_cparams(("parallel", "arbitrary")),
    )(h, w, d_hconv)


def _pool_fn(u, wp, scale, pg):
    seq = u.shape[0]
    t1 = (lax.broadcasted_iota(jnp.int32, (seq, 1), 0) + 1).astype(f32)
    outs = []
    for gi, win in enumerate(POOL_WINDOWS):
        ug = u[:, gi * pg:(gi + 1) * pg]
        acc, span = ug, 1
        while span < win:
            acc = acc + _shift_down(acc, span)
            span *= 2
        d = acc / jnp.minimum(t1, float(win)) - ug
        outs.append(_mm_bf16(d, wp[gi * pg:(gi + 1) * pg, :]) * scale[:, gi * pg:(gi + 1) * pg])
    return outs


def _pool_fwd(h, wp, scale, lay, nb, seq):
    pw, pg = lay["pw"], lay["pg"]

    def body(u_ref, wp_ref, sc_ref, o_ref):
        outs = _pool_fn(u_ref[...], wp_ref[...], sc_ref[...], pg)
        for gi in range(len(POOL_WINDOWS)):
            o_ref[:, gi * pg:(gi + 1) * pg] = outs[gi].astype(bf16)

    return pl.pallas_call(
        body, name="pool_fwd", grid=(nb,),
        in_specs=[pl.BlockSpec((seq, pw), lambda bb: (bb, lay["off_pool"] // pw)), _whole((pw, pg)), _whole((1, pw))],
        out_specs=pl.BlockSpec((seq, pw), lambda bb: (bb, 0)),
        out_shape=jax.ShapeDtypeStruct((nb * seq, pw), bf16),
        compiler_params=_cparams(("parallel",)),
    )(h, wp, scale)


def _pool_bwd(h, wp, scale, d_mixed, col_block, lay, nb, seq):
    pw, pg = lay["pw"], lay["pg"]
    ng = len(POOL_WINDOWS)

    def body(u_ref, wp_ref, sc_ref, dy_ref, du_ref, dwp_ref, dsc_ref):
        _, vjp = jax.vjp(functools.partial(_pool_fn, pg=pg), u_ref[...], wp_ref[...], sc_ref[...])
        dy = dy_ref[...]
        du, dwp, dsc = vjp([dy[:, gi * pg:(gi + 1) * pg] for gi in range(ng)])
        du_ref[...] = du.astype(bf16)

        @pl.when(pl.program_id(0) == 0)
        def _():
            dwp_ref[...] = jnp.zeros_like(dwp_ref)
            dsc_ref[...] = jnp.zeros_like(dsc_ref)

        dwp_ref[...] += dwp
        dsc_ref[...] += dsc

    return pl.pallas_call(
        body, name="pool_bwd", grid=(nb,),
        in_specs=[pl.BlockSpec((seq, pw), lambda bb: (bb, lay["off_pool"] // pw)), _whole((pw, pg)), _whole((1, pw)),
                  pl.BlockSpec((seq, pw), lambda bb: (bb, col_block))],
        out_specs=[pl.BlockSpec((seq, pw), lambda bb: (bb, 0)), _whole((pw, pg)), _whole((1, pw))],
        out_shape=[jax.ShapeDtypeStruct((nb * seq, pw), bf16), jax.ShapeDtypeStruct((pw, pg), f32),
                   jax.ShapeDtypeStruct((1, pw), f32)],
        compiler_params=_cparams(("arbitrary",)),
    )(h, wp, scale, d_mixed)


FFN_ROWS = 32
FFN_FWD_CT = 256
FFN_BWD_CT = 128
SUBLANES = 8


def _rows_before(ref, r0, s, n, cols):
    if s == 0:
        return ref[r0:r0 + n, cols]
    if r0 == 0:
        x = ref[0:n, cols]
        row = lax.broadcasted_iota(jnp.int32, x.shape, 0)
        return jnp.where(row >= s, pltpu.roll(x, s, axis=0), 0.0)
    return ref[pl.ds(r0 - s, n), cols]


def _ffn_conv_rows(x_ref, w_ref, b_ref, r0, n, cols):
    kk = w_ref.shape[0]
    xs = [_rows_before(x_ref, r0, s, n, cols) for s in range(kk)]
    c = b_ref[:, cols] + w_ref[kk - 1:kk, cols] * xs[0]
    for k in range(kk - 1):
        c = c + w_ref[k:k + 1, cols] * xs[kk - 1 - k]
    return c, xs


def _lane_tiles(width):
    return [slice(c0, c0 + LANES) for c0 in range(0, width, LANES)]


def _fold_rows(x):
    out = x[0:SUBLANES]
    for i in range(1, x.shape[0] // SUBLANES):
        out = out + x[i * SUBLANES:(i + 1) * SUBLANES]
    return out


def _ffn_specs(seq, ct, kk, nct):
    return [pl.BlockSpec((seq, ct), lambda j, bb: (bb, j)), pl.BlockSpec((seq, ct), lambda j, bb: (bb, nct + j)),
            pl.BlockSpec((kk, ct), lambda j, bb: (0, j)), pl.BlockSpec((kk, ct), lambda j, bb: (0, nct + j)),
            pl.BlockSpec((1, ct), lambda j, bb: (0, j)), pl.BlockSpec((1, ct), lambda j, bb: (0, nct + j))]


def _ffn_act_fwd(up, w, b, nb, seq, dff, comm=None):
    ct, kk = _tile(dff, FFN_FWD_CT), w.shape[0]
    rows = min(FFN_ROWS, seq)

    def body(ua_ref, ug_ref, wa_ref, wg_ref, ba_ref, bg_ref, o_ref):
        for cols in _lane_tiles(ct):
            for r0 in range(0, seq, rows):
                a, _ = _ffn_conv_rows(ua_ref, wa_ref, ba_ref, r0, rows, cols)
                g, _ = _ffn_conv_rows(ug_ref, wg_ref, bg_ref, r0, rows, cols)
                o_ref[r0:r0 + rows, cols] = (a * g * jax.nn.sigmoid(g)).astype(bf16)

    (act,), comm_outs = _call(body, "ffn_act_fwd", (dff // ct, nb), _ffn_specs(seq, ct, kk, dff // ct),
                              [pl.BlockSpec((seq, ct), lambda j, bb: (bb, j))],
                              [jax.ShapeDtypeStruct((nb * seq, dff), bf16)], (up, up, w, w, b, b), (),
                              ("parallel", "parallel"), comm)
    return act if comm is None else (act, comm_outs)


def _ffn_act_bwd(up, w, b, d_act, nb, seq, dff, comm=None):
    ct, kk = _tile(dff, FFN_BWD_CT), w.shape[0]
    rows = min(FFN_ROWS, seq)

    def body(ua_ref, ug_ref, wa_ref, wg_ref, ba_ref, bg_ref, da_ref, du_ref, dw_ref, db_ref, dc_ref):
        @pl.when(pl.program_id(1) == 0)
        def _():
            dw_ref[...] = jnp.zeros_like(dw_ref)
            db_ref[...] = jnp.zeros_like(db_ref)

        for cols in _lane_tiles(ct):
            dw_acc = [[jnp.zeros((SUBLANES, LANES), f32) for _ in range(kk)] for _ in range(2)]
            db_acc = [jnp.zeros((SUBLANES, LANES), f32) for _ in range(2)]
            for r0 in range(0, seq, rows):
                a, xa = _ffn_conv_rows(ua_ref, wa_ref, ba_ref, r0, rows, cols)
                g, xg = _ffn_conv_rows(ug_ref, wg_ref, bg_ref, r0, rows, cols)
                sg = jax.nn.sigmoid(g)
                dact = da_ref[r0:r0 + rows, cols].astype(f32)
                dcs = (dact * g * sg, dact * a * sg * (1.0 + g * (1.0 - sg)))
                for hf, (dc, xs) in enumerate(zip(dcs, (xa, xg))):
                    dc_ref[hf, r0:r0 + rows, cols] = dc
                    db_acc[hf] = db_acc[hf] + _fold_rows(dc)
                    for k in range(kk):
                        dw_acc[hf][k] = dw_acc[hf][k] + _fold_rows(dc * xs[kk - 1 - k])
            for hf in range(2):
                dc_ref[hf, seq:seq + SUBLANES, cols] = jnp.zeros((SUBLANES, LANES), f32)
                db_ref[hf, :, cols] += _colsum(db_acc[hf])
                for k in range(kk):
                    dw_ref[hf, k:k + 1, cols] += _colsum(dw_acc[hf][k])
            for r0 in range(0, seq, rows):
                for hf, w_ref in enumerate((wa_ref, wg_ref)):
                    dsrc = w_ref[kk - 1:kk, cols] * dc_ref[hf, r0:r0 + rows, cols]
                    for k in range(kk - 1):
                        dsrc = dsrc + w_ref[k:k + 1, cols] * dc_ref[hf, pl.ds(r0 + kk - 1 - k, rows), cols]
                    du_ref[hf, r0:r0 + rows, cols] = dsrc.astype(bf16)

    (d_up, dw, db), comm_outs = _call(
        body, "ffn_act_bwd", (dff // ct, nb),
        _ffn_specs(seq, ct, kk, dff // ct) + [pl.BlockSpec((seq, ct), lambda j, bb: (bb, j))],
        [pl.BlockSpec((2, seq, ct), lambda j, bb: (0, bb, j)), pl.BlockSpec((2, kk, ct), lambda j, bb: (0, 0, j)),
         pl.BlockSpec((2, 1, ct), lambda j, bb: (0, 0, j))],
        [jax.ShapeDtypeStruct((2, nb * seq, dff), bf16), jax.ShapeDtypeStruct((2, kk, dff), f32),
         jax.ShapeDtypeStruct((2, 1, dff), f32)],
        (up, up, w, w, b, b, d_act), [pltpu.VMEM((2, seq + SUBLANES, ct), f32)], ("parallel", "arbitrary"), comm)
    return (d_up, dw, db) if comm is None else (d_up, dw, db, comm_outs)


def _fill_keys(k_scr, kn_ref, krd_ref):
    @pl.when(pl.program_id(2) == 0)
    def _():
        k_scr[:, :LANES] = kn_ref[...]
        k_scr[:, LANES:] = krd_ref[...]


def _scores(q_ref, qt_ref, k_scr, qblk, tq):
    klen = (qblk + 1) * tq
    q = (q_ref[...] * qt_ref[...]).astype(bf16)
    s = _dot(q, k_scr[0:klen, :], _NT)
    row = qblk * tq + lax.broadcasted_iota(jnp.int32, s.shape, 0)
    col = lax.broadcasted_iota(jnp.int32, s.shape, 1)
    return q, jnp.where(col <= row, s, NEG_INF)


def _per_q_block(nq, fn):
    qi = pl.program_id(2)
    for qblk in range(nq):
        pl.when(qi == qblk)(functools.partial(fn, qblk))


def _attn_fwd(q_ext, qt, kv, krd, nb, seq, heads, comm=None):
    tq = _tile(seq, 512)
    nq = seq // tq

    def body(q_ref, qt_ref, kn_ref, krd_ref, v_ref, o_ref, lse_ref, k_scr):
        _fill_keys(k_scr, kn_ref, krd_ref)

        def work(qblk):
            klen = (qblk + 1) * tq
            _, s = _scores(q_ref, qt_ref, k_scr, qblk, tq)
            m = jnp.max(s, axis=-1, keepdims=True)
            p = jnp.exp(s - m)
            l = jnp.sum(p, axis=-1, keepdims=True)
            o_ref[...] = (_dot(p, v_ref[0:klen, :], _NN) / l).astype(bf16)
            lse_ref[...] = m + jnp.log(l)

        _per_q_block(nq, work)

    (o, lse), comm_outs = _call(
        body, "attn_fwd", (nb, heads, nq),
        [pl.BlockSpec((tq, 2 * LANES), lambda b, h, i: (b * nq + i, h)),
         pl.BlockSpec((tq, 2 * LANES), lambda b, h, i: (b * nq + i, 0)),
         pl.BlockSpec((seq, LANES), lambda b, h, i: (b, 2 * h)),
         pl.BlockSpec((seq, LANES), lambda b, h, i: (b, 0)),
         pl.BlockSpec((seq, LANES), lambda b, h, i: (b, 2 * h + 1))],
        [pl.BlockSpec((tq, LANES), lambda b, h, i: (b * nq + i, h)),
         pl.BlockSpec((None, tq, 1), lambda b, h, i: (b * heads + h, i, 0))],
        [jax.ShapeDtypeStruct((nb * seq, heads * LANES), bf16), jax.ShapeDtypeStruct((nb * heads, seq, 1), f32)],
        (q_ext, qt, kv, krd, kv), [pltpu.VMEM((seq, 2 * LANES), bf16)], ("parallel", "parallel", "arbitrary"), comm)
    return (o, lse) if comm is None else (o, lse, comm_outs)


def _attn_bwd(q_ext, qt, kv, krd, lse, d_mixed, nb, seq, heads, comm=None):
    tq = _tile(seq, 512)
    nq = seq // tq

    def body(q_ref, qt_ref, kn_ref, krd_ref, v_ref, lse_ref, do_ref, dq_ref, dkv_ref, dkrd_ref, dkv_acc, k_scr):
        h, qi = pl.program_id(1), pl.program_id(2)
        _fill_keys(k_scr, kn_ref, krd_ref)

        @pl.when(qi == 0)
        def _():
            dkv_acc[...] = jnp.zeros_like(dkv_acc)

        @pl.when((qi == 0) & (h == 0))
        def _():
            dkrd_ref[...] = jnp.zeros_like(dkrd_ref)

        def work(qblk):
            klen = (qblk + 1) * tq
            q, s = _scores(q_ref, qt_ref, k_scr, qblk, tq)
            p = jnp.exp(s - lse_ref[...])
            do = do_ref[...]
            dp = _dot(do, v_ref[0:klen, :], _NT)
            ds = (p * (dp - jnp.sum(p * dp, axis=-1, keepdims=True))).astype(bf16)
            dq_ref[...] = (_dot(ds, k_scr[0:klen, :], _NN) * qt_ref[...]).astype(bf16)
            dk = _dot(ds, q, _TN)
            dkv_acc[0:klen, :LANES] += dk[:, :LANES]
            dkv_acc[0:klen, LANES:] += _dot(p, do, _TN)
            dkrd_ref[0:klen, :] += dk[:, LANES:]

        _per_q_block(nq, work)

        @pl.when(qi == nq - 1)
        def _():
            dkv_ref[...] = dkv_acc[...].astype(bf16)

    (dq, dkv, dkrd), comm_outs = _call(
        body, "attn_bwd", (nb, heads, nq),
        [pl.BlockSpec((tq, 2 * LANES), lambda b, h, i: (b * nq + i, h)),
         pl.BlockSpec((tq, 2 * LANES), lambda b, h, i: (b * nq + i, 0)),
         pl.BlockSpec((seq, LANES), lambda b, h, i: (b, 2 * h)),
         pl.BlockSpec((seq, LANES), lambda b, h, i: (b, 0)),
         pl.BlockSpec((seq, LANES), lambda b, h, i: (b, 2 * h + 1)),
         pl.BlockSpec((None, tq, 1), lambda b, h, i: (b * heads + h, i, 0)),
         pl.BlockSpec((tq, LANES), lambda b, h, i: (b * nq + i, h))],
        [pl.BlockSpec((tq, 2 * LANES), lambda b, h, i: (b * nq + i, h)),
         pl.BlockSpec((seq, 2 * LANES), lambda b, h, i: (b, h)),
         pl.BlockSpec((seq, LANES), lambda b, h, i: (b, 0))],
        [jax.ShapeDtypeStruct((nb * seq, heads * 2 * LANES), bf16),
         jax.ShapeDtypeStruct((nb * seq, heads * 2 * LANES), bf16),
         jax.ShapeDtypeStruct((nb * seq, LANES), f32)],
        (q_ext, qt, kv, krd, kv, lse, d_mixed), [pltpu.VMEM((seq, 2 * LANES), f32), pltpu.VMEM((seq, 2 * LANES), bf16)],
        ("parallel", "arbitrary", "arbitrary"), comm)
    return (dq, dkv, dkrd) if comm is None else (dq, dkv, dkrd, comm_outs)


def _mesh_pos():
    x, y, c = lax.axis_index("x"), lax.axis_index("y"), lax.axis_index("c")
    return x, y, c, [(1 - x, y), (x, 1 - y), (1 - x, 1 - y)]


def _all_gather(shards, name):
    n = len(shards)

    def body(*refs):
        x_refs, out_refs, (send_sems, recv_sems, local_sems) = refs[:n], refs[n:2 * n], refs[2 * n:]
        x, y, c, chips = _mesh_pos()
        me, sibling = (x, y, c), (x, y, 1 - c)

        def copy(t, k, block, to, from_shard=False):
            px, py, pc = block
            rows = out_refs[t].at[4 * px + 2 * py + pc]
            return pltpu.make_async_remote_copy(
                src_ref=x_refs[t] if from_shard else rows, dst_ref=rows,
                send_sem=send_sems.at[t, k], recv_sem=recv_sems.at[t, k], device_id=to, device_id_type=MESH)

        mine = [pltpu.make_async_copy(x_refs[t], out_refs[t].at[4 * x + 2 * y + c], local_sems.at[t]) for t in range(n)]
        first = [[copy(t, 0, me, sibling, True)] + [copy(t, 1 + j, me, (*chip, c), True) for j, chip in enumerate(chips)]
                 for t in range(n)]
        passed = [[copy(t, 4 + j, (*chip, c), sibling) for j, chip in enumerate(chips)] for t in range(n)]
        for t in range(n):
            mine[t].start()
            for cp in first[t]:
                cp.start()
        for j, chip in enumerate(chips):
            for t in range(n):
                copy(t, 1 + j, (*chip, c), me).wait_recv()
                passed[t][j].start()
        for t in range(n):
            copy(t, 0, sibling, me).wait_recv()
            for j, chip in enumerate(chips):
                copy(t, 4 + j, (*chip, 1 - c), me).wait_recv()
        for t in range(n):
            for cp in first[t] + passed[t]:
                cp.wait_send()
            mine[t].wait()

    return pl.pallas_call(
        body, name=name, out_shape=[jax.ShapeDtypeStruct((N_DEV,) + s.shape, s.dtype) for s in shards],
        in_specs=[_ANY] * n, out_specs=[_ANY] * n,
        scratch_shapes=[pltpu.SemaphoreType.DMA((n, 7)), pltpu.SemaphoreType.DMA((n, 7)), pltpu.SemaphoreType.DMA((n,))],
    )(*shards)


def _gather_own(shards):
    n = len(shards)

    def remote(x_refs, out_refs, sems, arriving):
        send_sems, recv_sems, _ = sems
        x, y, c, chips = _mesh_pos()
        peers = [(x, y, 1 - c)] + [(*chip, c) for chip in chips]
        return [pltpu.make_async_remote_copy(
            src_ref=x_refs[t], dst_ref=out_refs[t].at[4 * px + 2 * py + pc if arriving else 4 * x + 2 * y + c],
            send_sem=send_sems.at[t, k], recv_sem=recv_sems.at[t, k], device_id=(px, py, pc), device_id_type=MESH)
            for t in range(n) for k, (px, py, pc) in enumerate(peers)]

    def local(x_refs, out_refs, sems):
        x, y, c, _ = _mesh_pos()
        return [pltpu.make_async_copy(x_refs[t], out_refs[t].at[4 * x + 2 * y + c], sems[2].at[t]) for t in range(n)]

    def start(x_refs, out_refs, sems):
        for cp in local(x_refs, out_refs, sems) + remote(x_refs, out_refs, sems, False):
            cp.start()

    def finish(x_refs, out_refs, sems):
        for cp in remote(x_refs, out_refs, sems, True):
            cp.wait_recv()
        for cp in remote(x_refs, out_refs, sems, False):
            cp.wait_send()
        for cp in local(x_refs, out_refs, sems):
            cp.wait()

    return _Comm(shards, [jax.ShapeDtypeStruct((N_DEV,) + s.shape, s.dtype) for s in shards],
                 [pltpu.SemaphoreType.DMA((n, 4)), pltpu.SemaphoreType.DMA((n, 4)), pltpu.SemaphoreType.DMA((n,))],
                 start, finish)


def _gather_pass(gathered):
    n = len(gathered)

    def copies(in_refs, out_refs, sems, arriving):
        send_sems, recv_sems = sems
        x, y, c, chips = _mesh_pos()
        return [pltpu.make_async_remote_copy(
            src_ref=in_refs[t].at[4 * px + 2 * py + c],
            dst_ref=out_refs[t].at[4 * px + 2 * py + (1 - c if arriving else c)],
            send_sem=send_sems.at[t, j], recv_sem=recv_sems.at[t, j], device_id=(x, y, 1 - c), device_id_type=MESH)
            for t in range(n) for j, (px, py) in enumerate(chips)]

    def start(in_refs, out_refs, sems):
        for cp in copies(in_refs, out_refs, sems, False):
            cp.start()

    def finish(in_refs, out_refs, sems):
        for cp in copies(in_refs, out_refs, sems, True):
            cp.wait_recv()
        for cp in copies(in_refs, out_refs, sems, False):
            cp.wait_send()

    return _Comm(gathered, [jax.ShapeDtypeStruct(g.shape, g.dtype) for g in gathered],
                 [pltpu.SemaphoreType.DMA((n, 3)), pltpu.SemaphoreType.DMA((n, 3))], start, finish,
                 aliases={t: t for t in range(n)})


def _sibling_swap(slots):
    n = len(slots)

    def start(g_refs, out_refs, sems):
        send_sems, recv_sems = sems
        x, y, c, _ = _mesh_pos()
        for t in range(n):
            for k in range(4):
                pltpu.make_async_remote_copy(
                    src_ref=g_refs[t].at[2 * k + (1 - c)], dst_ref=out_refs[t].at[k], send_sem=send_sems.at[t],
                    recv_sem=recv_sems.at[t], device_id=(x, y, 1 - c), device_id_type=MESH).start()

    def finish(g_refs, out_refs, sems):
        send_sems, recv_sems = sems
        x, y, c, _ = _mesh_pos()
        for t in range(n):
            pltpu.make_async_remote_copy(
                src_ref=g_refs[t].at[pl.ds(0, 4)], dst_ref=out_refs[t], send_sem=send_sems.at[t],
                recv_sem=recv_sems.at[t], device_id=(x, y, 1 - c), device_id_type=MESH).wait()

    return _Comm(slots, [jax.ShapeDtypeStruct((4,) + s.shape[1:], s.dtype) for s in slots],
                 [pltpu.SemaphoreType.DMA((n,)), pltpu.SemaphoreType.DMA((n,))], start, finish)


def _chip_swap(p4s, peers=(0, 1, 2), into=None):
    n = len(p4s)

    def copies(refs, out_refs, sems):
        send_sems, recv_sems = sems
        x, y, c, chips = _mesh_pos()
        return [pltpu.make_async_remote_copy(
            src_ref=refs[t].at[2 * chips[j][0] + chips[j][1]], dst_ref=out_refs[t].at[j], send_sem=send_sems.at[t, j],
            recv_sem=recv_sems.at[t, j], device_id=(*chips[j], c), device_id_type=MESH)
            for t in range(n) for j in peers]

    def start(refs, out_refs, sems):
        for cp in copies(refs, out_refs, sems):
            cp.start()

    def finish(refs, out_refs, sems):
        for cp in copies(refs, out_refs, sems):
            cp.wait()

    out_shapes = [jax.ShapeDtypeStruct((3,) + p.shape[1:], p.dtype) for p in p4s]
    sems = [pltpu.SemaphoreType.DMA((n, 3)), pltpu.SemaphoreType.DMA((n, 3))]
    if into is None:
        return _Comm(p4s, out_shapes, sems, start, finish)
    return _Comm(list(p4s) + list(into), out_shapes, sems, start, finish, aliases={n + t: t for t in range(n)})


def _comm_join(a, b):
    ai, ao, asem = len(a.inputs), len(a.out_shapes), len(a.sems)

    def start(ins, outs, sems):
        a.start(ins[:ai], outs[:ao], sems[:asem])
        b.start(ins[ai:], outs[ao:], sems[asem:])

    def finish(ins, outs, sems):
        a.finish(ins[:ai], outs[:ao], sems[:asem])
        b.finish(ins[ai:], outs[ao:], sems[asem:])

    aliases = dict(a.aliases)
    aliases.update({ai + i: ao + o for i, o in b.aliases.items()})
    return _Comm(a.inputs + b.inputs, a.out_shapes + b.out_shapes, a.sems + b.sems, start, finish, aliases)


def _row_tile(rows, cols, max_bytes=1024 * 1024):
    best = None
    for tr in range(16, rows + 1, 16):
        if rows % tr == 0 and tr * cols * 4 <= max_bytes:
            best = tr
    return best or rows


def _pair_add(slots, theirs, core, name):
    _, rows, cols = slots.shape
    tr = _row_tile(rows, cols, 4 * 1024 * 1024)

    def body(c_ref, a_ref, b_ref, o_ref):
        o_ref[...] = (a_ref[...].astype(f32) + b_ref[...].astype(f32)).astype(bf16)

    return pl.pallas_call(
        body, name=name,
        grid_spec=pltpu.PrefetchScalarGridSpec(
            num_scalar_prefetch=1, grid=(4, rows // tr),
            in_specs=[pl.BlockSpec((None, tr, cols), lambda k, i, c_ref: (2 * k + c_ref[0], i, 0)),
                      pl.BlockSpec((None, tr, cols), lambda k, i, c_ref: (k, i, 0))],
            out_specs=pl.BlockSpec((None, tr, cols), lambda k, i, c_ref: (k, i, 0))),
        out_shape=jax.ShapeDtypeStruct((4, rows, cols), bf16), compiler_params=_cparams(("parallel", "parallel")),
    )(core, slots, theirs)


def _adam_update(g, w, m, v):
    c1 = 1.0 / (1.0 - ADAM_B1 ** ADAM_STEP)
    c2 = 1.0 / (1.0 - ADAM_B2 ** ADAM_STEP)
    nm = ADAM_B1 * m + (1.0 - ADAM_B1) * g
    nv = ADAM_B2 * v + (1.0 - ADAM_B2) * jnp.square(g)
    delta = -ADAM_LR * ((nm * c1) / (jnp.sqrt(nv * c2) + ADAM_EPS) + ADAM_WD * w)
    return delta, nm, nv


def _adamw_layers(p4s, chips, chip_idx, w, m, v, name, comm=None):
    depth = len(p4s)
    _, rows_l, cols = p4s[0].shape
    tr = _row_tile(rows_l, cols)
    nr = rows_l // tr

    def body(idx_ref, *refs):
        p_refs, c_refs = refs[:depth], refs[depth:2 * depth]
        w_ref, m_ref, v_ref, g_ref, d_ref, nm_ref, nv_ref = refs[2 * depth:]
        layer = pl.program_id(0)
        for ll in range(depth):
            @pl.when(layer == ll)
            def _(ll=ll):
                g = p_refs[ll][...].astype(f32)
                for j in range(3):
                    g = g + c_refs[ll][j].astype(f32)
                delta, nm, nv = _adam_update(g, w_ref[...], m_ref[...], v_ref[...])
                g_ref[...] = g
                d_ref[...] = delta
                nm_ref[...] = nm
                nv_ref[...] = nv

    def of_layer(ll):
        return lambda l, i: jnp.where(l == ll, i, 0)

    p_specs = [pl.BlockSpec((None, tr, cols), lambda l, i, idx_ref, f=of_layer(ll): (idx_ref[0], f(l, i), 0))
               for ll in range(depth)]
    c_specs = [pl.BlockSpec((3, tr, cols), lambda l, i, idx_ref, f=of_layer(ll): (0, f(l, i), 0)) for ll in range(depth)]
    spec = pl.BlockSpec((tr, cols), lambda l, i, idx_ref: (l * nr + i, 0))
    out = jax.ShapeDtypeStruct(w.shape, f32)
    outs, comm_outs = _call(body, name, (depth, nr), p_specs + c_specs + [spec, spec, spec], [spec, spec, spec, spec],
                            [out, out, out, out], (*p4s, *chips, w, m, v), (), ("parallel", "parallel"), comm,
                            prefetch=[chip_idx])
    return outs if comm is None else (outs, comm_outs)


def _adamw(parts, w, m, v, name):
    rows, cols = w.shape
    tr = _row_tile(rows, cols, 512 * 1024)
    nparts = len(parts)

    def body(*refs):
        part_refs, (w_ref, m_ref, v_ref, g_ref, d_ref, nm_ref, nv_ref) = refs[:nparts], refs[nparts:]
        g = None
        for pr in part_refs:
            for s in range(pr.shape[0]):
                term = pr[s].astype(f32)
                g = term if g is None else g + term
        delta, nm, nv = _adam_update(g, w_ref[...], m_ref[...], v_ref[...])
        g_ref[...] = g
        d_ref[...] = delta
        nm_ref[...] = nm
        nv_ref[...] = nv

    spec = pl.BlockSpec((tr, cols), lambda i: (i, 0))
    part_specs = [pl.BlockSpec((p.shape[0], tr, cols), lambda i: (0, i, 0)) for p in parts]
    out = jax.ShapeDtypeStruct((rows, cols), f32)
    return pl.pallas_call(
        body, name=name, grid=(rows // tr,), in_specs=part_specs + [spec, spec, spec],
        out_specs=[spec, spec, spec, spec], out_shape=[out, out, out, out],
        compiler_params=_cparams(("parallel",)),
    )(*parts, w, m, v)


def _gathered_to_full(gathered, axis):
    s = gathered.shape[1:]
    full = jnp.moveaxis(gathered, 0, axis)
    return full.reshape(s[:axis] + (N_DEV * s[axis],) + s[axis + 1:])


def _interleave_halves(w, ct):
    n = w.shape[-1] // 2
    t = w.reshape(w.shape[:-1] + (2, n // ct, ct))
    return jnp.swapaxes(t, -3, -2).reshape(w.shape)


def _deinterleave_halves(w, ct):
    n = w.shape[-1] // 2
    t = w.reshape(w.shape[:-1] + (n // ct, 2, ct))
    return jnp.swapaxes(t, -3, -2).reshape(w.shape)


def _rot_cols(w):
    half = QK_ROPE_DIM // 2
    return jnp.concatenate([-w[..., half:], w[..., :half]], axis=-1)


def _rot_cols_t(dw):
    half = QK_ROPE_DIM // 2
    return jnp.concatenate([dw[..., half:], -dw[..., :half]], axis=-1)


def kernel(x, positions, ln_in_g, ln_in_b, w_in, q_norm_g, w_uq, kv_norm_g, w_ukv, conv_w, conv_b, conv_ln_g, conv_ln_b, w_pool, pool_scale, w_out, ln1_g, ln1_b, w_up, ffn_conv_w, ffn_conv_b, w_down, ln2_g, ln2_b, loss_target, m_ln_in_g, m_ln_in_b, m_w_in, m_q_norm_g, m_w_uq, m_kv_norm_g, m_w_ukv, m_conv_w, m_conv_b, m_conv_ln_g, m_conv_ln_b, m_w_pool, m_pool_scale, m_w_out, m_ln1_g, m_ln1_b, m_w_up, m_ffn_conv_w, m_ffn_conv_b, m_w_down, m_ln2_g, m_ln2_b, v_ln_in_g, v_ln_in_b, v_w_in, v_q_norm_g, v_w_uq, v_kv_norm_g, v_w_ukv, v_conv_w, v_conv_b, v_conv_ln_g, v_conv_ln_b, v_w_pool, v_pool_scale, v_w_out, v_ln1_g, v_ln1_b, v_w_up, v_ffn_conv_w, v_ffn_conv_b, v_w_down, v_ln2_g, v_ln2_b):
    weights = dict(ln_in_g=ln_in_g, ln_in_b=ln_in_b, w_in=w_in, q_norm_g=q_norm_g, w_uq=w_uq, kv_norm_g=kv_norm_g,
                   w_ukv=w_ukv, conv_w=conv_w, conv_b=conv_b, conv_ln_g=conv_ln_g, conv_ln_b=conv_ln_b, w_pool=w_pool,
                   pool_scale=pool_scale, w_out=w_out, ln1_g=ln1_g, ln1_b=ln1_b, w_up=w_up, ffn_conv_w=ffn_conv_w,
                   ffn_conv_b=ffn_conv_b, w_down=w_down, ln2_g=ln2_g, ln2_b=ln2_b)
    mom1 = dict(ln_in_g=m_ln_in_g, ln_in_b=m_ln_in_b, w_in=m_w_in, q_norm_g=m_q_norm_g, w_uq=m_w_uq,
                kv_norm_g=m_kv_norm_g, w_ukv=m_w_ukv, conv_w=m_conv_w, conv_b=m_conv_b, conv_ln_g=m_conv_ln_g,
                conv_ln_b=m_conv_ln_b, w_pool=m_w_pool, pool_scale=m_pool_scale, w_out=m_w_out, ln1_g=m_ln1_g,
                ln1_b=m_ln1_b, w_up=m_w_up, ffn_conv_w=m_ffn_conv_w, ffn_conv_b=m_ffn_conv_b, w_down=m_w_down,
                ln2_g=m_ln2_g, ln2_b=m_ln2_b)
    mom2 = dict(ln_in_g=v_ln_in_g, ln_in_b=v_ln_in_b, w_in=v_w_in, q_norm_g=v_q_norm_g, w_uq=v_w_uq,
                kv_norm_g=v_kv_norm_g, w_ukv=v_w_ukv, conv_w=v_conv_w, conv_b=v_conv_b, conv_ln_g=v_conv_ln_g,
                conv_ln_b=v_conv_ln_b, w_pool=v_w_pool, pool_scale=v_pool_scale, w_out=v_w_out, ln1_g=v_ln1_g,
                ln1_b=v_ln1_b, w_up=v_w_up, ffn_conv_w=v_ffn_conv_w, ffn_conv_b=v_ffn_conv_b, w_down=v_w_down,
                ln2_g=v_ln2_g, ln2_b=v_ln2_b)
    names = list(weights)

    nb, seq, d = x.shape
    t = nb * seq
    depth = w_in.shape[0]
    ql, kvl, cw, pw = q_norm_g.shape[1], kv_norm_g.shape[1], conv_b.shape[1], pool_scale.shape[1]
    pg = w_pool.shape[-1]
    heads = w_uq.shape[2]
    dff = w_down.shape[1] * N_DEV
    alpha = (2.0 * depth) ** 0.25
    scale = float(QK_NOPE_DIM + QK_ROPE_DIM) ** -0.5
    lay = dict(ql=ql, kvl=kvl, pw=pw, pg=pg, off_q=2 * cw, off_pool=2 * cw + ql, off_kv=2 * cw + ql + pw,
               off_kr=2 * cw + ql + pw + kvl)
    o1, o2, o3, o4 = ql, ql + kvl, ql + kvl + QK_ROPE_DIM, ql + kvl + QK_ROPE_DIM + 2 * cw
    my_x, my_y, my_c = lax.axis_index("x"), lax.axis_index("y"), lax.axis_index("c")
    my_dev = 4 * my_x + 2 * my_y + my_c

    big = ("w_in", "w_uq", "w_ukv", "w_out", "w_up", "w_down")
    g_conv, g_ffn = _all_gather([conv_w, ffn_conv_w], "ag_conv_taps")
    conv_w_full, ffn_w_full = _gathered_to_full(g_conv, 2), _gathered_to_full(g_ffn, 2)
    w_pool_2d = w_pool.reshape(depth, pw, pg)

    rest = ("w_in", "w_uq", "w_ukv", "w_out")

    def bf16_shards(l, which):
        return [weights[n][l].astype(bf16) for n in which]

    def small_weights(gathered):
        g_in, g_uq, g_ukv, g_out = (gathered[n] for n in rest)
        wi = _gathered_to_full(g_in, 1)
        kr_cols = wi[:, o2:o3]
        w_in_pad = jnp.concatenate([_interleave_halves(wi[:, o3:o4], SEQ_CT), wi[:, :o1], wi[:, o4:], wi[:, o1:o2],
                                    kr_cols, _rot_cols(kr_cols)], axis=-1)
        wq = g_uq.reshape(ql, heads, QK_NOPE_DIM + QK_ROPE_DIM)
        w_uq_ext = jnp.concatenate([wq, _rot_cols(wq[..., QK_NOPE_DIM:])], axis=-1).reshape(ql, heads * 2 * LANES)
        return w_in_pad, w_uq_ext, g_ukv.reshape(kvl, heads * 2 * LANES), g_out.reshape(-1, d)

    half = QK_ROPE_DIM // 2
    inv = 1.0 / (ROPE_THETA ** (jnp.arange(0, QK_ROPE_DIM, 2, dtype=f32) / QK_ROPE_DIM))
    inv_lanes = jnp.tile(inv, LANES // half).reshape(1, LANES)
    cs, qt = _rope_tables(positions.reshape(t, 1), inv_lanes, scale)

    x2 = x.reshape(t, d)
    xs, xs_bf = _ln_fwd(x2, None, ln_in_g.reshape(1, d), ln_in_b.reshape(1, d), 1.0, "ln_in_fwd")
    saved = []
    in_pad = o4 + pw + QK_ROPE_DIM
    gathered = dict(zip(rest, _all_gather(bf16_shards(0, rest), "ag_weights")))
    for l in range(depth):
        nxt = l + 1 < depth
        gq, gkv = q_norm_g[l].reshape(1, ql), kv_norm_g[l].reshape(1, kvl)
        w_in_l, w_uq_l, w_ukv_l, w_out_l = small_weights(gathered)
        h = _matmul(xs_bf, w_in_l, "nn", f32, "mm_in", tm=512, tn=in_pad)
        qn, kvn, krd = _prep_fwd(h, cs, gq, gkv, lay)
        q_ext = _matmul(qn, w_uq_l, "nn", f32, "mm_uq", tn=2048)
        kv = _matmul(kvn, w_ukv_l, "nn", bf16, "mm_ukv", tn=2048)
        riders = ("w_down", "w_up") if l == 0 else ("w_down",)
        y_mla, lse, g_half = _attn_fwd(q_ext, qt, kv, krd, nb, seq, heads, comm=_gather_own(bf16_shards(l, riders)))
        hconv = _conv_fwd(h, conv_w_full[l], conv_b[l].reshape(1, cw), nb, seq, cw)
        y_conv = _convln_fwd(hconv, conv_ln_g[l].reshape(1, cw), conv_ln_b[l].reshape(1, cw))
        y_pool = _pool_fwd(h, w_pool_2d[l], pool_scale[l].reshape(1, pw), lay, nb, seq)
        mixed = jnp.concatenate([y_mla, y_conv, y_pool], axis=-1)
        y1, g_full = _matmul(mixed, w_out_l, "nn", f32, "mm_out", comm=_gather_pass(g_half))
        gathered.update(zip(riders, g_full))
        wl = dict(w_in=w_in_l, w_uq=w_uq_l, w_ukv=w_ukv_l, w_out=w_out_l, w_up=gathered["w_up"],
                  w_down=gathered["w_down"].reshape(dff, d))
        gathered = {}
        x1, x1_bf = _ln_fwd(xs, y1, ln1_g[l].reshape(1, d), ln1_b[l].reshape(1, d), alpha, "ln1_fwd")
        if nxt:
            up, (g_up,) = _matmul_up(x1_bf, wl["w_up"], "mm_up", comm=_gather_own(bf16_shards(l + 1, ("w_up",))))
            act, g_rest = _ffn_act_fwd(up, ffn_w_full[l], ffn_conv_b[l].reshape(1, 2 * dff), nb, seq, dff,
                                       comm=_gather_own(bf16_shards(l + 1, rest[1:])))
            y2, (gathered["w_up"], g_in) = _matmul(
                act, wl["w_down"], "nn", f32, "mm_down", tk=dff // 2,
                comm=_comm_join(_gather_pass([g_up]), _gather_own(bf16_shards(l + 1, rest[:1]))))
            xn, xn_bf, g_rest = _ln_fwd(x1, y2, ln2_g[l].reshape(1, d), ln2_b[l].reshape(1, d), alpha, "ln2_fwd",
                                        comm=_gather_pass([g_in] + g_rest))
            gathered.update(zip(rest, g_rest))
        else:
            up = _matmul_up(x1_bf, wl["w_up"], "mm_up")
            act = _ffn_act_fwd(up, ffn_w_full[l], ffn_conv_b[l].reshape(1, 2 * dff), nb, seq, dff)
            y2 = _matmul(act, wl["w_down"], "nn", f32, "mm_down", tk=dff // 2)
            xn, xn_bf = _ln_fwd(x1, y2, ln2_g[l].reshape(1, d), ln2_b[l].reshape(1, d), alpha, "ln2_fwd")
        saved.append(dict(xs=xs, xs_bf=xs_bf, h=h, qn=qn, kvn=kvn, krd=krd, q_ext=q_ext, kv=kv, lse=lse, hconv=hconv,
                          mixed=mixed, y1=y1, x1=x1, x1_bf=x1_bf, up=up, act=act, y2=y2, wl=wl))
        xs, xs_bf = xn, xn_bf

    d_stream, loss_row = _loss_call(xs, loss_target.reshape(t, d))
    loss = lax.psum(loss_row[0, 0], MESH_AXES)

    gw = {n: [None] * depth for n in names if n not in ("ln_in_g", "ln_in_b")}
    small = [n for n in gw if n not in big]
    rs_own = {n: [None] * depth for n in big}
    rs_chips = {n: [None] * depth for n in big}
    core_idx = jnp.reshape(my_c, (1,)).astype(jnp.int32)
    chip_idx = jnp.reshape(2 * my_x + my_y, (1,)).astype(jnp.int32)

    def pair_add(l, which, slots, theirs):
        p4s = [_pair_add(slots[n], th, core_idx, "rs_add_" + n) for n, th in zip(which, theirs)]
        for n, p4 in zip(which, p4s):
            rs_own[n][l] = p4
        return p4s

    d_res, d_mm = None, d_stream
    pending = None
    for l in reversed(range(depth)):
        sv = saved[l]
        wl = sv["wl"]
        slots = {}
        gq, gkv = q_norm_g[l].reshape(1, ql), kv_norm_g[l].reshape(1, kvl)
        dz2, dz2_bf, gw["ln2_g"][l], gw["ln2_b"][l] = _ln_bwd(
            d_res, d_mm, sv["x1"], sv["y2"], ln2_g[l].reshape(1, d), ln2_b[l].reshape(1, d), alpha, "ln2_bwd")
        if pending is None:
            dw_down = _matmul(sv["act"], dz2_bf, "tn", bf16, "mm_down_dw", tm=dff // 4)
        else:
            dw_down, theirs = _matmul(sv["act"], dz2_bf, "tn", bf16, "mm_down_dw", tm=dff // 4,
                                      comm=_sibling_swap([pending["slots"][n] for n in rest]))
            rest_p4s = pair_add(pending["layer"], rest, pending["slots"], theirs)
        slots["w_down"] = dw_down.reshape(N_DEV, -1, d)
        d_act, theirs = _matmul(dz2_bf, wl["w_down"], "nt", bf16, "mm_down_dx", tn=dff // 4,
                                comm=_sibling_swap([slots["w_down"]]))
        down_p4s = pair_add(l, ("w_down",), slots, theirs)
        ffn_args = (sv["up"], ffn_w_full[l], ffn_conv_b[l].reshape(1, 2 * dff), d_act, nb, seq, dff)
        if pending is None:
            d_up, dffw, dffb = _ffn_act_bwd(*ffn_args)
        else:
            d_up, dffw, dffb, (rs_chips["w_up"][pending["layer"]],) = _ffn_act_bwd(
                *ffn_args, comm=_chip_swap(pending["up_p4s"], peers=(2,), into=pending["up_partial"]))
        gw["ffn_conv_w"][l] = jnp.concatenate([dffw[0], dffw[1]], axis=-1)
        gw["ffn_conv_b"][l] = dffb.reshape(2 * dff)
        d_x1, (rs_chips["w_down"][l],) = _matmul_up_dx(d_up, wl["w_up"], "mm_up_dx", comm=_chip_swap(down_p4s))
        if pending is None:
            slots["w_up"] = _matmul_up_dw(sv["x1_bf"], d_up, N_DEV, "mm_up_dw")
        else:
            slots["w_up"], from_chips = _matmul_up_dw(sv["x1_bf"], d_up, N_DEV, "mm_up_dw", comm=_chip_swap(rest_p4s))
            for n, fc in zip(rest, from_chips):
                rs_chips[n][pending["layer"]] = fc
        dz1, dz1_bf, gw["ln1_g"][l], gw["ln1_b"][l] = _ln_bwd(
            dz2, d_x1, sv["xs"], sv["y1"], ln1_g[l].reshape(1, d), ln1_b[l].reshape(1, d), alpha, "ln1_bwd")
        d_mixed, theirs = _matmul(dz1_bf, wl["w_out"], "nt", f32, "mm_out_dx", comm=_sibling_swap([slots["w_up"]]))
        up_p4s = pair_add(l, ("w_up",), slots, theirs)
        slots["w_out"] = _matmul(sv["mixed"], dz1_bf, "tn", bf16, "mm_out_dw").reshape(N_DEV, -1, d)
        d_upool, dwp, dps = _pool_bwd(sv["h"], w_pool_2d[l], pool_scale[l].reshape(1, pw), d_mixed,
                                      (heads * LANES + cw) // pw, lay, nb, seq)
        gw["w_pool"][l] = dwp.reshape(w_pool.shape[1:])
        gw["pool_scale"][l] = dps.reshape(pw)
        d_hconv, dclg, dclb = _convln_bwd(sv["hconv"], conv_ln_g[l].reshape(1, cw), conv_ln_b[l].reshape(1, cw), d_mixed,
                                          heads * LANES // cw)
        gw["conv_ln_g"][l], gw["conv_ln_b"][l] = dclg.reshape(cw), dclb.reshape(cw)
        d_conv, gw["conv_w"][l], dcb = _conv_bwd(sv["h"], conv_w_full[l], d_hconv, nb, seq, cw)
        gw["conv_b"][l] = dcb.reshape(cw)
        dq_ext, dkv, dkrd, up_partial = _attn_bwd(sv["q_ext"], qt, sv["kv"], sv["krd"], sv["lse"], d_mixed, nb, seq, heads,
                                                  comm=_chip_swap(up_p4s, peers=(0, 1, 2) if l == 0 else (0, 1)))
        if l == 0:
            rs_chips["w_up"][l] = up_partial[0]
        d_qn = _matmul(dq_ext, wl["w_uq"], "nt", f32, "mm_uq_dx")
        dwq = _matmul(sv["qn"], dq_ext, "tn", f32, "mm_uq_dw", tn=2048, tk=1024).reshape(ql, heads, 2 * LANES)
        dwq_rope = dwq[..., QK_NOPE_DIM:QK_NOPE_DIM + QK_ROPE_DIM] + _rot_cols_t(dwq[..., QK_NOPE_DIM + QK_ROPE_DIM:])
        slots["w_uq"] = jnp.concatenate([dwq[..., :QK_NOPE_DIM], dwq_rope], axis=-1).astype(bf16).reshape(
            N_DEV, -1, QK_NOPE_DIM + QK_ROPE_DIM)
        d_kvn = _matmul(dkv, wl["w_ukv"], "nt", f32, "mm_ukv_dx")
        slots["w_ukv"] = _matmul(sv["kvn"], dkv, "tn", bf16, "mm_ukv_dw", tn=2048, tk=1024).reshape(N_DEV, -1, 2 * LANES)
        d_cq, d_ckv, d_kr, dgq, dgkv = _prep_bwd(sv["h"], cs, gq, gkv, d_qn, d_kvn, dkrd, lay)
        gw["q_norm_g"][l], gw["kv_norm_g"][l] = dgq.reshape(ql), dgkv.reshape(kvl)
        d_h = jnp.concatenate([d_conv, d_cq, d_upool, d_ckv, d_kr], axis=-1)
        if l > 0:
            d_xs = _matmul(d_h, wl["w_in"], "nt", f32, "mm_in_dx", tm=512, tk=in_pad)
            dwi = _matmul(sv["xs_bf"], d_h, "tn", f32, "mm_in_dw", tn=in_pad, tk=1024)
        else:
            flat_small = jnp.concatenate([jnp.stack(gw[n]).astype(f32).reshape(-1) for n in small]).reshape(-1, LANES)
            d_xs, small_half = _matmul(d_h, wl["w_in"], "nt", f32, "mm_in_dx", tm=512, tk=in_pad,
                                       comm=_gather_own([flat_small]))
            dwi, (gathered_small,) = _matmul(sv["xs_bf"], d_h, "tn", f32, "mm_in_dw", tn=in_pad, tk=1024,
                                             comm=_gather_pass(small_half))
        dkr_cols = dwi[:, lay["off_kr"]:lay["off_kr"] + QK_ROPE_DIM] + _rot_cols_t(dwi[:, lay["off_kr"] + QK_ROPE_DIM:])
        dwi_nat = jnp.concatenate(
            [dwi[:, lay["off_q"]:lay["off_q"] + ql], dwi[:, lay["off_kv"]:lay["off_kv"] + kvl], dkr_cols,
             _deinterleave_halves(dwi[:, :2 * cw], SEQ_CT), dwi[:, lay["off_pool"]:lay["off_pool"] + pw]],
            axis=-1).astype(bf16)
        slots["w_in"] = jnp.moveaxis(dwi_nat.reshape(d, N_DEV, -1), 1, 0)
        pending = dict(layer=l, slots=slots, up_p4s=up_p4s, up_partial=up_partial)
        d_res, d_mm = dz1, d_xs

    grad_x, _, d_ln_in_g, d_ln_in_b, theirs = _ln_bwd(
        d_res, d_mm, x2, None, ln_in_g.reshape(1, d), ln_in_b.reshape(1, d), alpha, "ln_in_bwd",
        comm=_sibling_swap([pending["slots"][n] for n in rest]))
    grad_x = grad_x.reshape(x.shape)
    rest_p4s = pair_add(pending["layer"], rest, pending["slots"], theirs)

    grads, deltas, new_m, new_v = {}, {}, {}, {}

    def finish(n, parts):
        shp = weights[n].shape
        rows = math.prod(shp[:-1]) if len(shp) > 1 else 1
        as2d = lambda a: a.reshape(rows, shp[-1])
        parts = [p.reshape(p.shape[0], rows, shp[-1]) for p in parts]
        g, dl, nm, nv = _adamw(parts, as2d(weights[n]), as2d(mom1[n]), as2d(mom2[n]), "adamw_" + n)
        grads[n], deltas[n], new_m[n], new_v[n] = (a.reshape(shp) for a in (g, dl, nm, nv))

    def finish_big(n, comm=None):
        shp = weights[n].shape
        as2d = lambda a: a.reshape(-1, shp[-1])
        res = _adamw_layers(rs_own[n], rs_chips[n], chip_idx, as2d(weights[n]), as2d(mom1[n]), as2d(mom2[n]),
                            "adamw_" + n, comm)
        outs, comm_outs = res if comm is not None else (res, None)
        grads[n], deltas[n], new_m[n], new_v[n] = (a.reshape(shp) for a in outs)
        return comm_outs

    ln_in = ("ln_in_g", "ln_in_b")
    flat_ln = jnp.concatenate([d_ln_in_g.reshape(-1), d_ln_in_b.reshape(-1)]).reshape(-1, LANES)
    gathered_ln = _all_gather([flat_ln], "ag_ln_in_grads")[0].reshape(N_DEV, 2, d)
    finish_big("w_up")
    from_chips = finish_big("w_down", _chip_swap(rest_p4s))
    for n, fc in zip(rest, from_chips):
        rs_chips[n][pending["layer"]] = fc
    for n in rest:
        finish_big(n)
    for i, n in enumerate(ln_in):
        finish(n, [gathered_ln[:, i]])
    gathered_small = gathered_small.reshape(N_DEV, -1)
    off = 0
    for n in small:
        shape = (depth,) + gw[n][0].shape
        size = math.prod(shape)
        part = gathered_small[:, off:off + size].reshape((N_DEV,) + shape)
        off += size
        if n in ("conv_w", "ffn_conv_w"):
            width = weights[n].shape[-1]
            part = lax.dynamic_slice_in_dim(part, my_dev * width, width, axis=part.ndim - 1)
        finish(n, [part])

    return (loss, grad_x, *[grads[n] for n in names], *[deltas[n] for n in names], *[new_m[n] for n in names],
            *[new_v[n] for n in names])
```
